```python
import jax, jax.numpy as jnp
from jax import lax
import numpy as np

D_MODEL = 1024
BATCH = 16
SEQ = 256
DEPTH = 2
DEC_BATCH = 4
DEC_SEQ = 1024
PAST_LEN = 256

GRID_W = 64
HEAD_DIM = 64
BRANCH_DIM = D_MODEL // 2
A_Q_HEADS = BRANCH_DIM // HEAD_DIM
A_KV_HEADS = A_Q_HEADS // 4
A_GROUP = A_Q_HEADS // A_KV_HEADS
A_WINDOW = 128
A_BLOCK = 128
B_HEADS = BRANCH_DIM // HEAD_DIM
B_WIN_ROWS = 8
B_WIN_COLS = 16
C_GROUPS = 4
C_GROUP_DIM = BRANCH_DIM // C_GROUPS
N_BRANCH = 3
D_FF = -(-8 * D_MODEL // (3 * 256)) * 256
ROPE_BASE = 10000.0
NORM_EPS = 1e-6
NEG_INF = -1e30
IN_SIZES = (A_Q_HEADS * HEAD_DIM, A_KV_HEADS * HEAD_DIM, A_KV_HEADS * HEAD_DIM,
            B_HEADS * HEAD_DIM, B_HEADS * HEAD_DIM, B_HEADS * HEAD_DIM,
            BRANCH_DIM, N_BRANCH * D_MODEL)
IN_SPLITS = tuple(int(s) for s in np.cumsum(IN_SIZES)[:-1])
D_IN = int(sum(IN_SIZES))

kernel_name = "hybrid_flow_prefix_step"


def rms_norm(x, g):
    xf = x.astype(jnp.float32)
    y = xf * lax.rsqrt(jnp.mean(xf * xf, axis=-1, keepdims=True) + NORM_EPS)
    return (y * g.astype(jnp.float32)).astype(x.dtype)


def axial_rope(n_tokens):
    t = jnp.arange(n_tokens, dtype=jnp.int32)
    row = (t // GRID_W).astype(jnp.float32)
    col = (t % GRID_W).astype(jnp.float32)
    n_pairs_axis = HEAD_DIM // 4
    inv = ROPE_BASE ** (-jnp.arange(n_pairs_axis, dtype=jnp.float32) / n_pairs_axis)
    ang = jnp.concatenate([row[:, None] * inv, col[:, None] * inv], axis=-1)
    return jnp.cos(ang), jnp.sin(ang)


def apply_rope(x, cos, sin):
    half = HEAD_DIM // 2
    xf = x.astype(jnp.float32)
    x1, x2 = xf[..., :half], xf[..., half:]
    c = cos[None, :, None, :]
    s = sin[None, :, None, :]
    return jnp.concatenate([x1 * c - x2 * s, x1 * s + x2 * c], axis=-1).astype(x.dtype)


def sink_softmax(s, sink):
    m = jnp.maximum(jnp.max(s, axis=-1, keepdims=True), sink)
    e = jnp.exp(s - m)
    return e / (jnp.sum(e, axis=-1, keepdims=True) + jnp.exp(sink - m))


def combined_projection(h, w_in):
    B, T, _ = h.shape
    qa, ka, va, qb, kb, vb, uc, gates = jnp.split(h @ w_in, IN_SPLITS, axis=-1)
    heads = lambda z, n: z.reshape(B, T, n, HEAD_DIM)
    return (heads(qa, A_Q_HEADS), heads(ka, A_KV_HEADS), heads(va, A_KV_HEADS),
            heads(qb, B_HEADS), heads(kb, B_HEADS), heads(vb, B_HEADS), uc, gates)


def gqa_sink_context(q, k, v, sink):
    B, S = q.shape[:2]
    qg = q.reshape(B, S, A_KV_HEADS, A_GROUP, HEAD_DIM)
    s = jnp.einsum('bqkgd,bskd->bkgqs', qg, k, preferred_element_type=jnp.float32) * HEAD_DIM ** -0.5
    p = sink_softmax(s, sink.astype(jnp.float32).reshape(1, A_KV_HEADS, A_GROUP, 1, 1))
    o = jnp.einsum('bkgqs,bskd->bqkgd', p.astype(v.dtype), v)
    return o.reshape(B, S, A_Q_HEADS * HEAD_DIM)


def window_gqa_latent(q, k, v, ck, cv, sink):
    B, T = q.shape[:2]
    nb = T // A_BLOCK
    qb = q.reshape(B, nb, A_BLOCK, A_KV_HEADS, A_GROUP, HEAD_DIM)
    pad = ((0, 0), (A_BLOCK, A_BLOCK), (0, 0), (0, 0))
    kp = jnp.pad(k, pad)
    vp = jnp.pad(v, pad)
    idx = np.arange(nb)[:, None] * A_BLOCK + np.arange(3 * A_BLOCK)[None, :]
    kb = kp[:, idx]
    vb = vp[:, idx]
    qpos = np.arange(T).reshape(nb, A_BLOCK)
    kpos = idx - A_BLOCK
    valid = ((np.abs(qpos[:, :, None] - kpos[:, None, :]) <= A_WINDOW)
             & (kpos >= 0)[:, None, :] & (kpos < T)[:, None, :])
    scale = HEAD_DIM ** -0.5
    s_band = jnp.einsum('bnqkgd,bnskd->bnkgqs', qb, kb, preferred_element_type=jnp.float32) * scale
    s_band = jnp.where(valid[None, :, None, None], s_band, NEG_INF)
    s_ctx = jnp.einsum('bnqkgd,bpkd->bnkgqp', qb, ck, preferred_element_type=jnp.float32) * scale
    p = sink_softmax(jnp.concatenate([s_band, s_ctx], axis=-1),
                     sink.astype(jnp.float32).reshape(1, 1, A_KV_HEADS, A_GROUP, 1, 1))
    n_band = 3 * A_BLOCK
    o = (jnp.einsum('bnkgqs,bnskd->bnqkgd', p[..., :n_band].astype(v.dtype), vb)
         + jnp.einsum('bnkgqp,bpkd->bnqkgd', p[..., n_band:].astype(cv.dtype), cv))
    return o.reshape(B, T, A_Q_HEADS * HEAD_DIM)


def mha_context(q, k, v):
    B, S = q.shape[:2]
    s = jnp.einsum('bqhd,bshd->bhqs', q, k, preferred_element_type=jnp.float32) * HEAD_DIM ** -0.5
    p = jax.nn.softmax(s, axis=-1)
    o = jnp.einsum('bhqs,bshd->bqhd', p.astype(v.dtype), v)
    return o.reshape(B, S, B_HEADS * HEAD_DIM)


def neighbourhood_latent(q, k, v, ck, cv, rpb):
    B, T, H, D = q.shape
    rows = T // GRID_W
    kr = min(B_WIN_ROWS, rows)
    kc = B_WIN_COLS
    reg = 2 * kc
    ncb = GRID_W // kc
    cols = np.arange(GRID_W)
    win_start = np.clip(cols - kc // 2, 0, GRID_W - kc)
    reg_cols = np.clip(np.arange(ncb) * kc - kc // 2, 0, GRID_W - reg)[:, None] + np.arange(reg)
    q_cols = cols.reshape(ncb, kc)
    ws = win_start[q_cols][:, :, None]
    col_ok = (reg_cols[:, None, :] >= ws) & (reg_cols[:, None, :] < ws + kc)
    dc_idx = np.clip(reg_cols[:, None, :] - q_cols[:, :, None] + B_WIN_COLS - 1, 0, 2 * B_WIN_COLS - 2)
    scale = HEAD_DIM ** -0.5
    qg = q.reshape(B, rows, GRID_W, H, D)
    kg = k.reshape(B, rows, GRID_W, H, D)
    vg = v.reshape(B, rows, GRID_W, H, D)
    n_loc = kr * reg

    def row_block(r):
        r0 = jnp.clip(r - kr // 2, 0, rows - kr)
        k_reg = lax.dynamic_slice_in_dim(kg, r0, kr, axis=1)[:, :, reg_cols]
        v_reg = lax.dynamic_slice_in_dim(vg, r0, kr, axis=1)[:, :, reg_cols]
        q_blk = lax.dynamic_index_in_dim(qg, r, axis=1, keepdims=False).reshape(B, ncb, kc, H, D)
        s_loc = jnp.einsum('bnqhd,bmnshd->bhnqms', q_blk, k_reg, preferred_element_type=jnp.float32) * scale
        row_idx = (r0 + jnp.arange(kr) - r + B_WIN_ROWS - 1)[None, None, :, None]
        bias = rpb[:, row_idx, dc_idx[:, :, None, :]].astype(jnp.float32)
        s_loc = jnp.where(col_ok[None, None, :, :, None, :], s_loc + bias[None], NEG_INF)
        s_loc = s_loc.reshape(B, H, ncb, kc, n_loc)
        s_ctx = jnp.einsum('bnqhd,bphd->bhnqp', q_blk, ck, preferred_element_type=jnp.float32) * scale
        p = jax.nn.softmax(jnp.concatenate([s_loc, s_ctx], axis=-1), axis=-1)
        p_loc = p[..., :n_loc].reshape(B, H, ncb, kc, kr, reg).astype(v.dtype)
        o = (jnp.einsum('bhnqms,bmnshd->bnqhd', p_loc, v_reg)
             + jnp.einsum('bhnqp,bphd->bnqhd', p[..., n_loc:].astype(cv.dtype), cv))
        return o.reshape(B, GRID_W, H * D)

    o = lax.map(row_block, jnp.arange(rows))
    return jnp.moveaxis(o, 0, 1).reshape(B, T, H * D)


def fourier_mix(u):
    B, T, _ = u.shape
    ug = u.reshape(B, T, C_GROUPS, C_GROUP_DIM).astype(jnp.float32)
    f = jnp.fft.fft2(ug, axes=(1, 3), norm='ortho').real
    return f.reshape(B, T, BRANCH_DIM).astype(u.dtype)


def merge_branches(oa, ob, oc, gates, w_branch, w_out):
    ga, gb, gc = jnp.split(gates, N_BRANCH, axis=-1)
    m = (jax.nn.sigmoid(ga) * (oa @ w_branch[0]) + jax.nn.sigmoid(gb) * (ob @ w_branch[1])
         + jax.nn.sigmoid(gc) * (oc @ w_branch[2]))
    return m @ w_out


def swiglu(h, w_ffn_in, w_ffn_out):
    g, u = jnp.split(h @ w_ffn_in, 2, axis=-1)
    return (jax.nn.silu(g) * u) @ w_ffn_out


def trunk_layer(x, mod, g_pre, g_post, w_in_l, w_branch_l, w_out_l, w_ffn_in_l, w_ffn_out_l, attend):
    sh1, sc1, g1, sh2, sc2, g2 = jnp.split(mod, 6, axis=-1)
    h = rms_norm(x, g_pre[0]) * (1 + sc1) + sh1
    qa, ka, va, qb, kb, vb, uc, gates = combined_projection(h, w_in_l)
    oa, ob = attend(qa, ka, va, qb, kb, vb)
    oc = fourier_mix(uc)
    y = merge_branches(oa, ob, oc, gates, w_branch_l, w_out_l)
    x = x + g1 * rms_norm(y, g_post[0])
    h = rms_norm(x, g_pre[1]) * (1 + sc2) + sh2
    x = x + g2 * rms_norm(swiglu(h, w_ffn_in_l, w_ffn_out_l), g_post[1])
    return x, (ka, va, kb, vb)


def setup_inputs(seed: int = 0) -> dict:
    key = jax.random.key(seed)
    ks = jax.random.split(key, 19)
    f32 = jnp.float32
    nrm = lambda k, shape, s: jax.random.normal(k, shape, f32) * s
    return {
        "x_prompt": nrm(ks[0], (BATCH, SEQ, D_MODEL), 1.0),
        "x_sample": nrm(ks[1], (DEC_BATCH, DEC_SEQ, D_MODEL), 1.0),
        "cache_a_k": nrm(ks[2], (DEC_BATCH, DEPTH, PAST_LEN, A_KV_HEADS, HEAD_DIM), 1.0),
        "cache_a_v": nrm(ks[3], (DEC_BATCH, DEPTH, PAST_LEN, A_KV_HEADS, HEAD_DIM), 1.0),
        "cache_b_k": nrm(ks[4], (DEC_BATCH, DEPTH, PAST_LEN, B_HEADS, HEAD_DIM), 1.0),
        "cache_b_v": nrm(ks[5], (DEC_BATCH, DEPTH, PAST_LEN, B_HEADS, HEAD_DIM), 1.0),
        "c": nrm(ks[6], (DEC_BATCH, D_MODEL), 1.0),
        "c_ctx": nrm(ks[7], (D_MODEL,), 1.0),
        "w_ada": nrm(ks[8], (DEPTH, D_MODEL, 6 * D_MODEL), 0.5 * D_MODEL ** -0.5),
        "b_ada": nrm(ks[9], (DEPTH, 6 * D_MODEL), 0.02),
        "norm_pre": 1.0 + nrm(ks[10], (DEPTH, 2, D_MODEL), 0.02),
        "norm_post": 1.0 + nrm(ks[11], (DEPTH, 2, D_MODEL), 0.02),
        "w_in": nrm(ks[12], (DEPTH, D_MODEL, D_IN), D_MODEL ** -0.5),
        "a_sink": nrm(ks[13], (DEPTH, A_Q_HEADS), 1.0),
        "b_rpb": nrm(ks[14], (DEPTH, B_HEADS, 2 * B_WIN_ROWS - 1, 2 * B_WIN_COLS - 1), 0.1),
        "w_branch": nrm(ks[15], (DEPTH, N_BRANCH, BRANCH_DIM, D_MODEL), BRANCH_DIM ** -0.5),
        "w_out": nrm(ks[16], (DEPTH, D_MODEL, D_MODEL), D_MODEL ** -0.5),
        "w_ffn_in": nrm(ks[17], (DEPTH, D_MODEL, 2 * D_FF), D_MODEL ** -0.5),
        "w_ffn_out": nrm(ks[18], (DEPTH, D_FF, D_MODEL), D_FF ** -0.5),
    }


def reference(x_prompt, x_sample, cache_a_k, cache_a_v, cache_b_k, cache_b_v, c, c_ctx,
              w_ada, b_ada, norm_pre, norm_post, w_in, a_sink, b_rpb, w_branch, w_out,
              w_ffn_in, w_ffn_out):
    x = x_prompt
    ak, av, bk, bv = [], [], [], []
    for l in range(DEPTH):
        mod = jax.nn.silu(c_ctx) @ w_ada[l] + b_ada[l]
        x, (ka, va, kb, vb) = trunk_layer(
            x, mod, norm_pre[l], norm_post[l], w_in[l], w_branch[l], w_out[l], w_ffn_in[l], w_ffn_out[l],
            lambda qa, ka, va, qb, kb, vb: (gqa_sink_context(qa, ka, va, a_sink[l]), mha_context(qb, kb, vb)))
        ak.append(ka)
        av.append(va)
        bk.append(kb)
        bv.append(vb)
    y_prompt = x
    new_a_k = jnp.stack(ak, axis=1)
    new_a_v = jnp.stack(av, axis=1)
    new_b_k = jnp.stack(bk, axis=1)
    new_b_v = jnp.stack(bv, axis=1)

    cos, sin = axial_rope(x_sample.shape[1])
    x = x_sample
    for l in range(DEPTH):
        mod = (jax.nn.silu(c) @ w_ada[l] + b_ada[l])[:, None, :]
        x, _ = trunk_layer(
            x, mod, norm_pre[l], norm_post[l], w_in[l], w_branch[l], w_out[l], w_ffn_in[l], w_ffn_out[l],
            lambda qa, ka, va, qb, kb, vb: (
                window_gqa_latent(apply_rope(qa, cos, sin), apply_rope(ka, cos, sin), va,
                                  cache_a_k[:, l], cache_a_v[:, l], a_sink[l]),
                neighbourhood_latent(qb, kb, vb, cache_b_k[:, l], cache_b_v[:, l], b_rpb[l])))
    y_sample = x
    return (y_prompt, y_sample, new_a_k, new_a_v, new_b_k, new_b_v)
```

```python
import functools

import numpy as np
import jax
import jax.numpy as jnp
from jax import lax
from jax.experimental import pallas as pl
from jax.experimental.pallas import tpu as pltpu

F32 = jnp.float32
BF16 = jnp.bfloat16

D_MODEL = 1024
DEPTH = 2
GRID_W = 64
HEAD_DIM = 64
BRANCH_DIM = D_MODEL // 2
A_Q_HEADS = BRANCH_DIM // HEAD_DIM
A_KV_HEADS = A_Q_HEADS // 4
A_GROUP = A_Q_HEADS // A_KV_HEADS
A_BLOCK = 128
B_HEADS = BRANCH_DIM // HEAD_DIM
B_WIN_ROWS = 8
B_WIN_COLS = 16
C_GROUPS = 4
C_GROUP_DIM = BRANCH_DIM // C_GROUPS
N_BRANCH = 3
D_FF = -(-8 * D_MODEL // (3 * 256)) * 256
ROPE_BASE = 10000.0
NORM_EPS = 1e-6
NEG_INF = -1e30
SCALE = HEAD_DIM ** -0.5

LANES = 128
KV_A = A_KV_HEADS * HEAD_DIM
N_GATES = N_BRANCH * D_MODEL
N_MAIN = N_GATES + 3 * BRANCH_DIM
N_KV = 2 * BRANCH_DIM + 2 * KV_A
MAIN_TILE = BRANCH_DIM
QA_TILE = N_GATES // MAIN_TILE
QB_TILE = QA_TILE + 1
UC_TILE = QA_TILE + 2
KV_KA = 2 * BRANCH_DIM
KV_VA = KV_KA + KV_A
MOD_ROWS = 8
CTX_MOD_ROW = 4
VMEM_LIMIT = 56 * 1024 * 1024


def _cparams(sem):
    return pltpu.CompilerParams(dimension_semantics=sem, vmem_limit_bytes=VMEM_LIMIT)


def _resident(shape, index_map):
    return pl.BlockSpec(shape, index_map, pipeline_mode=pl.Buffered(1))


def _rms(x, g):
    return x * lax.rsqrt(jnp.mean(x * x, axis=-1, keepdims=True) + NORM_EPS) * g


def _sigmoid(x):
    return 1.0 / (1.0 + jnp.exp(-x))


def _dot_t(a, b):
    return lax.dot_general(a, b, (((1,), (1,)), ((), ())), preferred_element_type=F32)


def _dot(a, b):
    return jnp.dot(a, b, preferred_element_type=F32)


def _low_lanes(shape):
    return lax.broadcasted_iota(jnp.int32, shape, len(shape) - 1) < HEAD_DIM


def _mod_kernel(cv_ref, w_ref, b_ref, o_ref):
    cv = cv_ref[...]
    s = cv * _sigmoid(cv)
    o_ref[...] = jnp.dot(s, w_ref[...], preferred_element_type=F32,
                         precision=lax.Precision.HIGHEST) + b_ref[...]


def _modulation(cvec, w_ada, b_ada):
    tn = 1536
    n = 6 * D_MODEL
    return pl.pallas_call(
        _mod_kernel,
        grid=(DEPTH, n // tn),
        in_specs=[
            pl.BlockSpec((MOD_ROWS, D_MODEL), lambda l, j: (0, 0)),
            pl.BlockSpec((None, D_MODEL, tn), lambda l, j: (l, 0, j)),
            pl.BlockSpec((None, 1, tn), lambda l, j: (l, 0, j)),
        ],
        out_specs=pl.BlockSpec((None, MOD_ROWS, tn), lambda l, j: (l, 0, j)),
        out_shape=jax.ShapeDtypeStruct((DEPTH, MOD_ROWS, n), F32),
        compiler_params=_cparams(("parallel", "parallel")),
        name="modulation",
    )(cvec, w_ada, b_ada.reshape(DEPTH, 1, n))


def _rope_cols(v, cos, sin):
    first = (lax.broadcasted_iota(jnp.int32, (v.shape[0], LANES), 1) % HEAD_DIM) < HEAD_DIM // 2
    outs = []
    for c in range(v.shape[1] // LANES):
        vc = v[:, c * LANES:(c + 1) * LANES]
        partner = jnp.where(first, pltpu.roll(vc, LANES - HEAD_DIM // 2, 1),
                            pltpu.roll(vc, HEAD_DIM // 2, 1))
        outs.append(vc * cos + partner * sin)
    return outs[0] if len(outs) == 1 else jnp.concatenate(outs, axis=1)


def _norm_proj_kernel(*refs, rope):
    if rope is None:
        x_ref, mod_ref, g_ref, w_ref, o_ref, h_scr = refs
    else:
        x_ref, mod_ref, g_ref, w_ref, cos_ref, sin_ref, o_ref, h_scr = refs
    j = pl.program_id(1)

    @pl.when(j == 0)
    def _():
        h = _rms(x_ref[...], g_ref[...]) * (1.0 + mod_ref[1:2, :]) + mod_ref[0:1, :]
        h_scr[...] = h.astype(BF16)

    def plain():
        o_ref[...] = _dot(h_scr[...], w_ref[...]).astype(o_ref.dtype)

    if rope is None:
        plain()
        return
    tile, lo, hi = rope

    @pl.when(j != tile)
    def _():
        plain()

    @pl.when(j == tile)
    def _():
        acc = _dot(h_scr[...], w_ref[...])
        tn = acc.shape[1]
        parts = []
        if lo > 0:
            parts.append(acc[:, :lo])
        parts.append(_rope_cols(acc[:, lo:hi], cos_ref[...], sin_ref[...]))
        if hi < tn:
            parts.append(acc[:, hi:])
        out = parts[0] if len(parts) == 1 else jnp.concatenate(parts, axis=1)
        o_ref[...] = out.astype(o_ref.dtype)


def _norm_proj(x, mods, l, g_pre, w, *, tm, tn, mod_row, out_dtype, rope=None, rope_tabs=None, name):
    m = x.shape[0]
    n = w.shape[2]
    in_specs = [
        pl.BlockSpec((tm, D_MODEL), lambda i, j: (i, 0)),
        pl.BlockSpec((None, None, 6, D_MODEL), lambda i, j: (l, mod_row(i), 0, 0)),
        pl.BlockSpec((None, None, 1, D_MODEL), lambda i, j: (l, 0, 0, 0)),
        pl.BlockSpec((None, D_MODEL, tn), lambda i, j: (l, 0, j)),
    ]
    args = [x, mods, g_pre, w]
    if rope is not None:
        seq_tiles = rope_tabs[0].shape[0] // tm
        tab_spec = pl.BlockSpec((tm, LANES), lambda i, j: (i % seq_tiles, 0))
        in_specs += [tab_spec, tab_spec]
        args += list(rope_tabs)
    return pl.pallas_call(
        functools.partial(_norm_proj_kernel, rope=rope),
        grid=(m // tm, n // tn),
        in_specs=in_specs,
        out_specs=pl.BlockSpec((tm, tn), lambda i, j: (i, j)),
        out_shape=jax.ShapeDtypeStruct((m, n), out_dtype),
        scratch_shapes=[pltpu.VMEM((tm, D_MODEL), BF16)],
        compiler_params=_cparams(("parallel", "arbitrary")),
        name=name,
    )(*args)


def _softmax_pv(parts, sink=None):
    m = parts[0][0].max(axis=-1, keepdims=True)
    for s, _ in parts[1:]:
        m = jnp.maximum(m, s.max(axis=-1, keepdims=True))
    if sink is not None:
        m = jnp.maximum(m, sink)
    den = None
    acc = None
    for s, v in parts:
        e = jnp.exp(s - m)
        d = e.sum(axis=-1, keepdims=True)
        o = _dot(e.astype(BF16), v)
        den = d if den is None else den + d
        acc = o if acc is None else acc + o
    if sink is not None:
        den = den + jnp.exp(sink - m)
    return acc * (1.0 / den)


def _kv_head_variants(x2):
    low = _low_lanes(x2.shape)
    xr = pltpu.roll(x2, HEAD_DIM, 1)
    zero = jnp.zeros_like(x2)
    return [
        [jnp.where(low, x2, zero).astype(BF16), jnp.where(low, zero, xr).astype(BF16)],
        [jnp.where(low, xr, zero).astype(BF16), jnp.where(low, zero, x2).astype(BF16)],
    ]


def _attn_ctx_kernel(sink_ref, qa_ref, qb_ref, kv_ref, oa_ref, ob_ref):
    ka = _kv_head_variants(kv_ref[:, KV_KA:KV_KA + KV_A])
    va = _kv_head_variants(kv_ref[:, KV_VA:KV_VA + KV_A])
    for j in range(A_Q_HEADS // 2):
        g = (2 * j) // A_GROUP
        qc = qa_ref[:, j * LANES:(j + 1) * LANES]
        o = None
        for p in range(2):
            s = _dot_t(qc, ka[g][p]) * SCALE
            oh = _softmax_pv([(s, va[g][p])], sink=sink_ref[2 * j + p])
            o = oh if o is None else o + oh
        oa_ref[:, j * LANES:(j + 1) * LANES] = o.astype(oa_ref.dtype)
    for j in range(B_HEADS // 2):
        qc = qb_ref[:, j * LANES:(j + 1) * LANES]
        kc = kv_ref[:, j * LANES:(j + 1) * LANES]
        vc = kv_ref[:, BRANCH_DIM + j * LANES:BRANCH_DIM + (j + 1) * LANES]
        low = _low_lanes(kc.shape)
        zero = jnp.zeros_like(kc)
        o = None
        for p in range(2):
            keep = low if p == 0 else jnp.logical_not(low)
            kh = jnp.where(keep, kc, zero).astype(BF16)
            vh = jnp.where(keep, vc, zero).astype(BF16)
            s = _dot_t(qc, kh) * SCALE
            oh = _softmax_pv([(s, vh)])
            o = oh if o is None else o + oh
        ob_ref[:, j * LANES:(j + 1) * LANES] = o.astype(ob_ref.dtype)


def _attn_ctx(main, kv, sink, seq):
    m = main.shape[0]
    out = jax.ShapeDtypeStruct((m, BRANCH_DIM), BF16)
    return pl.pallas_call(
        _attn_ctx_kernel,
        grid=(m // seq,),
        in_specs=[
            pl.BlockSpec(memory_space=pltpu.SMEM),
            pl.BlockSpec((seq, BRANCH_DIM), lambda b: (b, QA_TILE)),
            pl.BlockSpec((seq, BRANCH_DIM), lambda b: (b, QB_TILE)),
            pl.BlockSpec((seq, N_KV), lambda b: (b, 0)),
        ],
        out_specs=[pl.BlockSpec((seq, BRANCH_DIM), lambda b: (b, 0))] * 2,
        out_shape=[out, out],
        compiler_params=_cparams(("parallel",)),
        name="attn_ctx",
    )(sink, main, main, kv)


def _attn_a_lat_kernel(sink_ref, q_ref, kv_ref, ck_ref, cv_ref, o_ref, *, n_blocks):
    n = pl.program_id(1)
    prev = pl.multiple_of(jnp.maximum(n - 1, 0) * A_BLOCK, A_BLOCK)
    cur = pl.multiple_of(n * A_BLOCK, A_BLOCK)
    nxt = pl.multiple_of(jnp.minimum(n + 1, n_blocks - 1) * A_BLOCK, A_BLOCK)

    def band(col):
        return jnp.concatenate(
            [kv_ref[pl.ds(s, A_BLOCK), col:col + KV_A] for s in (prev, cur, nxt)], axis=0).astype(F32)

    kb = _kv_head_variants(band(0))
    vb = _kv_head_variants(band(KV_A))
    kc = _kv_head_variants(ck_ref[...])
    vc = _kv_head_variants(cv_ref[...])
    qi = lax.broadcasted_iota(jnp.int32, (A_BLOCK, 3 * A_BLOCK), 0)
    kj = lax.broadcasted_iota(jnp.int32, (A_BLOCK, 3 * A_BLOCK), 1)
    no_prev = jnp.where(n > 0, 0, 1)
    no_next = jnp.where(n < n_blocks - 1, 0, 1)
    valid = (kj >= qi + no_prev * (A_BLOCK - qi)) & (kj <= 2 * A_BLOCK + qi - no_next * (qi + 1))
    for j in range(A_Q_HEADS // 2):
        g = (2 * j) // A_GROUP
        qc = q_ref[:, j * LANES:(j + 1) * LANES]
        o = None
        for p in range(2):
            s_band = jnp.where(valid, _dot_t(qc, kb[g][p]) * SCALE, NEG_INF)
            s_ctx = _dot_t(qc, kc[g][p]) * SCALE
            oh = _softmax_pv([(s_band, vb[g][p]), (s_ctx, vc[g][p])], sink=sink_ref[2 * j + p])
            o = oh if o is None else o + oh
        o_ref[:, j * LANES:(j + 1) * LANES] = o.astype(o_ref.dtype)


def _attn_a_lat(main, kv, cache_k, cache_v, sink, l, seq):
    m = main.shape[0]
    nb = seq // A_BLOCK
    n_seq = m // seq
    cache_spec = pl.BlockSpec((None, None, cache_k.shape[2], KV_A), lambda b, n: (b, l, 0, 0))
    return pl.pallas_call(
        functools.partial(_attn_a_lat_kernel, n_blocks=nb),
        grid=(n_seq, nb),
        in_specs=[
            pl.BlockSpec(memory_space=pltpu.SMEM),
            pl.BlockSpec((A_BLOCK, BRANCH_DIM), lambda b, n: (b * nb + n, QA_TILE)),
            pl.BlockSpec((seq, 2 * KV_A), lambda b, n: (b, KV_KA // (2 * KV_A))),
            cache_spec, cache_spec,
        ],
        out_specs=pl.BlockSpec((A_BLOCK, BRANCH_DIM), lambda b, n: (b * nb + n, 0)),
        out_shape=jax.ShapeDtypeStruct((m, BRANCH_DIM), BF16),
        compiler_params=_cparams(("parallel", "arbitrary")),
        name="attn_a_lat",
    )(sink, main, kv, cache_k, cache_v)


def _attn_b_lat_kernel(q_ref, k_ref, v_ref, ck_ref, cv_ref, bias_ref, o_ref, ckb_scr, cvb_scr, *, rows, kr):
    r = pl.program_id(1)

    @pl.when(r == 0)
    def _():
        ckb_scr[...] = ck_ref[...].astype(BF16)
        cvb_scr[...] = cv_ref[...].astype(BF16)

    r0 = jnp.clip(r - kr // 2, 0, rows - kr)
    start = pl.multiple_of(r0 * GRID_W, GRID_W)
    n_loc = kr * GRID_W
    low_q = _low_lanes((GRID_W, LANES))
    for j in range(B_HEADS // 2):
        cols = slice(j * LANES, (j + 1) * LANES)
        qc = q_ref[:, cols]
        zero = jnp.zeros_like(qc)
        qs = jnp.concatenate([jnp.where(low_q, qc, zero), jnp.where(low_q, zero, qc)], axis=0)
        kl = k_ref[pl.ds(start, n_loc), cols]
        vl = v_ref[pl.ds(start, n_loc), cols]
        s_loc = _dot_t(qs, kl) * SCALE + bias_ref[j]
        s_ctx = _dot_t(qs, ckb_scr[:, cols]) * SCALE
        o2 = _softmax_pv([(s_loc, vl), (s_ctx, cvb_scr[:, cols])])
        o = jnp.where(low_q, o2[:GRID_W], o2[GRID_W:])
        o_ref[:, cols] = o.astype(o_ref.dtype)


def _attn_b_lat(main, kv, cache_k, cache_v, bias, l, seq):
    m = main.shape[0]
    rows = seq // GRID_W
    kr = min(B_WIN_ROWS, rows)
    n_seq = m // seq
    past = cache_k.shape[2]
    cache_spec = pl.BlockSpec((None, None, past, BRANCH_DIM), lambda b, r: (b, l, 0, 0))
    half = kr // 2

    def variant(r):
        return jnp.minimum(r, half) + jnp.maximum(r - (rows - kr + half), 0)

    return pl.pallas_call(
        functools.partial(_attn_b_lat_kernel, rows=rows, kr=kr),
        grid=(n_seq, rows),
        in_specs=[
            pl.BlockSpec((GRID_W, BRANCH_DIM), lambda b, r: (b * rows + r, QB_TILE)),
            pl.BlockSpec((seq, BRANCH_DIM), lambda b, r: (b, 0)),
            pl.BlockSpec((seq, BRANCH_DIM), lambda b, r: (b, 1)),
            cache_spec, cache_spec,
            pl.BlockSpec((None, B_HEADS // 2, 2 * GRID_W, kr * GRID_W), lambda b, r: (variant(r), 0, 0, 0)),
        ],
        out_specs=pl.BlockSpec((GRID_W, BRANCH_DIM), lambda b, r: (b * rows + r, 0)),
        out_shape=jax.ShapeDtypeStruct((m, BRANCH_DIM), BF16),
        scratch_shapes=[pltpu.VMEM((past, BRANCH_DIM), BF16)] * 2,
        compiler_params=_cparams(("parallel", "arbitrary")),
        name="attn_b_lat",
    )(main, kv, kv, cache_k, cache_v, bias)


def _neighbourhood_bias(rpb, rows):
    kr = min(B_WIN_ROWS, rows)
    kc = B_WIN_COLS
    n_var = kr
    c = np.arange(GRID_W)
    ws = np.clip(c - kc // 2, 0, GRID_W - kc)
    c2 = np.arange(GRID_W)
    ok = (c2[None, :] >= ws[:, None]) & (c2[None, :] < ws[:, None] + kc)
    dc = np.clip(c2[None, :] - c[:, None] + B_WIN_COLS - 1, 0, 2 * B_WIN_COLS - 2)
    d = -np.arange(n_var)
    dr = d[:, None] + np.arange(kr)[None, :] + B_WIN_ROWS - 1
    g = rpb.astype(F32)[:, dr[:, :, None, None], dc[None, None, :, :]]
    g = jnp.where(ok[None, None, None], g, NEG_INF)
    g = jnp.transpose(g, (1, 0, 3, 2, 4))
    return g.reshape(n_var, B_HEADS // 2, 2 * GRID_W, kr * GRID_W)


def _dft_tables(seq):
    cd = C_GROUP_DIM
    kc = (np.arange(cd)[:, None] * np.arange(cd)[None, :]) % cd
    ang_c = 2.0 * np.pi * kc / cd
    eye2 = np.eye(2)
    bd_cos = np.kron(eye2, np.cos(ang_c))
    bd_sin = np.kron(eye2, np.sin(ang_c))
    kt = (np.arange(seq)[:, None] * np.arange(seq)[None, :]) % seq
    ang_t = 2.0 * np.pi * kt / seq
    norm = 1.0 / np.sqrt(float(seq * cd))
    pos = np.concatenate([np.cos(ang_t), -np.sin(ang_t)], axis=1) * norm
    return (jnp.asarray(bd_cos, F32).astype(BF16), jnp.asarray(bd_sin, F32).astype(BF16),
            jnp.asarray(pos, F32).astype(BF16))


def _fourier_kernel(u_ref, bc_ref, bs_ref, pos_ref, o_ref):
    pair = 2 * C_GROUP_DIM
    uc, us = [], []
    for p in range(BRANCH_DIM // pair):
        up = u_ref[:, p * pair:(p + 1) * pair]
        uc.append(_dot(up, bc_ref[...]))
        us.append(_dot(up, bs_ref[...]))
    z = jnp.concatenate([jnp.concatenate(uc, axis=1), jnp.concatenate(us, axis=1)], axis=0)
    o_ref[...] = _dot(pos_ref[...], z.astype(BF16)).astype(o_ref.dtype)


def _fourier(main, seq):
    m = main.shape[0]
    bc, bs, pos = _dft_tables(seq)
    pair = 2 * C_GROUP_DIM
    return pl.pallas_call(
        _fourier_kernel,
        grid=(m // seq,),
        in_specs=[
            pl.BlockSpec((seq, BRANCH_DIM), lambda b: (b, UC_TILE)),
            _resident((pair, pair), lambda b: (0, 0)),
            _resident((pair, pair), lambda b: (0, 0)),
            _resident((seq, 2 * seq), lambda b: (0, 0)),
        ],
        out_specs=pl.BlockSpec((seq, BRANCH_DIM), lambda b: (b, 0)),
        out_shape=jax.ShapeDtypeStruct((m, BRANCH_DIM), BF16),
        compiler_params=_cparams(("parallel",)),
        name="fourier",
    )(main, bc, bs, pos)


def _merge_kernel(oa_ref, ob_ref, oc_ref, gates_ref, x_ref, mod_ref, gpost_ref, gpre_ref,
                  wb_ref, wo_ref, x1_ref, h2_ref):
    mix = None
    for i, o_ref in enumerate((oa_ref, ob_ref, oc_ref)):
        gate = _sigmoid(gates_ref[:, i * D_MODEL:(i + 1) * D_MODEL].astype(F32))
        term = gate * _dot(o_ref[...], wb_ref[i])
        mix = term if mix is None else mix + term
    y = _dot(mix.astype(BF16), wo_ref[...])
    x1 = x_ref[...] + mod_ref[2:3, :] * _rms(y, gpost_ref[...])
    x1_ref[...] = x1
    h2 = _rms(x1, gpre_ref[...]) * (1.0 + mod_ref[4:5, :]) + mod_ref[3:4, :]
    h2_ref[...] = h2.astype(BF16)


def _merge(oa, ob, oc, main, x, mods, l, g_post, g_pre, w_branch, w_out, *, tm, mod_row):
    m = x.shape[0]
    o_spec = pl.BlockSpec((tm, BRANCH_DIM), lambda i: (i, 0))
    x_spec = pl.BlockSpec((tm, D_MODEL), lambda i: (i, 0))
    return pl.pallas_call(
        _merge_kernel,
        grid=(m // tm,),
        in_specs=[
            o_spec, o_spec, o_spec,
            pl.BlockSpec((tm, N_GATES), lambda i: (i, 0)),
            x_spec,
            pl.BlockSpec((None, None, 6, D_MODEL), lambda i: (l, mod_row(i), 0, 0)),
            pl.BlockSpec((None, None, 1, D_MODEL), lambda i: (l, 0, 0, 0)),
            pl.BlockSpec((None, None, 1, D_MODEL), lambda i: (l, 1, 0, 0)),
            _resident((None, N_BRANCH, BRANCH_DIM, D_MODEL), lambda i: (l, 0, 0, 0)),
            _resident((None, D_MODEL, D_MODEL), lambda i: (l, 0, 0)),
        ],
        out_specs=[x_spec, x_spec],
        out_shape=[jax.ShapeDtypeStruct((m, D_MODEL), F32), jax.ShapeDtypeStruct((m, D_MODEL), BF16)],
        compiler_params=_cparams(("parallel",)),
        name="merge",
    )(oa, ob, oc, main, x, mods, g_post, g_pre, w_branch, w_out)


def _ffn_kernel(h_ref, wg_ref, wu_ref, wo_ref, x_ref, mod_ref, gpost_ref, o_ref, acc_ref):
    k = pl.program_id(1)

    @pl.when(k == 0)
    def _():
        acc_ref[...] = jnp.zeros_like(acc_ref)

    h = h_ref[...]
    gate = _dot(h, wg_ref[...])
    up = _dot(h, wu_ref[...])
    act = gate * _sigmoid(gate) * up
    acc_ref[...] += _dot(act.astype(BF16), wo_ref[...])

    @pl.when(k == pl.num_programs(1) - 1)
    def _():
        o_ref[...] = x_ref[...] + mod_ref[5:6, :] * _rms(acc_ref[...], gpost_ref[...])


def _ffn(h2, x1, mods, l, g_post, w_ffn_in, w_ffn_out, *, tm, tf, mod_row):
    m = x1.shape[0]
    nk = D_FF // tf
    x_spec = pl.BlockSpec((tm, D_MODEL), lambda i, k: (i, 0))
    return pl.pallas_call(
        _ffn_kernel,
        grid=(m // tm, nk),
        in_specs=[
            x_spec,
            pl.BlockSpec((None, D_MODEL, tf), lambda i, k: (l, 0, k)),
            pl.BlockSpec((None, D_MODEL, tf), lambda i, k: (l, 0, k + nk)),
            pl.BlockSpec((None, tf, D_MODEL), lambda i, k: (l, k, 0)),
            x_spec,
            pl.BlockSpec((None, None, 6, D_MODEL), lambda i, k: (l, mod_row(i), 0, 0)),
            pl.BlockSpec((None, None, 1, D_MODEL), lambda i, k: (l, 1, 0, 0)),
        ],
        out_specs=x_spec,
        out_shape=jax.ShapeDtypeStruct((m, D_MODEL), F32),
        scratch_shapes=[pltpu.VMEM((tm, D_MODEL), F32)],
        compiler_params=_cparams(("parallel", "arbitrary")),
        name="ffn",
    )(h2, w_ffn_in, w_ffn_in, w_ffn_out, x1, mods, g_post)


def _rope_tables(seq):
    t = jnp.arange(seq, dtype=jnp.int32)
    row = (t // GRID_W).astype(F32)
    col = (t % GRID_W).astype(F32)
    n_pairs_axis = HEAD_DIM // 4
    inv = ROPE_BASE ** (-jnp.arange(n_pairs_axis, dtype=F32) / n_pairs_axis)
    ang = jnp.concatenate([row[:, None] * inv, col[:, None] * inv], axis=-1)
    cos, sin = jnp.cos(ang), jnp.sin(ang)
    cos_l = jnp.tile(cos, (1, LANES // cos.shape[1]))
    sin_l = jnp.tile(jnp.concatenate([-sin, sin], axis=-1), (1, LANES // HEAD_DIM))
    return cos_l, sin_l


def _permute_w_in(w_in):
    sizes = [BRANCH_DIM, KV_A, KV_A, BRANCH_DIM, BRANCH_DIM, BRANCH_DIM, BRANCH_DIM]
    qa, ka, va, qb, kb, vb, uc, gates = jnp.split(w_in, np.cumsum(sizes).tolist(), axis=-1)
    w_main = jnp.concatenate([gates, qa, qb, uc], axis=-1).astype(BF16)
    w_kv = jnp.concatenate([kb, vb, ka, va], axis=-1).astype(BF16)
    return w_main, w_kv


def kernel(x_prompt, x_sample, cache_a_k, cache_a_v, cache_b_k, cache_b_v, c, c_ctx, w_ada, b_ada,
           norm_pre, norm_post, w_in, a_sink, b_rpb, w_branch, w_out, w_ffn_in, w_ffn_out):
    batch, seq, _ = x_prompt.shape
    dec_batch, dec_seq, _ = x_sample.shape
    past = cache_a_k.shape[2]
    assert dec_batch <= CTX_MOD_ROW and seq % A_BLOCK == 0 and dec_seq % A_BLOCK == 0

    cvec = jnp.concatenate(
        [c, c_ctx[None, :], jnp.zeros((MOD_ROWS - dec_batch - 1, D_MODEL), F32)], axis=0)
    mods = _modulation(cvec, w_ada, b_ada).reshape(DEPTH, MOD_ROWS, 6, D_MODEL)

    w_main, w_kv = _permute_w_in(w_in)
    w_branch_b = w_branch.astype(BF16)
    w_out_b = w_out.astype(BF16)
    w_ffn_in_b = w_ffn_in.astype(BF16)
    w_ffn_out_b = w_ffn_out.astype(BF16)
    g_pre = norm_pre.reshape(DEPTH, 2, 1, D_MODEL)
    g_post = norm_post.reshape(DEPTH, 2, 1, D_MODEL)
    rope_tabs = _rope_tables(dec_seq)
    cak = cache_a_k.reshape(dec_batch, DEPTH, past, KV_A)
    cav = cache_a_v.reshape(dec_batch, DEPTH, past, KV_A)
    cbk = cache_b_k.reshape(dec_batch, DEPTH, past, BRANCH_DIM)
    cbv = cache_b_v.reshape(dec_batch, DEPTH, past, BRANCH_DIM)

    tm_proj, tm_merge, tm_ffn, tf = 1024, 512, 512, D_FF // 2

    def layer_tail(l, x, main, oa, ob, oc, mod_row_fn):
        x1, h2 = _merge(oa, ob, oc, main, x, mods, l, g_post, g_pre, w_branch_b, w_out_b,
                        tm=tm_merge, mod_row=mod_row_fn(tm_merge))
        return _ffn(h2, x1, mods, l, g_post, w_ffn_in_b, w_ffn_out_b,
                    tm=tm_ffn, tf=tf, mod_row=mod_row_fn(tm_ffn))

    ctx_row = lambda tm: (lambda i: CTX_MOD_ROW)
    x = x_prompt.reshape(batch * seq, D_MODEL)
    kvs = []
    for l in range(DEPTH):
        main = _norm_proj(x, mods, l, g_pre, w_main, tm=tm_proj, tn=MAIN_TILE, mod_row=ctx_row(tm_proj),
                          out_dtype=BF16, name="proj_main_ctx")
        kv = _norm_proj(x, mods, l, g_pre, w_kv, tm=tm_proj, tn=N_KV, mod_row=ctx_row(tm_proj),
                        out_dtype=F32, name="proj_kv_ctx")
        kvs.append(kv)
        oa, ob = _attn_ctx(main, kv, a_sink[l], seq)
        oc = _fourier(main, seq)
        x = layer_tail(l, x, main, oa, ob, oc, ctx_row)
    y_prompt = x.reshape(batch, seq, D_MODEL)
    kv_all = jnp.stack(kvs, axis=1).reshape(batch, seq, DEPTH, N_KV).transpose(0, 2, 1, 3)
    new_b_k = kv_all[..., 0:BRANCH_DIM].reshape(batch, DEPTH, seq, B_HEADS, HEAD_DIM)
    new_b_v = kv_all[..., BRANCH_DIM:2 * BRANCH_DIM].reshape(batch, DEPTH, seq, B_HEADS, HEAD_DIM)
    new_a_k = kv_all[..., KV_KA:KV_KA + KV_A].reshape(batch, DEPTH, seq, A_KV_HEADS, HEAD_DIM)
    new_a_v = kv_all[..., KV_VA:KV_VA + KV_A].reshape(batch, DEPTH, seq, A_KV_HEADS, HEAD_DIM)

    lat_row = lambda tm: (lambda i: (i * tm) // dec_seq)
    x = x_sample.reshape(dec_batch * dec_seq, D_MODEL)
    for l in range(DEPTH):
        main = _norm_proj(x, mods, l, g_pre, w_main, tm=tm_proj, tn=MAIN_TILE, mod_row=lat_row(tm_proj),
                          out_dtype=BF16, rope=(QA_TILE, 0, MAIN_TILE), rope_tabs=rope_tabs,
                          name="proj_main_lat")
        kv = _norm_proj(x, mods, l, g_pre, w_kv, tm=tm_proj, tn=N_KV, mod_row=lat_row(tm_proj),
                        out_dtype=BF16, rope=(0, KV_KA, KV_KA + KV_A), rope_tabs=rope_tabs,
                        name="proj_kv_lat")
        oa = _attn_a_lat(main, kv, cak, cav, a_sink[l], l, dec_seq)
        bias = _neighbourhood_bias(b_rpb[l], dec_seq // GRID_W)
        ob = _attn_b_lat(main, kv, cbk, cbv, bias, l, dec_seq)
        oc = _fourier(main, dec_seq)
        x = layer_tail(l, x, main, oa, ob, oc, lat_row)
    y_sample = x.reshape(dec_batch, dec_seq, D_MODEL)
    return (y_prompt, y_sample, new_a_k, new_a_v, new_b_k, new_b_v)
```

```python
import functools

import numpy as np
import jax
import jax.numpy as jnp
from jax import lax
from jax.experimental import pallas as pl
from jax.experimental.pallas import tpu as pltpu

F32 = jnp.float32
BF16 = jnp.bfloat16

D_MODEL = 1024
DEPTH = 2
GRID_W = 64
HEAD_DIM = 64
BRANCH_DIM = D_MODEL // 2
A_Q_HEADS = BRANCH_DIM // HEAD_DIM
A_KV_HEADS = A_Q_HEADS // 4
A_GROUP = A_Q_HEADS // A_KV_HEADS
A_BLOCK = 128
B_HEADS = BRANCH_DIM // HEAD_DIM
B_WIN_ROWS = 8
B_WIN_COLS = 16
C_GROUPS = 4
C_GROUP_DIM = BRANCH_DIM // C_GROUPS
N_BRANCH = 3
D_FF = -(-8 * D_MODEL // (3 * 256)) * 256
ROPE_BASE = 10000.0
NORM_EPS = 1e-6
NEG_INF = -1e30
SCALE = HEAD_DIM ** -0.5

LANES = 128
KV_A = A_KV_HEADS * HEAD_DIM
N_GATES = N_BRANCH * D_MODEL
N_MAIN = N_GATES + 3 * BRANCH_DIM
N_KV = 2 * BRANCH_DIM + 2 * KV_A
MAIN_TILE = BRANCH_DIM
QA_TILE = N_GATES // MAIN_TILE
QB_TILE = QA_TILE + 1
UC_TILE = QA_TILE + 2
KV_KA = 2 * BRANCH_DIM
KV_VA = KV_KA + KV_A
MOD_ROWS = 8
CTX_MOD_ROW = 4
VMEM_LIMIT = 56 * 1024 * 1024


def _cparams(sem):
    return pltpu.CompilerParams(dimension_semantics=sem, vmem_limit_bytes=VMEM_LIMIT)


def _resident(shape, index_map):
    return pl.BlockSpec(shape, index_map, pipeline_mode=pl.Buffered(1))


def _rms(x, g):
    return x * lax.rsqrt(jnp.mean(x * x, axis=-1, keepdims=True) + NORM_EPS) * g


def _sigmoid(x):
    return 1.0 / (1.0 + jnp.exp(-x))


def _dot_t(a, b):
    return lax.dot_general(a, b, (((1,), (1,)), ((), ())), preferred_element_type=F32)


def _dot(a, b):
    return jnp.dot(a, b, preferred_element_type=F32)


def _low_lanes(shape):
    return lax.broadcasted_iota(jnp.int32, shape, len(shape) - 1) < HEAD_DIM


def _mod_kernel(cv_ref, w_ref, b_ref, o_ref):
    cv = cv_ref[...]
    s = cv * _sigmoid(cv)
    o_ref[...] = jnp.dot(s, w_ref[...], preferred_element_type=F32,
                         precision=lax.Precision.HIGHEST) + b_ref[...]


def _modulation(cvec, w_ada, b_ada):
    tn = 1536
    n = 6 * D_MODEL
    return pl.pallas_call(
        _mod_kernel,
        grid=(DEPTH, n // tn),
        in_specs=[
            pl.BlockSpec((MOD_ROWS, D_MODEL), lambda l, j: (0, 0)),
            pl.BlockSpec((None, D_MODEL, tn), lambda l, j: (l, 0, j)),
            pl.BlockSpec((None, 1, tn), lambda l, j: (l, 0, j)),
        ],
        out_specs=pl.BlockSpec((None, MOD_ROWS, tn), lambda l, j: (l, 0, j)),
        out_shape=jax.ShapeDtypeStruct((DEPTH, MOD_ROWS, n), F32),
        compiler_params=_cparams(("parallel", "parallel")),
        name="modulation",
    )(cvec, w_ada, b_ada.reshape(DEPTH, 1, n))


def _rope_cols(v, cos, sin):
    first = (lax.broadcasted_iota(jnp.int32, (v.shape[0], LANES), 1) % HEAD_DIM) < HEAD_DIM // 2
    outs = []
    for c in range(v.shape[1] // LANES):
        vc = v[:, c * LANES:(c + 1) * LANES]
        partner = jnp.where(first, pltpu.roll(vc, LANES - HEAD_DIM // 2, 1),
                            pltpu.roll(vc, HEAD_DIM // 2, 1))
        outs.append(vc * cos + partner * sin)
    return outs[0] if len(outs) == 1 else jnp.concatenate(outs, axis=1)


def _norm_proj_kernel(*refs, rope):
    if rope is None:
        x_ref, mod_ref, g_ref, w_ref, o_ref, h_scr = refs
    else:
        x_ref, mod_ref, g_ref, w_ref, cos_ref, sin_ref, o_ref, h_scr = refs
    j = pl.program_id(1)

    @pl.when(j == 0)
    def _():
        h = _rms(x_ref[...], g_ref[...]) * (1.0 + mod_ref[1:2, :]) + mod_ref[0:1, :]
        h_scr[...] = h.astype(BF16)

    def plain():
        o_ref[...] = _dot(h_scr[...], w_ref[...]).astype(o_ref.dtype)

    if rope is None:
        plain()
        return
    tile, lo, hi = rope

    @pl.when(j != tile)
    def _():
        plain()

    @pl.when(j == tile)
    def _():
        acc = _dot(h_scr[...], w_ref[...])
        tn = acc.shape[1]
        parts = []
        if lo > 0:
            parts.append(acc[:, :lo])
        parts.append(_rope_cols(acc[:, lo:hi], cos_ref[...], sin_ref[...]))
        if hi < tn:
            parts.append(acc[:, hi:])
        out = parts[0] if len(parts) == 1 else jnp.concatenate(parts, axis=1)
        o_ref[...] = out.astype(o_ref.dtype)


def _norm_proj(x, mods, l, g_pre, w, *, tm, tn, mod_row, out_dtype, rope=None, rope_tabs=None, name):
    m = x.shape[0]
    n = w.shape[2]
    in_specs = [
        pl.BlockSpec((tm, D_MODEL), lambda i, j: (i, 0)),
        pl.BlockSpec((None, None, 6, D_MODEL), lambda i, j: (l, mod_row(i), 0, 0)),
        pl.BlockSpec((None, None, 1, D_MODEL), lambda i, j: (l, 0, 0, 0)),
        pl.BlockSpec((None, D_MODEL, tn), lambda i, j: (l, 0, j)),
    ]
    args = [x, mods, g_pre, w]
    if rope is not None:
        seq_tiles = rope_tabs[0].shape[0] // tm
        tab_spec = pl.BlockSpec((tm, LANES), lambda i, j: (i % seq_tiles, 0))
        in_specs += [tab_spec, tab_spec]
        args += list(rope_tabs)
    return pl.pallas_call(
        functools.partial(_norm_proj_kernel, rope=rope),
        grid=(m // tm, n // tn),
        in_specs=in_specs,
        out_specs=pl.BlockSpec((tm, tn), lambda i, j: (i, j)),
        out_shape=jax.ShapeDtypeStruct((m, n), out_dtype),
        scratch_shapes=[pltpu.VMEM((tm, D_MODEL), BF16)],
        compiler_params=_cparams(("parallel", "arbitrary")),
        name=name,
    )(*args)


def _softmax_pv(parts, sink=None):
    m = parts[0][0].max(axis=-1, keepdims=True)
    for s, _ in parts[1:]:
        m = jnp.maximum(m, s.max(axis=-1, keepdims=True))
    if sink is not None:
        m = jnp.maximum(m, sink)
    den = None
    acc = None
    for s, v in parts:
        e = jnp.exp(s - m)
        d = e.sum(axis=-1, keepdims=True)
        o = _dot(e.astype(BF16), v)
        den = d if den is None else den + d
        acc = o if acc is None else acc + o
    if sink is not None:
        den = den + jnp.exp(sink - m)
    return acc * (1.0 / den)


def _kv_head_variants(x2):
    low = _low_lanes(x2.shape)
    xr = pltpu.roll(x2, HEAD_DIM, 1)
    zero = jnp.zeros_like(x2)
    return [
        [jnp.where(low, x2, zero).astype(BF16), jnp.where(low, zero, xr).astype(BF16)],
        [jnp.where(low, xr, zero).astype(BF16), jnp.where(low, zero, x2).astype(BF16)],
    ]


def _attn_ctx_kernel(sink_ref, qa_ref, qb_ref, kv_ref, oa_ref, ob_ref):
    ka = _kv_head_variants(kv_ref[:, KV_KA:KV_KA + KV_A])
    va = _kv_head_variants(kv_ref[:, KV_VA:KV_VA + KV_A])
    for j in range(A_Q_HEADS // 2):
        g = (2 * j) // A_GROUP
        qc = qa_ref[:, j * LANES:(j + 1) * LANES]
        o = None
        for p in range(2):
            s = _dot_t(qc, ka[g][p]) * SCALE
            oh = _softmax_pv([(s, va[g][p])], sink=sink_ref[2 * j + p])
            o = oh if o is None else o + oh
        oa_ref[:, j * LANES:(j + 1) * LANES] = o.astype(oa_ref.dtype)
    for j in range(B_HEADS // 2):
        qc = qb_ref[:, j * LANES:(j + 1) * LANES]
        kc = kv_ref[:, j * LANES:(j + 1) * LANES]
        vc = kv_ref[:, BRANCH_DIM + j * LANES:BRANCH_DIM + (j + 1) * LANES]
        low = _low_lanes(kc.shape)
        zero = jnp.zeros_like(kc)
        o = None
        for p in range(2):
            keep = low if p == 0 else jnp.logical_not(low)
            kh = jnp.where(keep, kc, zero).astype(BF16)
            vh = jnp.where(keep, vc, zero).astype(BF16)
            s = _dot_t(qc, kh) * SCALE
            oh = _softmax_pv([(s, vh)])
            o = oh if o is None else o + oh
        ob_ref[:, j * LANES:(j + 1) * LANES] = o.astype(ob_ref.dtype)


def _attn_ctx(main, kv, sink, seq):
    m = main.shape[0]
    out = jax.ShapeDtypeStruct((m, BRANCH_DIM), BF16)
    return pl.pallas_call(
        _attn_ctx_kernel,
        grid=(m // seq,),
        in_specs=[
            pl.BlockSpec(memory_space=pltpu.SMEM),
            pl.BlockSpec((seq, BRANCH_DIM), lambda b: (b, QA_TILE)),
            pl.BlockSpec((seq, BRANCH_DIM), lambda b: (b, QB_TILE)),
            pl.BlockSpec((seq, N_KV), lambda b: (b, 0)),
        ],
        out_specs=[pl.BlockSpec((seq, BRANCH_DIM), lambda b: (b, 0))] * 2,
        out_shape=[out, out],
        compiler_params=_cparams(("parallel",)),
        name="attn_ctx",
    )(sink, main, main, kv)


def _attn_a_lat_kernel(sink_ref, q_ref, kv_ref, ck_ref, cv_ref, o_ref, *, n_blocks):
    n = pl.program_id(1)
    prev = pl.multiple_of(jnp.maximum(n - 1, 0) * A_BLOCK, A_BLOCK)
    cur = pl.multiple_of(n * A_BLOCK, A_BLOCK)
    nxt = pl.multiple_of(jnp.minimum(n + 1, n_blocks - 1) * A_BLOCK, A_BLOCK)

    def band(col):
        return jnp.concatenate(
            [kv_ref[pl.ds(s, A_BLOCK), col:col + KV_A] for s in (prev, cur, nxt)], axis=0).astype(F32)

    kb = _kv_head_variants(band(0))
    vb = _kv_head_variants(band(KV_A))
    kc = _kv_head_variants(ck_ref[...])
    vc = _kv_head_variants(cv_ref[...])
    qi = lax.broadcasted_iota(jnp.int32, (A_BLOCK, 3 * A_BLOCK), 0)
    kj = lax.broadcasted_iota(jnp.int32, (A_BLOCK, 3 * A_BLOCK), 1)
    no_prev = jnp.where(n > 0, 0, 1)
    no_next = jnp.where(n < n_blocks - 1, 0, 1)
    valid = (kj >= qi + no_prev * (A_BLOCK - qi)) & (kj <= 2 * A_BLOCK + qi - no_next * (qi + 1))
    for j in range(A_Q_HEADS // 2):
        g = (2 * j) // A_GROUP
        qc = q_ref[:, j * LANES:(j + 1) * LANES]
        o = None
        for p in range(2):
            s_band = jnp.where(valid, _dot_t(qc, kb[g][p]) * SCALE, NEG_INF)
            s_ctx = _dot_t(qc, kc[g][p]) * SCALE
            oh = _softmax_pv([(s_band, vb[g][p]), (s_ctx, vc[g][p])], sink=sink_ref[2 * j + p])
            o = oh if o is None else o + oh
        o_ref[:, j * LANES:(j + 1) * LANES] = o.astype(o_ref.dtype)


def _attn_a_lat(main, kv, cache_k, cache_v, sink, l, seq):
    m = main.shape[0]
    nb = seq // A_BLOCK
    n_seq = m // seq
    cache_spec = pl.BlockSpec((None, None, cache_k.shape[2], KV_A), lambda b, n: (b, l, 0, 0))
    return pl.pallas_call(
        functools.partial(_attn_a_lat_kernel, n_blocks=nb),
        grid=(n_seq, nb),
        in_specs=[
            pl.BlockSpec(memory_space=pltpu.SMEM),
            pl.BlockSpec((A_BLOCK, BRANCH_DIM), lambda b, n: (b * nb + n, QA_TILE)),
            pl.BlockSpec((seq, 2 * KV_A), lambda b, n: (b, KV_KA // (2 * KV_A))),
            cache_spec, cache_spec,
        ],
        out_specs=pl.BlockSpec((A_BLOCK, BRANCH_DIM), lambda b, n: (b * nb + n, 0)),
        out_shape=jax.ShapeDtypeStruct((m, BRANCH_DIM), BF16),
        compiler_params=_cparams(("parallel", "arbitrary")),
        name="attn_a_lat",
    )(sink, main, kv, cache_k, cache_v)


def _attn_b_lat_kernel(q_ref, k_ref, v_ref, ck_ref, cv_ref, bias_ref, o_ref, ckb_scr, cvb_scr, *, rows, kr):
    r = pl.program_id(1)

    @pl.when(r == 0)
    def _():
        ckb_scr[...] = ck_ref[...].astype(BF16)
        cvb_scr[...] = cv_ref[...].astype(BF16)

    r0 = jnp.clip(r - kr // 2, 0, rows - kr)
    start = pl.multiple_of(r0 * GRID_W, GRID_W)
    n_loc = kr * GRID_W
    low_q = _low_lanes((GRID_W, LANES))
    for j in range(B_HEADS // 2):
        cols = slice(j * LANES, (j + 1) * LANES)
        qc = q_ref[:, cols]
        zero = jnp.zeros_like(qc)
        qs = jnp.concatenate([jnp.where(low_q, qc, zero), jnp.where(low_q, zero, qc)], axis=0)
        kl = k_ref[pl.ds(start, n_loc), cols]
        vl = v_ref[pl.ds(start, n_loc), cols]
        s_loc = _dot_t(qs, kl) * SCALE + bias_ref[j]
        s_ctx = _dot_t(qs, ckb_scr[:, cols]) * SCALE
        o2 = _softmax_pv([(s_loc, vl), (s_ctx, cvb_scr[:, cols])])
        o = jnp.where(low_q, o2[:GRID_W], o2[GRID_W:])
        o_ref[:, cols] = o.astype(o_ref.dtype)


def _attn_b_lat(main, kv, cache_k, cache_v, bias, l, seq):
    m = main.shape[0]
    rows = seq // GRID_W
    kr = min(B_WIN_ROWS, rows)
    n_seq = m // seq
    past = cache_k.shape[2]
    cache_spec = pl.BlockSpec((None, None, past, BRANCH_DIM), lambda b, r: (b, l, 0, 0))
    half = kr // 2

    def variant(r):
        return jnp.minimum(r, half) + jnp.maximum(r - (rows - kr + half), 0)

    return pl.pallas_call(
        functools.partial(_attn_b_lat_kernel, rows=rows, kr=kr),
        grid=(n_seq, rows),
        in_specs=[
            pl.BlockSpec((GRID_W, BRANCH_DIM), lambda b, r: (b * rows + r, QB_TILE)),
            pl.BlockSpec((seq, BRANCH_DIM), lambda b, r: (b, 0)),
            pl.BlockSpec((seq, BRANCH_DIM), lambda b, r: (b, 1)),
            cache_spec, cache_spec,
            pl.BlockSpec((None, None, B_HEADS // 2, 2 * GRID_W, kr * GRID_W),
                         lambda b, r: (l, variant(r), 0, 0, 0)),
        ],
        out_specs=pl.BlockSpec((GRID_W, BRANCH_DIM), lambda b, r: (b * rows + r, 0)),
        out_shape=jax.ShapeDtypeStruct((m, BRANCH_DIM), BF16),
        scratch_shapes=[pltpu.VMEM((past, BRANCH_DIM), BF16)] * 2,
        compiler_params=_cparams(("parallel", "arbitrary")),
        name="attn_b_lat",
    )(main, kv, kv, cache_k, cache_v, bias)


def _neighbourhood_bias(rpb, rows):
    kr = min(B_WIN_ROWS, rows)
    assert kr % 2 == 0 and 2 * GRID_W == LANES
    depth, heads, n_dr, n_dc = rpb.shape
    rpb_pad = jnp.pad(rpb.astype(F32), ((0, 0), (0, 0), (0, (-n_dr) % 8), (0, LANES - n_dc)))
    return pl.pallas_call(
        functools.partial(_bias_kernel, kr=kr),
        grid=(depth, heads),
        in_specs=[pl.BlockSpec((None, None, rpb_pad.shape[2], LANES), lambda l, h: (l, h, 0, 0))],
        out_specs=pl.BlockSpec((None, kr, None, GRID_W, kr * GRID_W), lambda l, h: (l, 0, h // 2, h % 2, 0)),
        out_shape=jax.ShapeDtypeStruct((depth, kr, heads // 2, 2 * GRID_W, kr * GRID_W), F32),
        compiler_params=_cparams(("parallel", "parallel")),
        name="nbr_bias",
    )(rpb_pad)


def _bias_kernel(rpb_ref, o_ref, *, kr):
    shape = (GRID_W, LANES)
    c = lax.broadcasted_iota(jnp.int32, shape, 0)
    lane = lax.broadcasted_iota(jnp.int32, shape, 1)
    c2 = lane % GRID_W
    ws = jnp.clip(c - B_WIN_COLS // 2, 0, GRID_W - B_WIN_COLS)
    ok = (c2 >= ws) & (c2 < ws + B_WIN_COLS)
    low = lane < GRID_W

    def toeplitz(dr, lane0):
        row = jnp.broadcast_to(rpb_ref[dr:dr + 1, :], shape)
        return pltpu.roll(row, (lane0 - (B_WIN_COLS - 1)) % LANES, 1, stride=1, stride_axis=0)

    pairs = {}
    for v in range(kr):
        for m in range(0, kr, 2):
            dr = m - v + B_WIN_ROWS - 1
            if dr not in pairs:
                pair = jnp.where(low, toeplitz(dr, 0), toeplitz(dr + 1, GRID_W))
                pairs[dr] = jnp.where(ok, pair, NEG_INF)
            o_ref[v, :, m * GRID_W:(m + 2) * GRID_W] = pairs[dr]


def _dft_tables(seq):
    cd = C_GROUP_DIM
    kc = (np.arange(cd)[:, None] * np.arange(cd)[None, :]) % cd
    ang_c = 2.0 * np.pi * kc / cd
    eye2 = np.eye(2)
    bd_cos = np.kron(eye2, np.cos(ang_c))
    bd_sin = np.kron(eye2, np.sin(ang_c))
    kt = (np.arange(seq)[:, None] * np.arange(seq)[None, :]) % seq
    ang_t = 2.0 * np.pi * kt / seq
    norm = 1.0 / np.sqrt(float(seq * cd))
    pos = np.concatenate([np.cos(ang_t), -np.sin(ang_t)], axis=1) * norm
    return (jnp.asarray(bd_cos, F32).astype(BF16), jnp.asarray(bd_sin, F32).astype(BF16),
            jnp.asarray(pos, F32).astype(BF16))


def _fourier_kernel(u_ref, bc_ref, bs_ref, pos_ref, o_ref):
    pair = 2 * C_GROUP_DIM
    uc, us = [], []
    for p in range(BRANCH_DIM // pair):
        up = u_ref[:, p * pair:(p + 1) * pair]
        uc.append(_dot(up, bc_ref[...]))
        us.append(_dot(up, bs_ref[...]))
    z = jnp.concatenate([jnp.concatenate(uc, axis=1), jnp.concatenate(us, axis=1)], axis=0)
    o_ref[...] = _dot(pos_ref[...], z.astype(BF16)).astype(o_ref.dtype)


def _fourier(main, seq):
    m = main.shape[0]
    bc, bs, pos = _dft_tables(seq)
    pair = 2 * C_GROUP_DIM
    return pl.pallas_call(
        _fourier_kernel,
        grid=(m // seq,),
        in_specs=[
            pl.BlockSpec((seq, BRANCH_DIM), lambda b: (b, UC_TILE)),
            _resident((pair, pair), lambda b: (0, 0)),
            _resident((pair, pair), lambda b: (0, 0)),
            _resident((seq, 2 * seq), lambda b: (0, 0)),
        ],
        out_specs=pl.BlockSpec((seq, BRANCH_DIM), lambda b: (b, 0)),
        out_shape=jax.ShapeDtypeStruct((m, BRANCH_DIM), BF16),
        compiler_params=_cparams(("parallel",)),
        name="fourier",
    )(main, bc, bs, pos)


def _merge_kernel(oa_ref, ob_ref, oc_ref, gates_ref, x_ref, mod_ref, gpost_ref, gpre_ref,
                  wb_ref, wo_ref, x1_ref, h2_ref):
    mix = None
    for i, o_ref in enumerate((oa_ref, ob_ref, oc_ref)):
        gate = _sigmoid(gates_ref[:, i * D_MODEL:(i + 1) * D_MODEL].astype(F32))
        term = gate * _dot(o_ref[...], wb_ref[i])
        mix = term if mix is None else mix + term
    y = _dot(mix.astype(BF16), wo_ref[...])
    x1 = x_ref[...] + mod_ref[2:3, :] * _rms(y, gpost_ref[...])
    x1_ref[...] = x1
    h2 = _rms(x1, gpre_ref[...]) * (1.0 + mod_ref[4:5, :]) + mod_ref[3:4, :]
    h2_ref[...] = h2.astype(BF16)


def _merge(oa, ob, oc, main, x, mods, l, g_post, g_pre, w_branch, w_out, *, tm, mod_row):
    m = x.shape[0]
    o_spec = pl.BlockSpec((tm, BRANCH_DIM), lambda i: (i, 0))
    x_spec = pl.BlockSpec((tm, D_MODEL), lambda i: (i, 0))
    return pl.pallas_call(
        _merge_kernel,
        grid=(m // tm,),
        in_specs=[
            o_spec, o_spec, o_spec,
            pl.BlockSpec((tm, N_GATES), lambda i: (i, 0)),
            x_spec,
            pl.BlockSpec((None, None, 6, D_MODEL), lambda i: (l, mod_row(i), 0, 0)),
            pl.BlockSpec((None, None, 1, D_MODEL), lambda i: (l, 0, 0, 0)),
            pl.BlockSpec((None, None, 1, D_MODEL), lambda i: (l, 1, 0, 0)),
            _resident((None, N_BRANCH, BRANCH_DIM, D_MODEL), lambda i: (l, 0, 0, 0)),
            _resident((None, D_MODEL, D_MODEL), lambda i: (l, 0, 0)),
        ],
        out_specs=[x_spec, x_spec],
        out_shape=[jax.ShapeDtypeStruct((m, D_MODEL), F32), jax.ShapeDtypeStruct((m, D_MODEL), BF16)],
        compiler_params=_cparams(("parallel",)),
        name="merge",
    )(oa, ob, oc, main, x, mods, g_post, g_pre, w_branch, w_out)


def _ffn_kernel(h_ref, wg_ref, wu_ref, wo_ref, x_ref, mod_ref, gpost_ref, o_ref, acc_ref):
    k = pl.program_id(1)

    @pl.when(k == 0)
    def _():
        acc_ref[...] = jnp.zeros_like(acc_ref)

    h = h_ref[...]
    gate = _dot(h, wg_ref[...])
    up = _dot(h, wu_ref[...])
    act = gate * _sigmoid(gate) * up
    acc_ref[...] += _dot(act.astype(BF16), wo_ref[...])

    @pl.when(k == pl.num_programs(1) - 1)
    def _():
        o_ref[...] = x_ref[...] + mod_ref[5:6, :] * _rms(acc_ref[...], gpost_ref[...])


def _ffn(h2, x1, mods, l, g_post, w_ffn_in, w_ffn_out, *, tm, tf, mod_row):
    m = x1.shape[0]
    nk = D_FF // tf
    x_spec = pl.BlockSpec((tm, D_MODEL), lambda i, k: (i, 0))
    return pl.pallas_call(
        _ffn_kernel,
        grid=(m // tm, nk),
        in_specs=[
            x_spec,
            pl.BlockSpec((None, D_MODEL, tf), lambda i, k: (l, 0, k)),
            pl.BlockSpec((None, D_MODEL, tf), lambda i, k: (l, 0, k + nk)),
            pl.BlockSpec((None, tf, D_MODEL), lambda i, k: (l, k, 0)),
            x_spec,
            pl.BlockSpec((None, None, 6, D_MODEL), lambda i, k: (l, mod_row(i), 0, 0)),
            pl.BlockSpec((None, None, 1, D_MODEL), lambda i, k: (l, 1, 0, 0)),
        ],
        out_specs=x_spec,
        out_shape=jax.ShapeDtypeStruct((m, D_MODEL), F32),
        scratch_shapes=[pltpu.VMEM((tm, D_MODEL), F32)],
        compiler_params=_cparams(("parallel", "arbitrary")),
        name="ffn",
    )(h2, w_ffn_in, w_ffn_in, w_ffn_out, x1, mods, g_post)


def _rope_tables(seq):
    t = jnp.arange(seq, dtype=jnp.int32)
    row = (t // GRID_W).astype(F32)
    col = (t % GRID_W).astype(F32)
    n_pairs_axis = HEAD_DIM // 4
    inv = ROPE_BASE ** (-jnp.arange(n_pairs_axis, dtype=F32) / n_pairs_axis)
    ang = jnp.concatenate([row[:, None] * inv, col[:, None] * inv], axis=-1)
    cos, sin = jnp.cos(ang), jnp.sin(ang)
    cos_l = jnp.tile(cos, (1, LANES // cos.shape[1]))
    sin_l = jnp.tile(jnp.concatenate([-sin, sin], axis=-1), (1, LANES // HEAD_DIM))
    return cos_l, sin_l


def _permute_w_in(w_in):
    sizes = [BRANCH_DIM, KV_A, KV_A, BRANCH_DIM, BRANCH_DIM, BRANCH_DIM, BRANCH_DIM]
    qa, ka, va, qb, kb, vb, uc, gates = jnp.split(w_in, np.cumsum(sizes).tolist(), axis=-1)
    w_main = jnp.concatenate([gates, qa, qb, uc], axis=-1).astype(BF16)
    w_kv = jnp.concatenate([kb, vb, ka, va], axis=-1).astype(BF16)
    return w_main, w_kv


def kernel(x_prompt, x_sample, cache_a_k, cache_a_v, cache_b_k, cache_b_v, c, c_ctx, w_ada, b_ada,
           norm_pre, norm_post, w_in, a_sink, b_rpb, w_branch, w_out, w_ffn_in, w_ffn_out):
    batch, seq, _ = x_prompt.shape
    dec_batch, dec_seq, _ = x_sample.shape
    past = cache_a_k.shape[2]
    assert dec_batch <= CTX_MOD_ROW and seq % A_BLOCK == 0 and dec_seq % A_BLOCK == 0

    cvec = jnp.concatenate(
        [c, c_ctx[None, :], jnp.zeros((MOD_ROWS - dec_batch - 1, D_MODEL), F32)], axis=0)
    mods = _modulation(cvec, w_ada, b_ada).reshape(DEPTH, MOD_ROWS, 6, D_MODEL)

    w_main, w_kv = _permute_w_in(w_in)
    w_branch_b = w_branch.astype(BF16)
    w_out_b = w_out.astype(BF16)
    w_ffn_in_b = w_ffn_in.astype(BF16)
    w_ffn_out_b = w_ffn_out.astype(BF16)
    g_pre = norm_pre.reshape(DEPTH, 2, 1, D_MODEL)
    g_post = norm_post.reshape(DEPTH, 2, 1, D_MODEL)
    rope_tabs = _rope_tables(dec_seq)
    cak = cache_a_k.reshape(dec_batch, DEPTH, past, KV_A)
    cav = cache_a_v.reshape(dec_batch, DEPTH, past, KV_A)
    cbk = cache_b_k.reshape(dec_batch, DEPTH, past, BRANCH_DIM)
    cbv = cache_b_v.reshape(dec_batch, DEPTH, past, BRANCH_DIM)

    tm_proj, tm_merge, tm_ffn, tf = 1024, 512, 512, D_FF // 2

    def layer_tail(l, x, main, oa, ob, oc, mod_row_fn):
        x1, h2 = _merge(oa, ob, oc, main, x, mods, l, g_post, g_pre, w_branch_b, w_out_b,
                        tm=tm_merge, mod_row=mod_row_fn(tm_merge))
        return _ffn(h2, x1, mods, l, g_post, w_ffn_in_b, w_ffn_out_b,
                    tm=tm_ffn, tf=tf, mod_row=mod_row_fn(tm_ffn))

    ctx_row = lambda tm: (lambda i: CTX_MOD_ROW)
    x = x_prompt.reshape(batch * seq, D_MODEL)
    kvs = []
    for l in range(DEPTH):
        main = _norm_proj(x, mods, l, g_pre, w_main, tm=tm_proj, tn=MAIN_TILE, mod_row=ctx_row(tm_proj),
                          out_dtype=BF16, name="proj_main_ctx")
        kv = _norm_proj(x, mods, l, g_pre, w_kv, tm=tm_proj, tn=N_KV, mod_row=ctx_row(tm_proj),
                        out_dtype=F32, name="proj_kv_ctx")
        kvs.append(kv)
        oa, ob = _attn_ctx(main, kv, a_sink[l], seq)
        oc = _fourier(main, seq)
        x = layer_tail(l, x, main, oa, ob, oc, ctx_row)
    y_prompt = x.reshape(batch, seq, D_MODEL)
    kv_all = jnp.stack(kvs, axis=1).reshape(batch, seq, DEPTH, N_KV).transpose(0, 2, 1, 3)
    new_b_k = kv_all[..., 0:BRANCH_DIM].reshape(batch, DEPTH, seq, B_HEADS, HEAD_DIM)
    new_b_v = kv_all[..., BRANCH_DIM:2 * BRANCH_DIM].reshape(batch, DEPTH, seq, B_HEADS, HEAD_DIM)
    new_a_k = kv_all[..., KV_KA:KV_KA + KV_A].reshape(batch, DEPTH, seq, A_KV_HEADS, HEAD_DIM)
    new_a_v = kv_all[..., KV_VA:KV_VA + KV_A].reshape(batch, DEPTH, seq, A_KV_HEADS, HEAD_DIM)

    lat_row = lambda tm: (lambda i: (i * tm) // dec_seq)
    x = x_sample.reshape(dec_batch * dec_seq, D_MODEL)
    bias = _neighbourhood_bias(b_rpb, dec_seq // GRID_W)
    for l in range(DEPTH):
        main = _norm_proj(x, mods, l, g_pre, w_main, tm=tm_proj, tn=MAIN_TILE, mod_row=lat_row(tm_proj),
                          out_dtype=BF16, rope=(QA_TILE, 0, MAIN_TILE), rope_tabs=rope_tabs,
                          name="proj_main_lat")
        kv = _norm_proj(x, mods, l, g_pre, w_kv, tm=tm_proj, tn=N_KV, mod_row=lat_row(tm_proj),
                        out_dtype=BF16, rope=(0, KV_KA, KV_KA + KV_A), rope_tabs=rope_tabs,
                        name="proj_kv_lat")
        oa = _attn_a_lat(main, kv, cak, cav, a_sink[l], l, dec_seq)
        ob = _attn_b_lat(main, kv, cbk, cbv, bias, l, dec_seq)
        oc = _fourier(main, dec_seq)
        x = layer_tail(l, x, main, oa, ob, oc, lat_row)
    y_sample = x.reshape(dec_batch, dec_seq, D_MODEL)
    return (y_prompt, y_sample, new_a_k, new_a_v, new_b_k, new_b_v)
```

```python
import functools

import numpy as np
import jax
import jax.numpy as jnp
from jax import lax
from jax.experimental import pallas as pl
from jax.experimental.pallas import tpu as pltpu

F32 = jnp.float32
BF16 = jnp.bfloat16

D_MODEL = 1024
DEPTH = 2
GRID_W = 64
HEAD_DIM = 64
BRANCH_DIM = D_MODEL // 2
A_Q_HEADS = BRANCH_DIM // HEAD_DIM
A_KV_HEADS = A_Q_HEADS // 4
A_GROUP = A_Q_HEADS // A_KV_HEADS
A_BLOCK = 128
B_HEADS = BRANCH_DIM // HEAD_DIM
B_WIN_ROWS = 8
B_WIN_COLS = 16
C_GROUPS = 4
C_GROUP_DIM = BRANCH_DIM // C_GROUPS
N_BRANCH = 3
D_FF = -(-8 * D_MODEL // (3 * 256)) * 256
ROPE_BASE = 10000.0
NORM_EPS = 1e-6
NEG_INF = -1e30
SCALE = HEAD_DIM ** -0.5

LANES = 128
KV_A = A_KV_HEADS * HEAD_DIM
N_GATES = N_BRANCH * D_MODEL
N_MAIN = N_GATES + 3 * BRANCH_DIM
N_KV = 2 * BRANCH_DIM + 2 * KV_A
MAIN_TILE = BRANCH_DIM
QA_TILE = N_GATES // MAIN_TILE
QB_TILE = QA_TILE + 1
UC_TILE = QA_TILE + 2
KV_KA = 2 * BRANCH_DIM
KV_VA = KV_KA + KV_A
MOD_ROWS = 8
CTX_MOD_ROW = 4
VMEM_LIMIT = 56 * 1024 * 1024


def _cparams(sem):
    return pltpu.CompilerParams(dimension_semantics=sem, vmem_limit_bytes=VMEM_LIMIT)


def _resident(shape, index_map):
    return pl.BlockSpec(shape, index_map, pipeline_mode=pl.Buffered(1))


def _rms(x, g):
    return x * lax.rsqrt(jnp.mean(x * x, axis=-1, keepdims=True) + NORM_EPS) * g


def _sigmoid(x):
    return 0.5 * jnp.tanh(0.5 * x) + 0.5


def _silu(x):
    half = 0.5 * x
    return half + half * jnp.tanh(half)


def _dot_t(a, b):
    return lax.dot_general(a, b, (((1,), (1,)), ((), ())), preferred_element_type=F32)


def _dot(a, b):
    return jnp.dot(a, b, preferred_element_type=F32)


def _scaled(q):
    assert np.log2(SCALE) == round(np.log2(SCALE))
    return q * jnp.asarray(SCALE, q.dtype)


def _low_lanes(shape):
    return lax.broadcasted_iota(jnp.int32, shape, len(shape) - 1) < HEAD_DIM


def _mod_kernel(cv_ref, w_ref, b_ref, o_ref):
    o_ref[...] = jnp.dot(_silu(cv_ref[...]), w_ref[...], preferred_element_type=F32,
                         precision=lax.Precision.HIGHEST) + b_ref[...]


def _modulation(cvec, w_ada, b_ada):
    tn = 1536
    n = 6 * D_MODEL
    return pl.pallas_call(
        _mod_kernel,
        grid=(DEPTH, n // tn),
        in_specs=[
            pl.BlockSpec((MOD_ROWS, D_MODEL), lambda l, j: (0, 0)),
            pl.BlockSpec((None, D_MODEL, tn), lambda l, j: (l, 0, j)),
            pl.BlockSpec((None, 1, tn), lambda l, j: (l, 0, j)),
        ],
        out_specs=pl.BlockSpec((None, MOD_ROWS, tn), lambda l, j: (l, 0, j)),
        out_shape=jax.ShapeDtypeStruct((DEPTH, MOD_ROWS, n), F32),
        compiler_params=_cparams(("parallel", "parallel")),
        name="modulation",
    )(cvec, w_ada, b_ada.reshape(DEPTH, 1, n))


def _rope_cols(v, cos, sin):
    first = (lax.broadcasted_iota(jnp.int32, (v.shape[0], LANES), 1) % HEAD_DIM) < HEAD_DIM // 2
    outs = []
    for c in range(v.shape[1] // LANES):
        vc = v[:, c * LANES:(c + 1) * LANES]
        partner = jnp.where(first, pltpu.roll(vc, LANES - HEAD_DIM // 2, 1),
                            pltpu.roll(vc, HEAD_DIM // 2, 1))
        outs.append(vc * cos + partner * sin)
    return outs[0] if len(outs) == 1 else jnp.concatenate(outs, axis=1)


def _with_rope(acc, lo, hi, cos_ref, sin_ref):
    parts = []
    if lo > 0:
        parts.append(acc[:, :lo])
    parts.append(_rope_cols(acc[:, lo:hi], cos_ref[...], sin_ref[...]))
    if hi < acc.shape[1]:
        parts.append(acc[:, hi:])
    return parts[0] if len(parts) == 1 else jnp.concatenate(parts, axis=1)


def _proj_kernel(*refs, latent, n_alias, n_main):
    x_ref, mod_ref, g_ref, wm_ref, wkv_ref = refs[:5]
    if latent:
        cos_ref, sin_ref, main_ref, kv_ref, h_scr = refs[5:]
    else:
        main_ref, ka_ref, va_ref, kb_ref, vb_ref, h_scr = refs[5 + n_alias:]
    j = pl.program_id(1)

    @pl.when(j == 0)
    def _():
        h = _rms(x_ref[...], g_ref[...]) * (1.0 + mod_ref[1:2, :]) + mod_ref[0:1, :]
        h_scr[...] = h.astype(BF16)

    @pl.when(j < n_main - 1)
    def _():
        main_ref[...] = _dot(h_scr[...], wm_ref[...]).astype(main_ref.dtype)

    @pl.when(j == n_main - 1)
    def _():
        acc = _dot(h_scr[...], wm_ref[...])
        if latent:
            acc = _with_rope(acc, 0, BRANCH_DIM, cos_ref, sin_ref)
        main_ref[...] = acc.astype(main_ref.dtype)

    @pl.when(j == n_main)
    def _():
        acc = _dot(h_scr[...], wkv_ref[...])
        if latent:
            kv_ref[...] = _with_rope(acc, KV_KA, KV_KA + KV_A, cos_ref, sin_ref).astype(kv_ref.dtype)
        else:
            for ref, lo in ((kb_ref, 0), (vb_ref, BRANCH_DIM), (ka_ref, KV_KA), (va_ref, KV_VA)):
                ref[...] = acc[:, lo:lo + ref.shape[-1]].reshape(ref.shape)


def _proj(x, mods, l, g_pre, w_main, w_kv, *, tm, tn, mod_row, seq, rope_tabs=None, caches=None):
    m = x.shape[0]
    latent = rope_tabs is not None
    n_main = N_MAIN // tn
    last = n_main - 1
    in_specs = [
        pl.BlockSpec((tm, D_MODEL), lambda i, j: (i, 0)),
        pl.BlockSpec((None, None, 6, D_MODEL), lambda i, j: (l, mod_row(i), 0, 0)),
        pl.BlockSpec((None, None, 1, D_MODEL), lambda i, j: (l, 0, 0, 0)),
        pl.BlockSpec((None, D_MODEL, tn), lambda i, j: (l, 0, jnp.minimum(j, last))),
        _resident((None, D_MODEL, N_KV), lambda i, j: (l, 0, 0)),
    ]
    args = [x, mods, g_pre, w_main, w_kv]
    main_spec = pl.BlockSpec((tm, tn), lambda i, j: (i, jnp.minimum(j, last)))
    main_shape = jax.ShapeDtypeStruct((m, N_MAIN), BF16)
    aliases = {}
    if latent:
        seq_tiles = seq // tm
        tab_spec = pl.BlockSpec((tm, LANES), lambda i, j: (i % seq_tiles, 0))
        in_specs += [tab_spec, tab_spec]
        args += list(rope_tabs)
        out_specs = [main_spec, pl.BlockSpec((tm, N_KV), lambda i, j: (i, 0))]
        out_shape = [main_shape, jax.ShapeDtypeStruct((m, N_KV), BF16)]
    else:
        widths = (KV_A, KV_A, BRANCH_DIM, BRANCH_DIM)
        if caches is not None:
            in_specs += [pl.BlockSpec(memory_space=pl.ANY)] * len(caches)
            aliases = {len(args) + k: 1 + k for k in range(len(caches))}
            args += list(caches)
        out_specs = [main_spec] + [
            pl.BlockSpec((tm // seq, None, seq, w), lambda i, j: (i, l, 0, 0)) for w in widths]
        out_shape = [main_shape] + [jax.ShapeDtypeStruct((m // seq, DEPTH, seq, w), F32) for w in widths]
    return pl.pallas_call(
        functools.partial(_proj_kernel, latent=latent, n_alias=len(aliases), n_main=n_main),
        grid=(m // tm, n_main + 1),
        in_specs=in_specs,
        out_specs=out_specs,
        out_shape=out_shape,
        input_output_aliases=aliases,
        scratch_shapes=[pltpu.VMEM((tm, D_MODEL), BF16)],
        compiler_params=_cparams(("parallel", "arbitrary")),
        name="proj_lat" if latent else "proj_ctx",
    )(*args)


def _softmax_pv(parts, sink=None):
    m = parts[0][0].max(axis=-1, keepdims=True)
    for s, _ in parts[1:]:
        m = jnp.maximum(m, s.max(axis=-1, keepdims=True))
    if sink is not None:
        m = jnp.maximum(m, sink)
    den = None
    acc = None
    for s, v in parts:
        e = jnp.exp(s - m)
        d = e.sum(axis=-1, keepdims=True)
        o = _dot(e.astype(BF16), v)
        den = d if den is None else den + d
        acc = o if acc is None else acc + o
    if sink is not None:
        den = den + jnp.exp(sink - m)
    return acc * (1.0 / den)


def _kv_head_variants(x2):
    low = _low_lanes(x2.shape)
    xr = pltpu.roll(x2, HEAD_DIM, 1)
    zero = jnp.zeros_like(x2)
    return [
        [jnp.where(low, x2, zero).astype(BF16), jnp.where(low, zero, xr).astype(BF16)],
        [jnp.where(low, xr, zero).astype(BF16), jnp.where(low, zero, x2).astype(BF16)],
    ]


def _attn_ctx_kernel(sink_ref, qa_ref, qb_ref, ka_ref, va_ref, kb_ref, vb_ref, oa_ref, ob_ref):
    ka = _kv_head_variants(ka_ref[...])
    va = _kv_head_variants(va_ref[...])
    for j in range(A_Q_HEADS // 2):
        g = (2 * j) // A_GROUP
        qc = _scaled(qa_ref[:, j * LANES:(j + 1) * LANES])
        o = None
        for p in range(2):
            s = _dot_t(qc, ka[g][p])
            oh = _softmax_pv([(s, va[g][p])], sink=sink_ref[2 * j + p])
            o = oh if o is None else o + oh
        oa_ref[:, j * LANES:(j + 1) * LANES] = o.astype(oa_ref.dtype)
    for j in range(B_HEADS // 2):
        qc = _scaled(qb_ref[:, j * LANES:(j + 1) * LANES])
        kc = kb_ref[:, j * LANES:(j + 1) * LANES]
        vc = vb_ref[:, j * LANES:(j + 1) * LANES]
        low = _low_lanes(kc.shape)
        zero = jnp.zeros_like(kc)
        o = None
        for p in range(2):
            keep = low if p == 0 else jnp.logical_not(low)
            kh = jnp.where(keep, kc, zero).astype(BF16)
            vh = jnp.where(keep, vc, zero).astype(BF16)
            s = _dot_t(qc, kh)
            oh = _softmax_pv([(s, vh)])
            o = oh if o is None else o + oh
        ob_ref[:, j * LANES:(j + 1) * LANES] = o.astype(ob_ref.dtype)


def _attn_ctx(main, caches, sink, l, seq):
    m = main.shape[0]
    out = jax.ShapeDtypeStruct((m, BRANCH_DIM), BF16)
    cache_specs = [pl.BlockSpec((None, None, seq, c.shape[-1]), lambda b: (b, l, 0, 0)) for c in caches]
    return pl.pallas_call(
        _attn_ctx_kernel,
        grid=(m // seq,),
        in_specs=[
            pl.BlockSpec(memory_space=pltpu.SMEM),
            pl.BlockSpec((seq, BRANCH_DIM), lambda b: (b, QA_TILE)),
            pl.BlockSpec((seq, BRANCH_DIM), lambda b: (b, QB_TILE)),
        ] + cache_specs,
        out_specs=[pl.BlockSpec((seq, BRANCH_DIM), lambda b: (b, 0))] * 2,
        out_shape=[out, out],
        compiler_params=_cparams(("parallel",)),
        name="attn_ctx",
    )(sink, main, main, *caches)


def _attn_a_lat_kernel(sink_ref, q_ref, kv_ref, ck_ref, cv_ref, o_ref, *, n_blocks):
    n = pl.program_id(1)
    prev = pl.multiple_of(jnp.maximum(n - 1, 0) * A_BLOCK, A_BLOCK)
    cur = pl.multiple_of(n * A_BLOCK, A_BLOCK)
    nxt = pl.multiple_of(jnp.minimum(n + 1, n_blocks - 1) * A_BLOCK, A_BLOCK)

    def band(col):
        return jnp.concatenate(
            [kv_ref[pl.ds(s, A_BLOCK), col:col + KV_A] for s in (prev, cur, nxt)], axis=0).astype(F32)

    kb = _kv_head_variants(band(0))
    vb = _kv_head_variants(band(KV_A))
    kc = _kv_head_variants(ck_ref[...])
    vc = _kv_head_variants(cv_ref[...])
    qi = lax.broadcasted_iota(jnp.int32, (A_BLOCK, 3 * A_BLOCK), 0)
    kj = lax.broadcasted_iota(jnp.int32, (A_BLOCK, 3 * A_BLOCK), 1)
    no_prev = jnp.where(n > 0, 0, 1)
    no_next = jnp.where(n < n_blocks - 1, 0, 1)
    valid = (kj >= qi + no_prev * (A_BLOCK - qi)) & (kj <= 2 * A_BLOCK + qi - no_next * (qi + 1))
    for j in range(A_Q_HEADS // 2):
        g = (2 * j) // A_GROUP
        qc = _scaled(q_ref[:, j * LANES:(j + 1) * LANES])
        o = None
        for p in range(2):
            s_band = jnp.where(valid, _dot_t(qc, kb[g][p]), NEG_INF)
            s_ctx = _dot_t(qc, kc[g][p])
            oh = _softmax_pv([(s_band, vb[g][p]), (s_ctx, vc[g][p])], sink=sink_ref[2 * j + p])
            o = oh if o is None else o + oh
        o_ref[:, j * LANES:(j + 1) * LANES] = o.astype(o_ref.dtype)


def _attn_a_lat(main, kv, cache_k, cache_v, sink, l, seq):
    m = main.shape[0]
    nb = seq // A_BLOCK
    n_seq = m // seq
    cache_spec = pl.BlockSpec((None, None, cache_k.shape[2], KV_A), lambda b, n: (b, l, 0, 0))
    return pl.pallas_call(
        functools.partial(_attn_a_lat_kernel, n_blocks=nb),
        grid=(n_seq, nb),
        in_specs=[
            pl.BlockSpec(memory_space=pltpu.SMEM),
            pl.BlockSpec((A_BLOCK, BRANCH_DIM), lambda b, n: (b * nb + n, QA_TILE)),
            pl.BlockSpec((seq, 2 * KV_A), lambda b, n: (b, KV_KA // (2 * KV_A))),
            cache_spec, cache_spec,
        ],
        out_specs=pl.BlockSpec((A_BLOCK, BRANCH_DIM), lambda b, n: (b * nb + n, 0)),
        out_shape=jax.ShapeDtypeStruct((m, BRANCH_DIM), BF16),
        compiler_params=_cparams(("parallel", "arbitrary")),
        name="attn_a_lat",
    )(sink, main, kv, cache_k, cache_v)


def _attn_b_lat_kernel(q_ref, k_ref, v_ref, ck_ref, cv_ref, bias_ref, o_ref, ckb_scr, cvb_scr, *, rows, kr):
    r = pl.program_id(1)

    @pl.when(r == 0)
    def _():
        ckb_scr[...] = ck_ref[...].astype(BF16)
        cvb_scr[...] = cv_ref[...].astype(BF16)

    r0 = jnp.clip(r - kr // 2, 0, rows - kr)
    start = pl.multiple_of(r0 * GRID_W, GRID_W)
    n_loc = kr * GRID_W
    low_q = _low_lanes((GRID_W, LANES))
    for j in range(B_HEADS // 2):
        cols = slice(j * LANES, (j + 1) * LANES)
        qc = _scaled(q_ref[:, cols])
        zero = jnp.zeros_like(qc)
        qs = jnp.concatenate([jnp.where(low_q, qc, zero), jnp.where(low_q, zero, qc)], axis=0)
        kl = k_ref[pl.ds(start, n_loc), cols]
        vl = v_ref[pl.ds(start, n_loc), cols]
        s_loc = _dot_t(qs, kl) + bias_ref[j]
        s_ctx = _dot_t(qs, ckb_scr[:, cols])
        o2 = _softmax_pv([(s_loc, vl), (s_ctx, cvb_scr[:, cols])])
        o = jnp.where(low_q, o2[:GRID_W], o2[GRID_W:])
        o_ref[:, cols] = o.astype(o_ref.dtype)


def _attn_b_lat(main, kv, cache_k, cache_v, bias, l, seq):
    m = main.shape[0]
    rows = seq // GRID_W
    kr = min(B_WIN_ROWS, rows)
    n_seq = m // seq
    past = cache_k.shape[2]
    cache_spec = pl.BlockSpec((None, None, past, BRANCH_DIM), lambda b, r: (b, l, 0, 0))
    half = kr // 2

    def variant(r):
        return jnp.minimum(r, half) + jnp.maximum(r - (rows - kr + half), 0)

    return pl.pallas_call(
        functools.partial(_attn_b_lat_kernel, rows=rows, kr=kr),
        grid=(n_seq, rows),
        in_specs=[
            pl.BlockSpec((GRID_W, BRANCH_DIM), lambda b, r: (b * rows + r, QB_TILE)),
            pl.BlockSpec((seq, BRANCH_DIM), lambda b, r: (b, 0)),
            pl.BlockSpec((seq, BRANCH_DIM), lambda b, r: (b, 1)),
            cache_spec, cache_spec,
            pl.BlockSpec((None, None, B_HEADS // 2, 2 * GRID_W, kr * GRID_W),
                         lambda b, r: (l, variant(r), 0, 0, 0)),
        ],
        out_specs=pl.BlockSpec((GRID_W, BRANCH_DIM), lambda b, r: (b * rows + r, 0)),
        out_shape=jax.ShapeDtypeStruct((m, BRANCH_DIM), BF16),
        scratch_shapes=[pltpu.VMEM((past, BRANCH_DIM), BF16)] * 2,
        compiler_params=_cparams(("parallel", "arbitrary")),
        name="attn_b_lat",
    )(main, kv, kv, cache_k, cache_v, bias)


def _neighbourhood_bias(rpb, rows):
    kr = min(B_WIN_ROWS, rows)
    assert kr % 2 == 0 and 2 * GRID_W == LANES
    depth, heads, n_dr, n_dc = rpb.shape
    rpb_pad = jnp.pad(rpb.astype(F32), ((0, 0), (0, 0), (0, (-n_dr) % 8), (0, LANES - n_dc)))
    return pl.pallas_call(
        functools.partial(_bias_kernel, kr=kr),
        grid=(depth, heads),
        in_specs=[pl.BlockSpec((None, None, rpb_pad.shape[2], LANES), lambda l, h: (l, h, 0, 0))],
        out_specs=pl.BlockSpec((None, kr, None, GRID_W, kr * GRID_W), lambda l, h: (l, 0, h // 2, h % 2, 0)),
        out_shape=jax.ShapeDtypeStruct((depth, kr, heads // 2, 2 * GRID_W, kr * GRID_W), F32),
        compiler_params=_cparams(("parallel", "parallel")),
        name="nbr_bias",
    )(rpb_pad)


def _bias_kernel(rpb_ref, o_ref, *, kr):
    shape = (GRID_W, LANES)
    c = lax.broadcasted_iota(jnp.int32, shape, 0)
    lane = lax.broadcasted_iota(jnp.int32, shape, 1)
    c2 = lane % GRID_W
    ws = jnp.clip(c - B_WIN_COLS // 2, 0, GRID_W - B_WIN_COLS)
    ok = (c2 >= ws) & (c2 < ws + B_WIN_COLS)
    low = lane < GRID_W

    def toeplitz(dr, lane0):
        row = jnp.broadcast_to(rpb_ref[dr:dr + 1, :], shape)
        return pltpu.roll(row, (lane0 - (B_WIN_COLS - 1)) % LANES, 1, stride=1, stride_axis=0)

    pairs = {}
    for v in range(kr):
        for m in range(0, kr, 2):
            dr = m - v + B_WIN_ROWS - 1
            if dr not in pairs:
                pair = jnp.where(low, toeplitz(dr, 0), toeplitz(dr + 1, GRID_W))
                pairs[dr] = jnp.where(ok, pair, NEG_INF)
            o_ref[v, :, m * GRID_W:(m + 2) * GRID_W] = pairs[dr]


def _dft_tables(seq):
    cd = C_GROUP_DIM
    kc = (np.arange(cd)[:, None] * np.arange(cd)[None, :]) % cd
    ang_c = 2.0 * np.pi * kc / cd
    eye2 = np.eye(2)
    bd_cos = np.kron(eye2, np.cos(ang_c))
    bd_sin = np.kron(eye2, np.sin(ang_c))
    kt = (np.arange(seq)[:, None] * np.arange(seq)[None, :]) % seq
    ang_t = 2.0 * np.pi * kt / seq
    norm = 1.0 / np.sqrt(float(seq * cd))
    pos = np.concatenate([np.cos(ang_t), -np.sin(ang_t)], axis=1) * norm
    return (jnp.asarray(bd_cos, F32).astype(BF16), jnp.asarray(bd_sin, F32).astype(BF16),
            jnp.asarray(pos, F32).astype(BF16))


def _fourier_kernel(u_ref, bc_ref, bs_ref, pos_ref, o_ref):
    pair = 2 * C_GROUP_DIM
    uc, us = [], []
    for p in range(BRANCH_DIM // pair):
        up = u_ref[:, p * pair:(p + 1) * pair]
        uc.append(_dot(up, bc_ref[...]))
        us.append(_dot(up, bs_ref[...]))
    z = jnp.concatenate([jnp.concatenate(uc, axis=1), jnp.concatenate(us, axis=1)], axis=0)
    o_ref[...] = _dot(pos_ref[...], z.astype(BF16)).astype(o_ref.dtype)


def _fourier(main, seq):
    m = main.shape[0]
    bc, bs, pos = _dft_tables(seq)
    pair = 2 * C_GROUP_DIM
    return pl.pallas_call(
        _fourier_kernel,
        grid=(m // seq,),
        in_specs=[
            pl.BlockSpec((seq, BRANCH_DIM), lambda b: (b, UC_TILE)),
            _resident((pair, pair), lambda b: (0, 0)),
            _resident((pair, pair), lambda b: (0, 0)),
            _resident((seq, 2 * seq), lambda b: (0, 0)),
        ],
        out_specs=pl.BlockSpec((seq, BRANCH_DIM), lambda b: (b, 0)),
        out_shape=jax.ShapeDtypeStruct((m, BRANCH_DIM), BF16),
        compiler_params=_cparams(("parallel",)),
        name="fourier",
    )(main, bc, bs, pos)


def _merge_kernel(oa_ref, ob_ref, oc_ref, gates_ref, x_ref, mod_ref, gpost_ref, gpre_ref,
                  wb_ref, wo_ref, x1_ref, h2_ref):
    mix = None
    for i, o_ref in enumerate((oa_ref, ob_ref, oc_ref)):
        gate = _sigmoid(gates_ref[:, i * D_MODEL:(i + 1) * D_MODEL].astype(F32))
        term = gate * _dot(o_ref[...], wb_ref[i])
        mix = term if mix is None else mix + term
    y = _dot(mix.astype(BF16), wo_ref[...])
    x1 = x_ref[...] + mod_ref[2:3, :] * _rms(y, gpost_ref[...])
    x1_ref[...] = x1
    h2 = _rms(x1, gpre_ref[...]) * (1.0 + mod_ref[4:5, :]) + mod_ref[3:4, :]
    h2_ref[...] = h2.astype(BF16)


def _merge(oa, ob, oc, main, x, mods, l, g_post, g_pre, w_branch, w_out, *, tm, mod_row):
    m = x.shape[0]
    o_spec = pl.BlockSpec((tm, BRANCH_DIM), lambda i: (i, 0))
    x_spec = pl.BlockSpec((tm, D_MODEL), lambda i: (i, 0))
    return pl.pallas_call(
        _merge_kernel,
        grid=(m // tm,),
        in_specs=[
            o_spec, o_spec, o_spec,
            pl.BlockSpec((tm, N_GATES), lambda i: (i, 0)),
            x_spec,
            pl.BlockSpec((None, None, 6, D_MODEL), lambda i: (l, mod_row(i), 0, 0)),
            pl.BlockSpec((None, None, 1, D_MODEL), lambda i: (l, 0, 0, 0)),
            pl.BlockSpec((None, None, 1, D_MODEL), lambda i: (l, 1, 0, 0)),
            _resident((None, N_BRANCH, BRANCH_DIM, D_MODEL), lambda i: (l, 0, 0, 0)),
            _resident((None, D_MODEL, D_MODEL), lambda i: (l, 0, 0)),
        ],
        out_specs=[x_spec, x_spec],
        out_shape=[jax.ShapeDtypeStruct((m, D_MODEL), F32), jax.ShapeDtypeStruct((m, D_MODEL), BF16)],
        compiler_params=_cparams(("parallel",)),
        name="merge",
    )(oa, ob, oc, main, x, mods, g_post, g_pre, w_branch, w_out)


def _ffn_kernel(h_ref, wg_ref, wu_ref, wo_ref, x_ref, mod_ref, gpost_ref, o_ref, acc_ref):
    k = pl.program_id(1)

    @pl.when(k == 0)
    def _():
        acc_ref[...] = jnp.zeros_like(acc_ref)

    h = h_ref[...]
    gate = _dot(h, wg_ref[...])
    up = _dot(h, wu_ref[...])
    act = _silu(gate) * up
    acc_ref[...] += _dot(act.astype(BF16), wo_ref[...])

    @pl.when(k == pl.num_programs(1) - 1)
    def _():
        o_ref[...] = x_ref[...] + mod_ref[5:6, :] * _rms(acc_ref[...], gpost_ref[...])


def _ffn(h2, x1, mods, l, g_post, w_ffn_in, w_ffn_out, *, tm, tf, mod_row):
    m = x1.shape[0]
    nk = D_FF // tf
    x_spec = pl.BlockSpec((tm, D_MODEL), lambda i, k: (i, 0))
    return pl.pallas_call(
        _ffn_kernel,
        grid=(m // tm, nk),
        in_specs=[
            x_spec,
            pl.BlockSpec((None, D_MODEL, tf), lambda i, k: (l, 0, k)),
            pl.BlockSpec((None, D_MODEL, tf), lambda i, k: (l, 0, k + nk)),
            pl.BlockSpec((None, tf, D_MODEL), lambda i, k: (l, k, 0)),
            x_spec,
            pl.BlockSpec((None, None, 6, D_MODEL), lambda i, k: (l, mod_row(i), 0, 0)),
            pl.BlockSpec((None, None, 1, D_MODEL), lambda i, k: (l, 1, 0, 0)),
        ],
        out_specs=x_spec,
        out_shape=jax.ShapeDtypeStruct((m, D_MODEL), F32),
        scratch_shapes=[pltpu.VMEM((tm, D_MODEL), F32)],
        compiler_params=_cparams(("parallel", "arbitrary")),
        name="ffn",
    )(h2, w_ffn_in, w_ffn_in, w_ffn_out, x1, mods, g_post)


def _rope_tables(seq):
    t = jnp.arange(seq, dtype=jnp.int32)
    row = (t // GRID_W).astype(F32)
    col = (t % GRID_W).astype(F32)
    n_pairs_axis = HEAD_DIM // 4
    inv = ROPE_BASE ** (-jnp.arange(n_pairs_axis, dtype=F32) / n_pairs_axis)
    ang = jnp.concatenate([row[:, None] * inv, col[:, None] * inv], axis=-1)
    cos, sin = jnp.cos(ang), jnp.sin(ang)
    cos_l = jnp.tile(cos, (1, LANES // cos.shape[1]))
    sin_l = jnp.tile(jnp.concatenate([-sin, sin], axis=-1), (1, LANES // HEAD_DIM))
    return cos_l, sin_l


def _permute_w_in(w_in):
    sizes = [BRANCH_DIM, KV_A, KV_A, BRANCH_DIM, BRANCH_DIM, BRANCH_DIM, BRANCH_DIM]
    qa, ka, va, qb, kb, vb, uc, gates = jnp.split(w_in, np.cumsum(sizes).tolist(), axis=-1)
    w_main = jnp.concatenate([gates, qa, qb, uc], axis=-1).astype(BF16)
    w_kv = jnp.concatenate([kb, vb, ka, va], axis=-1).astype(BF16)
    return w_main, w_kv


def kernel(x_prompt, x_sample, cache_a_k, cache_a_v, cache_b_k, cache_b_v, c, c_ctx, w_ada, b_ada,
           norm_pre, norm_post, w_in, a_sink, b_rpb, w_branch, w_out, w_ffn_in, w_ffn_out):
    batch, seq, _ = x_prompt.shape
    dec_batch, dec_seq, _ = x_sample.shape
    past = cache_a_k.shape[2]
    assert dec_batch <= CTX_MOD_ROW and seq % A_BLOCK == 0 and dec_seq % A_BLOCK == 0

    cvec = jnp.concatenate(
        [c, c_ctx[None, :], jnp.zeros((MOD_ROWS - dec_batch - 1, D_MODEL), F32)], axis=0)
    mods = _modulation(cvec, w_ada, b_ada).reshape(DEPTH, MOD_ROWS, 6, D_MODEL)

    w_main, w_kv = _permute_w_in(w_in)
    w_branch_b = w_branch.astype(BF16)
    w_out_b = w_out.astype(BF16)
    w_ffn_in_b = w_ffn_in.astype(BF16)
    w_ffn_out_b = w_ffn_out.astype(BF16)
    g_pre = norm_pre.reshape(DEPTH, 2, 1, D_MODEL)
    g_post = norm_post.reshape(DEPTH, 2, 1, D_MODEL)
    rope_tabs = _rope_tables(dec_seq)
    cak = cache_a_k.reshape(dec_batch, DEPTH, past, KV_A)
    cav = cache_a_v.reshape(dec_batch, DEPTH, past, KV_A)
    cbk = cache_b_k.reshape(dec_batch, DEPTH, past, BRANCH_DIM)
    cbv = cache_b_v.reshape(dec_batch, DEPTH, past, BRANCH_DIM)

    tm_proj, tn_proj, tm_merge, tm_ffn, tf = 1024, N_MAIN // 3, 512, 512, D_FF // 2
    assert tn_proj == N_GATES // 2

    def layer_tail(l, x, main, oa, ob, oc, mod_row_fn):
        x1, h2 = _merge(oa, ob, oc, main, x, mods, l, g_post, g_pre, w_branch_b, w_out_b,
                        tm=tm_merge, mod_row=mod_row_fn(tm_merge))
        return _ffn(h2, x1, mods, l, g_post, w_ffn_in_b, w_ffn_out_b,
                    tm=tm_ffn, tf=tf, mod_row=mod_row_fn(tm_ffn))

    ctx_row = lambda tm: (lambda i: CTX_MOD_ROW)
    x = x_prompt.reshape(batch * seq, D_MODEL)
    caches = None
    for l in range(DEPTH):
        main, *caches = _proj(x, mods, l, g_pre, w_main, w_kv, tm=tm_proj, tn=tn_proj,
                              mod_row=ctx_row(tm_proj), seq=seq, caches=caches)
        oa, ob = _attn_ctx(main, caches, a_sink[l], l, seq)
        oc = _fourier(main, seq)
        x = layer_tail(l, x, main, oa, ob, oc, ctx_row)
    y_prompt = x.reshape(batch, seq, D_MODEL)
    nak, nav, nbk, nbv = caches
    new_a_k = nak.reshape(batch, DEPTH, seq, A_KV_HEADS, HEAD_DIM)
    new_a_v = nav.reshape(batch, DEPTH, seq, A_KV_HEADS, HEAD_DIM)
    new_b_k = nbk.reshape(batch, DEPTH, seq, B_HEADS, HEAD_DIM)
    new_b_v = nbv.reshape(batch, DEPTH, seq, B_HEADS, HEAD_DIM)

    lat_row = lambda tm: (lambda i: (i * tm) // dec_seq)
    x = x_sample.reshape(dec_batch * dec_seq, D_MODEL)
    bias = _neighbourhood_bias(b_rpb, dec_seq // GRID_W)
    for l in range(DEPTH):
        main, kv = _proj(x, mods, l, g_pre, w_main, w_kv, tm=tm_proj, tn=tn_proj,
                         mod_row=lat_row(tm_proj), seq=dec_seq, rope_tabs=rope_tabs)
        oa = _attn_a_lat(main, kv, cak, cav, a_sink[l], l, dec_seq)
        ob = _attn_b_lat(main, kv, cbk, cbv, bias, l, dec_seq)
        oc = _fourier(main, dec_seq)
        x = layer_tail(l, x, main, oa, ob, oc, lat_row)
    y_sample = x.reshape(dec_batch, dec_seq, D_MODEL)
    return (y_prompt, y_sample, new_a_k, new_a_v, new_b_k, new_b_v)
```

```python
import functools

import numpy as np
import jax
import jax.numpy as jnp
from jax import lax
from jax.experimental import pallas as pl
from jax.experimental.pallas import tpu as pltpu

F32 = jnp.float32
BF16 = jnp.bfloat16

D_MODEL = 1024
DEPTH = 2
GRID_W = 64
HEAD_DIM = 64
BRANCH_DIM = D_MODEL // 2
A_Q_HEADS = BRANCH_DIM // HEAD_DIM
A_KV_HEADS = A_Q_HEADS // 4
A_GROUP = A_Q_HEADS // A_KV_HEADS
A_BLOCK = 128
B_HEADS = BRANCH_DIM // HEAD_DIM
B_WIN_ROWS = 8
B_WIN_COLS = 16
C_GROUPS = 4
C_GROUP_DIM = BRANCH_DIM // C_GROUPS
N_BRANCH = 3
D_FF = -(-8 * D_MODEL // (3 * 256)) * 256
ROPE_BASE = 10000.0
NORM_EPS = 1e-6
NEG_INF = -1e30
SCALE = HEAD_DIM ** -0.5

LANES = 128
KV_A = A_KV_HEADS * HEAD_DIM
N_GATES = N_BRANCH * D_MODEL
N_MAIN = N_GATES + 3 * BRANCH_DIM
N_KV = 2 * BRANCH_DIM + 2 * KV_A
MAIN_TILE = BRANCH_DIM
QA_TILE = N_GATES // MAIN_TILE
QB_TILE = QA_TILE + 1
UC_TILE = QA_TILE + 2
KV_KA = 2 * BRANCH_DIM
KV_VA = KV_KA + KV_A
MOD_ROWS = 8
CTX_MOD_ROW = 4
VMEM_LIMIT = 56 * 1024 * 1024


def _cparams(sem):
    return pltpu.CompilerParams(dimension_semantics=sem, vmem_limit_bytes=VMEM_LIMIT)


def _resident(shape, index_map):
    return pl.BlockSpec(shape, index_map, pipeline_mode=pl.Buffered(1))


def _rms(x, g):
    return x * lax.rsqrt(jnp.mean(x * x, axis=-1, keepdims=True) + NORM_EPS) * g


def _sigmoid(x):
    return 0.5 * jnp.tanh(0.5 * x) + 0.5


def _silu(x):
    half = 0.5 * x
    return half + half * jnp.tanh(half)


def _dot_t(a, b):
    return lax.dot_general(a, b, (((1,), (1,)), ((), ())), preferred_element_type=F32)


def _dot(a, b):
    return jnp.dot(a, b, preferred_element_type=F32)


def _scaled(q):
    assert np.log2(SCALE) == round(np.log2(SCALE))
    return q * jnp.asarray(SCALE, q.dtype)


def _low_lanes(shape):
    return lax.broadcasted_iota(jnp.int32, shape, len(shape) - 1) < HEAD_DIM


def _mod_kernel(cv_ref, w_ref, b_ref, o_ref):
    o_ref[...] = jnp.dot(_silu(cv_ref[...]), w_ref[...], preferred_element_type=F32,
                         precision=lax.Precision.HIGHEST) + b_ref[...]


def _modulation(cvec, w_ada, b_ada):
    tn = 1536
    n = 6 * D_MODEL
    return pl.pallas_call(
        _mod_kernel,
        grid=(DEPTH, n // tn),
        in_specs=[
            pl.BlockSpec((MOD_ROWS, D_MODEL), lambda l, j: (0, 0)),
            pl.BlockSpec((None, D_MODEL, tn), lambda l, j: (l, 0, j)),
            pl.BlockSpec((None, 1, tn), lambda l, j: (l, 0, j)),
        ],
        out_specs=pl.BlockSpec((None, MOD_ROWS, tn), lambda l, j: (l, 0, j)),
        out_shape=jax.ShapeDtypeStruct((DEPTH, MOD_ROWS, n), F32),
        compiler_params=_cparams(("parallel", "parallel")),
        name="modulation",
    )(cvec, w_ada, b_ada.reshape(DEPTH, 1, n))


def _rope_cols(v, cos, sin):
    first = (lax.broadcasted_iota(jnp.int32, (v.shape[0], LANES), 1) % HEAD_DIM) < HEAD_DIM // 2
    outs = []
    for c in range(v.shape[1] // LANES):
        vc = v[:, c * LANES:(c + 1) * LANES]
        partner = jnp.where(first, pltpu.roll(vc, LANES - HEAD_DIM // 2, 1),
                            pltpu.roll(vc, HEAD_DIM // 2, 1))
        outs.append(vc * cos + partner * sin)
    return outs[0] if len(outs) == 1 else jnp.concatenate(outs, axis=1)


def _with_rope(acc, lo, hi, cos_ref, sin_ref):
    parts = []
    if lo > 0:
        parts.append(acc[:, :lo])
    parts.append(_rope_cols(acc[:, lo:hi], cos_ref[...], sin_ref[...]))
    if hi < acc.shape[1]:
        parts.append(acc[:, hi:])
    return parts[0] if len(parts) == 1 else jnp.concatenate(parts, axis=1)


def _proj_kernel(*refs, latent, n_alias, n_main):
    x_ref, mod_ref, g_ref, wm_ref, wkv_ref = refs[:5]
    if latent:
        cos_ref, sin_ref, main_ref, kv_ref, h_scr = refs[5:]
    else:
        main_ref, ka_ref, va_ref, kb_ref, vb_ref, h_scr = refs[5 + n_alias:]
    j = pl.program_id(1)

    @pl.when(j == 0)
    def _():
        h = _rms(x_ref[...], g_ref[...]) * (1.0 + mod_ref[1:2, :]) + mod_ref[0:1, :]
        h_scr[...] = h.astype(BF16)

    @pl.when(j < n_main - 1)
    def _():
        main_ref[...] = _dot(h_scr[...], wm_ref[...]).astype(main_ref.dtype)

    @pl.when(j == n_main - 1)
    def _():
        acc = _dot(h_scr[...], wm_ref[...])
        if latent:
            acc = _with_rope(acc, 0, BRANCH_DIM, cos_ref, sin_ref)
        main_ref[...] = acc.astype(main_ref.dtype)

    @pl.when(j == n_main)
    def _():
        acc = _dot(h_scr[...], wkv_ref[...])
        if latent:
            kv_ref[...] = _with_rope(acc, KV_KA, KV_KA + KV_A, cos_ref, sin_ref).astype(kv_ref.dtype)
        else:
            for ref, lo in ((kb_ref, 0), (vb_ref, BRANCH_DIM), (ka_ref, KV_KA), (va_ref, KV_VA)):
                ref[...] = acc[:, lo:lo + ref.shape[-1]].reshape(ref.shape)


def _proj(x, mods, l, g_pre, w_main, w_kv, *, tm, tn, mod_row, seq, rope_tabs=None, caches=None):
    m = x.shape[0]
    latent = rope_tabs is not None
    n_main = N_MAIN // tn
    last = n_main - 1
    in_specs = [
        pl.BlockSpec((tm, D_MODEL), lambda i, j: (i, 0)),
        pl.BlockSpec((None, None, 6, D_MODEL), lambda i, j: (l, mod_row(i), 0, 0)),
        pl.BlockSpec((None, None, 1, D_MODEL), lambda i, j: (l, 0, 0, 0)),
        pl.BlockSpec((None, D_MODEL, tn), lambda i, j: (l, 0, jnp.minimum(j, last))),
        _resident((None, D_MODEL, N_KV), lambda i, j: (l, 0, 0)),
    ]
    args = [x, mods, g_pre, w_main, w_kv]
    main_spec = pl.BlockSpec((tm, tn), lambda i, j: (i, jnp.minimum(j, last)))
    main_shape = jax.ShapeDtypeStruct((m, N_MAIN), BF16)
    aliases = {}
    if latent:
        seq_tiles = seq // tm
        tab_spec = pl.BlockSpec((tm, LANES), lambda i, j: (i % seq_tiles, 0))
        in_specs += [tab_spec, tab_spec]
        args += list(rope_tabs)
        out_specs = [main_spec, pl.BlockSpec((tm, N_KV), lambda i, j: (i, 0))]
        out_shape = [main_shape, jax.ShapeDtypeStruct((m, N_KV), BF16)]
    else:
        widths = (KV_A, KV_A, BRANCH_DIM, BRANCH_DIM)
        if caches is not None:
            in_specs += [pl.BlockSpec(memory_space=pl.ANY)] * len(caches)
            aliases = {len(args) + k: 1 + k for k in range(len(caches))}
            args += list(caches)
        out_specs = [main_spec] + [
            pl.BlockSpec((tm // seq, None, seq, w), lambda i, j: (i, l, 0, 0)) for w in widths]
        out_shape = [main_shape] + [jax.ShapeDtypeStruct((m // seq, DEPTH, seq, w), F32) for w in widths]
    return pl.pallas_call(
        functools.partial(_proj_kernel, latent=latent, n_alias=len(aliases), n_main=n_main),
        grid=(m // tm, n_main + 1),
        in_specs=in_specs,
        out_specs=out_specs,
        out_shape=out_shape,
        input_output_aliases=aliases,
        scratch_shapes=[pltpu.VMEM((tm, D_MODEL), BF16)],
        compiler_params=_cparams(("parallel", "arbitrary")),
        name="proj_lat" if latent else "proj_ctx",
    )(*args)


def _softmax_pv(parts, sink=None):
    m = parts[0][0].max(axis=-1, keepdims=True)
    for s, _ in parts[1:]:
        m = jnp.maximum(m, s.max(axis=-1, keepdims=True))
    if sink is not None:
        m = jnp.maximum(m, sink)
    den = None
    acc = None
    for s, v in parts:
        e = jnp.exp(s - m)
        d = e.sum(axis=-1, keepdims=True)
        o = _dot(e.astype(BF16), v)
        den = d if den is None else den + d
        acc = o if acc is None else acc + o
    if sink is not None:
        den = den + jnp.exp(sink - m)
    return acc * (1.0 / den)


def _kv_head_variants(x2):
    low = _low_lanes(x2.shape)
    xr = pltpu.roll(x2, HEAD_DIM, 1)
    zero = jnp.zeros_like(x2)
    return [
        [jnp.where(low, x2, zero).astype(BF16), jnp.where(low, zero, xr).astype(BF16)],
        [jnp.where(low, xr, zero).astype(BF16), jnp.where(low, zero, x2).astype(BF16)],
    ]


def _attn_ctx_kernel(sink_ref, qa_ref, qb_ref, ka_ref, va_ref, kb_ref, vb_ref, oa_ref, ob_ref):
    ka = [jnp.concatenate(v, axis=0) for v in _kv_head_variants(ka_ref[...])]
    va = [jnp.concatenate(v, axis=0) for v in _kv_head_variants(va_ref[...])]
    chunks = [slice(j * LANES, (j + 1) * LANES) for j in range(A_Q_HEADS // 2)]
    work = []
    for j, cols in enumerate(chunks):
        g = (2 * j) // A_GROUP
        s2 = _dot_t(_scaled(qa_ref[:, cols]), ka[g])
        work.append((oa_ref, cols, s2, va[g], (sink_ref[2 * j], sink_ref[2 * j + 1])))
    low = _low_lanes((kb_ref.shape[0], LANES))
    for cols in chunks:
        kc, vc = kb_ref[:, cols], vb_ref[:, cols]
        zero = jnp.zeros_like(kc)
        k2 = jnp.concatenate([jnp.where(low, kc, zero), jnp.where(low, zero, kc)], axis=0).astype(BF16)
        v2 = jnp.concatenate([jnp.where(low, vc, zero), jnp.where(low, zero, vc)], axis=0).astype(BF16)
        work.append((ob_ref, cols, _dot_t(_scaled(qb_ref[:, cols]), k2), v2, None))
    for o_ref, cols, s2, v2, sinks in work:
        o_ref[:, cols] = _pair_softmax_pv(s2, v2, sinks).astype(o_ref.dtype)


def _pair_softmax_pv(s2, v2, sinks=None):
    tk = s2.shape[1] // 2
    es, invs = [], []
    for p in range(2):
        s = s2[:, p * tk:(p + 1) * tk]
        m = s.max(axis=-1, keepdims=True)
        if sinks is not None:
            m = jnp.maximum(m, sinks[p])
        e = jnp.exp(s - m)
        den = e.sum(axis=-1, keepdims=True)
        if sinks is not None:
            den = den + jnp.exp(sinks[p] - m)
        es.append(e.astype(BF16))
        invs.append(1.0 / den)
    o = _dot(jnp.concatenate(es, axis=1), v2)
    return o * jnp.where(_low_lanes(o.shape), invs[0], invs[1])


def _attn_ctx(main, caches, sink, l, seq):
    m = main.shape[0]
    out = jax.ShapeDtypeStruct((m, BRANCH_DIM), BF16)
    cache_specs = [pl.BlockSpec((None, None, seq, c.shape[-1]), lambda b: (b, l, 0, 0)) for c in caches]
    return pl.pallas_call(
        _attn_ctx_kernel,
        grid=(m // seq,),
        in_specs=[
            pl.BlockSpec(memory_space=pltpu.SMEM),
            pl.BlockSpec((seq, BRANCH_DIM), lambda b: (b, QA_TILE)),
            pl.BlockSpec((seq, BRANCH_DIM), lambda b: (b, QB_TILE)),
        ] + cache_specs,
        out_specs=[pl.BlockSpec((seq, BRANCH_DIM), lambda b: (b, 0))] * 2,
        out_shape=[out, out],
        compiler_params=_cparams(("parallel",)),
        name="attn_ctx",
    )(sink, main, main, *caches)


def _attn_a_lat_kernel(sink_ref, q_ref, kv_ref, ck_ref, cv_ref, o_ref,
                       qh_scr, kp_scr, vp_scr, ckb_scr, cvb_scr, *, n_blocks):
    seq = n_blocks * A_BLOCK
    band = 3 * A_BLOCK
    low_seq = _low_lanes((seq, LANES))
    for h in range(A_Q_HEADS):
        j, p, g = h // 2, h % 2, h // A_GROUP
        x = q_ref[:, j * LANES:(j + 1) * LANES].astype(F32) * SCALE
        if p != g:
            x = pltpu.roll(x, HEAD_DIM, 1)
        qh_scr[h] = jnp.where(low_seq if g == 0 else jnp.logical_not(low_seq), x, 0.0).astype(BF16)
    pad = jnp.zeros((A_BLOCK, LANES), BF16)
    for scr, col in ((kp_scr, 0), (vp_scr, KV_A)):
        scr[0:A_BLOCK, :] = pad
        scr[A_BLOCK:A_BLOCK + seq, :] = kv_ref[:, col:col + KV_A]
        scr[A_BLOCK + seq:2 * A_BLOCK + seq, :] = pad
    ckb_scr[...] = ck_ref[...].astype(BF16)
    cvb_scr[...] = cv_ref[...].astype(BF16)

    rows = A_GROUP * A_BLOCK
    row = lax.broadcasted_iota(jnp.int32, (rows, band), 0)
    qi = row % A_BLOCK
    kj = lax.broadcasted_iota(jnp.int32, (rows, band), 1)
    head_row = lax.broadcasted_iota(jnp.int32, (rows, 1), 0) // A_BLOCK
    low_blk = _low_lanes((A_BLOCK, LANES))

    def body(n, carry):
        start = pl.multiple_of(n * A_BLOCK, A_BLOCK)
        no_prev = jnp.where(n > 0, 0, 1)
        no_next = jnp.where(n < n_blocks - 1, 0, 1)
        valid = (kj >= qi + no_prev * (A_BLOCK - qi)) & (kj <= 2 * A_BLOCK + qi - no_next * (qi + 1))
        scores = []
        for g in range(A_KV_HEADS):
            q = jnp.concatenate(
                [qh_scr[A_GROUP * g + i, pl.ds(start, A_BLOCK), :] for i in range(A_GROUP)], axis=0)
            s_band = jnp.where(valid, _dot_t(q, kp_scr[pl.ds(start, band), :]), NEG_INF)
            scores.append((s_band, _dot_t(q, ckb_scr[...])))
        outs = []
        for g, (s_band, s_ctx) in enumerate(scores):
            sink = jnp.full((rows, 1), sink_ref[A_GROUP * g], F32)
            for i in range(1, A_GROUP):
                sink = jnp.where(head_row == i, sink_ref[A_GROUP * g + i], sink)
            outs.append(_softmax_pv([(s_band, vp_scr[pl.ds(start, band), :]), (s_ctx, cvb_scr[...])],
                                    sink=sink))
        for j in range(A_Q_HEADS // 2):
            halves = []
            for p in range(2):
                h = 2 * j + p
                g, i = h // A_GROUP, h % A_GROUP
                t = outs[g][i * A_BLOCK:(i + 1) * A_BLOCK]
                halves.append(t if p == g else pltpu.roll(t, HEAD_DIM, 1))
            o_ref[pl.ds(start, A_BLOCK), j * LANES:(j + 1) * LANES] = (
                jnp.where(low_blk, halves[0], halves[1]).astype(o_ref.dtype))
        return carry

    lax.fori_loop(0, n_blocks, body, 0)


def _attn_a_lat(main, kv, cache_k, cache_v, sink, l, seq):
    m = main.shape[0]
    nb = seq // A_BLOCK
    n_seq = m // seq
    past = cache_k.shape[2]
    cache_spec = pl.BlockSpec((None, None, past, KV_A), lambda b: (b, l, 0, 0))
    return pl.pallas_call(
        functools.partial(_attn_a_lat_kernel, n_blocks=nb),
        grid=(n_seq,),
        in_specs=[
            pl.BlockSpec(memory_space=pltpu.SMEM),
            pl.BlockSpec((seq, BRANCH_DIM), lambda b: (b, QA_TILE)),
            pl.BlockSpec((seq, 2 * KV_A), lambda b: (b, KV_KA // (2 * KV_A))),
            cache_spec, cache_spec,
        ],
        out_specs=pl.BlockSpec((seq, BRANCH_DIM), lambda b: (b, 0)),
        out_shape=jax.ShapeDtypeStruct((m, BRANCH_DIM), BF16),
        scratch_shapes=[
            pltpu.VMEM((A_Q_HEADS, seq, LANES), BF16),
            pltpu.VMEM((seq + 2 * A_BLOCK, LANES), BF16),
            pltpu.VMEM((seq + 2 * A_BLOCK, LANES), BF16),
            pltpu.VMEM((past, LANES), BF16),
            pltpu.VMEM((past, LANES), BF16),
        ],
        compiler_params=_cparams(("parallel",)),
        name="attn_a_lat",
    )(sink, main, kv, cache_k, cache_v)


def _attn_b_lat_kernel(q_ref, k_ref, v_ref, ck_ref, cv_ref, bias_ref, o_ref, ckb_scr, cvb_scr, *, rows, kr):
    ckb_scr[...] = ck_ref[...].astype(BF16)
    cvb_scr[...] = cv_ref[...].astype(BF16)
    half = kr // 2
    n_loc = kr * GRID_W
    low_q = _low_lanes((GRID_W, LANES))

    def body(r, carry):
        r0 = jnp.clip(r - half, 0, rows - kr)
        variant = r - r0
        q0 = pl.multiple_of(r * GRID_W, GRID_W)
        start = pl.multiple_of(r0 * GRID_W, GRID_W)
        chunks = [slice(j * LANES, (j + 1) * LANES) for j in range(B_HEADS // 2)]
        scores = []
        for j, cols in enumerate(chunks):
            qc = _scaled(q_ref[pl.ds(q0, GRID_W), cols])
            zero = jnp.zeros_like(qc)
            qs = jnp.concatenate([jnp.where(low_q, qc, zero), jnp.where(low_q, zero, qc)], axis=0)
            s_loc = _dot_t(qs, k_ref[pl.ds(start, n_loc), cols]) + bias_ref[variant, j]
            scores.append((s_loc, _dot_t(qs, ckb_scr[:, cols])))
        for (s_loc, s_ctx), cols in zip(scores, chunks):
            o2 = _softmax_pv([(s_loc, v_ref[pl.ds(start, n_loc), cols]), (s_ctx, cvb_scr[:, cols])])
            o = jnp.where(low_q, o2[:GRID_W], o2[GRID_W:])
            o_ref[pl.ds(q0, GRID_W), cols] = o.astype(o_ref.dtype)
        return carry

    lax.fori_loop(0, rows, body, 0)


def _attn_b_lat(main, kv, cache_k, cache_v, bias, l, seq):
    m = main.shape[0]
    rows = seq // GRID_W
    kr = min(B_WIN_ROWS, rows)
    n_seq = m // seq
    past = cache_k.shape[2]
    cache_spec = pl.BlockSpec((None, None, past, BRANCH_DIM), lambda b: (b, l, 0, 0))
    return pl.pallas_call(
        functools.partial(_attn_b_lat_kernel, rows=rows, kr=kr),
        grid=(n_seq,),
        in_specs=[
            pl.BlockSpec((seq, BRANCH_DIM), lambda b: (b, QB_TILE)),
            pl.BlockSpec((seq, BRANCH_DIM), lambda b: (b, 0)),
            pl.BlockSpec((seq, BRANCH_DIM), lambda b: (b, 1)),
            cache_spec, cache_spec,
            _resident((None,) + bias.shape[1:], lambda b: (l, 0, 0, 0, 0)),
        ],
        out_specs=pl.BlockSpec((seq, BRANCH_DIM), lambda b: (b, 0)),
        out_shape=jax.ShapeDtypeStruct((m, BRANCH_DIM), BF16),
        scratch_shapes=[pltpu.VMEM((past, BRANCH_DIM), BF16)] * 2,
        compiler_params=_cparams(("parallel",)),
        name="attn_b_lat",
    )(main, kv, kv, cache_k, cache_v, bias)


def _neighbourhood_bias(rpb, rows):
    kr = min(B_WIN_ROWS, rows)
    assert kr % 2 == 0 and 2 * GRID_W == LANES
    depth, heads, n_dr, n_dc = rpb.shape
    rpb_pad = jnp.pad(rpb.astype(F32), ((0, 0), (0, 0), (0, (-n_dr) % 8), (0, LANES - n_dc)))
    return pl.pallas_call(
        functools.partial(_bias_kernel, kr=kr),
        grid=(depth, heads),
        in_specs=[pl.BlockSpec((None, None, rpb_pad.shape[2], LANES), lambda l, h: (l, h, 0, 0))],
        out_specs=pl.BlockSpec((None, kr, None, GRID_W, kr * GRID_W), lambda l, h: (l, 0, h // 2, h % 2, 0)),
        out_shape=jax.ShapeDtypeStruct((depth, kr, heads // 2, 2 * GRID_W, kr * GRID_W), F32),
        compiler_params=_cparams(("parallel", "parallel")),
        name="nbr_bias",
    )(rpb_pad)


def _bias_kernel(rpb_ref, o_ref, *, kr):
    shape = (GRID_W, LANES)
    c = lax.broadcasted_iota(jnp.int32, shape, 0)
    lane = lax.broadcasted_iota(jnp.int32, shape, 1)
    c2 = lane % GRID_W
    ws = jnp.clip(c - B_WIN_COLS // 2, 0, GRID_W - B_WIN_COLS)
    ok = (c2 >= ws) & (c2 < ws + B_WIN_COLS)
    low = lane < GRID_W

    def toeplitz(dr, lane0):
        row = jnp.broadcast_to(rpb_ref[dr:dr + 1, :], shape)
        return pltpu.roll(row, (lane0 - (B_WIN_COLS - 1)) % LANES, 1, stride=1, stride_axis=0)

    pairs = {}
    for v in range(kr):
        for m in range(0, kr, 2):
            dr = m - v + B_WIN_ROWS - 1
            if dr not in pairs:
                pair = jnp.where(low, toeplitz(dr, 0), toeplitz(dr + 1, GRID_W))
                pairs[dr] = jnp.where(ok, pair, NEG_INF)
            o_ref[v, :, m * GRID_W:(m + 2) * GRID_W] = pairs[dr]


def _dft_tables(seq):
    cd = C_GROUP_DIM
    kc = (np.arange(cd)[:, None] * np.arange(cd)[None, :]) % cd
    ang_c = 2.0 * np.pi * kc / cd
    eye2 = np.eye(2)
    bd_cos = np.kron(eye2, np.cos(ang_c))
    bd_sin = np.kron(eye2, np.sin(ang_c))
    kt = (np.arange(seq)[:, None] * np.arange(seq)[None, :]) % seq
    ang_t = 2.0 * np.pi * kt / seq
    norm = 1.0 / np.sqrt(float(seq * cd))
    pos = np.concatenate([np.cos(ang_t), -np.sin(ang_t)], axis=1) * norm
    return (jnp.asarray(bd_cos, F32).astype(BF16), jnp.asarray(bd_sin, F32).astype(BF16),
            jnp.asarray(pos, F32).astype(BF16))


def _fourier_kernel(u_ref, bc_ref, bs_ref, pos_ref, o_ref):
    pair = 2 * C_GROUP_DIM
    uc, us = [], []
    for p in range(BRANCH_DIM // pair):
        up = u_ref[:, p * pair:(p + 1) * pair]
        uc.append(_dot(up, bc_ref[...]))
        us.append(_dot(up, bs_ref[...]))
    z = jnp.concatenate([jnp.concatenate(uc, axis=1), jnp.concatenate(us, axis=1)], axis=0)
    o_ref[...] = _dot(pos_ref[...], z.astype(BF16)).astype(o_ref.dtype)


def _fourier(main, seq):
    m = main.shape[0]
    bc, bs, pos = _dft_tables(seq)
    pair = 2 * C_GROUP_DIM
    return pl.pallas_call(
        _fourier_kernel,
        grid=(m // seq,),
        in_specs=[
            pl.BlockSpec((seq, BRANCH_DIM), lambda b: (b, UC_TILE)),
            _resident((pair, pair), lambda b: (0, 0)),
            _resident((pair, pair), lambda b: (0, 0)),
            _resident((seq, 2 * seq), lambda b: (0, 0)),
        ],
        out_specs=pl.BlockSpec((seq, BRANCH_DIM), lambda b: (b, 0)),
        out_shape=jax.ShapeDtypeStruct((m, BRANCH_DIM), BF16),
        compiler_params=_cparams(("parallel",)),
        name="fourier",
    )(main, bc, bs, pos)


def _merge_kernel(oa_ref, ob_ref, oc_ref, gates_ref, x_ref, mod_ref, gpost_ref, gpre_ref,
                  wb_ref, wo_ref, x1_ref, h2_ref):
    mix = None
    for i, o_ref in enumerate((oa_ref, ob_ref, oc_ref)):
        gate = _sigmoid(gates_ref[:, i * D_MODEL:(i + 1) * D_MODEL].astype(F32))
        term = gate * _dot(o_ref[...], wb_ref[i])
        mix = term if mix is None else mix + term
    y = _dot(mix.astype(BF16), wo_ref[...])
    x1 = x_ref[...] + mod_ref[2:3, :] * _rms(y, gpost_ref[...])
    x1_ref[...] = x1
    h2 = _rms(x1, gpre_ref[...]) * (1.0 + mod_ref[4:5, :]) + mod_ref[3:4, :]
    h2_ref[...] = h2.astype(BF16)


def _merge(oa, ob, oc, main, x, mods, l, g_post, g_pre, w_branch, w_out, *, tm, mod_row):
    m = x.shape[0]
    o_spec = pl.BlockSpec((tm, BRANCH_DIM), lambda i: (i, 0))
    x_spec = pl.BlockSpec((tm, D_MODEL), lambda i: (i, 0))
    return pl.pallas_call(
        _merge_kernel,
        grid=(m // tm,),
        in_specs=[
            o_spec, o_spec, o_spec,
            pl.BlockSpec((tm, N_GATES), lambda i: (i, 0)),
            x_spec,
            pl.BlockSpec((None, None, 6, D_MODEL), lambda i: (l, mod_row(i), 0, 0)),
            pl.BlockSpec((None, None, 1, D_MODEL), lambda i: (l, 0, 0, 0)),
            pl.BlockSpec((None, None, 1, D_MODEL), lambda i: (l, 1, 0, 0)),
            _resident((None, N_BRANCH, BRANCH_DIM, D_MODEL), lambda i: (l, 0, 0, 0)),
            _resident((None, D_MODEL, D_MODEL), lambda i: (l, 0, 0)),
        ],
        out_specs=[x_spec, x_spec],
        out_shape=[jax.ShapeDtypeStruct((m, D_MODEL), F32), jax.ShapeDtypeStruct((m, D_MODEL), BF16)],
        compiler_params=_cparams(("parallel",)),
        name="merge",
    )(oa, ob, oc, main, x, mods, g_post, g_pre, w_branch, w_out)


def _ffn_kernel(h_ref, wg_ref, wu_ref, wo_ref, x_ref, mod_ref, gpost_ref, o_ref, acc_ref):
    k = pl.program_id(1)

    @pl.when(k == 0)
    def _():
        acc_ref[...] = jnp.zeros_like(acc_ref)

    h = h_ref[...]
    gate = _dot(h, wg_ref[...])
    up = _dot(h, wu_ref[...])
    act = _silu(gate) * up
    acc_ref[...] += _dot(act.astype(BF16), wo_ref[...])

    @pl.when(k == pl.num_programs(1) - 1)
    def _():
        o_ref[...] = x_ref[...] + mod_ref[5:6, :] * _rms(acc_ref[...], gpost_ref[...])


def _ffn(h2, x1, mods, l, g_post, w_ffn_in, w_ffn_out, *, tm, tf, mod_row):
    m = x1.shape[0]
    nk = D_FF // tf
    x_spec = pl.BlockSpec((tm, D_MODEL), lambda i, k: (i, 0))
    return pl.pallas_call(
        _ffn_kernel,
        grid=(m // tm, nk),
        in_specs=[
            x_spec,
            pl.BlockSpec((None, D_MODEL, tf), lambda i, k: (l, 0, k)),
            pl.BlockSpec((None, D_MODEL, tf), lambda i, k: (l, 0, k + nk)),
            pl.BlockSpec((None, tf, D_MODEL), lambda i, k: (l, k, 0)),
            x_spec,
            pl.BlockSpec((None, None, 6, D_MODEL), lambda i, k: (l, mod_row(i), 0, 0)),
            pl.BlockSpec((None, None, 1, D_MODEL), lambda i, k: (l, 1, 0, 0)),
        ],
        out_specs=x_spec,
        out_shape=jax.ShapeDtypeStruct((m, D_MODEL), F32),
        scratch_shapes=[pltpu.VMEM((tm, D_MODEL), F32)],
        compiler_params=_cparams(("parallel", "arbitrary")),
        name="ffn",
    )(h2, w_ffn_in, w_ffn_in, w_ffn_out, x1, mods, g_post)


def _rope_tables(seq):
    t = jnp.arange(seq, dtype=jnp.int32)
    row = (t // GRID_W).astype(F32)
    col = (t % GRID_W).astype(F32)
    n_pairs_axis = HEAD_DIM // 4
    inv = ROPE_BASE ** (-jnp.arange(n_pairs_axis, dtype=F32) / n_pairs_axis)
    ang = jnp.concatenate([row[:, None] * inv, col[:, None] * inv], axis=-1)
    cos, sin = jnp.cos(ang), jnp.sin(ang)
    cos_l = jnp.tile(cos, (1, LANES // cos.shape[1]))
    sin_l = jnp.tile(jnp.concatenate([-sin, sin], axis=-1), (1, LANES // HEAD_DIM))
    return cos_l, sin_l


def _permute_w_in(w_in):
    sizes = [BRANCH_DIM, KV_A, KV_A, BRANCH_DIM, BRANCH_DIM, BRANCH_DIM, BRANCH_DIM]
    qa, ka, va, qb, kb, vb, uc, gates = jnp.split(w_in, np.cumsum(sizes).tolist(), axis=-1)
    w_main = jnp.concatenate([gates, qa, qb, uc], axis=-1).astype(BF16)
    w_kv = jnp.concatenate([kb, vb, ka, va], axis=-1).astype(BF16)
    return w_main, w_kv


def kernel(x_prompt, x_sample, cache_a_k, cache_a_v, cache_b_k, cache_b_v, c, c_ctx, w_ada, b_ada,
           norm_pre, norm_post, w_in, a_sink, b_rpb, w_branch, w_out, w_ffn_in, w_ffn_out):
    batch, seq, _ = x_prompt.shape
    dec_batch, dec_seq, _ = x_sample.shape
    past = cache_a_k.shape[2]
    assert dec_batch <= CTX_MOD_ROW and seq % A_BLOCK == 0 and dec_seq % A_BLOCK == 0

    cvec = jnp.concatenate(
        [c, c_ctx[None, :], jnp.zeros((MOD_ROWS - dec_batch - 1, D_MODEL), F32)], axis=0)
    mods = _modulation(cvec, w_ada, b_ada).reshape(DEPTH, MOD_ROWS, 6, D_MODEL)

    w_main, w_kv = _permute_w_in(w_in)
    w_branch_b = w_branch.astype(BF16)
    w_out_b = w_out.astype(BF16)
    w_ffn_in_b = w_ffn_in.astype(BF16)
    w_ffn_out_b = w_ffn_out.astype(BF16)
    g_pre = norm_pre.reshape(DEPTH, 2, 1, D_MODEL)
    g_post = norm_post.reshape(DEPTH, 2, 1, D_MODEL)
    rope_tabs = _rope_tables(dec_seq)
    cak = cache_a_k.reshape(dec_batch, DEPTH, past, KV_A)
    cav = cache_a_v.reshape(dec_batch, DEPTH, past, KV_A)
    cbk = cache_b_k.reshape(dec_batch, DEPTH, past, BRANCH_DIM)
    cbv = cache_b_v.reshape(dec_batch, DEPTH, past, BRANCH_DIM)

    tm_proj, tn_proj, tm_merge, tm_ffn, tf = 1024, N_MAIN // 3, 512, 512, D_FF // 2
    assert tn_proj == N_GATES // 2

    def layer_tail(l, x, main, oa, ob, oc, mod_row_fn):
        x1, h2 = _merge(oa, ob, oc, main, x, mods, l, g_post, g_pre, w_branch_b, w_out_b,
                        tm=tm_merge, mod_row=mod_row_fn(tm_merge))
        return _ffn(h2, x1, mods, l, g_post, w_ffn_in_b, w_ffn_out_b,
                    tm=tm_ffn, tf=tf, mod_row=mod_row_fn(tm_ffn))

    ctx_row = lambda tm: (lambda i: CTX_MOD_ROW)
    x = x_prompt.reshape(batch * seq, D_MODEL)
    caches = None
    for l in range(DEPTH):
        main, *caches = _proj(x, mods, l, g_pre, w_main, w_kv, tm=tm_proj, tn=tn_proj,
                              mod_row=ctx_row(tm_proj), seq=seq, caches=caches)
        oa, ob = _attn_ctx(main, caches, a_sink[l], l, seq)
        oc = _fourier(main, seq)
        x = layer_tail(l, x, main, oa, ob, oc, ctx_row)
    y_prompt = x.reshape(batch, seq, D_MODEL)
    nak, nav, nbk, nbv = caches
    new_a_k = nak.reshape(batch, DEPTH, seq, A_KV_HEADS, HEAD_DIM)
    new_a_v = nav.reshape(batch, DEPTH, seq, A_KV_HEADS, HEAD_DIM)
    new_b_k = nbk.reshape(batch, DEPTH, seq, B_HEADS, HEAD_DIM)
    new_b_v = nbv.reshape(batch, DEPTH, seq, B_HEADS, HEAD_DIM)

    lat_row = lambda tm: (lambda i: (i * tm) // dec_seq)
    x = x_sample.reshape(dec_batch * dec_seq, D_MODEL)
    bias = _neighbourhood_bias(b_rpb, dec_seq // GRID_W)
    for l in range(DEPTH):
        main, kv = _proj(x, mods, l, g_pre, w_main, w_kv, tm=tm_proj, tn=tn_proj,
                         mod_row=lat_row(tm_proj), seq=dec_seq, rope_tabs=rope_tabs)
        oa = _attn_a_lat(main, kv, cak, cav, a_sink[l], l, dec_seq)
        ob = _attn_b_lat(main, kv, cbk, cbv, bias, l, dec_seq)
        oc = _fourier(main, dec_seq)
        x = layer_tail(l, x, main, oa, ob, oc, lat_row)
    y_sample = x.reshape(dec_batch, dec_seq, D_MODEL)
    return (y_prompt, y_sample, new_a_k, new_a_v, new_b_k, new_b_v)
```

```python
import functools

import numpy as np
import jax
import jax.numpy as jnp
from jax import lax
from jax.experimental import pallas as pl
from jax.experimental.pallas import tpu as pltpu

F32 = jnp.float32
BF16 = jnp.bfloat16

D_MODEL = 1024
DEPTH = 2
GRID_W = 64
HEAD_DIM = 64
BRANCH_DIM = D_MODEL // 2
A_Q_HEADS = BRANCH_DIM // HEAD_DIM
A_KV_HEADS = A_Q_HEADS // 4
A_GROUP = A_Q_HEADS // A_KV_HEADS
A_BLOCK = 128
B_HEADS = BRANCH_DIM // HEAD_DIM
B_WIN_ROWS = 8
B_WIN_COLS = 16
C_GROUPS = 4
C_GROUP_DIM = BRANCH_DIM // C_GROUPS
N_BRANCH = 3
D_FF = -(-8 * D_MODEL // (3 * 256)) * 256
ROPE_BASE = 10000.0
NORM_EPS = 1e-6
NEG_INF = -1e30
SCALE = HEAD_DIM ** -0.5

LANES = 128
KV_A = A_KV_HEADS * HEAD_DIM
N_GATES = N_BRANCH * D_MODEL
W_QA = 0
W_KA = W_QA + BRANCH_DIM
W_VA = W_KA + KV_A
W_QB = W_VA + KV_A
W_KB = W_QB + BRANCH_DIM
W_VB = W_KB + BRANCH_DIM
W_UC = W_VB + BRANCH_DIM
W_GATES = W_UC + BRANCH_DIM
D_IN = W_GATES + N_GATES
QX_QA, QX_QB, QX_KVA = 0, BRANCH_DIM, 2 * BRANCH_DIM
KVU_KB, KVU_VB, KVU_UC = 0, BRANCH_DIM, 2 * BRANCH_DIM
MOD_ROWS = 8
CTX_MOD_ROW = 4
VMEM_LIMIT = 56 * 1024 * 1024


def _cparams(sem):
    return pltpu.CompilerParams(dimension_semantics=sem, vmem_limit_bytes=VMEM_LIMIT)


def _resident(shape, index_map):
    return pl.BlockSpec(shape, index_map, pipeline_mode=pl.Buffered(1))


def _rms(x, g):
    return x * lax.rsqrt(jnp.mean(x * x, axis=-1, keepdims=True) + NORM_EPS) * g


def _sigmoid(x):
    return 0.5 * jnp.tanh(0.5 * x) + 0.5


def _silu(x):
    half = 0.5 * x
    return half + half * jnp.tanh(half)


def _dot_t(a, b):
    return lax.dot_general(a, b, (((1,), (1,)), ((), ())), preferred_element_type=F32)


def _dot(a, b):
    return jnp.dot(a, b, preferred_element_type=F32)


def _scaled(q):
    assert np.log2(SCALE) == round(np.log2(SCALE))
    return q * jnp.asarray(SCALE, q.dtype)


def _low_lanes(shape):
    return lax.broadcasted_iota(jnp.int32, shape, len(shape) - 1) < HEAD_DIM


def _mod_kernel(cv_ref, w_ref, b_ref, o_ref):
    o_ref[...] = jnp.dot(_silu(cv_ref[...]), w_ref[...], preferred_element_type=F32,
                         precision=lax.Precision.HIGHEST) + b_ref[...]


def _modulation(cvec, w_ada, b_ada):
    tn = 1536
    n = 6 * D_MODEL
    return pl.pallas_call(
        _mod_kernel,
        grid=(DEPTH, n // tn),
        in_specs=[
            pl.BlockSpec((MOD_ROWS, D_MODEL), lambda l, j: (0, 0)),
            pl.BlockSpec((None, D_MODEL, tn), lambda l, j: (l, 0, j)),
            pl.BlockSpec((None, 1, tn), lambda l, j: (l, 0, j)),
        ],
        out_specs=pl.BlockSpec((None, MOD_ROWS, tn), lambda l, j: (l, 0, j)),
        out_shape=jax.ShapeDtypeStruct((DEPTH, MOD_ROWS, n), F32),
        compiler_params=_cparams(("parallel", "parallel")),
        name="modulation",
    )(cvec, w_ada, b_ada.reshape(DEPTH, 1, n))


def _rope_cols(v, cos, sin):
    first = (lax.broadcasted_iota(jnp.int32, (v.shape[0], LANES), 1) % HEAD_DIM) < HEAD_DIM // 2
    outs = []
    for c in range(v.shape[1] // LANES):
        vc = v[:, c * LANES:(c + 1) * LANES]
        partner = jnp.where(first, pltpu.roll(vc, LANES - HEAD_DIM // 2, 1),
                            pltpu.roll(vc, HEAD_DIM // 2, 1))
        outs.append(vc * cos + partner * sin)
    return outs[0] if len(outs) == 1 else jnp.concatenate(outs, axis=1)


def _with_rope(acc, lo, hi, cos_ref, sin_ref):
    parts = []
    if lo > 0:
        parts.append(acc[:, :lo])
    parts.append(_rope_cols(acc[:, lo:hi], cos_ref[...], sin_ref[...]))
    if hi < acc.shape[1]:
        parts.append(acc[:, hi:])
    return parts[0] if len(parts) == 1 else jnp.concatenate(parts, axis=1)


def _proj_kernel(*refs, latent, n_alias):
    x_ref, mod_ref, g_ref, w_ref = refs[:4]
    if latent:
        cos_ref, sin_ref, qx_ref, kvu_ref, h_scr = refs[4:]
    else:
        qx_ref, kvu_ref, ka_ref, va_ref, kb_ref, vb_ref, h_scr = refs[4 + n_alias:]
    j = pl.program_id(1)

    @pl.when(j == 0)
    def _():
        h = _rms(x_ref[...], g_ref[...]) * (1.0 + mod_ref[1:2, :]) + mod_ref[0:1, :]
        h_scr[...] = h.astype(BF16)
        acc = _dot(h_scr[...], w_ref[:, W_QA:W_KB])
        qa = acc[:, W_QA:W_QA + BRANCH_DIM]
        qb = acc[:, W_QB:W_QB + BRANCH_DIM]
        kva = acc[:, W_KA:W_KA + 2 * KV_A]
        if latent:
            qa = _rope_cols(qa, cos_ref[...], sin_ref[...])
            kva = _with_rope(kva, 0, KV_A, cos_ref, sin_ref)
            qx_ref[:, QX_KVA:QX_KVA + 2 * KV_A] = kva.astype(qx_ref.dtype)
        else:
            ka_ref[...] = kva[:, :KV_A].reshape(ka_ref.shape)
            va_ref[...] = kva[:, KV_A:].reshape(va_ref.shape)
        qx_ref[:, QX_QA:QX_QA + BRANCH_DIM] = qa.astype(qx_ref.dtype)
        qx_ref[:, QX_QB:QX_QB + BRANCH_DIM] = qb.astype(qx_ref.dtype)

    @pl.when(j == 1)
    def _():
        acc = _dot(h_scr[...], w_ref[:, W_KB:W_GATES])
        kb = acc[:, 0:BRANCH_DIM]
        vb = acc[:, W_VB - W_KB:W_VB - W_KB + BRANCH_DIM]
        uc = acc[:, W_UC - W_KB:W_UC - W_KB + BRANCH_DIM]
        if latent:
            kvu_ref[...] = acc.astype(kvu_ref.dtype)
        else:
            kb_ref[...] = kb.reshape(kb_ref.shape)
            vb_ref[...] = vb.reshape(vb_ref.shape)
            kvu_ref[...] = uc.astype(kvu_ref.dtype)


def _proj(x, mods, l, g_pre, w_in, *, tm, mod_row, seq, rope_tabs=None, caches=None):
    m = x.shape[0]
    latent = rope_tabs is not None
    in_specs = [
        pl.BlockSpec((tm, D_MODEL), lambda i, j: (i, 0)),
        pl.BlockSpec((None, None, 6, D_MODEL), lambda i, j: (l, mod_row(i), 0, 0)),
        pl.BlockSpec((None, None, 1, D_MODEL), lambda i, j: (l, 0, 0, 0)),
        _resident((None, D_MODEL, D_IN), lambda i, j: (l, 0, 0)),
    ]
    args = [x, mods, g_pre, w_in]
    aliases = {}
    if latent:
        seq_tiles = seq // tm
        tab_spec = pl.BlockSpec((tm, LANES), lambda i, j: (i % seq_tiles, 0))
        in_specs += [tab_spec, tab_spec]
        args += list(rope_tabs)
        widths = (2 * BRANCH_DIM + 2 * KV_A, 3 * BRANCH_DIM)
        cache_specs, cache_shapes = [], []
    else:
        widths = (2 * BRANCH_DIM, BRANCH_DIM)
        cache_widths = (KV_A, KV_A, BRANCH_DIM, BRANCH_DIM)
        if caches is not None:
            in_specs += [pl.BlockSpec(memory_space=pl.ANY)] * len(caches)
            aliases = {len(args) + k: 2 + k for k in range(len(caches))}
            args += list(caches)
        cache_specs = [pl.BlockSpec((tm // seq, None, seq, w), lambda i, j: (i, l, 0, 0))
                       for w in cache_widths]
        cache_shapes = [jax.ShapeDtypeStruct((m // seq, DEPTH, seq, w), F32) for w in cache_widths]
    return pl.pallas_call(
        functools.partial(_proj_kernel, latent=latent, n_alias=len(aliases)),
        grid=(m // tm, 2),
        in_specs=in_specs,
        out_specs=[pl.BlockSpec((tm, w), lambda i, j: (i, 0)) for w in widths] + cache_specs,
        out_shape=[jax.ShapeDtypeStruct((m, w), BF16) for w in widths] + cache_shapes,
        input_output_aliases=aliases,
        scratch_shapes=[pltpu.VMEM((tm, D_MODEL), BF16)],
        compiler_params=_cparams(("parallel", "arbitrary")),
        name="proj_lat" if latent else "proj_ctx",
    )(*args)


def _softmax_pv(parts, sink=None):
    m = parts[0][0].max(axis=-1, keepdims=True)
    for s, _ in parts[1:]:
        m = jnp.maximum(m, s.max(axis=-1, keepdims=True))
    if sink is not None:
        m = jnp.maximum(m, sink)
    den = None
    acc = None
    for s, v in parts:
        e = jnp.exp(s - m)
        d = e.sum(axis=-1, keepdims=True)
        o = _dot(e.astype(BF16), v)
        den = d if den is None else den + d
        acc = o if acc is None else acc + o
    if sink is not None:
        den = den + jnp.exp(sink - m)
    return acc * (1.0 / den)


def _kv_head_variants(x2):
    low = _low_lanes(x2.shape)
    xr = pltpu.roll(x2, HEAD_DIM, 1)
    zero = jnp.zeros_like(x2)
    return [
        [jnp.where(low, x2, zero).astype(BF16), jnp.where(low, zero, xr).astype(BF16)],
        [jnp.where(low, xr, zero).astype(BF16), jnp.where(low, zero, x2).astype(BF16)],
    ]


def _attn_ctx_kernel(sink_ref, qa_ref, qb_ref, ka_ref, va_ref, kb_ref, vb_ref, oa_ref, ob_ref):
    ka = [jnp.concatenate(v, axis=0) for v in _kv_head_variants(ka_ref[...])]
    va = [jnp.concatenate(v, axis=0) for v in _kv_head_variants(va_ref[...])]
    chunks = [slice(j * LANES, (j + 1) * LANES) for j in range(A_Q_HEADS // 2)]
    work = []
    for j, cols in enumerate(chunks):
        g = (2 * j) // A_GROUP
        s2 = _dot_t(_scaled(qa_ref[:, cols]), ka[g])
        work.append((oa_ref, cols, s2, va[g], (sink_ref[2 * j], sink_ref[2 * j + 1])))
    low = _low_lanes((kb_ref.shape[0], LANES))
    for cols in chunks:
        kc, vc = kb_ref[:, cols], vb_ref[:, cols]
        zero = jnp.zeros_like(kc)
        k2 = jnp.concatenate([jnp.where(low, kc, zero), jnp.where(low, zero, kc)], axis=0).astype(BF16)
        v2 = jnp.concatenate([jnp.where(low, vc, zero), jnp.where(low, zero, vc)], axis=0).astype(BF16)
        work.append((ob_ref, cols, _dot_t(_scaled(qb_ref[:, cols]), k2), v2, None))
    for o_ref, cols, s2, v2, sinks in work:
        o_ref[:, cols] = _pair_softmax_pv(s2, v2, sinks).astype(o_ref.dtype)


def _pair_softmax_pv(s2, v2, sinks=None):
    tk = s2.shape[1] // 2
    es, invs = [], []
    for p in range(2):
        s = s2[:, p * tk:(p + 1) * tk]
        m = s.max(axis=-1, keepdims=True)
        if sinks is not None:
            m = jnp.maximum(m, sinks[p])
        e = jnp.exp(s - m)
        den = e.sum(axis=-1, keepdims=True)
        if sinks is not None:
            den = den + jnp.exp(sinks[p] - m)
        es.append(e.astype(BF16))
        invs.append(1.0 / den)
    o = _dot(jnp.concatenate(es, axis=1), v2)
    return o * jnp.where(_low_lanes(o.shape), invs[0], invs[1])


def _attn_ctx(main, caches, sink, l, seq):
    m = main.shape[0]
    out = jax.ShapeDtypeStruct((m, BRANCH_DIM), BF16)
    cache_specs = [pl.BlockSpec((None, None, seq, c.shape[-1]), lambda b: (b, l, 0, 0)) for c in caches]
    return pl.pallas_call(
        _attn_ctx_kernel,
        grid=(m // seq,),
        in_specs=[
            pl.BlockSpec(memory_space=pltpu.SMEM),
            pl.BlockSpec((seq, BRANCH_DIM), lambda b: (b, QX_QA // BRANCH_DIM)),
            pl.BlockSpec((seq, BRANCH_DIM), lambda b: (b, QX_QB // BRANCH_DIM)),
        ] + cache_specs,
        out_specs=[pl.BlockSpec((seq, BRANCH_DIM), lambda b: (b, 0))] * 2,
        out_shape=[out, out],
        compiler_params=_cparams(("parallel",)),
        name="attn_ctx",
    )(sink, main, main, *caches)


def _attn_a_lat_kernel(sink_ref, q_ref, kv_ref, ck_ref, cv_ref, o_ref,
                       qh_scr, kp_scr, vp_scr, ckb_scr, cvb_scr, *, n_blocks):
    seq = n_blocks * A_BLOCK
    band = 3 * A_BLOCK
    low_seq = _low_lanes((seq, LANES))
    for h in range(A_Q_HEADS):
        j, p, g = h // 2, h % 2, h // A_GROUP
        x = q_ref[:, j * LANES:(j + 1) * LANES].astype(F32) * SCALE
        if p != g:
            x = pltpu.roll(x, HEAD_DIM, 1)
        qh_scr[h] = jnp.where(low_seq if g == 0 else jnp.logical_not(low_seq), x, 0.0).astype(BF16)
    pad = jnp.zeros((A_BLOCK, LANES), BF16)
    for scr, col in ((kp_scr, 0), (vp_scr, KV_A)):
        scr[0:A_BLOCK, :] = pad
        scr[A_BLOCK:A_BLOCK + seq, :] = kv_ref[:, col:col + KV_A]
        scr[A_BLOCK + seq:2 * A_BLOCK + seq, :] = pad
    ckb_scr[...] = ck_ref[...].astype(BF16)
    cvb_scr[...] = cv_ref[...].astype(BF16)

    rows = A_GROUP * A_BLOCK
    row = lax.broadcasted_iota(jnp.int32, (rows, band), 0)
    qi = row % A_BLOCK
    kj = lax.broadcasted_iota(jnp.int32, (rows, band), 1)
    head_row = lax.broadcasted_iota(jnp.int32, (rows, 1), 0) // A_BLOCK
    low_blk = _low_lanes((A_BLOCK, LANES))

    def body(n, carry):
        start = pl.multiple_of(n * A_BLOCK, A_BLOCK)
        no_prev = jnp.where(n > 0, 0, 1)
        no_next = jnp.where(n < n_blocks - 1, 0, 1)
        valid = (kj >= qi + no_prev * (A_BLOCK - qi)) & (kj <= 2 * A_BLOCK + qi - no_next * (qi + 1))
        scores = []
        for g in range(A_KV_HEADS):
            q = jnp.concatenate(
                [qh_scr[A_GROUP * g + i, pl.ds(start, A_BLOCK), :] for i in range(A_GROUP)], axis=0)
            s_band = jnp.where(valid, _dot_t(q, kp_scr[pl.ds(start, band), :]), NEG_INF)
            scores.append((s_band, _dot_t(q, ckb_scr[...])))
        outs = []
        for g, (s_band, s_ctx) in enumerate(scores):
            sink = jnp.full((rows, 1), sink_ref[A_GROUP * g], F32)
            for i in range(1, A_GROUP):
                sink = jnp.where(head_row == i, sink_ref[A_GROUP * g + i], sink)
            outs.append(_softmax_pv([(s_band, vp_scr[pl.ds(start, band), :]), (s_ctx, cvb_scr[...])],
                                    sink=sink))
        for j in range(A_Q_HEADS // 2):
            halves = []
            for p in range(2):
                h = 2 * j + p
                g, i = h // A_GROUP, h % A_GROUP
                t = outs[g][i * A_BLOCK:(i + 1) * A_BLOCK]
                halves.append(t if p == g else pltpu.roll(t, HEAD_DIM, 1))
            o_ref[pl.ds(start, A_BLOCK), j * LANES:(j + 1) * LANES] = (
                jnp.where(low_blk, halves[0], halves[1]).astype(o_ref.dtype))
        return carry

    lax.fori_loop(0, n_blocks, body, 0)


def _attn_a_lat(main, kv, cache_k, cache_v, sink, l, seq):
    m = main.shape[0]
    nb = seq // A_BLOCK
    n_seq = m // seq
    past = cache_k.shape[2]
    cache_spec = pl.BlockSpec((None, None, past, KV_A), lambda b: (b, l, 0, 0))
    return pl.pallas_call(
        functools.partial(_attn_a_lat_kernel, n_blocks=nb),
        grid=(n_seq,),
        in_specs=[
            pl.BlockSpec(memory_space=pltpu.SMEM),
            pl.BlockSpec((seq, BRANCH_DIM), lambda b: (b, QX_QA // BRANCH_DIM)),
            pl.BlockSpec((seq, 2 * KV_A), lambda b: (b, QX_KVA // (2 * KV_A))),
            cache_spec, cache_spec,
        ],
        out_specs=pl.BlockSpec((seq, BRANCH_DIM), lambda b: (b, 0)),
        out_shape=jax.ShapeDtypeStruct((m, BRANCH_DIM), BF16),
        scratch_shapes=[
            pltpu.VMEM((A_Q_HEADS, seq, LANES), BF16),
            pltpu.VMEM((seq + 2 * A_BLOCK, LANES), BF16),
            pltpu.VMEM((seq + 2 * A_BLOCK, LANES), BF16),
            pltpu.VMEM((past, LANES), BF16),
            pltpu.VMEM((past, LANES), BF16),
        ],
        compiler_params=_cparams(("parallel",)),
        name="attn_a_lat",
    )(sink, main, kv, cache_k, cache_v)


def _attn_b_lat_kernel(q_ref, k_ref, v_ref, ck_ref, cv_ref, bias_ref, o_ref, ckb_scr, cvb_scr, *, rows, kr):
    ckb_scr[...] = ck_ref[...].astype(BF16)
    cvb_scr[...] = cv_ref[...].astype(BF16)
    half = kr // 2
    n_loc = kr * GRID_W
    low_q = _low_lanes((GRID_W, LANES))

    def body(r, carry):
        r0 = jnp.clip(r - half, 0, rows - kr)
        variant = r - r0
        q0 = pl.multiple_of(r * GRID_W, GRID_W)
        start = pl.multiple_of(r0 * GRID_W, GRID_W)
        chunks = [slice(j * LANES, (j + 1) * LANES) for j in range(B_HEADS // 2)]
        scores = []
        for j, cols in enumerate(chunks):
            qc = _scaled(q_ref[pl.ds(q0, GRID_W), cols])
            zero = jnp.zeros_like(qc)
            qs = jnp.concatenate([jnp.where(low_q, qc, zero), jnp.where(low_q, zero, qc)], axis=0)
            s_loc = _dot_t(qs, k_ref[pl.ds(start, n_loc), cols]) + bias_ref[variant, j]
            scores.append((s_loc, _dot_t(qs, ckb_scr[:, cols])))
        for (s_loc, s_ctx), cols in zip(scores, chunks):
            o2 = _softmax_pv([(s_loc, v_ref[pl.ds(start, n_loc), cols]), (s_ctx, cvb_scr[:, cols])])
            o = jnp.where(low_q, o2[:GRID_W], o2[GRID_W:])
            o_ref[pl.ds(q0, GRID_W), cols] = o.astype(o_ref.dtype)
        return carry

    lax.fori_loop(0, rows, body, 0)


def _attn_b_lat(main, kv, cache_k, cache_v, bias, l, seq):
    m = main.shape[0]
    rows = seq // GRID_W
    kr = min(B_WIN_ROWS, rows)
    n_seq = m // seq
    past = cache_k.shape[2]
    cache_spec = pl.BlockSpec((None, None, past, BRANCH_DIM), lambda b: (b, l, 0, 0))
    return pl.pallas_call(
        functools.partial(_attn_b_lat_kernel, rows=rows, kr=kr),
        grid=(n_seq,),
        in_specs=[
            pl.BlockSpec((seq, BRANCH_DIM), lambda b: (b, QX_QB // BRANCH_DIM)),
            pl.BlockSpec((seq, BRANCH_DIM), lambda b: (b, KVU_KB // BRANCH_DIM)),
            pl.BlockSpec((seq, BRANCH_DIM), lambda b: (b, KVU_VB // BRANCH_DIM)),
            cache_spec, cache_spec,
            _resident((None,) + bias.shape[1:], lambda b: (l, 0, 0, 0, 0)),
        ],
        out_specs=pl.BlockSpec((seq, BRANCH_DIM), lambda b: (b, 0)),
        out_shape=jax.ShapeDtypeStruct((m, BRANCH_DIM), BF16),
        scratch_shapes=[pltpu.VMEM((past, BRANCH_DIM), BF16)] * 2,
        compiler_params=_cparams(("parallel",)),
        name="attn_b_lat",
    )(main, kv, kv, cache_k, cache_v, bias)


def _neighbourhood_bias(rpb, rows):
    kr = min(B_WIN_ROWS, rows)
    assert kr % 2 == 0 and 2 * GRID_W == LANES
    depth, heads, n_dr, n_dc = rpb.shape
    rpb_pad = jnp.pad(rpb.astype(F32), ((0, 0), (0, 0), (0, (-n_dr) % 8), (0, LANES - n_dc)))
    return pl.pallas_call(
        functools.partial(_bias_kernel, kr=kr),
        grid=(depth, heads),
        in_specs=[pl.BlockSpec((None, None, rpb_pad.shape[2], LANES), lambda l, h: (l, h, 0, 0))],
        out_specs=pl.BlockSpec((None, kr, None, GRID_W, kr * GRID_W), lambda l, h: (l, 0, h // 2, h % 2, 0)),
        out_shape=jax.ShapeDtypeStruct((depth, kr, heads // 2, 2 * GRID_W, kr * GRID_W), F32),
        compiler_params=_cparams(("parallel", "parallel")),
        name="nbr_bias",
    )(rpb_pad)


def _bias_kernel(rpb_ref, o_ref, *, kr):
    shape = (GRID_W, LANES)
    c = lax.broadcasted_iota(jnp.int32, shape, 0)
    lane = lax.broadcasted_iota(jnp.int32, shape, 1)
    c2 = lane % GRID_W
    ws = jnp.clip(c - B_WIN_COLS // 2, 0, GRID_W - B_WIN_COLS)
    ok = (c2 >= ws) & (c2 < ws + B_WIN_COLS)
    low = lane < GRID_W

    def toeplitz(dr, lane0):
        row = jnp.broadcast_to(rpb_ref[dr:dr + 1, :], shape)
        return pltpu.roll(row, (lane0 - (B_WIN_COLS - 1)) % LANES, 1, stride=1, stride_axis=0)

    pairs = {}
    for v in range(kr):
        for m in range(0, kr, 2):
            dr = m - v + B_WIN_ROWS - 1
            if dr not in pairs:
                pair = jnp.where(low, toeplitz(dr, 0), toeplitz(dr + 1, GRID_W))
                pairs[dr] = jnp.where(ok, pair, NEG_INF)
            o_ref[v, :, m * GRID_W:(m + 2) * GRID_W] = pairs[dr]


def _dft_tables(seq):
    cd = C_GROUP_DIM
    kc = (np.arange(cd)[:, None] * np.arange(cd)[None, :]) % cd
    ang_c = 2.0 * np.pi * kc / cd
    eye2 = np.eye(2)
    bd_cos = np.kron(eye2, np.cos(ang_c))
    bd_sin = np.kron(eye2, np.sin(ang_c))
    kt = (np.arange(seq)[:, None] * np.arange(seq)[None, :]) % seq
    ang_t = 2.0 * np.pi * kt / seq
    norm = 1.0 / np.sqrt(float(seq * cd))
    pos = np.concatenate([np.cos(ang_t), -np.sin(ang_t)], axis=1) * norm
    return (jnp.asarray(bd_cos, F32).astype(BF16), jnp.asarray(bd_sin, F32).astype(BF16),
            jnp.asarray(pos, F32).astype(BF16))


def _fourier_kernel(u_ref, bc_ref, bs_ref, pos_ref, o_ref, *, seq):
    pair = 2 * C_GROUP_DIM
    uc, us = [], []
    for p in range(BRANCH_DIM // pair):
        up = u_ref[:, p * pair:(p + 1) * pair]
        uc.append(_dot(up, bc_ref[...]))
        us.append(_dot(up, bs_ref[...]))
    zc = jnp.concatenate(uc, axis=1).astype(BF16)
    zs = jnp.concatenate(us, axis=1).astype(BF16)
    for s in range(u_ref.shape[0] // seq):
        rows = slice(s * seq, (s + 1) * seq)
        z = jnp.concatenate([zc[rows], zs[rows]], axis=0)
        o_ref[rows, :] = _dot(pos_ref[...], z).astype(o_ref.dtype)


def _fourier(kvu, seq, tm):
    m = kvu.shape[0]
    bc, bs, pos = _dft_tables(seq)
    pair = 2 * C_GROUP_DIM
    uc_block = kvu.shape[1] // BRANCH_DIM - 1
    return pl.pallas_call(
        functools.partial(_fourier_kernel, seq=seq),
        grid=(m // tm,),
        in_specs=[
            pl.BlockSpec((tm, BRANCH_DIM), lambda b: (b, uc_block)),
            _resident((pair, pair), lambda b: (0, 0)),
            _resident((pair, pair), lambda b: (0, 0)),
            _resident((seq, 2 * seq), lambda b: (0, 0)),
        ],
        out_specs=pl.BlockSpec((tm, BRANCH_DIM), lambda b: (b, 0)),
        out_shape=jax.ShapeDtypeStruct((m, BRANCH_DIM), BF16),
        compiler_params=_cparams(("parallel",)),
        name="fourier",
    )(kvu, bc, bs, pos)


def _merge_kernel(oa_ref, ob_ref, oc_ref, x_ref, mod_ref, gpre0_ref, gpost_ref, gpre1_ref,
                  win_ref, wb_ref, wo_ref, x1_ref, h2_ref):
    x = x_ref[...]
    h = (_rms(x, gpre0_ref[...]) * (1.0 + mod_ref[1:2, :]) + mod_ref[0:1, :]).astype(BF16)
    mix = None
    for i, o_ref in enumerate((oa_ref, ob_ref, oc_ref)):
        lo = W_GATES + i * D_MODEL
        gate = _sigmoid(_dot(h, win_ref[:, lo:lo + D_MODEL]))
        term = gate * _dot(o_ref[...], wb_ref[i])
        mix = term if mix is None else mix + term
    y = _dot(mix.astype(BF16), wo_ref[...])
    x1 = x + mod_ref[2:3, :] * _rms(y, gpost_ref[...])
    x1_ref[...] = x1
    h2 = _rms(x1, gpre1_ref[...]) * (1.0 + mod_ref[4:5, :]) + mod_ref[3:4, :]
    h2_ref[...] = h2.astype(BF16)


def _merge(oa, ob, oc, x, mods, l, g_post, g_pre, w_in, w_branch, w_out, *, tm, mod_row):
    m = x.shape[0]
    o_spec = pl.BlockSpec((tm, BRANCH_DIM), lambda i: (i, 0))
    x_spec = pl.BlockSpec((tm, D_MODEL), lambda i: (i, 0))
    gain_spec = lambda which: pl.BlockSpec((None, None, 1, D_MODEL), lambda i: (l, which, 0, 0))
    return pl.pallas_call(
        _merge_kernel,
        grid=(m // tm,),
        in_specs=[
            o_spec, o_spec, o_spec,
            x_spec,
            pl.BlockSpec((None, None, 6, D_MODEL), lambda i: (l, mod_row(i), 0, 0)),
            gain_spec(0), gain_spec(0), gain_spec(1),
            _resident((None, D_MODEL, D_IN), lambda i: (l, 0, 0)),
            _resident((None, N_BRANCH, BRANCH_DIM, D_MODEL), lambda i: (l, 0, 0, 0)),
            _resident((None, D_MODEL, D_MODEL), lambda i: (l, 0, 0)),
        ],
        out_specs=[x_spec, x_spec],
        out_shape=[jax.ShapeDtypeStruct((m, D_MODEL), F32), jax.ShapeDtypeStruct((m, D_MODEL), BF16)],
        compiler_params=_cparams(("parallel",)),
        name="merge",
    )(oa, ob, oc, x, mods, g_pre, g_post, g_pre, w_in, w_branch, w_out)


def _ffn_kernel(h_ref, wg_ref, wu_ref, wo_ref, x_ref, mod_ref, gpost_ref, o_ref, acc_ref):
    k = pl.program_id(1)

    @pl.when(k == 0)
    def _():
        acc_ref[...] = jnp.zeros_like(acc_ref)

    h = h_ref[...]
    gate = _dot(h, wg_ref[...])
    up = _dot(h, wu_ref[...])
    act = _silu(gate) * up
    acc_ref[...] += _dot(act.astype(BF16), wo_ref[...])

    @pl.when(k == pl.num_programs(1) - 1)
    def _():
        o_ref[...] = x_ref[...] + mod_ref[5:6, :] * _rms(acc_ref[...], gpost_ref[...])


def _ffn(h2, x1, mods, l, g_post, w_ffn_in, w_ffn_out, *, tm, tf, mod_row):
    m = x1.shape[0]
    nk = D_FF // tf
    x_spec = pl.BlockSpec((tm, D_MODEL), lambda i, k: (i, 0))
    return pl.pallas_call(
        _ffn_kernel,
        grid=(m // tm, nk),
        in_specs=[
            x_spec,
            pl.BlockSpec((None, D_MODEL, tf), lambda i, k: (l, 0, k)),
            pl.BlockSpec((None, D_MODEL, tf), lambda i, k: (l, 0, k + nk)),
            pl.BlockSpec((None, tf, D_MODEL), lambda i, k: (l, k, 0)),
            x_spec,
            pl.BlockSpec((None, None, 6, D_MODEL), lambda i, k: (l, mod_row(i), 0, 0)),
            pl.BlockSpec((None, None, 1, D_MODEL), lambda i, k: (l, 1, 0, 0)),
        ],
        out_specs=x_spec,
        out_shape=jax.ShapeDtypeStruct((m, D_MODEL), F32),
        scratch_shapes=[pltpu.VMEM((tm, D_MODEL), F32)],
        compiler_params=_cparams(("parallel", "arbitrary")),
        name="ffn",
    )(h2, w_ffn_in, w_ffn_in, w_ffn_out, x1, mods, g_post)


def _rope_tables(seq):
    t = jnp.arange(seq, dtype=jnp.int32)
    row = (t // GRID_W).astype(F32)
    col = (t % GRID_W).astype(F32)
    n_pairs_axis = HEAD_DIM // 4
    inv = ROPE_BASE ** (-jnp.arange(n_pairs_axis, dtype=F32) / n_pairs_axis)
    ang = jnp.concatenate([row[:, None] * inv, col[:, None] * inv], axis=-1)
    cos, sin = jnp.cos(ang), jnp.sin(ang)
    cos_l = jnp.tile(cos, (1, LANES // cos.shape[1]))
    sin_l = jnp.tile(jnp.concatenate([-sin, sin], axis=-1), (1, LANES // HEAD_DIM))
    return cos_l, sin_l


def kernel(x_prompt, x_sample, cache_a_k, cache_a_v, cache_b_k, cache_b_v, c, c_ctx, w_ada, b_ada,
           norm_pre, norm_post, w_in, a_sink, b_rpb, w_branch, w_out, w_ffn_in, w_ffn_out):
    batch, seq, _ = x_prompt.shape
    dec_batch, dec_seq, _ = x_sample.shape
    past = cache_a_k.shape[2]
    assert dec_batch <= CTX_MOD_ROW and seq % A_BLOCK == 0 and dec_seq % A_BLOCK == 0

    cvec = jnp.concatenate(
        [c, c_ctx[None, :], jnp.zeros((MOD_ROWS - dec_batch - 1, D_MODEL), F32)], axis=0)
    mods = _modulation(cvec, w_ada, b_ada).reshape(DEPTH, MOD_ROWS, 6, D_MODEL)

    assert w_in.shape[-1] == D_IN
    w_in_b = w_in.astype(BF16)
    w_branch_b = w_branch.astype(BF16)
    w_out_b = w_out.astype(BF16)
    w_ffn_in_b = w_ffn_in.astype(BF16)
    w_ffn_out_b = w_ffn_out.astype(BF16)
    g_pre = norm_pre.reshape(DEPTH, 2, 1, D_MODEL)
    g_post = norm_post.reshape(DEPTH, 2, 1, D_MODEL)
    rope_tabs = _rope_tables(dec_seq)
    cak = cache_a_k.reshape(dec_batch, DEPTH, past, KV_A)
    cav = cache_a_v.reshape(dec_batch, DEPTH, past, KV_A)
    cbk = cache_b_k.reshape(dec_batch, DEPTH, past, BRANCH_DIM)
    cbv = cache_b_v.reshape(dec_batch, DEPTH, past, BRANCH_DIM)

    tm_proj, tm_merge, tm_ffn, tf = 1024, 512, 512, D_FF // 2

    def layer_tail(l, x, oa, ob, oc, mod_row_fn):
        x1, h2 = _merge(oa, ob, oc, x, mods, l, g_post, g_pre, w_in_b, w_branch_b, w_out_b,
                        tm=tm_merge, mod_row=mod_row_fn(tm_merge))
        return _ffn(h2, x1, mods, l, g_post, w_ffn_in_b, w_ffn_out_b,
                    tm=tm_ffn, tf=tf, mod_row=mod_row_fn(tm_ffn))

    ctx_row = lambda tm: (lambda i: CTX_MOD_ROW)
    x = x_prompt.reshape(batch * seq, D_MODEL)
    caches = None
    for l in range(DEPTH):
        qx, kvu, *caches = _proj(x, mods, l, g_pre, w_in_b, tm=tm_proj,
                                 mod_row=ctx_row(tm_proj), seq=seq, caches=caches)
        oa, ob = _attn_ctx(qx, caches, a_sink[l], l, seq)
        oc = _fourier(kvu, seq, tm_proj)
        x = layer_tail(l, x, oa, ob, oc, ctx_row)
    y_prompt = x.reshape(batch, seq, D_MODEL)
    nak, nav, nbk, nbv = caches
    new_a_k = nak.reshape(batch, DEPTH, seq, A_KV_HEADS, HEAD_DIM)
    new_a_v = nav.reshape(batch, DEPTH, seq, A_KV_HEADS, HEAD_DIM)
    new_b_k = nbk.reshape(batch, DEPTH, seq, B_HEADS, HEAD_DIM)
    new_b_v = nbv.reshape(batch, DEPTH, seq, B_HEADS, HEAD_DIM)

    lat_row = lambda tm: (lambda i: (i * tm) // dec_seq)
    x = x_sample.reshape(dec_batch * dec_seq, D_MODEL)
    bias = _neighbourhood_bias(b_rpb, dec_seq // GRID_W)
    for l in range(DEPTH):
        qx, kvu = _proj(x, mods, l, g_pre, w_in_b, tm=tm_proj,
                        mod_row=lat_row(tm_proj), seq=dec_seq, rope_tabs=rope_tabs)
        oa = _attn_a_lat(qx, qx, cak, cav, a_sink[l], l, dec_seq)
        ob = _attn_b_lat(qx, kvu, cbk, cbv, bias, l, dec_seq)
        oc = _fourier(kvu, dec_seq, dec_seq)
        x = layer_tail(l, x, oa, ob, oc, lat_row)
    y_sample = x.reshape(dec_batch, dec_seq, D_MODEL)
    return (y_prompt, y_sample, new_a_k, new_a_v, new_b_k, new_b_v)
```

```python
import functools

import numpy as np
import jax
import jax.numpy as jnp
from jax import lax
from jax.experimental import pallas as pl
from jax.experimental.pallas import tpu as pltpu

F32 = jnp.float32
BF16 = jnp.bfloat16

D_MODEL = 1024
DEPTH = 2
GRID_W = 64
HEAD_DIM = 64
BRANCH_DIM = D_MODEL // 2
A_Q_HEADS = BRANCH_DIM // HEAD_DIM
A_KV_HEADS = A_Q_HEADS // 4
A_GROUP = A_Q_HEADS // A_KV_HEADS
A_BLOCK = 128
B_HEADS = BRANCH_DIM // HEAD_DIM
B_WIN_ROWS = 8
B_WIN_COLS = 16
C_GROUPS = 4
C_GROUP_DIM = BRANCH_DIM // C_GROUPS
N_BRANCH = 3
D_FF = -(-8 * D_MODEL // (3 * 256)) * 256
ROPE_BASE = 10000.0
NORM_EPS = 1e-6
NEG_INF = -1e30
SCALE = HEAD_DIM ** -0.5

LANES = 128
KV_A = A_KV_HEADS * HEAD_DIM
N_GATES = N_BRANCH * D_MODEL
W_QA = 0
W_KA = W_QA + BRANCH_DIM
W_VA = W_KA + KV_A
W_QB = W_VA + KV_A
W_KB = W_QB + BRANCH_DIM
W_VB = W_KB + BRANCH_DIM
W_UC = W_VB + BRANCH_DIM
W_GATES = W_UC + BRANCH_DIM
D_IN = W_GATES + N_GATES
QX_QA, QX_QB, QX_KVA = 0, BRANCH_DIM, 2 * BRANCH_DIM
KVU_KB, KVU_VB, KVU_UC = 0, BRANCH_DIM, 2 * BRANCH_DIM
MOD_ROWS = 8
CTX_MOD_ROW = 4
VMEM_LIMIT = 56 * 1024 * 1024


def _cparams(sem):
    return pltpu.CompilerParams(dimension_semantics=sem, vmem_limit_bytes=VMEM_LIMIT)


def _resident(shape, index_map):
    return pl.BlockSpec(shape, index_map, pipeline_mode=pl.Buffered(1))


def _rms(x, g):
    return x * lax.rsqrt(jnp.mean(x * x, axis=-1, keepdims=True) + NORM_EPS) * g


def _sigmoid(x):
    return 0.5 * jnp.tanh(0.5 * x) + 0.5


def _silu(x):
    half = 0.5 * x
    return half + half * jnp.tanh(half)


def _dot_t(a, b):
    return lax.dot_general(a, b, (((1,), (1,)), ((), ())), preferred_element_type=F32)


def _dot(a, b):
    return jnp.dot(a, b, preferred_element_type=F32)


def _scaled(q):
    assert np.log2(SCALE) == round(np.log2(SCALE))
    return q * jnp.asarray(SCALE, q.dtype)


def _low_lanes(shape):
    return lax.broadcasted_iota(jnp.int32, shape, len(shape) - 1) < HEAD_DIM


def _mod_kernel(cv_ref, w_ref, b_ref, o_ref):
    o_ref[...] = jnp.dot(_silu(cv_ref[...]), w_ref[...], preferred_element_type=F32,
                         precision=lax.Precision.HIGHEST) + b_ref[...]


def _modulation(cvec, w_ada, b_ada):
    tn = 1536
    n = 6 * D_MODEL
    return pl.pallas_call(
        _mod_kernel,
        grid=(DEPTH, n // tn),
        in_specs=[
            pl.BlockSpec((MOD_ROWS, D_MODEL), lambda l, j: (0, 0)),
            pl.BlockSpec((None, D_MODEL, tn), lambda l, j: (l, 0, j)),
            pl.BlockSpec((None, 1, tn), lambda l, j: (l, 0, j)),
        ],
        out_specs=pl.BlockSpec((None, MOD_ROWS, tn), lambda l, j: (l, 0, j)),
        out_shape=jax.ShapeDtypeStruct((DEPTH, MOD_ROWS, n), F32),
        compiler_params=_cparams(("parallel", "parallel")),
        name="modulation",
    )(cvec, w_ada, b_ada.reshape(DEPTH, 1, n))


def _rope_cols(v, cos, sin):
    first = (lax.broadcasted_iota(jnp.int32, (v.shape[0], LANES), 1) % HEAD_DIM) < HEAD_DIM // 2
    outs = []
    for c in range(v.shape[1] // LANES):
        vc = v[:, c * LANES:(c + 1) * LANES]
        partner = jnp.where(first, pltpu.roll(vc, LANES - HEAD_DIM // 2, 1),
                            pltpu.roll(vc, HEAD_DIM // 2, 1))
        outs.append(vc * cos + partner * sin)
    return outs[0] if len(outs) == 1 else jnp.concatenate(outs, axis=1)


def _with_rope(acc, lo, hi, cos_ref, sin_ref):
    parts = []
    if lo > 0:
        parts.append(acc[:, :lo])
    parts.append(_rope_cols(acc[:, lo:hi], cos_ref[...], sin_ref[...]))
    if hi < acc.shape[1]:
        parts.append(acc[:, hi:])
    return parts[0] if len(parts) == 1 else jnp.concatenate(parts, axis=1)


def _proj_kernel(*refs, latent, n_alias):
    x_ref, mod_ref, g_ref, w_ref = refs[:4]
    if latent:
        cos_ref, sin_ref, qx_ref, kvu_ref, h_scr = refs[4:]
    else:
        qx_ref, kvu_ref, ka_ref, va_ref, kb_ref, vb_ref, h_scr = refs[4 + n_alias:]
    j = pl.program_id(1)

    @pl.when(j == 0)
    def _():
        h = _rms(x_ref[...], g_ref[...]) * (1.0 + mod_ref[1:2, :]) + mod_ref[0:1, :]
        h_scr[...] = h.astype(BF16)
        acc = _dot(h_scr[...], w_ref[:, W_QA:W_KB])
        qa = acc[:, W_QA:W_QA + BRANCH_DIM]
        qb = acc[:, W_QB:W_QB + BRANCH_DIM]
        kva = acc[:, W_KA:W_KA + 2 * KV_A]
        if latent:
            qa = _rope_cols(qa, cos_ref[...], sin_ref[...])
            kva = _with_rope(kva, 0, KV_A, cos_ref, sin_ref)
            qx_ref[:, QX_KVA:QX_KVA + 2 * KV_A] = kva.astype(qx_ref.dtype)
        else:
            ka_ref[...] = kva[:, :KV_A].reshape(ka_ref.shape)
            va_ref[...] = kva[:, KV_A:].reshape(va_ref.shape)
        qx_ref[:, QX_QA:QX_QA + BRANCH_DIM] = qa.astype(qx_ref.dtype)
        qx_ref[:, QX_QB:QX_QB + BRANCH_DIM] = qb.astype(qx_ref.dtype)

    @pl.when(j == 1)
    def _():
        acc = _dot(h_scr[...], w_ref[:, W_KB:W_GATES])
        kb = acc[:, 0:BRANCH_DIM]
        vb = acc[:, W_VB - W_KB:W_VB - W_KB + BRANCH_DIM]
        uc = acc[:, W_UC - W_KB:W_UC - W_KB + BRANCH_DIM]
        if latent:
            kvu_ref[...] = acc.astype(kvu_ref.dtype)
        else:
            kb_ref[...] = kb.reshape(kb_ref.shape)
            vb_ref[...] = vb.reshape(vb_ref.shape)
            kvu_ref[...] = uc.astype(kvu_ref.dtype)


def _proj(x, mods, l, g_pre, w_in, *, tm, mod_row, seq, rope_tabs=None, caches=None):
    m = x.shape[0]
    latent = rope_tabs is not None
    in_specs = [
        pl.BlockSpec((tm, D_MODEL), lambda i, j: (i, 0)),
        pl.BlockSpec((None, None, 6, D_MODEL), lambda i, j: (l, mod_row(i), 0, 0)),
        pl.BlockSpec((None, None, 1, D_MODEL), lambda i, j: (l, 0, 0, 0)),
        _resident((None, D_MODEL, D_IN), lambda i, j: (l, 0, 0)),
    ]
    args = [x, mods, g_pre, w_in]
    aliases = {}
    if latent:
        seq_tiles = seq // tm
        tab_spec = pl.BlockSpec((tm, LANES), lambda i, j: (i % seq_tiles, 0))
        in_specs += [tab_spec, tab_spec]
        args += list(rope_tabs)
        widths = (2 * BRANCH_DIM + 2 * KV_A, 3 * BRANCH_DIM)
        cache_specs, cache_shapes = [], []
    else:
        widths = (2 * BRANCH_DIM, BRANCH_DIM)
        cache_widths = (KV_A, KV_A, BRANCH_DIM, BRANCH_DIM)
        if caches is not None:
            in_specs += [pl.BlockSpec(memory_space=pl.ANY)] * len(caches)
            aliases = {len(args) + k: 2 + k for k in range(len(caches))}
            args += list(caches)
        cache_specs = [pl.BlockSpec((tm // seq, None, seq, w), lambda i, j: (i, l, 0, 0))
                       for w in cache_widths]
        cache_shapes = [jax.ShapeDtypeStruct((m // seq, DEPTH, seq, w), F32) for w in cache_widths]
    return pl.pallas_call(
        functools.partial(_proj_kernel, latent=latent, n_alias=len(aliases)),
        grid=(m // tm, 2),
        in_specs=in_specs,
        out_specs=[pl.BlockSpec((tm, w), lambda i, j: (i, 0)) for w in widths] + cache_specs,
        out_shape=[jax.ShapeDtypeStruct((m, w), BF16) for w in widths] + cache_shapes,
        input_output_aliases=aliases,
        scratch_shapes=[pltpu.VMEM((tm, D_MODEL), BF16)],
        compiler_params=_cparams(("parallel", "arbitrary")),
        name="proj_lat" if latent else "proj_ctx",
    )(*args)


def _softmax_pv(parts, sink=None):
    m = parts[0][0].max(axis=-1, keepdims=True)
    for s, _ in parts[1:]:
        m = jnp.maximum(m, s.max(axis=-1, keepdims=True))
    if sink is not None:
        m = jnp.maximum(m, sink)
    den = None
    acc = None
    for s, v in parts:
        e = jnp.exp(s - m)
        d = e.sum(axis=-1, keepdims=True)
        o = _dot(e.astype(BF16), v)
        den = d if den is None else den + d
        acc = o if acc is None else acc + o
    if sink is not None:
        den = den + jnp.exp(sink - m)
    return acc * (1.0 / den)


def _kv_head_variants(x2):
    low = _low_lanes(x2.shape)
    xr = pltpu.roll(x2, HEAD_DIM, 1)
    zero = jnp.zeros_like(x2)
    return [
        [jnp.where(low, x2, zero).astype(BF16), jnp.where(low, zero, xr).astype(BF16)],
        [jnp.where(low, xr, zero).astype(BF16), jnp.where(low, zero, x2).astype(BF16)],
    ]


def _attn_ctx_kernel(sink_ref, qa_ref, qb_ref, ka_ref, va_ref, kb_ref, vb_ref, *rest, n_weights):
    w_src, (oa_ref, ob_ref), w_dst = rest[:n_weights], rest[n_weights:n_weights + 2], rest[n_weights + 2:]
    for src, dst in zip(w_src, w_dst):
        dst[...] = src[...].astype(dst.dtype)
    ka = [jnp.concatenate(v, axis=0) for v in _kv_head_variants(ka_ref[...])]
    va = [jnp.concatenate(v, axis=0) for v in _kv_head_variants(va_ref[...])]
    chunks = [slice(j * LANES, (j + 1) * LANES) for j in range(A_Q_HEADS // 2)]
    work = []
    for j, cols in enumerate(chunks):
        g = (2 * j) // A_GROUP
        s2 = _dot_t(_scaled(qa_ref[:, cols]), ka[g])
        work.append((oa_ref, cols, s2, va[g], (sink_ref[2 * j], sink_ref[2 * j + 1])))
    low = _low_lanes((kb_ref.shape[0], LANES))
    for cols in chunks:
        kc, vc = kb_ref[:, cols], vb_ref[:, cols]
        zero = jnp.zeros_like(kc)
        k2 = jnp.concatenate([jnp.where(low, kc, zero), jnp.where(low, zero, kc)], axis=0).astype(BF16)
        v2 = jnp.concatenate([jnp.where(low, vc, zero), jnp.where(low, zero, vc)], axis=0).astype(BF16)
        work.append((ob_ref, cols, _dot_t(_scaled(qb_ref[:, cols]), k2), v2, None))
    for o_ref, cols, s2, v2, sinks in work:
        o_ref[:, cols] = _pair_softmax_pv(s2, v2, sinks).astype(o_ref.dtype)


def _pair_softmax_pv(s2, v2, sinks=None):
    tk = s2.shape[1] // 2
    es, invs = [], []
    for p in range(2):
        s = s2[:, p * tk:(p + 1) * tk]
        m = s.max(axis=-1, keepdims=True)
        if sinks is not None:
            m = jnp.maximum(m, sinks[p])
        e = jnp.exp(s - m)
        den = e.sum(axis=-1, keepdims=True)
        if sinks is not None:
            den = den + jnp.exp(sinks[p] - m)
        es.append(e.astype(BF16))
        invs.append(1.0 / den)
    o = _dot(jnp.concatenate(es, axis=1), v2)
    return o * jnp.where(_low_lanes(o.shape), invs[0], invs[1])


def _attn_ctx(qx, caches, sink, l, seq, weights):
    m = qx.shape[0]
    steps = m // seq
    out = jax.ShapeDtypeStruct((m, BRANCH_DIM), BF16)
    cache_specs = [pl.BlockSpec((None, None, seq, c.shape[-1]), lambda b: (b, l, 0, 0)) for c in caches]
    w_in_specs, w_out_specs, w_shapes = [], [], []
    for w in weights:
        _, rows, cols = w.shape
        assert rows % (steps * 16) == 0
        w_in_specs.append(pl.BlockSpec((None, rows // steps, cols), lambda b: (l, b, 0)))
        w_out_specs.append(pl.BlockSpec((rows // steps, cols), lambda b: (b, 0)))
        w_shapes.append(jax.ShapeDtypeStruct((rows, cols), BF16))
    return pl.pallas_call(
        functools.partial(_attn_ctx_kernel, n_weights=len(weights)),
        grid=(steps,),
        in_specs=[
            pl.BlockSpec(memory_space=pltpu.SMEM),
            pl.BlockSpec((seq, BRANCH_DIM), lambda b: (b, QX_QA // BRANCH_DIM)),
            pl.BlockSpec((seq, BRANCH_DIM), lambda b: (b, QX_QB // BRANCH_DIM)),
        ] + cache_specs + w_in_specs,
        out_specs=[pl.BlockSpec((seq, BRANCH_DIM), lambda b: (b, 0))] * 2 + w_out_specs,
        out_shape=[out, out] + w_shapes,
        compiler_params=_cparams(("parallel",)),
        name="attn_ctx",
    )(sink, qx, qx, *caches, *weights)


def _attn_a_lat_kernel(sink_ref, q_ref, kv_ref, ck_ref, cv_ref, o_ref,
                       qh_scr, kp_scr, vp_scr, ckb_scr, cvb_scr, *, n_blocks):
    seq = n_blocks * A_BLOCK
    band = 3 * A_BLOCK
    low_seq = _low_lanes((seq, LANES))
    for h in range(A_Q_HEADS):
        j, p, g = h // 2, h % 2, h // A_GROUP
        x = q_ref[:, j * LANES:(j + 1) * LANES].astype(F32) * SCALE
        if p != g:
            x = pltpu.roll(x, HEAD_DIM, 1)
        qh_scr[h] = jnp.where(low_seq if g == 0 else jnp.logical_not(low_seq), x, 0.0).astype(BF16)
    pad = jnp.zeros((A_BLOCK, LANES), BF16)
    for scr, col in ((kp_scr, 0), (vp_scr, KV_A)):
        scr[0:A_BLOCK, :] = pad
        scr[A_BLOCK:A_BLOCK + seq, :] = kv_ref[:, col:col + KV_A]
        scr[A_BLOCK + seq:2 * A_BLOCK + seq, :] = pad
    ckb_scr[...] = ck_ref[...].astype(BF16)
    cvb_scr[...] = cv_ref[...].astype(BF16)

    rows = A_GROUP * A_BLOCK
    row = lax.broadcasted_iota(jnp.int32, (rows, band), 0)
    qi = row % A_BLOCK
    kj = lax.broadcasted_iota(jnp.int32, (rows, band), 1)
    head_row = lax.broadcasted_iota(jnp.int32, (rows, 1), 0) // A_BLOCK
    low_blk = _low_lanes((A_BLOCK, LANES))

    def body(n, carry):
        start = pl.multiple_of(n * A_BLOCK, A_BLOCK)
        no_prev = jnp.where(n > 0, 0, 1)
        no_next = jnp.where(n < n_blocks - 1, 0, 1)
        valid = (kj >= qi + no_prev * (A_BLOCK - qi)) & (kj <= 2 * A_BLOCK + qi - no_next * (qi + 1))
        scores = []
        for g in range(A_KV_HEADS):
            q = jnp.concatenate(
                [qh_scr[A_GROUP * g + i, pl.ds(start, A_BLOCK), :] for i in range(A_GROUP)], axis=0)
            s_band = jnp.where(valid, _dot_t(q, kp_scr[pl.ds(start, band), :]), NEG_INF)
            scores.append((s_band, _dot_t(q, ckb_scr[...])))
        outs = []
        for g, (s_band, s_ctx) in enumerate(scores):
            sink = jnp.full((rows, 1), sink_ref[A_GROUP * g], F32)
            for i in range(1, A_GROUP):
                sink = jnp.where(head_row == i, sink_ref[A_GROUP * g + i], sink)
            outs.append(_softmax_pv([(s_band, vp_scr[pl.ds(start, band), :]), (s_ctx, cvb_scr[...])],
                                    sink=sink))
        for j in range(A_Q_HEADS // 2):
            halves = []
            for p in range(2):
                h = 2 * j + p
                g, i = h // A_GROUP, h % A_GROUP
                t = outs[g][i * A_BLOCK:(i + 1) * A_BLOCK]
                halves.append(t if p == g else pltpu.roll(t, HEAD_DIM, 1))
            o_ref[pl.ds(start, A_BLOCK), j * LANES:(j + 1) * LANES] = (
                jnp.where(low_blk, halves[0], halves[1]).astype(o_ref.dtype))
        return carry

    lax.fori_loop(0, n_blocks, body, 0)


def _attn_a_lat(main, kv, cache_k, cache_v, sink, l, seq):
    m = main.shape[0]
    nb = seq // A_BLOCK
    n_seq = m // seq
    past = cache_k.shape[2]
    cache_spec = pl.BlockSpec((None, None, past, KV_A), lambda b: (b, l, 0, 0))
    return pl.pallas_call(
        functools.partial(_attn_a_lat_kernel, n_blocks=nb),
        grid=(n_seq,),
        in_specs=[
            pl.BlockSpec(memory_space=pltpu.SMEM),
            pl.BlockSpec((seq, BRANCH_DIM), lambda b: (b, QX_QA // BRANCH_DIM)),
            pl.BlockSpec((seq, 2 * KV_A), lambda b: (b, QX_KVA // (2 * KV_A))),
            cache_spec, cache_spec,
        ],
        out_specs=pl.BlockSpec((seq, BRANCH_DIM), lambda b: (b, 0)),
        out_shape=jax.ShapeDtypeStruct((m, BRANCH_DIM), BF16),
        scratch_shapes=[
            pltpu.VMEM((A_Q_HEADS, seq, LANES), BF16),
            pltpu.VMEM((seq + 2 * A_BLOCK, LANES), BF16),
            pltpu.VMEM((seq + 2 * A_BLOCK, LANES), BF16),
            pltpu.VMEM((past, LANES), BF16),
            pltpu.VMEM((past, LANES), BF16),
        ],
        compiler_params=_cparams(("parallel",)),
        name="attn_a_lat",
    )(sink, main, kv, cache_k, cache_v)


def _attn_b_lat_kernel(q_ref, k_ref, v_ref, ck_ref, cv_ref, bias_ref, o_ref, ckb_scr, cvb_scr, *, rows, kr):
    ckb_scr[...] = ck_ref[...].astype(BF16)
    cvb_scr[...] = cv_ref[...].astype(BF16)
    half = kr // 2
    n_loc = kr * GRID_W
    low_q = _low_lanes((GRID_W, LANES))

    chunks = [slice(j * LANES, (j + 1) * LANES) for j in range(B_HEADS // 2)]
    rows_per_iter = 2
    assert rows % rows_per_iter == 0

    def body(it, carry):
        work = []
        for u in range(rows_per_iter):
            r = it * rows_per_iter + u
            r0 = jnp.clip(r - half, 0, rows - kr)
            q0 = pl.multiple_of(r * GRID_W, GRID_W)
            start = pl.multiple_of(r0 * GRID_W, GRID_W)
            for j, cols in enumerate(chunks):
                qc = _scaled(q_ref[pl.ds(q0, GRID_W), cols])
                zero = jnp.zeros_like(qc)
                qs = jnp.concatenate([jnp.where(low_q, qc, zero), jnp.where(low_q, zero, qc)], axis=0)
                s_loc = _dot_t(qs, k_ref[pl.ds(start, n_loc), cols]) + bias_ref[r - r0, j]
                work.append((q0, start, cols, s_loc, _dot_t(qs, ckb_scr[:, cols])))
        for q0, start, cols, s_loc, s_ctx in work:
            o2 = _softmax_pv([(s_loc, v_ref[pl.ds(start, n_loc), cols]), (s_ctx, cvb_scr[:, cols])])
            o = jnp.where(low_q, o2[:GRID_W], o2[GRID_W:])
            o_ref[pl.ds(q0, GRID_W), cols] = o.astype(o_ref.dtype)
        return carry

    lax.fori_loop(0, rows // rows_per_iter, body, 0)


def _attn_b_lat(main, kv, cache_k, cache_v, bias, l, seq):
    m = main.shape[0]
    rows = seq // GRID_W
    kr = min(B_WIN_ROWS, rows)
    n_seq = m // seq
    past = cache_k.shape[2]
    cache_spec = pl.BlockSpec((None, None, past, BRANCH_DIM), lambda b: (b, l, 0, 0))
    return pl.pallas_call(
        functools.partial(_attn_b_lat_kernel, rows=rows, kr=kr),
        grid=(n_seq,),
        in_specs=[
            pl.BlockSpec((seq, BRANCH_DIM), lambda b: (b, QX_QB // BRANCH_DIM)),
            pl.BlockSpec((seq, BRANCH_DIM), lambda b: (b, KVU_KB // BRANCH_DIM)),
            pl.BlockSpec((seq, BRANCH_DIM), lambda b: (b, KVU_VB // BRANCH_DIM)),
            cache_spec, cache_spec,
            _resident((None,) + bias.shape[1:], lambda b: (l, 0, 0, 0, 0)),
        ],
        out_specs=pl.BlockSpec((seq, BRANCH_DIM), lambda b: (b, 0)),
        out_shape=jax.ShapeDtypeStruct((m, BRANCH_DIM), BF16),
        scratch_shapes=[pltpu.VMEM((past, BRANCH_DIM), BF16)] * 2,
        compiler_params=_cparams(("parallel",)),
        name="attn_b_lat",
    )(main, kv, kv, cache_k, cache_v, bias)


def _neighbourhood_bias(rpb, rows):
    kr = min(B_WIN_ROWS, rows)
    assert kr % 2 == 0 and 2 * GRID_W == LANES
    depth, heads, n_dr, n_dc = rpb.shape
    rpb_pad = jnp.pad(rpb.astype(F32), ((0, 0), (0, 0), (0, (-n_dr) % 8), (0, LANES - n_dc)))
    return pl.pallas_call(
        functools.partial(_bias_kernel, kr=kr),
        grid=(depth, heads),
        in_specs=[pl.BlockSpec((None, None, rpb_pad.shape[2], LANES), lambda l, h: (l, h, 0, 0))],
        out_specs=pl.BlockSpec((None, kr, None, GRID_W, kr * GRID_W), lambda l, h: (l, 0, h // 2, h % 2, 0)),
        out_shape=jax.ShapeDtypeStruct((depth, kr, heads // 2, 2 * GRID_W, kr * GRID_W), F32),
        compiler_params=_cparams(("parallel", "parallel")),
        name="nbr_bias",
    )(rpb_pad)


def _bias_kernel(rpb_ref, o_ref, *, kr):
    shape = (GRID_W, LANES)
    c = lax.broadcasted_iota(jnp.int32, shape, 0)
    lane = lax.broadcasted_iota(jnp.int32, shape, 1)
    c2 = lane % GRID_W
    ws = jnp.clip(c - B_WIN_COLS // 2, 0, GRID_W - B_WIN_COLS)
    ok = (c2 >= ws) & (c2 < ws + B_WIN_COLS)
    low = lane < GRID_W

    def toeplitz(dr, lane0):
        row = jnp.broadcast_to(rpb_ref[dr:dr + 1, :], shape)
        return pltpu.roll(row, (lane0 - (B_WIN_COLS - 1)) % LANES, 1, stride=1, stride_axis=0)

    pairs = {}
    for v in range(kr):
        for m in range(0, kr, 2):
            dr = m - v + B_WIN_ROWS - 1
            if dr not in pairs:
                pair = jnp.where(low, toeplitz(dr, 0), toeplitz(dr + 1, GRID_W))
                pairs[dr] = jnp.where(ok, pair, NEG_INF)
            o_ref[v, :, m * GRID_W:(m + 2) * GRID_W] = pairs[dr]


def _dft_tables(seq):
    cd = C_GROUP_DIM
    kc = (np.arange(cd)[:, None] * np.arange(cd)[None, :]) % cd
    ang_c = 2.0 * np.pi * kc / cd
    eye2 = np.eye(2)
    bd_cos = np.kron(eye2, np.cos(ang_c))
    bd_sin = np.kron(eye2, np.sin(ang_c))
    kt = (np.arange(seq)[:, None] * np.arange(seq)[None, :]) % seq
    ang_t = 2.0 * np.pi * kt / seq
    norm = 1.0 / np.sqrt(float(seq * cd))
    pos = np.concatenate([np.cos(ang_t), -np.sin(ang_t)], axis=1) * norm
    return (jnp.asarray(bd_cos, F32).astype(BF16), jnp.asarray(bd_sin, F32).astype(BF16),
            jnp.asarray(pos, F32).astype(BF16))


def _fourier_kernel(u_ref, bc_ref, bs_ref, pos_ref, o_ref, *, seq):
    pair = 2 * C_GROUP_DIM
    uc, us = [], []
    for p in range(BRANCH_DIM // pair):
        up = u_ref[:, p * pair:(p + 1) * pair]
        uc.append(_dot(up, bc_ref[...]))
        us.append(_dot(up, bs_ref[...]))
    zc = jnp.concatenate(uc, axis=1).astype(BF16)
    zs = jnp.concatenate(us, axis=1).astype(BF16)
    for s in range(u_ref.shape[0] // seq):
        rows = slice(s * seq, (s + 1) * seq)
        z = jnp.concatenate([zc[rows], zs[rows]], axis=0)
        o_ref[rows, :] = _dot(pos_ref[...], z).astype(o_ref.dtype)


def _fourier(kvu, seq, tm):
    m = kvu.shape[0]
    bc, bs, pos = _dft_tables(seq)
    pair = 2 * C_GROUP_DIM
    uc_block = kvu.shape[1] // BRANCH_DIM - 1
    return pl.pallas_call(
        functools.partial(_fourier_kernel, seq=seq),
        grid=(m // tm,),
        in_specs=[
            pl.BlockSpec((tm, BRANCH_DIM), lambda b: (b, uc_block)),
            _resident((pair, pair), lambda b: (0, 0)),
            _resident((pair, pair), lambda b: (0, 0)),
            _resident((seq, 2 * seq), lambda b: (0, 0)),
        ],
        out_specs=pl.BlockSpec((tm, BRANCH_DIM), lambda b: (b, 0)),
        out_shape=jax.ShapeDtypeStruct((m, BRANCH_DIM), BF16),
        compiler_params=_cparams(("parallel",)),
        name="fourier",
    )(kvu, bc, bs, pos)


def _merge_kernel(oa_ref, ob_ref, oc_ref, x_ref, mod_ref, gpre0_ref, gpost_ref, gpre1_ref,
                  win_ref, wb_ref, wo_ref, x1_ref, h2_ref):
    x = x_ref[...]
    h = (_rms(x, gpre0_ref[...]) * (1.0 + mod_ref[1:2, :]) + mod_ref[0:1, :]).astype(BF16)
    mix = None
    for i, o_ref in enumerate((oa_ref, ob_ref, oc_ref)):
        lo = W_GATES + i * D_MODEL
        gate = _sigmoid(_dot(h, win_ref[:, lo:lo + D_MODEL]))
        term = gate * _dot(o_ref[...], wb_ref[i])
        mix = term if mix is None else mix + term
    y = _dot(mix.astype(BF16), wo_ref[...])
    x1 = x + mod_ref[2:3, :] * _rms(y, gpost_ref[...])
    x1_ref[...] = x1
    h2 = _rms(x1, gpre1_ref[...]) * (1.0 + mod_ref[4:5, :]) + mod_ref[3:4, :]
    h2_ref[...] = h2.astype(BF16)


def _merge(oa, ob, oc, x, mods, l, g_post, g_pre, w_in, w_branch, w_out, *, tm, mod_row):
    m = x.shape[0]
    o_spec = pl.BlockSpec((tm, BRANCH_DIM), lambda i: (i, 0))
    x_spec = pl.BlockSpec((tm, D_MODEL), lambda i: (i, 0))
    gain_spec = lambda which: pl.BlockSpec((None, None, 1, D_MODEL), lambda i: (l, which, 0, 0))
    return pl.pallas_call(
        _merge_kernel,
        grid=(m // tm,),
        in_specs=[
            o_spec, o_spec, o_spec,
            x_spec,
            pl.BlockSpec((None, None, 6, D_MODEL), lambda i: (l, mod_row(i), 0, 0)),
            gain_spec(0), gain_spec(0), gain_spec(1),
            _resident((None, D_MODEL, D_IN), lambda i: (l, 0, 0)),
            _resident((N_BRANCH, BRANCH_DIM, D_MODEL), lambda i: (0, 0, 0)),
            _resident((D_MODEL, D_MODEL), lambda i: (0, 0)),
        ],
        out_specs=[x_spec, x_spec],
        out_shape=[jax.ShapeDtypeStruct((m, D_MODEL), F32), jax.ShapeDtypeStruct((m, D_MODEL), BF16)],
        compiler_params=_cparams(("parallel",)),
        name="merge",
    )(oa, ob, oc, x, mods, g_pre, g_post, g_pre, w_in, w_branch, w_out)


def _ffn_kernel(h_ref, wg_ref, wu_ref, wo_ref, x_ref, mod_ref, gpost_ref, o_ref, acc_ref):
    k = pl.program_id(1)

    @pl.when(k == 0)
    def _():
        acc_ref[...] = jnp.zeros_like(acc_ref)

    h = h_ref[...]
    gate = _dot(h, wg_ref[...])
    up = _dot(h, wu_ref[...])
    act = _silu(gate) * up
    acc_ref[...] += _dot(act.astype(BF16), wo_ref[...])

    @pl.when(k == pl.num_programs(1) - 1)
    def _():
        o_ref[...] = x_ref[...] + mod_ref[5:6, :] * _rms(acc_ref[...], gpost_ref[...])


def _ffn(h2, x1, mods, l, g_post, w_ffn_in, w_ffn_out, *, tm, tf, mod_row):
    m = x1.shape[0]
    nk = D_FF // tf
    x_spec = pl.BlockSpec((tm, D_MODEL), lambda i, k: (i, 0))
    return pl.pallas_call(
        _ffn_kernel,
        grid=(m // tm, nk),
        in_specs=[
            x_spec,
            pl.BlockSpec((D_MODEL, tf), lambda i, k: (0, k)),
            pl.BlockSpec((D_MODEL, tf), lambda i, k: (0, k + nk)),
            pl.BlockSpec((tf, D_MODEL), lambda i, k: (k, 0)),
            x_spec,
            pl.BlockSpec((None, None, 6, D_MODEL), lambda i, k: (l, mod_row(i), 0, 0)),
            pl.BlockSpec((None, None, 1, D_MODEL), lambda i, k: (l, 1, 0, 0)),
        ],
        out_specs=x_spec,
        out_shape=jax.ShapeDtypeStruct((m, D_MODEL), F32),
        scratch_shapes=[pltpu.VMEM((tm, D_MODEL), F32)],
        compiler_params=_cparams(("parallel", "arbitrary")),
        name="ffn",
    )(h2, w_ffn_in, w_ffn_in, w_ffn_out, x1, mods, g_post)


def _rope_tables(seq):
    t = jnp.arange(seq, dtype=jnp.int32)
    row = (t // GRID_W).astype(F32)
    col = (t % GRID_W).astype(F32)
    n_pairs_axis = HEAD_DIM // 4
    inv = ROPE_BASE ** (-jnp.arange(n_pairs_axis, dtype=F32) / n_pairs_axis)
    ang = jnp.concatenate([row[:, None] * inv, col[:, None] * inv], axis=-1)
    cos, sin = jnp.cos(ang), jnp.sin(ang)
    cos_l = jnp.tile(cos, (1, LANES // cos.shape[1]))
    sin_l = jnp.tile(jnp.concatenate([-sin, sin], axis=-1), (1, LANES // HEAD_DIM))
    return cos_l, sin_l


def kernel(x_prompt, x_sample, cache_a_k, cache_a_v, cache_b_k, cache_b_v, c, c_ctx, w_ada, b_ada,
           norm_pre, norm_post, w_in, a_sink, b_rpb, w_branch, w_out, w_ffn_in, w_ffn_out):
    batch, seq, _ = x_prompt.shape
    dec_batch, dec_seq, _ = x_sample.shape
    past = cache_a_k.shape[2]
    assert dec_batch <= CTX_MOD_ROW and seq % A_BLOCK == 0 and dec_seq % A_BLOCK == 0

    cvec = jnp.concatenate(
        [c, c_ctx[None, :], jnp.zeros((MOD_ROWS - dec_batch - 1, D_MODEL), F32)], axis=0)
    mods = _modulation(cvec, w_ada, b_ada).reshape(DEPTH, MOD_ROWS, 6, D_MODEL)

    assert w_in.shape[-1] == D_IN
    w_in_b = w_in.astype(BF16)
    late_weights = (w_branch.reshape(DEPTH, N_BRANCH * BRANCH_DIM, D_MODEL), w_out, w_ffn_in, w_ffn_out)
    layer_weights = []
    g_pre = norm_pre.reshape(DEPTH, 2, 1, D_MODEL)
    g_post = norm_post.reshape(DEPTH, 2, 1, D_MODEL)
    rope_tabs = _rope_tables(dec_seq)
    cak = cache_a_k.reshape(dec_batch, DEPTH, past, KV_A)
    cav = cache_a_v.reshape(dec_batch, DEPTH, past, KV_A)
    cbk = cache_b_k.reshape(dec_batch, DEPTH, past, BRANCH_DIM)
    cbv = cache_b_v.reshape(dec_batch, DEPTH, past, BRANCH_DIM)

    tm_proj, tm_merge, tm_ffn, tf = 1024, 512, 512, D_FF // 2

    def layer_tail(l, x, oa, ob, oc, mod_row_fn):
        w_branch_b, w_out_b, w_ffn_in_b, w_ffn_out_b = layer_weights[l]
        x1, h2 = _merge(oa, ob, oc, x, mods, l, g_post, g_pre, w_in_b,
                        w_branch_b.reshape(N_BRANCH, BRANCH_DIM, D_MODEL), w_out_b,
                        tm=tm_merge, mod_row=mod_row_fn(tm_merge))
        return _ffn(h2, x1, mods, l, g_post, w_ffn_in_b, w_ffn_out_b,
                    tm=tm_ffn, tf=tf, mod_row=mod_row_fn(tm_ffn))

    ctx_row = lambda tm: (lambda i: CTX_MOD_ROW)
    x = x_prompt.reshape(batch * seq, D_MODEL)
    caches = None
    for l in range(DEPTH):
        qx, kvu, *caches = _proj(x, mods, l, g_pre, w_in_b, tm=tm_proj,
                                 mod_row=ctx_row(tm_proj), seq=seq, caches=caches)
        oa, ob, *cast = _attn_ctx(qx, caches, a_sink[l], l, seq, late_weights)
        layer_weights.append(cast)
        oc = _fourier(kvu, seq, tm_proj)
        x = layer_tail(l, x, oa, ob, oc, ctx_row)
    y_prompt = x.reshape(batch, seq, D_MODEL)
    nak, nav, nbk, nbv = caches
    new_a_k = nak.reshape(batch, DEPTH, seq, A_KV_HEADS, HEAD_DIM)
    new_a_v = nav.reshape(batch, DEPTH, seq, A_KV_HEADS, HEAD_DIM)
    new_b_k = nbk.reshape(batch, DEPTH, seq, B_HEADS, HEAD_DIM)
    new_b_v = nbv.reshape(batch, DEPTH, seq, B_HEADS, HEAD_DIM)

    lat_row = lambda tm: (lambda i: (i * tm) // dec_seq)
    x = x_sample.reshape(dec_batch * dec_seq, D_MODEL)
    bias = _neighbourhood_bias(b_rpb, dec_seq // GRID_W)
    for l in range(DEPTH):
        qx, kvu = _proj(x, mods, l, g_pre, w_in_b, tm=tm_proj,
                        mod_row=lat_row(tm_proj), seq=dec_seq, rope_tabs=rope_tabs)
        oa = _attn_a_lat(qx, qx, cak, cav, a_sink[l], l, dec_seq)
        ob = _attn_b_lat(qx, kvu, cbk, cbv, bias, l, dec_seq)
        oc = _fourier(kvu, dec_seq, dec_seq)
        x = layer_tail(l, x, oa, ob, oc, lat_row)
    y_sample = x.reshape(dec_batch, dec_seq, D_MODEL)
    return (y_prompt, y_sample, new_a_k, new_a_v, new_b_k, new_b_v)
```

```python
import functools

import numpy as np
import jax
import jax.numpy as jnp
from jax import lax
from jax.experimental import pallas as pl
from jax.experimental.pallas import tpu as pltpu

F32 = jnp.float32
BF16 = jnp.bfloat16

D_MODEL = 1024
DEPTH = 2
GRID_W = 64
HEAD_DIM = 64
BRANCH_DIM = D_MODEL // 2
A_Q_HEADS = BRANCH_DIM // HEAD_DIM
A_KV_HEADS = A_Q_HEADS // 4
A_GROUP = A_Q_HEADS // A_KV_HEADS
A_BLOCK = 128
B_HEADS = BRANCH_DIM // HEAD_DIM
B_WIN_ROWS = 8
B_WIN_COLS = 16
C_GROUPS = 4
C_GROUP_DIM = BRANCH_DIM // C_GROUPS
N_BRANCH = 3
D_FF = -(-8 * D_MODEL // (3 * 256)) * 256
ROPE_BASE = 10000.0
NORM_EPS = 1e-6
NEG_INF = -1e30
SCALE = HEAD_DIM ** -0.5

LANES = 128
KV_A = A_KV_HEADS * HEAD_DIM
N_GATES = N_BRANCH * D_MODEL
W_QA = 0
W_KA = W_QA + BRANCH_DIM
W_VA = W_KA + KV_A
W_QB = W_VA + KV_A
W_KB = W_QB + BRANCH_DIM
W_VB = W_KB + BRANCH_DIM
W_UC = W_VB + BRANCH_DIM
W_GATES = W_UC + BRANCH_DIM
D_IN = W_GATES + N_GATES
QX_QA, QX_QB, QX_KVA = 0, BRANCH_DIM, 2 * BRANCH_DIM
KVU_KB, KVU_VB, KVU_UC = 0, BRANCH_DIM, 2 * BRANCH_DIM
MOD_ROWS = 8
CTX_MOD_ROW = 4
VMEM_LIMIT = 56 * 1024 * 1024


def _cparams(sem):
    return pltpu.CompilerParams(dimension_semantics=sem, vmem_limit_bytes=VMEM_LIMIT)


def _resident(shape, index_map):
    return pl.BlockSpec(shape, index_map, pipeline_mode=pl.Buffered(1))


def _rms(x, g):
    return x * lax.rsqrt(jnp.mean(x * x, axis=-1, keepdims=True) + NORM_EPS) * g


def _sigmoid(x):
    return 0.5 * jnp.tanh(0.5 * x) + 0.5


def _silu(x):
    half = 0.5 * x
    return half + half * jnp.tanh(half)


def _dot_t(a, b):
    return lax.dot_general(a, b, (((1,), (1,)), ((), ())), preferred_element_type=F32)


def _dot(a, b):
    return jnp.dot(a, b, preferred_element_type=F32)


def _scaled(q):
    assert np.log2(SCALE) == round(np.log2(SCALE))
    return q * jnp.asarray(SCALE, q.dtype)


def _low_lanes(shape):
    return lax.broadcasted_iota(jnp.int32, shape, len(shape) - 1) < HEAD_DIM


def _mod_kernel(cv_ref, w_ref, b_ref, win_ref, o_ref, wproj_ref, wgates_ref):
    a, b = _silu(cv_ref[...]), w_ref[...]
    a_hi, b_hi = a.astype(BF16), b.astype(BF16)
    a_lo = (a - a_hi.astype(F32)).astype(BF16)
    b_lo = (b - b_hi.astype(F32)).astype(BF16)
    o_ref[...] = _dot(a_hi, b_hi) + (_dot(a_lo, b_hi) + _dot(a_hi, b_lo)) + b_ref[...]
    wproj_ref[...] = win_ref[:, :W_GATES].astype(wproj_ref.dtype)
    wgates_ref[...] = win_ref[:, W_GATES:].astype(wgates_ref.dtype)


def _modulation(cvec, w_ada, b_ada, w_in):
    tn = 1536
    n = 6 * D_MODEL
    steps = n // tn
    rows = w_in.shape[1] // steps
    return pl.pallas_call(
        _mod_kernel,
        grid=(DEPTH, steps),
        in_specs=[
            pl.BlockSpec((MOD_ROWS, D_MODEL), lambda l, j: (0, 0)),
            pl.BlockSpec((None, D_MODEL, tn), lambda l, j: (l, 0, j)),
            pl.BlockSpec((None, 1, tn), lambda l, j: (l, 0, j)),
            pl.BlockSpec((None, rows, D_IN), lambda l, j: (l, j, 0)),
        ],
        out_specs=[pl.BlockSpec((None, MOD_ROWS, tn), lambda l, j: (l, 0, j)),
                   pl.BlockSpec((None, rows, W_GATES), lambda l, j: (l, j, 0)),
                   pl.BlockSpec((None, rows, N_GATES), lambda l, j: (l, j, 0))],
        out_shape=[jax.ShapeDtypeStruct((DEPTH, MOD_ROWS, n), F32),
                   jax.ShapeDtypeStruct((DEPTH, D_MODEL, W_GATES), BF16),
                   jax.ShapeDtypeStruct((DEPTH, D_MODEL, N_GATES), BF16)],
        compiler_params=_cparams(("parallel", "parallel")),
        name="modulation",
    )(cvec, w_ada, b_ada.reshape(DEPTH, 1, n), w_in)


def _rope_cols(v, cos, sin):
    first = (lax.broadcasted_iota(jnp.int32, (v.shape[0], LANES), 1) % HEAD_DIM) < HEAD_DIM // 2
    outs = []
    for c in range(v.shape[1] // LANES):
        vc = v[:, c * LANES:(c + 1) * LANES]
        partner = jnp.where(first, pltpu.roll(vc, LANES - HEAD_DIM // 2, 1),
                            pltpu.roll(vc, HEAD_DIM // 2, 1))
        outs.append(vc * cos + partner * sin)
    return outs[0] if len(outs) == 1 else jnp.concatenate(outs, axis=1)


def _with_rope(acc, lo, hi, cos_ref, sin_ref):
    parts = []
    if lo > 0:
        parts.append(acc[:, :lo])
    parts.append(_rope_cols(acc[:, lo:hi], cos_ref[...], sin_ref[...]))
    if hi < acc.shape[1]:
        parts.append(acc[:, hi:])
    return parts[0] if len(parts) == 1 else jnp.concatenate(parts, axis=1)


def _proj_kernel(*refs, latent, n_alias):
    x_ref, mod_ref, g_ref, w_ref = refs[:4]
    if latent:
        cos_ref, sin_ref, qx_ref, kvu_ref, h_scr = refs[4:]
    else:
        qx_ref, kvu_ref, ka_ref, va_ref, kb_ref, vb_ref, h_scr = refs[4 + n_alias:]
    j = pl.program_id(1)

    @pl.when(j == 0)
    def _():
        h = _rms(x_ref[...], g_ref[...]) * (1.0 + mod_ref[1:2, :]) + mod_ref[0:1, :]
        h_scr[...] = h.astype(BF16)
        acc = _dot(h_scr[...], w_ref[:, W_QA:W_KB])
        qa = acc[:, W_QA:W_QA + BRANCH_DIM]
        qb = acc[:, W_QB:W_QB + BRANCH_DIM]
        kva = acc[:, W_KA:W_KA + 2 * KV_A]
        if latent:
            qa = _rope_cols(qa, cos_ref[...], sin_ref[...])
            kva = _with_rope(kva, 0, KV_A, cos_ref, sin_ref)
            qx_ref[:, QX_KVA:QX_KVA + 2 * KV_A] = kva.astype(qx_ref.dtype)
        else:
            ka_ref[...] = kva[:, :KV_A].reshape(ka_ref.shape)
            va_ref[...] = kva[:, KV_A:].reshape(va_ref.shape)
        qx_ref[:, QX_QA:QX_QA + BRANCH_DIM] = qa.astype(qx_ref.dtype)
        qx_ref[:, QX_QB:QX_QB + BRANCH_DIM] = qb.astype(qx_ref.dtype)

    @pl.when(j == 1)
    def _():
        acc = _dot(h_scr[...], w_ref[:, W_KB:W_GATES])
        kb = acc[:, 0:BRANCH_DIM]
        vb = acc[:, W_VB - W_KB:W_VB - W_KB + BRANCH_DIM]
        uc = acc[:, W_UC - W_KB:W_UC - W_KB + BRANCH_DIM]
        if latent:
            kvu_ref[...] = acc.astype(kvu_ref.dtype)
        else:
            kb_ref[...] = kb.reshape(kb_ref.shape)
            vb_ref[...] = vb.reshape(vb_ref.shape)
            kvu_ref[...] = uc.astype(kvu_ref.dtype)


def _proj(x, mods, l, g_pre, w_in, *, tm, mod_row, seq, rope_tabs=None, caches=None):
    m = x.shape[0]
    latent = rope_tabs is not None
    in_specs = [
        pl.BlockSpec((tm, D_MODEL), lambda i, j: (i, 0)),
        pl.BlockSpec((None, None, 6, D_MODEL), lambda i, j: (l, mod_row(i), 0, 0)),
        pl.BlockSpec((None, None, 1, D_MODEL), lambda i, j: (l, 0, 0, 0)),
        _resident((None, D_MODEL, W_GATES), lambda i, j: (l, 0, 0)),
    ]
    args = [x, mods, g_pre, w_in]
    aliases = {}
    if latent:
        seq_tiles = seq // tm
        tab_spec = pl.BlockSpec((tm, LANES), lambda i, j: (i % seq_tiles, 0))
        in_specs += [tab_spec, tab_spec]
        args += list(rope_tabs)
        widths = (2 * BRANCH_DIM + 2 * KV_A, 3 * BRANCH_DIM)
        cache_specs, cache_shapes = [], []
    else:
        widths = (2 * BRANCH_DIM, BRANCH_DIM)
        cache_widths = (KV_A, KV_A, BRANCH_DIM, BRANCH_DIM)
        if caches is not None:
            in_specs += [pl.BlockSpec(memory_space=pl.ANY)] * len(caches)
            aliases = {len(args) + k: 2 + k for k in range(len(caches))}
            args += list(caches)
        cache_specs = [pl.BlockSpec((tm // seq, None, seq, w), lambda i, j: (i, l, 0, 0))
                       for w in cache_widths]
        cache_shapes = [jax.ShapeDtypeStruct((m // seq, DEPTH, seq, w), F32) for w in cache_widths]
    return pl.pallas_call(
        functools.partial(_proj_kernel, latent=latent, n_alias=len(aliases)),
        grid=(m // tm, 2),
        in_specs=in_specs,
        out_specs=[pl.BlockSpec((tm, w), lambda i, j: (i, 0)) for w in widths] + cache_specs,
        out_shape=[jax.ShapeDtypeStruct((m, w), BF16) for w in widths] + cache_shapes,
        input_output_aliases=aliases,
        scratch_shapes=[pltpu.VMEM((tm, D_MODEL), BF16)],
        compiler_params=_cparams(("parallel", "arbitrary")),
        name="proj_lat" if latent else "proj_ctx",
    )(*args)


def _with_ones(v):
    return jnp.concatenate([v, jnp.ones_like(v)], axis=1)


def _softmax_pv(parts, sink=None):
    m = parts[0][0].max(axis=-1, keepdims=True)
    for s, _ in parts[1:]:
        m = jnp.maximum(m, s.max(axis=-1, keepdims=True))
    if sink is not None:
        m = jnp.maximum(m, sink)
    ones_half = parts[0][1].shape[1] == 2 * LANES
    acc = None
    den = None
    for s, v in parts:
        e = jnp.exp(s - m)
        if not ones_half:
            d = e.sum(axis=-1, keepdims=True)
            den = d if den is None else den + d
        o = _dot(e.astype(BF16), v)
        acc = o if acc is None else acc + o
    if ones_half:
        acc, den = acc[:, :LANES], acc[:, LANES:]
    if sink is not None:
        den = den + jnp.exp(sink - m)
    return acc * (1.0 / den)


def _kv_head_variants(x2):
    low = _low_lanes(x2.shape)
    xr = pltpu.roll(x2, HEAD_DIM, 1)
    zero = jnp.zeros_like(x2)
    return [
        [jnp.where(low, x2, zero).astype(BF16), jnp.where(low, zero, xr).astype(BF16)],
        [jnp.where(low, xr, zero).astype(BF16), jnp.where(low, zero, x2).astype(BF16)],
    ]


def _attn_ctx_kernel(sink_ref, qa_ref, qb_ref, ka_ref, va_ref, kb_ref, vb_ref, *rest, n_weights):
    w_src, (oa_ref, ob_ref), w_dst = rest[:n_weights], rest[n_weights:n_weights + 2], rest[n_weights + 2:]
    for src, dst in zip(w_src, w_dst):
        dst[...] = src[...].astype(dst.dtype)
    ka = [jnp.concatenate(v, axis=0) for v in _kv_head_variants(ka_ref[...])]
    va = [jnp.concatenate(v, axis=0) for v in _kv_head_variants(va_ref[...])]
    chunks = [slice(j * LANES, (j + 1) * LANES) for j in range(A_Q_HEADS // 2)]
    work = []
    for j, cols in enumerate(chunks):
        g = (2 * j) // A_GROUP
        s2 = _dot_t(_scaled(qa_ref[:, cols]), ka[g])
        work.append((oa_ref, cols, s2, va[g], (sink_ref[2 * j], sink_ref[2 * j + 1])))
    low = _low_lanes((kb_ref.shape[0], LANES))
    for cols in chunks:
        kc, vc = kb_ref[:, cols], vb_ref[:, cols]
        zero = jnp.zeros_like(kc)
        k2 = jnp.concatenate([jnp.where(low, kc, zero), jnp.where(low, zero, kc)], axis=0).astype(BF16)
        v2 = jnp.concatenate([jnp.where(low, vc, zero), jnp.where(low, zero, vc)], axis=0).astype(BF16)
        work.append((ob_ref, cols, _dot_t(_scaled(qb_ref[:, cols]), k2), v2, None))
    for o_ref, cols, s2, v2, sinks in work:
        o_ref[:, cols] = _pair_softmax_pv(s2, v2, sinks).astype(o_ref.dtype)


def _pair_softmax_pv(s2, v2, sinks=None):
    tk = s2.shape[1] // 2
    es, extra = [], []
    for p in range(2):
        s = s2[:, p * tk:(p + 1) * tk]
        m = s.max(axis=-1, keepdims=True)
        if sinks is not None:
            m = jnp.maximum(m, sinks[p])
            extra.append(jnp.exp(sinks[p] - m))
        es.append(jnp.exp(s - m).astype(BF16))
    lane_half = lax.broadcasted_iota(jnp.int32, v2.shape, 1) // HEAD_DIM
    row_half = lax.broadcasted_iota(jnp.int32, v2.shape, 0) // tk
    ones2 = jnp.where(lane_half == row_half, 1.0, 0.0).astype(v2.dtype)
    acc = _dot(jnp.concatenate(es, axis=1), jnp.concatenate([v2, ones2], axis=1))
    den = acc[:, LANES:]
    if sinks is not None:
        den = den + jnp.where(_low_lanes(den.shape), extra[0], extra[1])
    return acc[:, :LANES] * (1.0 / den)


def _attn_ctx(qx, caches, sink, l, seq, weights):
    m = qx.shape[0]
    steps = m // seq
    out = jax.ShapeDtypeStruct((m, BRANCH_DIM), BF16)
    cache_specs = [pl.BlockSpec((None, None, seq, c.shape[-1]), lambda b: (b, l, 0, 0)) for c in caches]
    w_in_specs, w_out_specs, w_shapes = [], [], []
    for w in weights:
        _, rows, cols = w.shape
        assert rows % (steps * 16) == 0
        w_in_specs.append(pl.BlockSpec((None, rows // steps, cols), lambda b: (l, b, 0)))
        w_out_specs.append(pl.BlockSpec((rows // steps, cols), lambda b: (b, 0)))
        w_shapes.append(jax.ShapeDtypeStruct((rows, cols), BF16))
    return pl.pallas_call(
        functools.partial(_attn_ctx_kernel, n_weights=len(weights)),
        grid=(steps,),
        in_specs=[
            pl.BlockSpec(memory_space=pltpu.SMEM),
            pl.BlockSpec((seq, BRANCH_DIM), lambda b: (b, QX_QA // BRANCH_DIM)),
            pl.BlockSpec((seq, BRANCH_DIM), lambda b: (b, QX_QB // BRANCH_DIM)),
        ] + cache_specs + w_in_specs,
        out_specs=[pl.BlockSpec((seq, BRANCH_DIM), lambda b: (b, 0))] * 2 + w_out_specs,
        out_shape=[out, out] + w_shapes,
        compiler_params=_cparams(("parallel",)),
        name="attn_ctx",
    )(sink, qx, qx, *caches, *weights)


def _attn_a_lat_kernel(sink_ref, q_ref, kv_ref, ck_ref, cv_ref, o_ref,
                       qh_scr, kp_scr, vp_scr, ckb_scr, cvb_scr, mask_scr, *, n_blocks):
    seq = n_blocks * A_BLOCK
    band = 3 * A_BLOCK
    low_seq = _low_lanes((seq, LANES))
    for h in range(A_Q_HEADS):
        j, p, g = h // 2, h % 2, h // A_GROUP
        x = q_ref[:, j * LANES:(j + 1) * LANES].astype(F32) * SCALE
        if p != g:
            x = pltpu.roll(x, HEAD_DIM, 1)
        qh_scr[h] = jnp.where(low_seq if g == 0 else jnp.logical_not(low_seq), x, 0.0).astype(BF16)
    pad = jnp.zeros((A_BLOCK, LANES), BF16)
    for scr, col in ((kp_scr, 0), (vp_scr, KV_A)):
        scr[0:A_BLOCK, :] = pad
        scr[A_BLOCK:A_BLOCK + seq, :] = kv_ref[:, col:col + KV_A]
        scr[A_BLOCK + seq:2 * A_BLOCK + seq, :] = pad
    ckb_scr[...] = ck_ref[...].astype(BF16)
    cvb_scr[...] = cv_ref[...].astype(BF16)

    rows = A_GROUP * A_BLOCK
    row = lax.broadcasted_iota(jnp.int32, (rows, band), 0)
    qi = row % A_BLOCK
    kj = lax.broadcasted_iota(jnp.int32, (rows, band), 1)
    head_row = lax.broadcasted_iota(jnp.int32, (rows, 1), 0) // A_BLOCK
    low_blk = _low_lanes((A_BLOCK, LANES))
    for case, (no_prev, no_next) in enumerate(((1, 0), (0, 0), (0, 1))):
        valid = (kj >= qi + no_prev * (A_BLOCK - qi)) & (kj <= 2 * A_BLOCK + qi - no_next * (qi + 1))
        mask_scr[case] = jnp.where(valid, 0.0, NEG_INF)

    def body(n, carry):
        start = pl.multiple_of(n * A_BLOCK, A_BLOCK)
        case = jnp.where(n > 0, 1, 0) + jnp.where(n < n_blocks - 1, 0, 1)
        scores = []
        for g in range(A_KV_HEADS):
            q = jnp.concatenate(
                [qh_scr[A_GROUP * g + i, pl.ds(start, A_BLOCK), :] for i in range(A_GROUP)], axis=0)
            s_band = _dot_t(q, kp_scr[pl.ds(start, band), :]) + mask_scr[case]
            scores.append((s_band, _dot_t(q, ckb_scr[...])))
        outs = []
        for g, (s_band, s_ctx) in enumerate(scores):
            sink = jnp.full((rows, 1), sink_ref[A_GROUP * g], F32)
            for i in range(1, A_GROUP):
                sink = jnp.where(head_row == i, sink_ref[A_GROUP * g + i], sink)
            outs.append(_softmax_pv([(s_band, vp_scr[pl.ds(start, band), :]), (s_ctx, cvb_scr[...])],
                                    sink=sink))
        for j in range(A_Q_HEADS // 2):
            halves = []
            for p in range(2):
                h = 2 * j + p
                g, i = h // A_GROUP, h % A_GROUP
                t = outs[g][i * A_BLOCK:(i + 1) * A_BLOCK]
                halves.append(t if p == g else pltpu.roll(t, HEAD_DIM, 1))
            o_ref[pl.ds(start, A_BLOCK), j * LANES:(j + 1) * LANES] = (
                jnp.where(low_blk, halves[0], halves[1]).astype(o_ref.dtype))
        return carry

    lax.fori_loop(0, n_blocks, body, 0, unroll=2)


def _attn_a_lat(main, kv, cache_k, cache_v, sink, l, seq):
    m = main.shape[0]
    nb = seq // A_BLOCK
    assert nb >= 2
    n_seq = m // seq
    past = cache_k.shape[2]
    cache_spec = pl.BlockSpec((None, None, past, KV_A), lambda b: (b, l, 0, 0))
    return pl.pallas_call(
        functools.partial(_attn_a_lat_kernel, n_blocks=nb),
        grid=(n_seq,),
        in_specs=[
            pl.BlockSpec(memory_space=pltpu.SMEM),
            pl.BlockSpec((seq, BRANCH_DIM), lambda b: (b, QX_QA // BRANCH_DIM)),
            pl.BlockSpec((seq, 2 * KV_A), lambda b: (b, QX_KVA // (2 * KV_A))),
            cache_spec, cache_spec,
        ],
        out_specs=pl.BlockSpec((seq, BRANCH_DIM), lambda b: (b, 0)),
        out_shape=jax.ShapeDtypeStruct((m, BRANCH_DIM), BF16),
        scratch_shapes=[
            pltpu.VMEM((A_Q_HEADS, seq, LANES), BF16),
            pltpu.VMEM((seq + 2 * A_BLOCK, LANES), BF16),
            pltpu.VMEM((seq + 2 * A_BLOCK, LANES), BF16),
            pltpu.VMEM((past, LANES), BF16),
            pltpu.VMEM((past, LANES), BF16),
            pltpu.VMEM((3, A_GROUP * A_BLOCK, 3 * A_BLOCK), F32),
        ],
        compiler_params=_cparams(("parallel",)),
        name="attn_a_lat",
    )(sink, main, kv, cache_k, cache_v)


def _attn_b_lat_kernel(q_ref, k_ref, v_ref, ck_ref, cv_ref, bias_ref, o_ref, ckb_scr, vx_scr, cvx_scr,
                       *, rows, kr):
    half = kr // 2
    n_loc = kr * GRID_W
    low_q = _low_lanes((GRID_W, LANES))
    chunks = [slice(j * LANES, (j + 1) * LANES) for j in range(B_HEADS // 2)]
    ckb_scr[...] = ck_ref[...].astype(BF16)
    for j, cols in enumerate(chunks):
        vx_scr[j] = _with_ones(v_ref[:, cols])
        cvx_scr[j] = _with_ones(cv_ref[:, cols].astype(BF16))
    rows_per_iter = 2
    assert rows % rows_per_iter == 0

    def body(it, carry):
        work = []
        for u in range(rows_per_iter):
            r = it * rows_per_iter + u
            r0 = jnp.clip(r - half, 0, rows - kr)
            q0 = pl.multiple_of(r * GRID_W, GRID_W)
            start = pl.multiple_of(r0 * GRID_W, GRID_W)
            for j, cols in enumerate(chunks):
                qc = _scaled(q_ref[pl.ds(q0, GRID_W), cols])
                zero = jnp.zeros_like(qc)
                qs = jnp.concatenate([jnp.where(low_q, qc, zero), jnp.where(low_q, zero, qc)], axis=0)
                s_loc = _dot_t(qs, k_ref[pl.ds(start, n_loc), cols]) + bias_ref[r - r0, j]
                work.append((q0, start, j, cols, s_loc, _dot_t(qs, ckb_scr[:, cols])))
        for q0, start, j, cols, s_loc, s_ctx in work:
            o2 = _softmax_pv([(s_loc, vx_scr[j, pl.ds(start, n_loc), :]), (s_ctx, cvx_scr[j])])
            o = jnp.where(low_q, o2[:GRID_W], o2[GRID_W:])
            o_ref[pl.ds(q0, GRID_W), cols] = o.astype(o_ref.dtype)
        return carry

    lax.fori_loop(0, rows // rows_per_iter, body, 0)


def _attn_b_lat(main, kv, cache_k, cache_v, bias, l, seq):
    m = main.shape[0]
    rows = seq // GRID_W
    kr = min(B_WIN_ROWS, rows)
    n_seq = m // seq
    past = cache_k.shape[2]
    cache_spec = pl.BlockSpec((None, None, past, BRANCH_DIM), lambda b: (b, l, 0, 0))
    return pl.pallas_call(
        functools.partial(_attn_b_lat_kernel, rows=rows, kr=kr),
        grid=(n_seq,),
        in_specs=[
            pl.BlockSpec((seq, BRANCH_DIM), lambda b: (b, QX_QB // BRANCH_DIM)),
            pl.BlockSpec((seq, BRANCH_DIM), lambda b: (b, KVU_KB // BRANCH_DIM)),
            pl.BlockSpec((seq, BRANCH_DIM), lambda b: (b, KVU_VB // BRANCH_DIM)),
            cache_spec, cache_spec,
            _resident((None,) + bias.shape[1:], lambda b: (l, 0, 0, 0, 0)),
        ],
        out_specs=pl.BlockSpec((seq, BRANCH_DIM), lambda b: (b, 0)),
        out_shape=jax.ShapeDtypeStruct((m, BRANCH_DIM), BF16),
        scratch_shapes=[
            pltpu.VMEM((past, BRANCH_DIM), BF16),
            pltpu.VMEM((B_HEADS // 2, seq, 2 * LANES), BF16),
            pltpu.VMEM((B_HEADS // 2, past, 2 * LANES), BF16),
        ],
        compiler_params=_cparams(("parallel",)),
        name="attn_b_lat",
    )(main, kv, kv, cache_k, cache_v, bias)


def _neighbourhood_bias(rpb, rows):
    kr = min(B_WIN_ROWS, rows)
    assert kr % 2 == 0 and 2 * GRID_W == LANES
    depth, heads, n_dr, n_dc = rpb.shape
    rpb_pad = jnp.pad(rpb.astype(F32), ((0, 0), (0, 0), (0, (-n_dr) % 8), (0, LANES - n_dc)))
    return pl.pallas_call(
        functools.partial(_bias_kernel, kr=kr),
        grid=(depth, heads // 2),
        in_specs=[pl.BlockSpec((None, 2, rpb_pad.shape[2], LANES), lambda l, j: (l, j, 0, 0))],
        out_specs=pl.BlockSpec((None, kr, None, 2 * GRID_W, kr * GRID_W), lambda l, j: (l, 0, j, 0, 0)),
        out_shape=jax.ShapeDtypeStruct((depth, kr, heads // 2, 2 * GRID_W, kr * GRID_W), F32),
        compiler_params=_cparams(("parallel", "parallel")),
        name="nbr_bias",
    )(rpb_pad)


def _bias_kernel(rpb_ref, o_ref, *, kr):
    shape = (GRID_W, LANES)
    c = lax.broadcasted_iota(jnp.int32, shape, 0)
    lane = lax.broadcasted_iota(jnp.int32, shape, 1)
    c2 = lane % GRID_W
    ws = jnp.clip(c - B_WIN_COLS // 2, 0, GRID_W - B_WIN_COLS)
    ok = (c2 >= ws) & (c2 < ws + B_WIN_COLS)
    low = lane < GRID_W

    def toeplitz(p, dr, lane0):
        row = jnp.broadcast_to(rpb_ref[p, dr:dr + 1, :], shape)
        return pltpu.roll(row, (lane0 - (B_WIN_COLS - 1)) % LANES, 1, stride=1, stride_axis=0)

    for p in range(2):
        pairs = {}
        for v in range(kr):
            for m in range(0, kr, 2):
                dr = m - v + B_WIN_ROWS - 1
                if dr not in pairs:
                    pair = jnp.where(low, toeplitz(p, dr, 0), toeplitz(p, dr + 1, GRID_W))
                    pairs[dr] = jnp.where(ok, pair, NEG_INF)
                o_ref[v, p * GRID_W:(p + 1) * GRID_W, m * GRID_W:(m + 2) * GRID_W] = pairs[dr]


def _dft_tables(seq):
    cd = C_GROUP_DIM
    kc = (np.arange(cd)[:, None] * np.arange(cd)[None, :]) % cd
    ang_c = 2.0 * np.pi * kc / cd
    eye2 = np.eye(2)
    bd_cos = np.kron(eye2, np.cos(ang_c))
    bd_sin = np.kron(eye2, np.sin(ang_c))
    kt = (np.arange(seq)[:, None] * np.arange(seq)[None, :]) % seq
    ang_t = 2.0 * np.pi * kt / seq
    norm = 1.0 / np.sqrt(float(seq * cd))
    pos = np.concatenate([np.cos(ang_t), -np.sin(ang_t)], axis=1) * norm
    return (jnp.asarray(bd_cos, F32).astype(BF16), jnp.asarray(bd_sin, F32).astype(BF16),
            jnp.asarray(pos, F32).astype(BF16))


def _fourier_kernel(u_ref, bc_ref, bs_ref, pos_ref, o_ref, *, seq):
    pair = 2 * C_GROUP_DIM
    uc, us = [], []
    for p in range(BRANCH_DIM // pair):
        up = u_ref[:, p * pair:(p + 1) * pair]
        uc.append(_dot(up, bc_ref[...]))
        us.append(_dot(up, bs_ref[...]))
    zc = jnp.concatenate(uc, axis=1).astype(BF16)
    zs = jnp.concatenate(us, axis=1).astype(BF16)
    for s in range(u_ref.shape[0] // seq):
        rows = slice(s * seq, (s + 1) * seq)
        z = jnp.concatenate([zc[rows], zs[rows]], axis=0)
        o_ref[rows, :] = _dot(pos_ref[...], z).astype(o_ref.dtype)


def _fourier(kvu, seq, tm):
    m = kvu.shape[0]
    bc, bs, pos = _dft_tables(seq)
    pair = 2 * C_GROUP_DIM
    uc_block = kvu.shape[1] // BRANCH_DIM - 1
    return pl.pallas_call(
        functools.partial(_fourier_kernel, seq=seq),
        grid=(m // tm,),
        in_specs=[
            pl.BlockSpec((tm, BRANCH_DIM), lambda b: (b, uc_block)),
            _resident((pair, pair), lambda b: (0, 0)),
            _resident((pair, pair), lambda b: (0, 0)),
            _resident((seq, 2 * seq), lambda b: (0, 0)),
        ],
        out_specs=pl.BlockSpec((tm, BRANCH_DIM), lambda b: (b, 0)),
        out_shape=jax.ShapeDtypeStruct((m, BRANCH_DIM), BF16),
        compiler_params=_cparams(("parallel",)),
        name="fourier",
    )(kvu, bc, bs, pos)


def _merge_kernel(oa_ref, ob_ref, oc_ref, x_ref, mod_ref, gpre0_ref, gpost_ref, gpre1_ref,
                  wg_ref, wb_ref, wo_ref, x1_ref, h2_ref):
    x = x_ref[...]
    h = (_rms(x, gpre0_ref[...]) * (1.0 + mod_ref[1:2, :]) + mod_ref[0:1, :]).astype(BF16)
    mix = None
    for i, o_ref in enumerate((oa_ref, ob_ref, oc_ref)):
        gate = _sigmoid(_dot(h, wg_ref[:, i * D_MODEL:(i + 1) * D_MODEL]))
        term = gate * _dot(o_ref[...], wb_ref[i])
        mix = term if mix is None else mix + term
    y = _dot(mix.astype(BF16), wo_ref[...])
    x1 = x + mod_ref[2:3, :] * _rms(y, gpost_ref[...])
    x1_ref[...] = x1
    h2 = _rms(x1, gpre1_ref[...]) * (1.0 + mod_ref[4:5, :]) + mod_ref[3:4, :]
    h2_ref[...] = h2.astype(BF16)


def _merge(oa, ob, oc, x, mods, l, g_post, g_pre, w_in, w_branch, w_out, *, tm, mod_row):
    m = x.shape[0]
    o_spec = pl.BlockSpec((tm, BRANCH_DIM), lambda i: (i, 0))
    x_spec = pl.BlockSpec((tm, D_MODEL), lambda i: (i, 0))
    gain_spec = lambda which: pl.BlockSpec((None, None, 1, D_MODEL), lambda i: (l, which, 0, 0))
    return pl.pallas_call(
        _merge_kernel,
        grid=(m // tm,),
        in_specs=[
            o_spec, o_spec, o_spec,
            x_spec,
            pl.BlockSpec((None, None, 6, D_MODEL), lambda i: (l, mod_row(i), 0, 0)),
            gain_spec(0), gain_spec(0), gain_spec(1),
            _resident((None, D_MODEL, N_GATES), lambda i: (l, 0, 0)),
            _resident((N_BRANCH, BRANCH_DIM, D_MODEL), lambda i: (0, 0, 0)),
            _resident((D_MODEL, D_MODEL), lambda i: (0, 0)),
        ],
        out_specs=[x_spec, x_spec],
        out_shape=[jax.ShapeDtypeStruct((m, D_MODEL), F32), jax.ShapeDtypeStruct((m, D_MODEL), BF16)],
        compiler_params=_cparams(("parallel",)),
        name="merge",
    )(oa, ob, oc, x, mods, g_pre, g_post, g_pre, w_in, w_branch, w_out)


def _ffn_kernel(h_ref, wg_ref, wu_ref, wo_ref, x_ref, mod_ref, gpost_ref, o_ref, acc_ref):
    k = pl.program_id(1)

    @pl.when(k == 0)
    def _():
        acc_ref[...] = jnp.zeros_like(acc_ref)

    h = h_ref[...]
    gate = _dot(h, wg_ref[...])
    up = _dot(h, wu_ref[...])
    act = _silu(gate) * up
    acc_ref[...] += _dot(act.astype(BF16), wo_ref[...])

    @pl.when(k == pl.num_programs(1) - 1)
    def _():
        o_ref[...] = x_ref[...] + mod_ref[5:6, :] * _rms(acc_ref[...], gpost_ref[...])


def _ffn(h2, x1, mods, l, g_post, w_ffn_in, w_ffn_out, *, tm, tf, mod_row):
    m = x1.shape[0]
    nk = D_FF // tf
    x_spec = pl.BlockSpec((tm, D_MODEL), lambda i, k: (i, 0))
    return pl.pallas_call(
        _ffn_kernel,
        grid=(m // tm, nk),
        in_specs=[
            x_spec,
            pl.BlockSpec((D_MODEL, tf), lambda i, k: (0, k)),
            pl.BlockSpec((D_MODEL, tf), lambda i, k: (0, k + nk)),
            pl.BlockSpec((tf, D_MODEL), lambda i, k: (k, 0)),
            x_spec,
            pl.BlockSpec((None, None, 6, D_MODEL), lambda i, k: (l, mod_row(i), 0, 0)),
            pl.BlockSpec((None, None, 1, D_MODEL), lambda i, k: (l, 1, 0, 0)),
        ],
        out_specs=x_spec,
        out_shape=jax.ShapeDtypeStruct((m, D_MODEL), F32),
        scratch_shapes=[pltpu.VMEM((tm, D_MODEL), F32)],
        compiler_params=_cparams(("parallel", "arbitrary")),
        name="ffn",
    )(h2, w_ffn_in, w_ffn_in, w_ffn_out, x1, mods, g_post)


def _rope_tables(seq):
    t = jnp.arange(seq, dtype=jnp.int32)
    row = (t // GRID_W).astype(F32)
    col = (t % GRID_W).astype(F32)
    n_pairs_axis = HEAD_DIM // 4
    inv = ROPE_BASE ** (-jnp.arange(n_pairs_axis, dtype=F32) / n_pairs_axis)
    ang = jnp.concatenate([row[:, None] * inv, col[:, None] * inv], axis=-1)
    cos, sin = jnp.cos(ang), jnp.sin(ang)
    cos_l = jnp.tile(cos, (1, LANES // cos.shape[1]))
    sin_l = jnp.tile(jnp.concatenate([-sin, sin], axis=-1), (1, LANES // HEAD_DIM))
    return cos_l, sin_l


def kernel(x_prompt, x_sample, cache_a_k, cache_a_v, cache_b_k, cache_b_v, c, c_ctx, w_ada, b_ada,
           norm_pre, norm_post, w_in, a_sink, b_rpb, w_branch, w_out, w_ffn_in, w_ffn_out):
    batch, seq, _ = x_prompt.shape
    dec_batch, dec_seq, _ = x_sample.shape
    past = cache_a_k.shape[2]
    assert dec_batch <= CTX_MOD_ROW and seq % A_BLOCK == 0 and dec_seq % A_BLOCK == 0

    cvec = jnp.concatenate(
        [c, c_ctx[None, :], jnp.zeros((MOD_ROWS - dec_batch - 1, D_MODEL), F32)], axis=0)
    assert w_in.shape[-1] == D_IN
    mods, w_proj_b, w_gates_b = _modulation(cvec, w_ada, b_ada, w_in)
    mods = mods.reshape(DEPTH, MOD_ROWS, 6, D_MODEL)
    late_weights = (w_branch.reshape(DEPTH, N_BRANCH * BRANCH_DIM, D_MODEL), w_out, w_ffn_in, w_ffn_out)
    layer_weights = []
    g_pre = norm_pre.reshape(DEPTH, 2, 1, D_MODEL)
    g_post = norm_post.reshape(DEPTH, 2, 1, D_MODEL)
    rope_tabs = _rope_tables(dec_seq)
    cak = cache_a_k.reshape(dec_batch, DEPTH, past, KV_A)
    cav = cache_a_v.reshape(dec_batch, DEPTH, past, KV_A)
    cbk = cache_b_k.reshape(dec_batch, DEPTH, past, BRANCH_DIM)
    cbv = cache_b_v.reshape(dec_batch, DEPTH, past, BRANCH_DIM)

    tm_proj, tm_merge, tm_ffn, tf = 1024, 512, 512, D_FF // 2

    def layer_tail(l, x, oa, ob, oc, mod_row_fn):
        w_branch_b, w_out_b, w_ffn_in_b, w_ffn_out_b = layer_weights[l]
        x1, h2 = _merge(oa, ob, oc, x, mods, l, g_post, g_pre, w_gates_b,
                        w_branch_b.reshape(N_BRANCH, BRANCH_DIM, D_MODEL), w_out_b,
                        tm=tm_merge, mod_row=mod_row_fn(tm_merge))
        return _ffn(h2, x1, mods, l, g_post, w_ffn_in_b, w_ffn_out_b,
                    tm=tm_ffn, tf=tf, mod_row=mod_row_fn(tm_ffn))

    ctx_row = lambda tm: (lambda i: CTX_MOD_ROW)
    x = x_prompt.reshape(batch * seq, D_MODEL)
    caches = None
    for l in range(DEPTH):
        qx, kvu, *caches = _proj(x, mods, l, g_pre, w_proj_b, tm=tm_proj,
                                 mod_row=ctx_row(tm_proj), seq=seq, caches=caches)
        oa, ob, *cast = _attn_ctx(qx, caches, a_sink[l], l, seq, late_weights)
        layer_weights.append(cast)
        oc = _fourier(kvu, seq, tm_proj)
        x = layer_tail(l, x, oa, ob, oc, ctx_row)
    y_prompt = x.reshape(batch, seq, D_MODEL)
    nak, nav, nbk, nbv = caches
    new_a_k = nak.reshape(batch, DEPTH, seq, A_KV_HEADS, HEAD_DIM)
    new_a_v = nav.reshape(batch, DEPTH, seq, A_KV_HEADS, HEAD_DIM)
    new_b_k = nbk.reshape(batch, DEPTH, seq, B_HEADS, HEAD_DIM)
    new_b_v = nbv.reshape(batch, DEPTH, seq, B_HEADS, HEAD_DIM)

    lat_row = lambda tm: (lambda i: (i * tm) // dec_seq)
    x = x_sample.reshape(dec_batch * dec_seq, D_MODEL)
    bias = _neighbourhood_bias(b_rpb, dec_seq // GRID_W)
    for l in range(DEPTH):
        qx, kvu = _proj(x, mods, l, g_pre, w_proj_b, tm=tm_proj,
                        mod_row=lat_row(tm_proj), seq=dec_seq, rope_tabs=rope_tabs)
        oa = _attn_a_lat(qx, qx, cak, cav, a_sink[l], l, dec_seq)
        ob = _attn_b_lat(qx, kvu, cbk, cbv, bias, l, dec_seq)
        oc = _fourier(kvu, dec_seq, dec_seq)
        x = layer_tail(l, x, oa, ob, oc, lat_row)
    y_sample = x.reshape(dec_batch, dec_seq, D_MODEL)
    return (y_prompt, y_sample, new_a_k, new_a_v, new_b_k, new_b_v)
```

```python
import functools

import numpy as np
import jax
import jax.numpy as jnp
from jax import lax
from jax.experimental import pallas as pl
from jax.experimental.pallas import tpu as pltpu

F32 = jnp.float32
BF16 = jnp.bfloat16

D_MODEL = 1024
DEPTH = 2
GRID_W = 64
HEAD_DIM = 64
BRANCH_DIM = D_MODEL // 2
A_Q_HEADS = BRANCH_DIM // HEAD_DIM
A_KV_HEADS = A_Q_HEADS // 4
A_GROUP = A_Q_HEADS // A_KV_HEADS
A_BLOCK = 128
B_HEADS = BRANCH_DIM // HEAD_DIM
B_WIN_ROWS = 8
B_WIN_COLS = 16
C_GROUPS = 4
C_GROUP_DIM = BRANCH_DIM // C_GROUPS
N_BRANCH = 3
D_FF = -(-8 * D_MODEL // (3 * 256)) * 256
ROPE_BASE = 10000.0
NORM_EPS = 1e-6
NEG_INF = -1e30
SCALE = HEAD_DIM ** -0.5

LANES = 128
KV_A = A_KV_HEADS * HEAD_DIM
N_GATES = N_BRANCH * D_MODEL
W_QA = 0
W_KA = W_QA + BRANCH_DIM
W_VA = W_KA + KV_A
W_QB = W_VA + KV_A
W_KB = W_QB + BRANCH_DIM
W_VB = W_KB + BRANCH_DIM
W_UC = W_VB + BRANCH_DIM
W_GATES = W_UC + BRANCH_DIM
D_IN = W_GATES + N_GATES
QX_QA, QX_QB, QX_KVA = 0, BRANCH_DIM, 2 * BRANCH_DIM
KV_KB, KV_VB = 0, BRANCH_DIM
MOD_ROWS = 8
CTX_MOD_ROW = 4
VMEM_LIMIT = 56 * 1024 * 1024
FFN_SUB_ROWS = 512


def _cparams(sem):
    return pltpu.CompilerParams(dimension_semantics=sem, vmem_limit_bytes=VMEM_LIMIT)


def _resident(shape, index_map):
    return pl.BlockSpec(shape, index_map, pipeline_mode=pl.Buffered(1))


def _rms(x, g):
    return x * lax.rsqrt(jnp.mean(x * x, axis=-1, keepdims=True) + NORM_EPS) * g


def _sigmoid(x):
    return 0.5 * jnp.tanh(0.5 * x) + 0.5


def _silu(x):
    half = 0.5 * x
    return half + half * jnp.tanh(half)


def _dot_t(a, b):
    return lax.dot_general(a, b, (((1,), (1,)), ((), ())), preferred_element_type=F32)


def _dot(a, b):
    return jnp.dot(a, b, preferred_element_type=F32)


def _scaled(q):
    assert np.log2(SCALE) == round(np.log2(SCALE))
    return q * jnp.asarray(SCALE, q.dtype)


def _low_lanes(shape):
    return lax.broadcasted_iota(jnp.int32, shape, len(shape) - 1) < HEAD_DIM


def _mod_kernel(cv_ref, w_ref, b_ref, win_ref, o_ref, wproj_ref, wgates_ref):
    a, b = _silu(cv_ref[...]), w_ref[...]
    a_hi, b_hi = a.astype(BF16), b.astype(BF16)
    a_lo = (a - a_hi.astype(F32)).astype(BF16)
    b_lo = (b - b_hi.astype(F32)).astype(BF16)
    o_ref[...] = _dot(a_hi, b_hi) + (_dot(a_lo, b_hi) + _dot(a_hi, b_lo)) + b_ref[...]
    wproj_ref[...] = win_ref[:, :W_GATES].astype(wproj_ref.dtype)
    wgates_ref[...] = win_ref[:, W_GATES:].astype(wgates_ref.dtype)


def _modulation(cvec, w_ada, b_ada, w_in):
    tn = 1536
    n = 6 * D_MODEL
    steps = n // tn
    rows = w_in.shape[1] // steps
    return pl.pallas_call(
        _mod_kernel,
        grid=(DEPTH, steps),
        in_specs=[
            pl.BlockSpec((MOD_ROWS, D_MODEL), lambda l, j: (0, 0)),
            pl.BlockSpec((None, D_MODEL, tn), lambda l, j: (l, 0, j)),
            pl.BlockSpec((None, 1, tn), lambda l, j: (l, 0, j)),
            pl.BlockSpec((None, rows, D_IN), lambda l, j: (l, j, 0)),
        ],
        out_specs=[pl.BlockSpec((None, MOD_ROWS, tn), lambda l, j: (l, 0, j)),
                   pl.BlockSpec((None, rows, W_GATES), lambda l, j: (l, j, 0)),
                   pl.BlockSpec((None, rows, N_GATES), lambda l, j: (l, j, 0))],
        out_shape=[jax.ShapeDtypeStruct((DEPTH, MOD_ROWS, n), F32),
                   jax.ShapeDtypeStruct((DEPTH, D_MODEL, W_GATES), BF16),
                   jax.ShapeDtypeStruct((DEPTH, D_MODEL, N_GATES), BF16)],
        compiler_params=_cparams(("parallel", "parallel")),
        name="modulation",
    )(cvec, w_ada, b_ada.reshape(DEPTH, 1, n), w_in)


def _rope_cols(v, cos, sin):
    first = (lax.broadcasted_iota(jnp.int32, (v.shape[0], LANES), 1) % HEAD_DIM) < HEAD_DIM // 2
    outs = []
    for c in range(v.shape[1] // LANES):
        vc = v[:, c * LANES:(c + 1) * LANES]
        partner = jnp.where(first, pltpu.roll(vc, LANES - HEAD_DIM // 2, 1),
                            pltpu.roll(vc, HEAD_DIM // 2, 1))
        outs.append(vc * cos + partner * sin)
    return outs[0] if len(outs) == 1 else jnp.concatenate(outs, axis=1)


def _with_rope(acc, lo, hi, cos_ref, sin_ref):
    parts = []
    if lo > 0:
        parts.append(acc[:, :lo])
    parts.append(_rope_cols(acc[:, lo:hi], cos_ref[...], sin_ref[...]))
    if hi < acc.shape[1]:
        parts.append(acc[:, hi:])
    return parts[0] if len(parts) == 1 else jnp.concatenate(parts, axis=1)


def _proj_kernel(*refs, latent, n_alias):
    x_ref, mod_ref, g_ref, w_ref, bc_ref, bs_ref, pos_ref = refs[:7]
    if latent:
        cos_ref, sin_ref, qx_ref, kv_ref, oc_ref, h_scr = refs[7:]
    else:
        qx_ref, oc_ref, ka_ref, va_ref, kb_ref, vb_ref, h_scr = refs[7 + n_alias:]
    j = pl.program_id(1)

    @pl.when(j == 0)
    def _():
        h = _rms(x_ref[...], g_ref[...]) * (1.0 + mod_ref[1:2, :]) + mod_ref[0:1, :]
        h_scr[...] = h.astype(BF16)
        acc = _dot(h_scr[...], w_ref[:, W_QA:W_KB])
        qa = acc[:, W_QA:W_QA + BRANCH_DIM]
        qb = acc[:, W_QB:W_QB + BRANCH_DIM]
        kva = acc[:, W_KA:W_KA + 2 * KV_A]
        if latent:
            qa = _rope_cols(qa, cos_ref[...], sin_ref[...])
            kva = _with_rope(kva, 0, KV_A, cos_ref, sin_ref)
            qx_ref[:, QX_KVA:QX_KVA + 2 * KV_A] = kva.astype(qx_ref.dtype)
        else:
            ka_ref[...] = kva[:, :KV_A].reshape(ka_ref.shape)
            va_ref[...] = kva[:, KV_A:].reshape(va_ref.shape)
        qx_ref[:, QX_QA:QX_QA + BRANCH_DIM] = qa.astype(qx_ref.dtype)
        qx_ref[:, QX_QB:QX_QB + BRANCH_DIM] = qb.astype(qx_ref.dtype)

    @pl.when(j == 1)
    def _():
        acc = _dot(h_scr[...], w_ref[:, W_KB:W_GATES])
        kb = acc[:, 0:BRANCH_DIM]
        vb = acc[:, W_VB - W_KB:W_VB - W_KB + BRANCH_DIM]
        uc = acc[:, W_UC - W_KB:W_UC - W_KB + BRANCH_DIM]
        if latent:
            kv_ref[:, KV_KB:KV_KB + BRANCH_DIM] = kb.astype(kv_ref.dtype)
            kv_ref[:, KV_VB:KV_VB + BRANCH_DIM] = vb.astype(kv_ref.dtype)
        else:
            kb_ref[...] = kb.reshape(kb_ref.shape)
            vb_ref[...] = vb.reshape(vb_ref.shape)
        _fourier_mix(uc.astype(BF16), bc_ref, bs_ref, pos_ref, oc_ref)


def _proj(x, mods, l, g_pre, w_in, *, tm, mod_row, seq, rope_tabs=None, caches=None):
    m = x.shape[0]
    assert tm % seq == 0
    latent = rope_tabs is not None
    bc, bs, pos = _dft_tables(seq)
    pair = 2 * C_GROUP_DIM
    in_specs = [
        pl.BlockSpec((tm, D_MODEL), lambda i, j: (i, 0)),
        pl.BlockSpec((None, None, 6, D_MODEL), lambda i, j: (l, mod_row(i), 0, 0)),
        pl.BlockSpec((None, None, 1, D_MODEL), lambda i, j: (l, 0, 0, 0)),
        _resident((None, D_MODEL, W_GATES), lambda i, j: (l, 0, 0)),
        _resident((pair, pair), lambda i, j: (0, 0)),
        _resident((pair, pair), lambda i, j: (0, 0)),
        _resident((seq, 2 * seq), lambda i, j: (0, 0)),
    ]
    args = [x, mods, g_pre, w_in, bc, bs, pos]
    aliases = {}
    if latent:
        seq_tiles = seq // tm
        tab_spec = pl.BlockSpec((tm, LANES), lambda i, j: (i % seq_tiles, 0))
        in_specs += [tab_spec, tab_spec]
        args += list(rope_tabs)
        widths = (2 * BRANCH_DIM + 2 * KV_A, 2 * BRANCH_DIM, BRANCH_DIM)
        cache_specs, cache_shapes = [], []
    else:
        widths = (2 * BRANCH_DIM, BRANCH_DIM)
        cache_widths = (KV_A, KV_A, BRANCH_DIM, BRANCH_DIM)
        if caches is not None:
            in_specs += [pl.BlockSpec(memory_space=pl.ANY)] * len(caches)
            aliases = {len(args) + k: len(widths) + k for k in range(len(caches))}
            args += list(caches)
        cache_specs = [pl.BlockSpec((tm // seq, None, seq, w), lambda i, j: (i, l, 0, 0))
                       for w in cache_widths]
        cache_shapes = [jax.ShapeDtypeStruct((m // seq, DEPTH, seq, w), F32) for w in cache_widths]
    return pl.pallas_call(
        functools.partial(_proj_kernel, latent=latent, n_alias=len(aliases)),
        grid=(m // tm, 2),
        in_specs=in_specs,
        out_specs=[pl.BlockSpec((tm, w), lambda i, j: (i, 0)) for w in widths] + cache_specs,
        out_shape=[jax.ShapeDtypeStruct((m, w), BF16) for w in widths] + cache_shapes,
        input_output_aliases=aliases,
        scratch_shapes=[pltpu.VMEM((tm, D_MODEL), BF16)],
        compiler_params=_cparams(("parallel", "arbitrary")),
        name="proj_lat" if latent else "proj_ctx",
    )(*args)


def _with_ones(v):
    return jnp.concatenate([v, jnp.ones_like(v)], axis=1)


def _softmax_pv(parts, sink=None):
    m = parts[0][0].max(axis=-1, keepdims=True)
    for s, _ in parts[1:]:
        m = jnp.maximum(m, s.max(axis=-1, keepdims=True))
    if sink is not None:
        m = jnp.maximum(m, sink)
    ones_half = parts[0][1].shape[1] == 2 * LANES
    acc = None
    den = None
    for s, v in parts:
        e = jnp.exp(s - m)
        if not ones_half:
            d = e.sum(axis=-1, keepdims=True)
            den = d if den is None else den + d
        o = _dot(e.astype(BF16), v)
        acc = o if acc is None else acc + o
    if ones_half:
        acc, den = acc[:, :LANES], acc[:, LANES:]
    if sink is not None:
        den = den + jnp.exp(sink - m)
    return acc * (1.0 / den)


def _kv_head_variants(x2):
    low = _low_lanes(x2.shape)
    xr = pltpu.roll(x2, HEAD_DIM, 1)
    zero = jnp.zeros_like(x2)
    return [
        [jnp.where(low, x2, zero).astype(BF16), jnp.where(low, zero, xr).astype(BF16)],
        [jnp.where(low, xr, zero).astype(BF16), jnp.where(low, zero, x2).astype(BF16)],
    ]


def _attn_ctx_kernel(sink_ref, qa_ref, qb_ref, ka_ref, va_ref, kb_ref, vb_ref, *rest, n_weights):
    w_src, (oa_ref, ob_ref), w_dst = rest[:n_weights], rest[n_weights:n_weights + 2], rest[n_weights + 2:]
    for src, dst in zip(w_src, w_dst):
        dst[...] = src[...].astype(dst.dtype)
    ka = [jnp.concatenate(v, axis=0) for v in _kv_head_variants(ka_ref[...])]
    va = [jnp.concatenate(v, axis=0) for v in _kv_head_variants(va_ref[...])]
    chunks = [slice(j * LANES, (j + 1) * LANES) for j in range(A_Q_HEADS // 2)]
    work = []
    for j, cols in enumerate(chunks):
        g = (2 * j) // A_GROUP
        s2 = _dot_t(_scaled(qa_ref[:, cols]), ka[g])
        work.append((oa_ref, cols, s2, va[g], (sink_ref[2 * j], sink_ref[2 * j + 1])))
    low = _low_lanes((kb_ref.shape[0], LANES))
    for cols in chunks:
        kc, vc = kb_ref[:, cols], vb_ref[:, cols]
        zero = jnp.zeros_like(kc)
        k2 = jnp.concatenate([jnp.where(low, kc, zero), jnp.where(low, zero, kc)], axis=0).astype(BF16)
        v2 = jnp.concatenate([jnp.where(low, vc, zero), jnp.where(low, zero, vc)], axis=0).astype(BF16)
        work.append((ob_ref, cols, _dot_t(_scaled(qb_ref[:, cols]), k2), v2, None))
    for o_ref, cols, s2, v2, sinks in work:
        o_ref[:, cols] = _pair_softmax_pv(s2, v2, sinks).astype(o_ref.dtype)


def _pair_softmax_pv(s2, v2, sinks=None):
    tk = s2.shape[1] // 2
    es, extra = [], []
    for p in range(2):
        s = s2[:, p * tk:(p + 1) * tk]
        m = s.max(axis=-1, keepdims=True)
        if sinks is not None:
            m = jnp.maximum(m, sinks[p])
            extra.append(jnp.exp(sinks[p] - m))
        es.append(jnp.exp(s - m).astype(BF16))
    lane_half = lax.broadcasted_iota(jnp.int32, v2.shape, 1) // HEAD_DIM
    row_half = lax.broadcasted_iota(jnp.int32, v2.shape, 0) // tk
    ones2 = jnp.where(lane_half == row_half, 1.0, 0.0).astype(v2.dtype)
    acc = _dot(jnp.concatenate(es, axis=1), jnp.concatenate([v2, ones2], axis=1))
    den = acc[:, LANES:]
    if sinks is not None:
        den = den + jnp.where(_low_lanes(den.shape), extra[0], extra[1])
    return acc[:, :LANES] * (1.0 / den)


def _attn_ctx(qx, caches, sink, l, seq, weights):
    m = qx.shape[0]
    steps = m // seq
    out = jax.ShapeDtypeStruct((m, BRANCH_DIM), BF16)
    cache_specs = [pl.BlockSpec((None, None, seq, c.shape[-1]), lambda b: (b, l, 0, 0)) for c in caches]
    w_in_specs, w_out_specs, w_shapes = [], [], []
    for w in weights:
        _, rows, cols = w.shape
        assert rows % (steps * 16) == 0
        w_in_specs.append(pl.BlockSpec((None, rows // steps, cols), lambda b: (l, b, 0)))
        w_out_specs.append(pl.BlockSpec((rows // steps, cols), lambda b: (b, 0)))
        w_shapes.append(jax.ShapeDtypeStruct((rows, cols), BF16))
    return pl.pallas_call(
        functools.partial(_attn_ctx_kernel, n_weights=len(weights)),
        grid=(steps,),
        in_specs=[
            pl.BlockSpec(memory_space=pltpu.SMEM),
            pl.BlockSpec((seq, BRANCH_DIM), lambda b: (b, QX_QA // BRANCH_DIM)),
            pl.BlockSpec((seq, BRANCH_DIM), lambda b: (b, QX_QB // BRANCH_DIM)),
        ] + cache_specs + w_in_specs,
        out_specs=[pl.BlockSpec((seq, BRANCH_DIM), lambda b: (b, 0))] * 2 + w_out_specs,
        out_shape=[out, out] + w_shapes,
        compiler_params=_cparams(("parallel",)),
        name="attn_ctx",
    )(sink, qx, qx, *caches, *weights)


def _attn_a_lat_kernel(sink_ref, q_ref, kv_ref, ck_ref, cv_ref, o_ref,
                       qh_scr, kp_scr, vp_scr, ckb_scr, cvb_scr, mask_scr, *, n_blocks):
    seq = n_blocks * A_BLOCK
    band = 3 * A_BLOCK
    low_seq = _low_lanes((seq, LANES))
    for h in range(A_Q_HEADS):
        j, p, g = h // 2, h % 2, h // A_GROUP
        x = q_ref[:, j * LANES:(j + 1) * LANES].astype(F32) * SCALE
        if p != g:
            x = pltpu.roll(x, HEAD_DIM, 1)
        qh_scr[h] = jnp.where(low_seq if g == 0 else jnp.logical_not(low_seq), x, 0.0).astype(BF16)
    pad = jnp.zeros((A_BLOCK, LANES), BF16)
    for scr, col in ((kp_scr, 0), (vp_scr, KV_A)):
        scr[0:A_BLOCK, :] = pad
        scr[A_BLOCK:A_BLOCK + seq, :] = kv_ref[:, col:col + KV_A]
        scr[A_BLOCK + seq:2 * A_BLOCK + seq, :] = pad
    ckb_scr[...] = ck_ref[...].astype(BF16)
    cvb_scr[...] = cv_ref[...].astype(BF16)

    rows = A_GROUP * A_BLOCK
    row = lax.broadcasted_iota(jnp.int32, (rows, band), 0)
    qi = row % A_BLOCK
    kj = lax.broadcasted_iota(jnp.int32, (rows, band), 1)
    head_row = lax.broadcasted_iota(jnp.int32, (rows, 1), 0) // A_BLOCK
    low_blk = _low_lanes((A_BLOCK, LANES))
    for case, (no_prev, no_next) in enumerate(((1, 0), (0, 0), (0, 1))):
        valid = (kj >= qi + no_prev * (A_BLOCK - qi)) & (kj <= 2 * A_BLOCK + qi - no_next * (qi + 1))
        mask_scr[case] = jnp.where(valid, 0.0, NEG_INF)

    def body(n, carry):
        start = pl.multiple_of(n * A_BLOCK, A_BLOCK)
        case = jnp.where(n > 0, 1, 0) + jnp.where(n < n_blocks - 1, 0, 1)
        scores = []
        for g in range(A_KV_HEADS):
            q = jnp.concatenate(
                [qh_scr[A_GROUP * g + i, pl.ds(start, A_BLOCK), :] for i in range(A_GROUP)], axis=0)
            s_band = _dot_t(q, kp_scr[pl.ds(start, band), :]) + mask_scr[case]
            scores.append((s_band, _dot_t(q, ckb_scr[...])))
        outs = []
        for g, (s_band, s_ctx) in enumerate(scores):
            sink = jnp.full((rows, 1), sink_ref[A_GROUP * g], F32)
            for i in range(1, A_GROUP):
                sink = jnp.where(head_row == i, sink_ref[A_GROUP * g + i], sink)
            outs.append(_softmax_pv([(s_band, vp_scr[pl.ds(start, band), :]), (s_ctx, cvb_scr[...])],
                                    sink=sink))
        for j in range(A_Q_HEADS // 2):
            halves = []
            for p in range(2):
                h = 2 * j + p
                g, i = h // A_GROUP, h % A_GROUP
                t = outs[g][i * A_BLOCK:(i + 1) * A_BLOCK]
                halves.append(t if p == g else pltpu.roll(t, HEAD_DIM, 1))
            o_ref[pl.ds(start, A_BLOCK), j * LANES:(j + 1) * LANES] = (
                jnp.where(low_blk, halves[0], halves[1]).astype(o_ref.dtype))
        return carry

    lax.fori_loop(0, n_blocks, body, 0, unroll=2)


def _attn_a_lat(main, kv, cache_k, cache_v, sink, l, seq):
    m = main.shape[0]
    nb = seq // A_BLOCK
    assert nb >= 2
    n_seq = m // seq
    past = cache_k.shape[2]
    cache_spec = pl.BlockSpec((None, None, past, KV_A), lambda b: (b, l, 0, 0))
    return pl.pallas_call(
        functools.partial(_attn_a_lat_kernel, n_blocks=nb),
        grid=(n_seq,),
        in_specs=[
            pl.BlockSpec(memory_space=pltpu.SMEM),
            pl.BlockSpec((seq, BRANCH_DIM), lambda b: (b, QX_QA // BRANCH_DIM)),
            pl.BlockSpec((seq, 2 * KV_A), lambda b: (b, QX_KVA // (2 * KV_A))),
            cache_spec, cache_spec,
        ],
        out_specs=pl.BlockSpec((seq, BRANCH_DIM), lambda b: (b, 0)),
        out_shape=jax.ShapeDtypeStruct((m, BRANCH_DIM), BF16),
        scratch_shapes=[
            pltpu.VMEM((A_Q_HEADS, seq, LANES), BF16),
            pltpu.VMEM((seq + 2 * A_BLOCK, LANES), BF16),
            pltpu.VMEM((seq + 2 * A_BLOCK, LANES), BF16),
            pltpu.VMEM((past, LANES), BF16),
            pltpu.VMEM((past, LANES), BF16),
            pltpu.VMEM((3, A_GROUP * A_BLOCK, 3 * A_BLOCK), F32),
        ],
        compiler_params=_cparams(("parallel",)),
        name="attn_a_lat",
    )(sink, main, kv, cache_k, cache_v)


def _attn_b_lat_kernel(q_ref, k_ref, v_ref, ck_ref, cv_ref, bias_ref, o_ref, ckb_scr, vx_scr, cvx_scr,
                       *, rows, kr):
    half = kr // 2
    n_loc = kr * GRID_W
    low_q = _low_lanes((GRID_W, LANES))
    chunks = [slice(j * LANES, (j + 1) * LANES) for j in range(B_HEADS // 2)]
    ckb_scr[...] = ck_ref[...].astype(BF16)
    for j, cols in enumerate(chunks):
        vx_scr[j] = _with_ones(v_ref[:, cols])
        cvx_scr[j] = _with_ones(cv_ref[:, cols].astype(BF16))
    rows_per_iter = 2
    assert rows % rows_per_iter == 0

    def body(it, carry):
        work = []
        for u in range(rows_per_iter):
            r = it * rows_per_iter + u
            r0 = jnp.clip(r - half, 0, rows - kr)
            q0 = pl.multiple_of(r * GRID_W, GRID_W)
            start = pl.multiple_of(r0 * GRID_W, GRID_W)
            for j, cols in enumerate(chunks):
                qc = _scaled(q_ref[pl.ds(q0, GRID_W), cols])
                zero = jnp.zeros_like(qc)
                qs = jnp.concatenate([jnp.where(low_q, qc, zero), jnp.where(low_q, zero, qc)], axis=0)
                s_loc = _dot_t(qs, k_ref[pl.ds(start, n_loc), cols]) + bias_ref[r - r0, j]
                work.append((q0, start, j, cols, s_loc, _dot_t(qs, ckb_scr[:, cols])))
        for q0, start, j, cols, s_loc, s_ctx in work:
            o2 = _softmax_pv([(s_loc, vx_scr[j, pl.ds(start, n_loc), :]), (s_ctx, cvx_scr[j])])
            o = jnp.where(low_q, o2[:GRID_W], o2[GRID_W:])
            o_ref[pl.ds(q0, GRID_W), cols] = o.astype(o_ref.dtype)
        return carry

    lax.fori_loop(0, rows // rows_per_iter, body, 0)


def _attn_b_lat(main, kv, cache_k, cache_v, bias, l, seq):
    m = main.shape[0]
    rows = seq // GRID_W
    kr = min(B_WIN_ROWS, rows)
    n_seq = m // seq
    past = cache_k.shape[2]
    cache_spec = pl.BlockSpec((None, None, past, BRANCH_DIM), lambda b: (b, l, 0, 0))
    return pl.pallas_call(
        functools.partial(_attn_b_lat_kernel, rows=rows, kr=kr),
        grid=(n_seq,),
        in_specs=[
            pl.BlockSpec((seq, BRANCH_DIM), lambda b: (b, QX_QB // BRANCH_DIM)),
            pl.BlockSpec((seq, BRANCH_DIM), lambda b: (b, KV_KB // BRANCH_DIM)),
            pl.BlockSpec((seq, BRANCH_DIM), lambda b: (b, KV_VB // BRANCH_DIM)),
            cache_spec, cache_spec,
            _resident((None,) + bias.shape[1:], lambda b: (l, 0, 0, 0, 0)),
        ],
        out_specs=pl.BlockSpec((seq, BRANCH_DIM), lambda b: (b, 0)),
        out_shape=jax.ShapeDtypeStruct((m, BRANCH_DIM), BF16),
        scratch_shapes=[
            pltpu.VMEM((past, BRANCH_DIM), BF16),
            pltpu.VMEM((B_HEADS // 2, seq, 2 * LANES), BF16),
            pltpu.VMEM((B_HEADS // 2, past, 2 * LANES), BF16),
        ],
        compiler_params=_cparams(("parallel",)),
        name="attn_b_lat",
    )(main, kv, kv, cache_k, cache_v, bias)


def _neighbourhood_bias(rpb, rows):
    kr = min(B_WIN_ROWS, rows)
    assert kr % 2 == 0 and 2 * GRID_W == LANES
    depth, heads, n_dr, n_dc = rpb.shape
    rpb_pad = jnp.pad(rpb.astype(F32), ((0, 0), (0, 0), (0, (-n_dr) % 8), (0, LANES - n_dc)))
    return pl.pallas_call(
        functools.partial(_bias_kernel, kr=kr),
        grid=(depth, heads // 2),
        in_specs=[pl.BlockSpec((None, 2, rpb_pad.shape[2], LANES), lambda l, j: (l, j, 0, 0))],
        out_specs=pl.BlockSpec((None, kr, None, 2 * GRID_W, kr * GRID_W), lambda l, j: (l, 0, j, 0, 0)),
        out_shape=jax.ShapeDtypeStruct((depth, kr, heads // 2, 2 * GRID_W, kr * GRID_W), F32),
        compiler_params=_cparams(("parallel", "parallel")),
        name="nbr_bias",
    )(rpb_pad)


def _bias_kernel(rpb_ref, o_ref, *, kr):
    shape = (GRID_W, LANES)
    c = lax.broadcasted_iota(jnp.int32, shape, 0)
    lane = lax.broadcasted_iota(jnp.int32, shape, 1)
    c2 = lane % GRID_W
    ws = jnp.clip(c - B_WIN_COLS // 2, 0, GRID_W - B_WIN_COLS)
    ok = (c2 >= ws) & (c2 < ws + B_WIN_COLS)
    low = lane < GRID_W

    def toeplitz(p, dr, lane0):
        row = jnp.broadcast_to(rpb_ref[p, dr:dr + 1, :], shape)
        return pltpu.roll(row, (lane0 - (B_WIN_COLS - 1)) % LANES, 1, stride=1, stride_axis=0)

    for p in range(2):
        pairs = {}
        for v in range(kr):
            for m in range(0, kr, 2):
                dr = m - v + B_WIN_ROWS - 1
                if dr not in pairs:
                    pair = jnp.where(low, toeplitz(p, dr, 0), toeplitz(p, dr + 1, GRID_W))
                    pairs[dr] = jnp.where(ok, pair, NEG_INF)
                o_ref[v, p * GRID_W:(p + 1) * GRID_W, m * GRID_W:(m + 2) * GRID_W] = pairs[dr]


def _dft_tables(seq):
    cd = C_GROUP_DIM
    kc = (np.arange(cd)[:, None] * np.arange(cd)[None, :]) % cd
    ang_c = 2.0 * np.pi * kc / cd
    eye2 = np.eye(2)
    bd_cos = np.kron(eye2, np.cos(ang_c))
    bd_sin = np.kron(eye2, np.sin(ang_c))
    kt = (np.arange(seq)[:, None] * np.arange(seq)[None, :]) % seq
    ang_t = 2.0 * np.pi * kt / seq
    norm = 1.0 / np.sqrt(float(seq * cd))
    pos = np.concatenate([np.cos(ang_t), -np.sin(ang_t)], axis=1) * norm
    return (jnp.asarray(bd_cos, F32).astype(BF16), jnp.asarray(bd_sin, F32).astype(BF16),
            jnp.asarray(pos, F32).astype(BF16))


def _fourier_mix(u, bc_ref, bs_ref, pos_ref, o_ref):
    pair = 2 * C_GROUP_DIM
    seq = pos_ref.shape[0]
    uc, us = [], []
    for p in range(BRANCH_DIM // pair):
        up = u[:, p * pair:(p + 1) * pair]
        uc.append(_dot(up, bc_ref[...]))
        us.append(_dot(up, bs_ref[...]))
    zc = jnp.concatenate(uc, axis=1).astype(BF16)
    zs = jnp.concatenate(us, axis=1).astype(BF16)
    for s in range(u.shape[0] // seq):
        rows = slice(s * seq, (s + 1) * seq)
        z = jnp.concatenate([zc[rows], zs[rows]], axis=0)
        o_ref[rows, :] = _dot(pos_ref[...], z).astype(o_ref.dtype)


def _merge_kernel(oa_ref, ob_ref, oc_ref, x_ref, mod_ref, gpre0_ref, gpost_ref, gpre1_ref,
                  wg_ref, wb_ref, wo_ref, x1_ref, h2_ref):
    x = x_ref[...]
    h = (_rms(x, gpre0_ref[...]) * (1.0 + mod_ref[1:2, :]) + mod_ref[0:1, :]).astype(BF16)
    mix = None
    for i, o_ref in enumerate((oa_ref, ob_ref, oc_ref)):
        gate = _sigmoid(_dot(h, wg_ref[:, i * D_MODEL:(i + 1) * D_MODEL]))
        term = gate * _dot(o_ref[...], wb_ref[i])
        mix = term if mix is None else mix + term
    y = _dot(mix.astype(BF16), wo_ref[...])
    x1 = x + mod_ref[2:3, :] * _rms(y, gpost_ref[...])
    x1_ref[...] = x1
    h2 = _rms(x1, gpre1_ref[...]) * (1.0 + mod_ref[4:5, :]) + mod_ref[3:4, :]
    h2_ref[...] = h2.astype(BF16)


def _merge(oa, ob, oc, x, mods, l, g_post, g_pre, w_in, w_branch, w_out, *, tm, mod_row):
    m = x.shape[0]
    o_spec = pl.BlockSpec((tm, BRANCH_DIM), lambda i: (i, 0))
    x_spec = pl.BlockSpec((tm, D_MODEL), lambda i: (i, 0))
    gain_spec = lambda which: pl.BlockSpec((None, None, 1, D_MODEL), lambda i: (l, which, 0, 0))
    return pl.pallas_call(
        _merge_kernel,
        grid=(m // tm,),
        in_specs=[
            o_spec, o_spec, o_spec,
            x_spec,
            pl.BlockSpec((None, None, 6, D_MODEL), lambda i: (l, mod_row(i), 0, 0)),
            gain_spec(0), gain_spec(0), gain_spec(1),
            _resident((None, D_MODEL, N_GATES), lambda i: (l, 0, 0)),
            _resident((N_BRANCH, BRANCH_DIM, D_MODEL), lambda i: (0, 0, 0)),
            _resident((D_MODEL, D_MODEL), lambda i: (0, 0)),
        ],
        out_specs=[x_spec, x_spec],
        out_shape=[jax.ShapeDtypeStruct((m, D_MODEL), F32), jax.ShapeDtypeStruct((m, D_MODEL), BF16)],
        compiler_params=_cparams(("parallel",)),
        name="merge",
    )(oa, ob, oc, x, mods, g_pre, g_post, g_pre, w_in, w_branch, w_out)


def _ffn_kernel(h_ref, wg_ref, wu_ref, wo_ref, x_ref, mod_ref, gpost_ref, o_ref, acc_ref):
    k = pl.program_id(1)

    @pl.when(k == 0)
    def _():
        acc_ref[...] = jnp.zeros_like(acc_ref)

    tm = h_ref.shape[0]
    for r0 in range(0, tm, FFN_SUB_ROWS):
        rows = slice(r0, min(r0 + FFN_SUB_ROWS, tm))
        h = h_ref[rows, :]
        act = _silu(_dot(h, wg_ref[...])) * _dot(h, wu_ref[...])
        acc_ref[rows, :] += _dot(act.astype(BF16), wo_ref[...])

    @pl.when(k == pl.num_programs(1) - 1)
    def _():
        o_ref[...] = x_ref[...] + mod_ref[5:6, :] * _rms(acc_ref[...], gpost_ref[...])


def _ffn(h2, x1, mods, l, g_post, w_ffn_in, w_ffn_out, *, tm, tf, mod_row):
    m = x1.shape[0]
    nk = D_FF // tf
    x_spec = pl.BlockSpec((tm, D_MODEL), lambda i, k: (i, 0))
    return pl.pallas_call(
        _ffn_kernel,
        grid=(m // tm, nk),
        in_specs=[
            x_spec,
            pl.BlockSpec((D_MODEL, tf), lambda i, k: (0, k)),
            pl.BlockSpec((D_MODEL, tf), lambda i, k: (0, k + nk)),
            pl.BlockSpec((tf, D_MODEL), lambda i, k: (k, 0)),
            x_spec,
            pl.BlockSpec((None, None, 6, D_MODEL), lambda i, k: (l, mod_row(i), 0, 0)),
            pl.BlockSpec((None, None, 1, D_MODEL), lambda i, k: (l, 1, 0, 0)),
        ],
        out_specs=x_spec,
        out_shape=jax.ShapeDtypeStruct((m, D_MODEL), F32),
        scratch_shapes=[pltpu.VMEM((tm, D_MODEL), F32)],
        compiler_params=_cparams(("parallel", "arbitrary")),
        name="ffn",
    )(h2, w_ffn_in, w_ffn_in, w_ffn_out, x1, mods, g_post)


def _rope_tables(seq):
    t = jnp.arange(seq, dtype=jnp.int32)
    row = (t // GRID_W).astype(F32)
    col = (t % GRID_W).astype(F32)
    n_pairs_axis = HEAD_DIM // 4
    inv = ROPE_BASE ** (-jnp.arange(n_pairs_axis, dtype=F32) / n_pairs_axis)
    ang = jnp.concatenate([row[:, None] * inv, col[:, None] * inv], axis=-1)
    cos, sin = jnp.cos(ang), jnp.sin(ang)
    cos_l = jnp.tile(cos, (1, LANES // cos.shape[1]))
    sin_l = jnp.tile(jnp.concatenate([-sin, sin], axis=-1), (1, LANES // HEAD_DIM))
    return cos_l, sin_l


def kernel(x_prompt, x_sample, cache_a_k, cache_a_v, cache_b_k, cache_b_v, c, c_ctx, w_ada, b_ada,
           norm_pre, norm_post, w_in, a_sink, b_rpb, w_branch, w_out, w_ffn_in, w_ffn_out):
    batch, seq, _ = x_prompt.shape
    dec_batch, dec_seq, _ = x_sample.shape
    past = cache_a_k.shape[2]
    assert dec_batch <= CTX_MOD_ROW and seq % A_BLOCK == 0 and dec_seq % A_BLOCK == 0

    cvec = jnp.concatenate(
        [c, c_ctx[None, :], jnp.zeros((MOD_ROWS - dec_batch - 1, D_MODEL), F32)], axis=0)
    assert w_in.shape[-1] == D_IN
    mods, w_proj_b, w_gates_b = _modulation(cvec, w_ada, b_ada, w_in)
    mods = mods.reshape(DEPTH, MOD_ROWS, 6, D_MODEL)
    late_weights = (w_branch.reshape(DEPTH, N_BRANCH * BRANCH_DIM, D_MODEL), w_out, w_ffn_in, w_ffn_out)
    layer_weights = []
    g_pre = norm_pre.reshape(DEPTH, 2, 1, D_MODEL)
    g_post = norm_post.reshape(DEPTH, 2, 1, D_MODEL)
    rope_tabs = _rope_tables(dec_seq)
    cak = cache_a_k.reshape(dec_batch, DEPTH, past, KV_A)
    cav = cache_a_v.reshape(dec_batch, DEPTH, past, KV_A)
    cbk = cache_b_k.reshape(dec_batch, DEPTH, past, BRANCH_DIM)
    cbv = cache_b_v.reshape(dec_batch, DEPTH, past, BRANCH_DIM)

    tm_proj, tm_merge, tm_ffn, tf = 1024, 1024, 1024, D_FF // 2

    def layer_tail(l, x, oa, ob, oc, mod_row_fn):
        w_branch_b, w_out_b, w_ffn_in_b, w_ffn_out_b = layer_weights[l]
        x1, h2 = _merge(oa, ob, oc, x, mods, l, g_post, g_pre, w_gates_b,
                        w_branch_b.reshape(N_BRANCH, BRANCH_DIM, D_MODEL), w_out_b,
                        tm=tm_merge, mod_row=mod_row_fn(tm_merge))
        return _ffn(h2, x1, mods, l, g_post, w_ffn_in_b, w_ffn_out_b,
                    tm=tm_ffn, tf=tf, mod_row=mod_row_fn(tm_ffn))

    ctx_row = lambda tm: (lambda i: CTX_MOD_ROW)
    x = x_prompt.reshape(batch * seq, D_MODEL)
    caches = None
    for l in range(DEPTH):
        qx, oc, *caches = _proj(x, mods, l, g_pre, w_proj_b, tm=tm_proj,
                                mod_row=ctx_row(tm_proj), seq=seq, caches=caches)
        oa, ob, *cast = _attn_ctx(qx, caches, a_sink[l], l, seq, late_weights)
        layer_weights.append(cast)
        x = layer_tail(l, x, oa, ob, oc, ctx_row)
    y_prompt = x.reshape(batch, seq, D_MODEL)
    nak, nav, nbk, nbv = caches
    new_a_k = nak.reshape(batch, DEPTH, seq, A_KV_HEADS, HEAD_DIM)
    new_a_v = nav.reshape(batch, DEPTH, seq, A_KV_HEADS, HEAD_DIM)
    new_b_k = nbk.reshape(batch, DEPTH, seq, B_HEADS, HEAD_DIM)
    new_b_v = nbv.reshape(batch, DEPTH, seq, B_HEADS, HEAD_DIM)

    lat_row = lambda tm: (lambda i: (i * tm) // dec_seq)
    x = x_sample.reshape(dec_batch * dec_seq, D_MODEL)
    bias = _neighbourhood_bias(b_rpb, dec_seq // GRID_W)
    for l in range(DEPTH):
        qx, kv, oc = _proj(x, mods, l, g_pre, w_proj_b, tm=tm_proj,
                           mod_row=lat_row(tm_proj), seq=dec_seq, rope_tabs=rope_tabs)
        oa = _attn_a_lat(qx, qx, cak, cav, a_sink[l], l, dec_seq)
        ob = _attn_b_lat(qx, kv, cbk, cbv, bias, l, dec_seq)
        x = layer_tail(l, x, oa, ob, oc, lat_row)
    y_sample = x.reshape(dec_batch, dec_seq, D_MODEL)
    return (y_prompt, y_sample, new_a_k, new_a_v, new_b_k, new_b_v)
```

```python
import functools

import numpy as np
import jax
import jax.numpy as jnp
from jax import lax
from jax.experimental import pallas as pl
from jax.experimental.pallas import tpu as pltpu

F32 = jnp.float32
BF16 = jnp.bfloat16

D_MODEL = 1024
DEPTH = 2
GRID_W = 64
HEAD_DIM = 64
BRANCH_DIM = D_MODEL // 2
A_Q_HEADS = BRANCH_DIM // HEAD_DIM
A_KV_HEADS = A_Q_HEADS // 4
A_GROUP = A_Q_HEADS // A_KV_HEADS
A_BLOCK = 128
B_HEADS = BRANCH_DIM // HEAD_DIM
B_WIN_ROWS = 8
B_WIN_COLS = 16
C_GROUPS = 4
C_GROUP_DIM = BRANCH_DIM // C_GROUPS
N_BRANCH = 3
D_FF = -(-8 * D_MODEL // (3 * 256)) * 256
ROPE_BASE = 10000.0
NORM_EPS = 1e-6
NEG_INF = -1e30
SCALE = HEAD_DIM ** -0.5

LANES = 128
KV_A = A_KV_HEADS * HEAD_DIM
N_GATES = N_BRANCH * D_MODEL
W_QA = 0
W_KA = W_QA + BRANCH_DIM
W_VA = W_KA + KV_A
W_QB = W_VA + KV_A
W_KB = W_QB + BRANCH_DIM
W_VB = W_KB + BRANCH_DIM
W_UC = W_VB + BRANCH_DIM
W_GATES = W_UC + BRANCH_DIM
D_IN = W_GATES + N_GATES
QX_QA, QX_QB, QX_KVA = 0, BRANCH_DIM, 2 * BRANCH_DIM
KV_KB, KV_VB = 0, BRANCH_DIM
MOD_ROWS = 8
CTX_MOD_ROW = 4
VMEM_LIMIT = 56 * 1024 * 1024


def _cparams(sem):
    return pltpu.CompilerParams(dimension_semantics=sem, vmem_limit_bytes=VMEM_LIMIT)


def _resident(shape, index_map):
    return pl.BlockSpec(shape, index_map, pipeline_mode=pl.Buffered(1))


def _rms(x, g):
    return x * lax.rsqrt(jnp.mean(x * x, axis=-1, keepdims=True) + NORM_EPS) * g


def _sigmoid(x):
    return 0.5 * jnp.tanh(0.5 * x) + 0.5


def _silu(x):
    half = 0.5 * x
    return half + half * jnp.tanh(half)


def _dot_t(a, b):
    return lax.dot_general(a, b, (((1,), (1,)), ((), ())), preferred_element_type=F32)


def _dot(a, b):
    return jnp.dot(a, b, preferred_element_type=F32)


def _scaled(q):
    assert np.log2(SCALE) == round(np.log2(SCALE))
    return q * jnp.asarray(SCALE, q.dtype)


def _low_lanes(shape):
    return lax.broadcasted_iota(jnp.int32, shape, len(shape) - 1) < HEAD_DIM


def _mod_kernel(cv_ref, w_ref, b_ref, win_ref, o_ref, wproj_ref, wgates_ref):
    a, b = _silu(cv_ref[...]), w_ref[...]
    a_hi, b_hi = a.astype(BF16), b.astype(BF16)
    a_lo = (a - a_hi.astype(F32)).astype(BF16)
    b_lo = (b - b_hi.astype(F32)).astype(BF16)
    o_ref[...] = _dot(a_hi, b_hi) + (_dot(a_lo, b_hi) + _dot(a_hi, b_lo)) + b_ref[...]
    wproj_ref[...] = win_ref[:, :W_GATES].astype(wproj_ref.dtype)
    wgates_ref[...] = win_ref[:, W_GATES:].astype(wgates_ref.dtype)


def _modulation(cvec, w_ada, b_ada, w_in):
    tn = 1536
    n = 6 * D_MODEL
    steps = n // tn
    rows = w_in.shape[1] // steps
    return pl.pallas_call(
        _mod_kernel,
        grid=(DEPTH, steps),
        in_specs=[
            pl.BlockSpec((MOD_ROWS, D_MODEL), lambda l, j: (0, 0)),
            pl.BlockSpec((None, D_MODEL, tn), lambda l, j: (l, 0, j)),
            pl.BlockSpec((None, 1, tn), lambda l, j: (l, 0, j)),
            pl.BlockSpec((None, rows, D_IN), lambda l, j: (l, j, 0)),
        ],
        out_specs=[pl.BlockSpec((None, MOD_ROWS, tn), lambda l, j: (l, 0, j)),
                   pl.BlockSpec((None, rows, W_GATES), lambda l, j: (l, j, 0)),
                   pl.BlockSpec((None, rows, N_GATES), lambda l, j: (l, j, 0))],
        out_shape=[jax.ShapeDtypeStruct((DEPTH, MOD_ROWS, n), F32),
                   jax.ShapeDtypeStruct((DEPTH, D_MODEL, W_GATES), BF16),
                   jax.ShapeDtypeStruct((DEPTH, D_MODEL, N_GATES), BF16)],
        compiler_params=_cparams(("parallel", "parallel")),
        name="modulation",
    )(cvec, w_ada, b_ada.reshape(DEPTH, 1, n), w_in)


def _rope_cols(v, cos, sin):
    first = (lax.broadcasted_iota(jnp.int32, (v.shape[0], LANES), 1) % HEAD_DIM) < HEAD_DIM // 2
    outs = []
    for c in range(v.shape[1] // LANES):
        vc = v[:, c * LANES:(c + 1) * LANES]
        partner = jnp.where(first, pltpu.roll(vc, LANES - HEAD_DIM // 2, 1),
                            pltpu.roll(vc, HEAD_DIM // 2, 1))
        outs.append(vc * cos + partner * sin)
    return outs[0] if len(outs) == 1 else jnp.concatenate(outs, axis=1)


def _with_rope(acc, lo, hi, cos_ref, sin_ref):
    parts = []
    if lo > 0:
        parts.append(acc[:, :lo])
    parts.append(_rope_cols(acc[:, lo:hi], cos_ref[...], sin_ref[...]))
    if hi < acc.shape[1]:
        parts.append(acc[:, hi:])
    return parts[0] if len(parts) == 1 else jnp.concatenate(parts, axis=1)


def _proj_kernel(*refs, latent, n_alias):
    x_ref, mod_ref, g_ref, w_ref, bc_ref, bs_ref, pos_ref = refs[:7]
    if latent:
        cos_ref, sin_ref, qx_ref, kv_ref, oc_ref, h_scr = refs[7:]
    else:
        qx_ref, oc_ref, ka_ref, va_ref, kb_ref, vb_ref, h_scr = refs[7 + n_alias:]
    j = pl.program_id(1)

    @pl.when(j == 0)
    def _():
        h = _rms(x_ref[...], g_ref[...]) * (1.0 + mod_ref[1:2, :]) + mod_ref[0:1, :]
        h_scr[...] = h.astype(BF16)
        acc = _dot(h_scr[...], w_ref[:, W_QA:W_KB])
        qa = acc[:, W_QA:W_QA + BRANCH_DIM]
        qb = acc[:, W_QB:W_QB + BRANCH_DIM]
        kva = acc[:, W_KA:W_KA + 2 * KV_A]
        if latent:
            qa = _rope_cols(qa, cos_ref[...], sin_ref[...])
            kva = _with_rope(kva, 0, KV_A, cos_ref, sin_ref)
            qx_ref[:, QX_KVA:QX_KVA + 2 * KV_A] = kva.astype(qx_ref.dtype)
        else:
            ka_ref[...] = kva[:, :KV_A].reshape(ka_ref.shape)
            va_ref[...] = kva[:, KV_A:].reshape(va_ref.shape)
        qx_ref[:, QX_QA:QX_QA + BRANCH_DIM] = qa.astype(qx_ref.dtype)
        qx_ref[:, QX_QB:QX_QB + BRANCH_DIM] = qb.astype(qx_ref.dtype)

    @pl.when(j == 1)
    def _():
        acc = _dot(h_scr[...], w_ref[:, W_KB:W_GATES])
        kb = acc[:, 0:BRANCH_DIM]
        vb = acc[:, W_VB - W_KB:W_VB - W_KB + BRANCH_DIM]
        uc = acc[:, W_UC - W_KB:W_UC - W_KB + BRANCH_DIM]
        if latent:
            kv_ref[:, KV_KB:KV_KB + BRANCH_DIM] = kb.astype(kv_ref.dtype)
            kv_ref[:, KV_VB:KV_VB + BRANCH_DIM] = vb.astype(kv_ref.dtype)
        else:
            kb_ref[...] = kb.reshape(kb_ref.shape)
            vb_ref[...] = vb.reshape(vb_ref.shape)
        _fourier_mix(uc.astype(BF16), bc_ref, bs_ref, pos_ref, oc_ref)


def _proj(x, mods, l, g_pre, w_in, *, tm, mod_row, seq, rope_tabs=None, caches=None):
    m = x.shape[0]
    assert tm % seq == 0
    latent = rope_tabs is not None
    bc, bs, pos = _dft_tables(seq)
    pair = 2 * C_GROUP_DIM
    in_specs = [
        pl.BlockSpec((tm, D_MODEL), lambda i, j: (i, 0)),
        pl.BlockSpec((None, None, 6, D_MODEL), lambda i, j: (l, mod_row(i), 0, 0)),
        pl.BlockSpec((None, None, 1, D_MODEL), lambda i, j: (l, 0, 0, 0)),
        _resident((None, D_MODEL, W_GATES), lambda i, j: (l, 0, 0)),
        _resident((pair, pair), lambda i, j: (0, 0)),
        _resident((pair, pair), lambda i, j: (0, 0)),
        _resident((seq, 2 * seq), lambda i, j: (0, 0)),
    ]
    args = [x, mods, g_pre, w_in, bc, bs, pos]
    aliases = {}
    if latent:
        seq_tiles = seq // tm
        tab_spec = pl.BlockSpec((tm, LANES), lambda i, j: (i % seq_tiles, 0))
        in_specs += [tab_spec, tab_spec]
        args += list(rope_tabs)
        widths = (2 * BRANCH_DIM + 2 * KV_A, 2 * BRANCH_DIM, BRANCH_DIM)
        cache_specs, cache_shapes = [], []
    else:
        widths = (2 * BRANCH_DIM, BRANCH_DIM)
        cache_widths = (KV_A, KV_A, BRANCH_DIM, BRANCH_DIM)
        if caches is not None:
            in_specs += [pl.BlockSpec(memory_space=pl.ANY)] * len(caches)
            aliases = {len(args) + k: len(widths) + k for k in range(len(caches))}
            args += list(caches)
        cache_specs = [pl.BlockSpec((tm // seq, None, seq, w), lambda i, j: (i, l, 0, 0))
                       for w in cache_widths]
        cache_shapes = [jax.ShapeDtypeStruct((m // seq, DEPTH, seq, w), F32) for w in cache_widths]
    return pl.pallas_call(
        functools.partial(_proj_kernel, latent=latent, n_alias=len(aliases)),
        grid=(m // tm, 2),
        in_specs=in_specs,
        out_specs=[pl.BlockSpec((tm, w), lambda i, j: (i, 0)) for w in widths] + cache_specs,
        out_shape=[jax.ShapeDtypeStruct((m, w), BF16) for w in widths] + cache_shapes,
        input_output_aliases=aliases,
        scratch_shapes=[pltpu.VMEM((tm, D_MODEL), BF16)],
        compiler_params=_cparams(("parallel", "arbitrary")),
        name="proj_lat" if latent else "proj_ctx",
    )(*args)


def _with_ones(v):
    return jnp.concatenate([v, jnp.ones_like(v)], axis=1)


def _softmax_pv(parts, sink=None):
    m = parts[0][0].max(axis=-1, keepdims=True)
    for s, _ in parts[1:]:
        m = jnp.maximum(m, s.max(axis=-1, keepdims=True))
    if sink is not None:
        m = jnp.maximum(m, sink)
    ones_half = parts[0][1].shape[1] == 2 * LANES
    acc = None
    den = None
    for s, v in parts:
        e = jnp.exp(s - m)
        if not ones_half:
            d = e.sum(axis=-1, keepdims=True)
            den = d if den is None else den + d
        o = _dot(e.astype(BF16), v)
        acc = o if acc is None else acc + o
    if ones_half:
        acc, den = acc[:, :LANES], acc[:, LANES:]
    if sink is not None:
        den = den + jnp.exp(sink - m)
    return acc * (1.0 / den)


def _kv_head_variants(x2):
    low = _low_lanes(x2.shape)
    xr = pltpu.roll(x2, HEAD_DIM, 1)
    zero = jnp.zeros_like(x2)
    return [
        [jnp.where(low, x2, zero).astype(BF16), jnp.where(low, zero, xr).astype(BF16)],
        [jnp.where(low, xr, zero).astype(BF16), jnp.where(low, zero, x2).astype(BF16)],
    ]


def _attn_ctx_kernel(sink_ref, qa_ref, qb_ref, ka_ref, va_ref, kb_ref, vb_ref, *rest, n_weights):
    w_src, (oa_ref, ob_ref), w_dst = rest[:n_weights], rest[n_weights:n_weights + 2], rest[n_weights + 2:]
    for src, dst in zip(w_src, w_dst):
        dst[...] = src[...].astype(dst.dtype)
    ka = [jnp.concatenate(v, axis=0) for v in _kv_head_variants(ka_ref[...])]
    va = [jnp.concatenate(v, axis=0) for v in _kv_head_variants(va_ref[...])]
    chunks = [slice(j * LANES, (j + 1) * LANES) for j in range(A_Q_HEADS // 2)]
    work = []
    for j, cols in enumerate(chunks):
        g = (2 * j) // A_GROUP
        s2 = _dot_t(_scaled(qa_ref[:, cols]), ka[g])
        work.append((oa_ref, cols, s2, va[g], (sink_ref[2 * j], sink_ref[2 * j + 1])))
    low = _low_lanes((kb_ref.shape[0], LANES))
    for cols in chunks:
        kc, vc = kb_ref[:, cols], vb_ref[:, cols]
        zero = jnp.zeros_like(kc)
        k2 = jnp.concatenate([jnp.where(low, kc, zero), jnp.where(low, zero, kc)], axis=0).astype(BF16)
        v2 = jnp.concatenate([jnp.where(low, vc, zero), jnp.where(low, zero, vc)], axis=0).astype(BF16)
        work.append((ob_ref, cols, _dot_t(_scaled(qb_ref[:, cols]), k2), v2, None))
    for o_ref, cols, s2, v2, sinks in work:
        o_ref[:, cols] = _pair_softmax_pv(s2, v2, sinks).astype(o_ref.dtype)


def _pair_softmax_pv(s2, v2, sinks=None):
    tk = s2.shape[1] // 2
    es, extra = [], []
    for p in range(2):
        s = s2[:, p * tk:(p + 1) * tk]
        m = s.max(axis=-1, keepdims=True)
        if sinks is not None:
            m = jnp.maximum(m, sinks[p])
            extra.append(jnp.exp(sinks[p] - m))
        es.append(jnp.exp(s - m).astype(BF16))
    lane_half = lax.broadcasted_iota(jnp.int32, v2.shape, 1) // HEAD_DIM
    row_half = lax.broadcasted_iota(jnp.int32, v2.shape, 0) // tk
    ones2 = jnp.where(lane_half == row_half, 1.0, 0.0).astype(v2.dtype)
    acc = _dot(jnp.concatenate(es, axis=1), jnp.concatenate([v2, ones2], axis=1))
    den = acc[:, LANES:]
    if sinks is not None:
        den = den + jnp.where(_low_lanes(den.shape), extra[0], extra[1])
    return acc[:, :LANES] * (1.0 / den)


def _attn_ctx(qx, caches, sink, l, seq, weights):
    m = qx.shape[0]
    steps = m // seq
    out = jax.ShapeDtypeStruct((m, BRANCH_DIM), BF16)
    cache_specs = [pl.BlockSpec((None, None, seq, c.shape[-1]), lambda b: (b, l, 0, 0)) for c in caches]
    w_in_specs, w_out_specs, w_shapes = [], [], []
    for w in weights:
        _, rows, cols = w.shape
        assert rows % (steps * 16) == 0
        w_in_specs.append(pl.BlockSpec((None, rows // steps, cols), lambda b: (l, b, 0)))
        w_out_specs.append(pl.BlockSpec((rows // steps, cols), lambda b: (b, 0)))
        w_shapes.append(jax.ShapeDtypeStruct((rows, cols), BF16))
    return pl.pallas_call(
        functools.partial(_attn_ctx_kernel, n_weights=len(weights)),
        grid=(steps,),
        in_specs=[
            pl.BlockSpec(memory_space=pltpu.SMEM),
            pl.BlockSpec((seq, BRANCH_DIM), lambda b: (b, QX_QA // BRANCH_DIM)),
            pl.BlockSpec((seq, BRANCH_DIM), lambda b: (b, QX_QB // BRANCH_DIM)),
        ] + cache_specs + w_in_specs,
        out_specs=[pl.BlockSpec((seq, BRANCH_DIM), lambda b: (b, 0))] * 2 + w_out_specs,
        out_shape=[out, out] + w_shapes,
        compiler_params=_cparams(("parallel",)),
        name="attn_ctx",
    )(sink, qx, qx, *caches, *weights)


def _attn_a_lat_kernel(sink_ref, q_ref, kv_ref, ck_ref, cv_ref, o_ref,
                       qh_scr, kp_scr, vp_scr, ckb_scr, cvb_scr, mask_scr, *, n_blocks):
    seq = n_blocks * A_BLOCK
    band = 3 * A_BLOCK
    low_seq = _low_lanes((seq, LANES))
    for h in range(A_Q_HEADS):
        j, p, g = h // 2, h % 2, h // A_GROUP
        x = q_ref[:, j * LANES:(j + 1) * LANES].astype(F32) * SCALE
        if p != g:
            x = pltpu.roll(x, HEAD_DIM, 1)
        qh_scr[h] = jnp.where(low_seq if g == 0 else jnp.logical_not(low_seq), x, 0.0).astype(BF16)
    pad = jnp.zeros((A_BLOCK, LANES), BF16)
    for scr, col in ((kp_scr, 0), (vp_scr, KV_A)):
        scr[0:A_BLOCK, :] = pad
        scr[A_BLOCK:A_BLOCK + seq, :] = kv_ref[:, col:col + KV_A]
        scr[A_BLOCK + seq:2 * A_BLOCK + seq, :] = pad
    ckb_scr[...] = ck_ref[...].astype(BF16)
    cvb_scr[...] = cv_ref[...].astype(BF16)

    rows = A_GROUP * A_BLOCK
    row = lax.broadcasted_iota(jnp.int32, (rows, band), 0)
    qi = row % A_BLOCK
    kj = lax.broadcasted_iota(jnp.int32, (rows, band), 1)
    head_row = lax.broadcasted_iota(jnp.int32, (rows, 1), 0) // A_BLOCK
    low_blk = _low_lanes((A_BLOCK, LANES))
    for case, (no_prev, no_next) in enumerate(((1, 0), (0, 0), (0, 1))):
        valid = (kj >= qi + no_prev * (A_BLOCK - qi)) & (kj <= 2 * A_BLOCK + qi - no_next * (qi + 1))
        mask_scr[case] = jnp.where(valid, 0.0, NEG_INF)

    def body(n, carry):
        start = pl.multiple_of(n * A_BLOCK, A_BLOCK)
        case = jnp.where(n > 0, 1, 0) + jnp.where(n < n_blocks - 1, 0, 1)
        scores = []
        for g in range(A_KV_HEADS):
            q = jnp.concatenate(
                [qh_scr[A_GROUP * g + i, pl.ds(start, A_BLOCK), :] for i in range(A_GROUP)], axis=0)
            s_band = _dot_t(q, kp_scr[pl.ds(start, band), :]) + mask_scr[case]
            scores.append((s_band, _dot_t(q, ckb_scr[...])))
        outs = []
        for g, (s_band, s_ctx) in enumerate(scores):
            sink = jnp.full((rows, 1), sink_ref[A_GROUP * g], F32)
            for i in range(1, A_GROUP):
                sink = jnp.where(head_row == i, sink_ref[A_GROUP * g + i], sink)
            outs.append(_softmax_pv([(s_band, vp_scr[pl.ds(start, band), :]), (s_ctx, cvb_scr[...])],
                                    sink=sink))
        for j in range(A_Q_HEADS // 2):
            halves = []
            for p in range(2):
                h = 2 * j + p
                g, i = h // A_GROUP, h % A_GROUP
                t = outs[g][i * A_BLOCK:(i + 1) * A_BLOCK]
                halves.append(t if p == g else pltpu.roll(t, HEAD_DIM, 1))
            o_ref[pl.ds(start, A_BLOCK), j * LANES:(j + 1) * LANES] = (
                jnp.where(low_blk, halves[0], halves[1]).astype(o_ref.dtype))
        return carry

    lax.fori_loop(0, n_blocks, body, 0, unroll=2)


def _attn_a_lat(main, kv, cache_k, cache_v, sink, l, seq):
    m = main.shape[0]
    nb = seq // A_BLOCK
    assert nb >= 2
    n_seq = m // seq
    past = cache_k.shape[2]
    cache_spec = pl.BlockSpec((None, None, past, KV_A), lambda b: (b, l, 0, 0))
    return pl.pallas_call(
        functools.partial(_attn_a_lat_kernel, n_blocks=nb),
        grid=(n_seq,),
        in_specs=[
            pl.BlockSpec(memory_space=pltpu.SMEM),
            pl.BlockSpec((seq, BRANCH_DIM), lambda b: (b, QX_QA // BRANCH_DIM)),
            pl.BlockSpec((seq, 2 * KV_A), lambda b: (b, QX_KVA // (2 * KV_A))),
            cache_spec, cache_spec,
        ],
        out_specs=pl.BlockSpec((seq, BRANCH_DIM), lambda b: (b, 0)),
        out_shape=jax.ShapeDtypeStruct((m, BRANCH_DIM), BF16),
        scratch_shapes=[
            pltpu.VMEM((A_Q_HEADS, seq, LANES), BF16),
            pltpu.VMEM((seq + 2 * A_BLOCK, LANES), BF16),
            pltpu.VMEM((seq + 2 * A_BLOCK, LANES), BF16),
            pltpu.VMEM((past, LANES), BF16),
            pltpu.VMEM((past, LANES), BF16),
            pltpu.VMEM((3, A_GROUP * A_BLOCK, 3 * A_BLOCK), F32),
        ],
        compiler_params=_cparams(("parallel",)),
        name="attn_a_lat",
    )(sink, main, kv, cache_k, cache_v)


def _attn_b_lat_kernel(q_ref, k_ref, v_ref, ck_ref, cv_ref, bias_ref, o_ref, ckb_scr, vx_scr, cvx_scr,
                       *, rows, kr):
    half = kr // 2
    n_loc = kr * GRID_W
    low_q = _low_lanes((GRID_W, LANES))
    chunks = [slice(j * LANES, (j + 1) * LANES) for j in range(B_HEADS // 2)]
    ckb_scr[...] = ck_ref[...].astype(BF16)
    for j, cols in enumerate(chunks):
        vx_scr[j] = _with_ones(v_ref[:, cols])
        cvx_scr[j] = _with_ones(cv_ref[:, cols].astype(BF16))
    rows_per_iter = 4
    assert rows % rows_per_iter == 0

    def body(it, carry):
        work = []
        for u in range(rows_per_iter):
            r = it * rows_per_iter + u
            r0 = jnp.clip(r - half, 0, rows - kr)
            q0 = pl.multiple_of(r * GRID_W, GRID_W)
            start = pl.multiple_of(r0 * GRID_W, GRID_W)
            for j, cols in enumerate(chunks):
                qc = _scaled(q_ref[pl.ds(q0, GRID_W), cols])
                zero = jnp.zeros_like(qc)
                qs = jnp.concatenate([jnp.where(low_q, qc, zero), jnp.where(low_q, zero, qc)], axis=0)
                s_loc = _dot_t(qs, k_ref[pl.ds(start, n_loc), cols]) + bias_ref[r - r0, j]
                work.append((q0, start, j, cols, s_loc, _dot_t(qs, ckb_scr[:, cols])))
        for q0, start, j, cols, s_loc, s_ctx in work:
            o2 = _softmax_pv([(s_loc, vx_scr[j, pl.ds(start, n_loc), :]), (s_ctx, cvx_scr[j])])
            o = jnp.where(low_q, o2[:GRID_W], o2[GRID_W:])
            o_ref[pl.ds(q0, GRID_W), cols] = o.astype(o_ref.dtype)
        return carry

    lax.fori_loop(0, rows // rows_per_iter, body, 0)


def _attn_b_lat(main, kv, cache_k, cache_v, bias, l, seq):
    m = main.shape[0]
    rows = seq // GRID_W
    kr = min(B_WIN_ROWS, rows)
    n_seq = m // seq
    past = cache_k.shape[2]
    cache_spec = pl.BlockSpec((None, None, past, BRANCH_DIM), lambda b: (b, l, 0, 0))
    return pl.pallas_call(
        functools.partial(_attn_b_lat_kernel, rows=rows, kr=kr),
        grid=(n_seq,),
        in_specs=[
            pl.BlockSpec((seq, BRANCH_DIM), lambda b: (b, QX_QB // BRANCH_DIM)),
            pl.BlockSpec((seq, BRANCH_DIM), lambda b: (b, KV_KB // BRANCH_DIM)),
            pl.BlockSpec((seq, BRANCH_DIM), lambda b: (b, KV_VB // BRANCH_DIM)),
            cache_spec, cache_spec,
            _resident((None,) + bias.shape[1:], lambda b: (l, 0, 0, 0, 0)),
        ],
        out_specs=pl.BlockSpec((seq, BRANCH_DIM), lambda b: (b, 0)),
        out_shape=jax.ShapeDtypeStruct((m, BRANCH_DIM), BF16),
        scratch_shapes=[
            pltpu.VMEM((past, BRANCH_DIM), BF16),
            pltpu.VMEM((B_HEADS // 2, seq, 2 * LANES), BF16),
            pltpu.VMEM((B_HEADS // 2, past, 2 * LANES), BF16),
        ],
        compiler_params=_cparams(("parallel",)),
        name="attn_b_lat",
    )(main, kv, kv, cache_k, cache_v, bias)


def _neighbourhood_bias(rpb, rows):
    kr = min(B_WIN_ROWS, rows)
    assert kr % 2 == 0 and 2 * GRID_W == LANES
    depth, heads, n_dr, n_dc = rpb.shape
    rpb_pad = jnp.pad(rpb.astype(F32), ((0, 0), (0, 0), (0, (-n_dr) % 8), (0, LANES - n_dc)))
    return pl.pallas_call(
        functools.partial(_bias_kernel, kr=kr),
        grid=(depth, heads // 2),
        in_specs=[pl.BlockSpec((None, 2, rpb_pad.shape[2], LANES), lambda l, j: (l, j, 0, 0))],
        out_specs=pl.BlockSpec((None, kr, None, 2 * GRID_W, kr * GRID_W), lambda l, j: (l, 0, j, 0, 0)),
        out_shape=jax.ShapeDtypeStruct((depth, kr, heads // 2, 2 * GRID_W, kr * GRID_W), F32),
        compiler_params=_cparams(("parallel", "parallel")),
        name="nbr_bias",
    )(rpb_pad)


def _bias_kernel(rpb_ref, o_ref, *, kr):
    shape = (GRID_W, LANES)
    c = lax.broadcasted_iota(jnp.int32, shape, 0)
    lane = lax.broadcasted_iota(jnp.int32, shape, 1)
    c2 = lane % GRID_W
    ws = jnp.clip(c - B_WIN_COLS // 2, 0, GRID_W - B_WIN_COLS)
    ok = (c2 >= ws) & (c2 < ws + B_WIN_COLS)
    low = lane < GRID_W

    def toeplitz(p, dr, lane0):
        row = jnp.broadcast_to(rpb_ref[p, dr:dr + 1, :], shape)
        return pltpu.roll(row, (lane0 - (B_WIN_COLS - 1)) % LANES, 1, stride=1, stride_axis=0)

    for p in range(2):
        pairs = {}
        for v in range(kr):
            for m in range(0, kr, 2):
                dr = m - v + B_WIN_ROWS - 1
                if dr not in pairs:
                    pair = jnp.where(low, toeplitz(p, dr, 0), toeplitz(p, dr + 1, GRID_W))
                    pairs[dr] = jnp.where(ok, pair, NEG_INF)
                o_ref[v, p * GRID_W:(p + 1) * GRID_W, m * GRID_W:(m + 2) * GRID_W] = pairs[dr]


def _dft_tables(seq):
    cd = C_GROUP_DIM
    kc = (np.arange(cd)[:, None] * np.arange(cd)[None, :]) % cd
    ang_c = 2.0 * np.pi * kc / cd
    eye2 = np.eye(2)
    bd_cos = np.kron(eye2, np.cos(ang_c))
    bd_sin = np.kron(eye2, np.sin(ang_c))
    kt = (np.arange(seq)[:, None] * np.arange(seq)[None, :]) % seq
    ang_t = 2.0 * np.pi * kt / seq
    norm = 1.0 / np.sqrt(float(seq * cd))
    pos = np.concatenate([np.cos(ang_t), -np.sin(ang_t)], axis=1) * norm
    return (jnp.asarray(bd_cos, F32).astype(BF16), jnp.asarray(bd_sin, F32).astype(BF16),
            jnp.asarray(pos, F32).astype(BF16))


def _fourier_mix(u, bc_ref, bs_ref, pos_ref, o_ref):
    pair = 2 * C_GROUP_DIM
    seq = pos_ref.shape[0]
    uc, us = [], []
    for p in range(BRANCH_DIM // pair):
        up = u[:, p * pair:(p + 1) * pair]
        uc.append(_dot(up, bc_ref[...]))
        us.append(_dot(up, bs_ref[...]))
    zc = jnp.concatenate(uc, axis=1).astype(BF16)
    zs = jnp.concatenate(us, axis=1).astype(BF16)
    for s in range(u.shape[0] // seq):
        rows = slice(s * seq, (s + 1) * seq)
        z = jnp.concatenate([zc[rows], zs[rows]], axis=0)
        o_ref[rows, :] = _dot(pos_ref[...], z).astype(o_ref.dtype)


def _merge_kernel(oa_ref, ob_ref, oc_ref, x_ref, mod_ref, gpre0_ref, gpost_ref, gpre1_ref,
                  wg_ref, wb_ref, wo_ref, x1_ref, h2_ref):
    x = x_ref[...]
    h = (_rms(x, gpre0_ref[...]) * (1.0 + mod_ref[1:2, :]) + mod_ref[0:1, :]).astype(BF16)
    mix = None
    for i, o_ref in enumerate((oa_ref, ob_ref, oc_ref)):
        gate = _sigmoid(_dot(h, wg_ref[:, i * D_MODEL:(i + 1) * D_MODEL]))
        term = gate * _dot(o_ref[...], wb_ref[i])
        mix = term if mix is None else mix + term
    y = _dot(mix.astype(BF16), wo_ref[...])
    x1 = x + mod_ref[2:3, :] * _rms(y, gpost_ref[...])
    x1_ref[...] = x1
    h2 = _rms(x1, gpre1_ref[...]) * (1.0 + mod_ref[4:5, :]) + mod_ref[3:4, :]
    h2_ref[...] = h2.astype(BF16)


def _merge(oa, ob, oc, x, mods, l, g_post, g_pre, w_in, w_branch, w_out, *, tm, mod_row):
    m = x.shape[0]
    o_spec = pl.BlockSpec((tm, BRANCH_DIM), lambda i: (i, 0))
    x_spec = pl.BlockSpec((tm, D_MODEL), lambda i: (i, 0))
    gain_spec = lambda which: pl.BlockSpec((None, None, 1, D_MODEL), lambda i: (l, which, 0, 0))
    return pl.pallas_call(
        _merge_kernel,
        grid=(m // tm,),
        in_specs=[
            o_spec, o_spec, o_spec,
            x_spec,
            pl.BlockSpec((None, None, 6, D_MODEL), lambda i: (l, mod_row(i), 0, 0)),
            gain_spec(0), gain_spec(0), gain_spec(1),
            _resident((None, D_MODEL, N_GATES), lambda i: (l, 0, 0)),
            _resident((N_BRANCH, BRANCH_DIM, D_MODEL), lambda i: (0, 0, 0)),
            _resident((D_MODEL, D_MODEL), lambda i: (0, 0)),
        ],
        out_specs=[x_spec, x_spec],
        out_shape=[jax.ShapeDtypeStruct((m, D_MODEL), F32), jax.ShapeDtypeStruct((m, D_MODEL), BF16)],
        compiler_params=_cparams(("parallel",)),
        name="merge",
    )(oa, ob, oc, x, mods, g_pre, g_post, g_pre, w_in, w_branch, w_out)


def _ffn_kernel(h_ref, wg_ref, wu_ref, wo_ref, x_ref, mod_ref, gpost_ref, o_ref, acc_ref):
    k = pl.program_id(1)

    @pl.when(k == 0)
    def _():
        acc_ref[...] = jnp.zeros_like(acc_ref)

    h = h_ref[...]
    act = _silu(_dot(h, wg_ref[...])) * _dot(h, wu_ref[...])
    acc_ref[...] += _dot(act.astype(BF16), wo_ref[...])

    @pl.when(k == pl.num_programs(1) - 1)
    def _():
        o_ref[...] = x_ref[...] + mod_ref[5:6, :] * _rms(acc_ref[...], gpost_ref[...])


def _ffn(h2, x1, mods, l, g_post, w_ffn_in, w_ffn_out, *, tm, tf, mod_row):
    m = x1.shape[0]
    nk = D_FF // tf
    x_spec = pl.BlockSpec((tm, D_MODEL), lambda i, k: (i, 0))
    return pl.pallas_call(
        _ffn_kernel,
        grid=(m // tm, nk),
        in_specs=[
            x_spec,
            pl.BlockSpec((D_MODEL, tf), lambda i, k: (0, k)),
            pl.BlockSpec((D_MODEL, tf), lambda i, k: (0, k + nk)),
            pl.BlockSpec((tf, D_MODEL), lambda i, k: (k, 0)),
            x_spec,
            pl.BlockSpec((None, None, 6, D_MODEL), lambda i, k: (l, mod_row(i), 0, 0)),
            pl.BlockSpec((None, None, 1, D_MODEL), lambda i, k: (l, 1, 0, 0)),
        ],
        out_specs=x_spec,
        out_shape=jax.ShapeDtypeStruct((m, D_MODEL), F32),
        scratch_shapes=[pltpu.VMEM((tm, D_MODEL), F32)],
        compiler_params=_cparams(("parallel", "arbitrary")),
        name="ffn",
    )(h2, w_ffn_in, w_ffn_in, w_ffn_out, x1, mods, g_post)


def _rope_tables(seq):
    t = jnp.arange(seq, dtype=jnp.int32)
    row = (t // GRID_W).astype(F32)
    col = (t % GRID_W).astype(F32)
    n_pairs_axis = HEAD_DIM // 4
    inv = ROPE_BASE ** (-jnp.arange(n_pairs_axis, dtype=F32) / n_pairs_axis)
    ang = jnp.concatenate([row[:, None] * inv, col[:, None] * inv], axis=-1)
    cos, sin = jnp.cos(ang), jnp.sin(ang)
    cos_l = jnp.tile(cos, (1, LANES // cos.shape[1]))
    sin_l = jnp.tile(jnp.concatenate([-sin, sin], axis=-1), (1, LANES // HEAD_DIM))
    return cos_l, sin_l


def kernel(x_prompt, x_sample, cache_a_k, cache_a_v, cache_b_k, cache_b_v, c, c_ctx, w_ada, b_ada,
           norm_pre, norm_post, w_in, a_sink, b_rpb, w_branch, w_out, w_ffn_in, w_ffn_out):
    batch, seq, _ = x_prompt.shape
    dec_batch, dec_seq, _ = x_sample.shape
    past = cache_a_k.shape[2]
    assert dec_batch <= CTX_MOD_ROW and seq % A_BLOCK == 0 and dec_seq % A_BLOCK == 0

    cvec = jnp.concatenate(
        [c, c_ctx[None, :], jnp.zeros((MOD_ROWS - dec_batch - 1, D_MODEL), F32)], axis=0)
    assert w_in.shape[-1] == D_IN
    mods, w_proj_b, w_gates_b = _modulation(cvec, w_ada, b_ada, w_in)
    mods = mods.reshape(DEPTH, MOD_ROWS, 6, D_MODEL)
    late_weights = (w_branch.reshape(DEPTH, N_BRANCH * BRANCH_DIM, D_MODEL), w_out, w_ffn_in, w_ffn_out)
    layer_weights = []
    g_pre = norm_pre.reshape(DEPTH, 2, 1, D_MODEL)
    g_post = norm_post.reshape(DEPTH, 2, 1, D_MODEL)
    rope_tabs = _rope_tables(dec_seq)
    cak = cache_a_k.reshape(dec_batch, DEPTH, past, KV_A)
    cav = cache_a_v.reshape(dec_batch, DEPTH, past, KV_A)
    cbk = cache_b_k.reshape(dec_batch, DEPTH, past, BRANCH_DIM)
    cbv = cache_b_v.reshape(dec_batch, DEPTH, past, BRANCH_DIM)

    tm_proj, tm_merge, tm_ffn, tf = 1024, 512, 512, D_FF // 2

    def layer_tail(l, x, oa, ob, oc, mod_row_fn):
        w_branch_b, w_out_b, w_ffn_in_b, w_ffn_out_b = layer_weights[l]
        x1, h2 = _merge(oa, ob, oc, x, mods, l, g_post, g_pre, w_gates_b,
                        w_branch_b.reshape(N_BRANCH, BRANCH_DIM, D_MODEL), w_out_b,
                        tm=tm_merge, mod_row=mod_row_fn(tm_merge))
        return _ffn(h2, x1, mods, l, g_post, w_ffn_in_b, w_ffn_out_b,
                    tm=tm_ffn, tf=tf, mod_row=mod_row_fn(tm_ffn))

    ctx_row = lambda tm: (lambda i: CTX_MOD_ROW)
    x = x_prompt.reshape(batch * seq, D_MODEL)
    caches = None
    for l in range(DEPTH):
        qx, oc, *caches = _proj(x, mods, l, g_pre, w_proj_b, tm=tm_proj,
                                mod_row=ctx_row(tm_proj), seq=seq, caches=caches)
        oa, ob, *cast = _attn_ctx(qx, caches, a_sink[l], l, seq, late_weights)
        layer_weights.append(cast)
        x = layer_tail(l, x, oa, ob, oc, ctx_row)
    y_prompt = x.reshape(batch, seq, D_MODEL)
    nak, nav, nbk, nbv = caches
    new_a_k = nak.reshape(batch, DEPTH, seq, A_KV_HEADS, HEAD_DIM)
    new_a_v = nav.reshape(batch, DEPTH, seq, A_KV_HEADS, HEAD_DIM)
    new_b_k = nbk.reshape(batch, DEPTH, seq, B_HEADS, HEAD_DIM)
    new_b_v = nbv.reshape(batch, DEPTH, seq, B_HEADS, HEAD_DIM)

    lat_row = lambda tm: (lambda i: (i * tm) // dec_seq)
    x = x_sample.reshape(dec_batch * dec_seq, D_MODEL)
    bias = _neighbourhood_bias(b_rpb, dec_seq // GRID_W)
    for l in range(DEPTH):
        qx, kv, oc = _proj(x, mods, l, g_pre, w_proj_b, tm=tm_proj,
                           mod_row=lat_row(tm_proj), seq=dec_seq, rope_tabs=rope_tabs)
        oa = _attn_a_lat(qx, qx, cak, cav, a_sink[l], l, dec_seq)
        ob = _attn_b_lat(qx, kv, cbk, cbv, bias, l, dec_seq)
        x = layer_tail(l, x, oa, ob, oc, lat_row)
    y_sample = x.reshape(dec_batch, dec_seq, D_MODEL)
    return (y_prompt, y_sample, new_a_k, new_a_v, new_b_k, new_b_v)
```

```python
import functools

import numpy as np
import jax
import jax.numpy as jnp
from jax import lax
from jax.experimental import pallas as pl
from jax.experimental.pallas import tpu as pltpu

F32 = jnp.float32
BF16 = jnp.bfloat16

D_MODEL = 1024
DEPTH = 2
GRID_W = 64
HEAD_DIM = 64
BRANCH_DIM = D_MODEL // 2
A_Q_HEADS = BRANCH_DIM // HEAD_DIM
A_KV_HEADS = A_Q_HEADS // 4
A_GROUP = A_Q_HEADS // A_KV_HEADS
A_BLOCK = 128
B_HEADS = BRANCH_DIM // HEAD_DIM
B_WIN_ROWS = 8
B_WIN_COLS = 16
C_GROUPS = 4
C_GROUP_DIM = BRANCH_DIM // C_GROUPS
N_BRANCH = 3
D_FF = -(-8 * D_MODEL // (3 * 256)) * 256
ROPE_BASE = 10000.0
NORM_EPS = 1e-6
NEG_INF = -1e30
SCALE = HEAD_DIM ** -0.5

LANES = 128
KV_A = A_KV_HEADS * HEAD_DIM
N_GATES = N_BRANCH * D_MODEL
W_QA = 0
W_KA = W_QA + BRANCH_DIM
W_VA = W_KA + KV_A
W_QB = W_VA + KV_A
W_KB = W_QB + BRANCH_DIM
W_VB = W_KB + BRANCH_DIM
W_UC = W_VB + BRANCH_DIM
W_GATES = W_UC + BRANCH_DIM
D_IN = W_GATES + N_GATES
QX_QA, QX_QB, QX_KVA = 0, BRANCH_DIM, 2 * BRANCH_DIM
KV_KB, KV_VB = 0, BRANCH_DIM
MOD_ROWS = 8
CTX_MOD_ROW = 4
VMEM_LIMIT = 56 * 1024 * 1024
BF16_SUBLANES = 16


def _cparams(sem):
    return pltpu.CompilerParams(dimension_semantics=sem, vmem_limit_bytes=VMEM_LIMIT)


def _resident(shape, index_map):
    return pl.BlockSpec(shape, index_map, pipeline_mode=pl.Buffered(1))


def _rms(x, g):
    return x * lax.rsqrt(jnp.mean(x * x, axis=-1, keepdims=True) + NORM_EPS) * g


def _sigmoid(x):
    return 0.5 * jnp.tanh(0.5 * x) + 0.5


def _silu(x):
    half = 0.5 * x
    return half + half * jnp.tanh(half)


def _dot_t(a, b):
    return lax.dot_general(a, b, (((1,), (1,)), ((), ())), preferred_element_type=F32)


def _dot(a, b):
    return jnp.dot(a, b, preferred_element_type=F32)


def _scaled(q):
    assert np.log2(SCALE) == round(np.log2(SCALE))
    return q * jnp.asarray(SCALE, q.dtype)


def _low_lanes(shape):
    return lax.broadcasted_iota(jnp.int32, shape, len(shape) - 1) < HEAD_DIM


def _mod_kernel(cv_ref, w_ref, b_ref, win_ref, o_ref, wproj_ref, wgates_ref):
    a, b = _silu(cv_ref[...]), w_ref[...]
    a_hi, b_hi = a.astype(BF16), b.astype(BF16)
    a_lo = (a - a_hi.astype(F32)).astype(BF16)
    b_lo = (b - b_hi.astype(F32)).astype(BF16)
    o_ref[...] = _dot(a_hi, b_hi) + (_dot(a_lo, b_hi) + _dot(a_hi, b_lo)) + b_ref[...]
    wproj_ref[...] = win_ref[:, :W_GATES].astype(wproj_ref.dtype)
    wgates_ref[...] = win_ref[:, W_GATES:].astype(wgates_ref.dtype)


def _modulation(cvec, w_ada, b_ada, w_in):
    tn = 1536
    n = 6 * D_MODEL
    steps = n // tn
    rows = w_in.shape[1] // steps
    return pl.pallas_call(
        _mod_kernel,
        grid=(DEPTH, steps),
        in_specs=[
            pl.BlockSpec((MOD_ROWS, D_MODEL), lambda l, j: (0, 0)),
            pl.BlockSpec((None, D_MODEL, tn), lambda l, j: (l, 0, j)),
            pl.BlockSpec((None, 1, tn), lambda l, j: (l, 0, j)),
            pl.BlockSpec((None, rows, D_IN), lambda l, j: (l, j, 0)),
        ],
        out_specs=[pl.BlockSpec((None, MOD_ROWS, tn), lambda l, j: (l, 0, j)),
                   pl.BlockSpec((None, rows, W_GATES), lambda l, j: (l, j, 0)),
                   pl.BlockSpec((None, rows, N_GATES), lambda l, j: (l, j, 0))],
        out_shape=[jax.ShapeDtypeStruct((DEPTH, MOD_ROWS, n), F32),
                   jax.ShapeDtypeStruct((DEPTH, D_MODEL, W_GATES), BF16),
                   jax.ShapeDtypeStruct((DEPTH, D_MODEL, N_GATES), BF16)],
        compiler_params=_cparams(("parallel", "parallel")),
        name="modulation",
    )(cvec, w_ada, b_ada.reshape(DEPTH, 1, n), w_in)


def _rope_cols(v, cos, sin):
    first = (lax.broadcasted_iota(jnp.int32, (v.shape[0], LANES), 1) % HEAD_DIM) < HEAD_DIM // 2
    outs = []
    for c in range(v.shape[1] // LANES):
        vc = v[:, c * LANES:(c + 1) * LANES]
        partner = jnp.where(first, pltpu.roll(vc, LANES - HEAD_DIM // 2, 1),
                            pltpu.roll(vc, HEAD_DIM // 2, 1))
        outs.append(vc * cos + partner * sin)
    return outs[0] if len(outs) == 1 else jnp.concatenate(outs, axis=1)


def _with_rope(acc, lo, hi, cos_ref, sin_ref):
    parts = []
    if lo > 0:
        parts.append(acc[:, :lo])
    parts.append(_rope_cols(acc[:, lo:hi], cos_ref[...], sin_ref[...]))
    if hi < acc.shape[1]:
        parts.append(acc[:, hi:])
    return parts[0] if len(parts) == 1 else jnp.concatenate(parts, axis=1)


def _proj_kernel(*refs, latent, n_alias, n_cast):
    x_ref, mod_ref, g_ref, w_ref, bc_ref, bs_ref, pos_ref = refs[:7]
    if latent:
        cos_ref, sin_ref, qx_ref, kv_ref, oc_ref, h_scr = refs[7:]
    else:
        rest = refs[7 + n_alias:]
        cast_src, rest = rest[:n_cast], rest[n_cast:]
        qx_ref, oc_ref, ka_ref, va_ref, kb_ref, vb_ref = rest[:6]
        cast_dst, h_scr = rest[6:6 + n_cast], rest[6 + n_cast]
        for src, dst in zip(cast_src, cast_dst):
            dst[...] = src[...].astype(dst.dtype)
    j = pl.program_id(1)

    @pl.when(j == 0)
    def _():
        h = _rms(x_ref[...], g_ref[...]) * (1.0 + mod_ref[1:2, :]) + mod_ref[0:1, :]
        h_scr[...] = h.astype(BF16)
        acc = _dot(h_scr[...], w_ref[:, W_QA:W_KB])
        qa = acc[:, W_QA:W_QA + BRANCH_DIM]
        qb = acc[:, W_QB:W_QB + BRANCH_DIM]
        kva = acc[:, W_KA:W_KA + 2 * KV_A]
        if latent:
            qa = _rope_cols(qa, cos_ref[...], sin_ref[...])
            kva = _with_rope(kva, 0, KV_A, cos_ref, sin_ref)
            qx_ref[:, QX_KVA:QX_KVA + 2 * KV_A] = kva.astype(qx_ref.dtype)
        else:
            ka_ref[...] = kva[:, :KV_A].reshape(ka_ref.shape)
            va_ref[...] = kva[:, KV_A:].reshape(va_ref.shape)
        qx_ref[:, QX_QA:QX_QA + BRANCH_DIM] = qa.astype(qx_ref.dtype)
        qx_ref[:, QX_QB:QX_QB + BRANCH_DIM] = qb.astype(qx_ref.dtype)

    @pl.when(j == 1)
    def _():
        acc = _dot(h_scr[...], w_ref[:, W_KB:W_GATES])
        kb = acc[:, 0:BRANCH_DIM]
        vb = acc[:, W_VB - W_KB:W_VB - W_KB + BRANCH_DIM]
        uc = acc[:, W_UC - W_KB:W_UC - W_KB + BRANCH_DIM]
        if latent:
            kv_ref[:, KV_KB:KV_KB + BRANCH_DIM] = kb.astype(kv_ref.dtype)
            kv_ref[:, KV_VB:KV_VB + BRANCH_DIM] = vb.astype(kv_ref.dtype)
        else:
            kb_ref[...] = kb.reshape(kb_ref.shape)
            vb_ref[...] = vb.reshape(vb_ref.shape)
        _fourier_mix(uc.astype(BF16), bc_ref, bs_ref, pos_ref, oc_ref)


def _proj(x, mods, l, g_pre, w_in, *, tm, mod_row, seq, rope_tabs=None, caches=None, cast_weights=()):
    m = x.shape[0]
    assert tm % seq == 0
    latent = rope_tabs is not None
    bc, bs, pos = _dft_tables(seq)
    pair = 2 * C_GROUP_DIM
    in_specs = [
        pl.BlockSpec((tm, D_MODEL), lambda i, j: (i, 0)),
        pl.BlockSpec((None, None, 6, D_MODEL), lambda i, j: (l, mod_row(i), 0, 0)),
        pl.BlockSpec((None, None, 1, D_MODEL), lambda i, j: (l, 0, 0, 0)),
        _resident((None, D_MODEL, W_GATES), lambda i, j: (l, 0, 0)),
        _resident((pair, pair), lambda i, j: (0, 0)),
        _resident((pair, pair), lambda i, j: (0, 0)),
        _resident((seq, 2 * seq), lambda i, j: (0, 0)),
    ]
    args = [x, mods, g_pre, w_in, bc, bs, pos]
    aliases = {}
    if latent:
        seq_tiles = seq // tm
        tab_spec = pl.BlockSpec((tm, LANES), lambda i, j: (i % seq_tiles, 0))
        in_specs += [tab_spec, tab_spec]
        args += list(rope_tabs)
        widths = (2 * BRANCH_DIM + 2 * KV_A, 2 * BRANCH_DIM, BRANCH_DIM)
        cache_specs, cache_shapes = [], []
    else:
        widths = (2 * BRANCH_DIM, BRANCH_DIM)
        cache_widths = (KV_A, KV_A, BRANCH_DIM, BRANCH_DIM)
        if caches is not None:
            in_specs += [pl.BlockSpec(memory_space=pl.ANY)] * len(caches)
            aliases = {len(args) + k: len(widths) + k for k in range(len(caches))}
            args += list(caches)
        cache_specs = [pl.BlockSpec((tm // seq, None, seq, w), lambda i, j: (i, l, 0, 0))
                       for w in cache_widths]
        cache_shapes = [jax.ShapeDtypeStruct((m // seq, DEPTH, seq, w), F32) for w in cache_widths]
        steps = 2 * (m // tm)
        for w in cast_weights:
            _, rows, cols = w.shape
            assert rows % (steps * BF16_SUBLANES) == 0
            in_specs.append(pl.BlockSpec((None, rows // steps, cols), lambda i, j: (l, 2 * i + j, 0)))
            cache_specs.append(pl.BlockSpec((rows // steps, cols), lambda i, j: (2 * i + j, 0)))
            cache_shapes.append(jax.ShapeDtypeStruct((rows, cols), BF16))
            args.append(w)
    return pl.pallas_call(
        functools.partial(_proj_kernel, latent=latent, n_alias=len(aliases), n_cast=len(cast_weights)),
        grid=(m // tm, 2),
        in_specs=in_specs,
        out_specs=[pl.BlockSpec((tm, w), lambda i, j: (i, 0)) for w in widths] + cache_specs,
        out_shape=[jax.ShapeDtypeStruct((m, w), BF16) for w in widths] + cache_shapes,
        input_output_aliases=aliases,
        scratch_shapes=[pltpu.VMEM((tm, D_MODEL), BF16)],
        compiler_params=_cparams(("parallel", "arbitrary")),
        name="proj_lat" if latent else "proj_ctx",
    )(*args)


def _with_ones(v):
    return jnp.concatenate([v, jnp.ones_like(v)], axis=1)


def _softmax_pv(parts, sink=None):
    m = parts[0][0].max(axis=-1, keepdims=True)
    for s, _ in parts[1:]:
        m = jnp.maximum(m, s.max(axis=-1, keepdims=True))
    if sink is not None:
        m = jnp.maximum(m, sink)
    ones_half = parts[0][1].shape[1] == 2 * LANES
    acc = None
    den = None
    for s, v in parts:
        e = jnp.exp(s - m)
        if not ones_half:
            d = e.sum(axis=-1, keepdims=True)
            den = d if den is None else den + d
        o = _dot(e.astype(BF16), v)
        acc = o if acc is None else acc + o
    if ones_half:
        acc, den = acc[:, :LANES], acc[:, LANES:]
    if sink is not None:
        den = den + jnp.exp(sink - m)
    return acc * (1.0 / den)


def _kv_head_variants(x2):
    low = _low_lanes(x2.shape)
    xr = pltpu.roll(x2, HEAD_DIM, 1)
    zero = jnp.zeros_like(x2)
    return [
        [jnp.where(low, x2, zero).astype(BF16), jnp.where(low, zero, xr).astype(BF16)],
        [jnp.where(low, xr, zero).astype(BF16), jnp.where(low, zero, x2).astype(BF16)],
    ]


def _attn_ctx_kernel(sink_ref, qa_ref, qb_ref, ka_ref, va_ref, kb_ref, vb_ref, *rest, n_weights):
    w_src, (oa_ref, ob_ref), w_dst = rest[:n_weights], rest[n_weights:n_weights + 2], rest[n_weights + 2:]
    for src, dst in zip(w_src, w_dst):
        dst[...] = src[...].astype(dst.dtype)
    ka = [jnp.concatenate(v, axis=0) for v in _kv_head_variants(ka_ref[...])]
    va = [jnp.concatenate(v, axis=0) for v in _kv_head_variants(va_ref[...])]
    chunks = [slice(j * LANES, (j + 1) * LANES) for j in range(A_Q_HEADS // 2)]
    work = []
    for j, cols in enumerate(chunks):
        g = (2 * j) // A_GROUP
        s2 = _dot_t(_scaled(qa_ref[:, cols]), ka[g])
        work.append((oa_ref, cols, s2, va[g], (sink_ref[2 * j], sink_ref[2 * j + 1])))
    low = _low_lanes((kb_ref.shape[0], LANES))
    for cols in chunks:
        kc, vc = kb_ref[:, cols], vb_ref[:, cols]
        zero = jnp.zeros_like(kc)
        k2 = jnp.concatenate([jnp.where(low, kc, zero), jnp.where(low, zero, kc)], axis=0).astype(BF16)
        v2 = jnp.concatenate([jnp.where(low, vc, zero), jnp.where(low, zero, vc)], axis=0).astype(BF16)
        work.append((ob_ref, cols, _dot_t(_scaled(qb_ref[:, cols]), k2), v2, None))
    for o_ref, cols, s2, v2, sinks in work:
        o_ref[:, cols] = _pair_softmax_pv(s2, v2, sinks).astype(o_ref.dtype)


def _pair_softmax_pv(s2, v2, sinks=None):
    tk = s2.shape[1] // 2
    es, extra = [], []
    for p in range(2):
        s = s2[:, p * tk:(p + 1) * tk]
        m = s.max(axis=-1, keepdims=True)
        if sinks is not None:
            m = jnp.maximum(m, sinks[p])
            extra.append(jnp.exp(sinks[p] - m))
        es.append(jnp.exp(s - m).astype(BF16))
    lane_half = lax.broadcasted_iota(jnp.int32, v2.shape, 1) // HEAD_DIM
    row_half = lax.broadcasted_iota(jnp.int32, v2.shape, 0) // tk
    ones2 = jnp.where(lane_half == row_half, 1.0, 0.0).astype(v2.dtype)
    acc = _dot(jnp.concatenate(es, axis=1), jnp.concatenate([v2, ones2], axis=1))
    den = acc[:, LANES:]
    if sinks is not None:
        den = den + jnp.where(_low_lanes(den.shape), extra[0], extra[1])
    return acc[:, :LANES] * (1.0 / den)


def _attn_ctx(qx, caches, sink, l, seq, weights):
    m = qx.shape[0]
    steps = m // seq
    out = jax.ShapeDtypeStruct((m, BRANCH_DIM), BF16)
    cache_specs = [pl.BlockSpec((None, None, seq, c.shape[-1]), lambda b: (b, l, 0, 0)) for c in caches]
    w_in_specs, w_out_specs, w_shapes = [], [], []
    for w in weights:
        _, rows, cols = w.shape
        assert rows % (steps * BF16_SUBLANES) == 0
        w_in_specs.append(pl.BlockSpec((None, rows // steps, cols), lambda b: (l, b, 0)))
        w_out_specs.append(pl.BlockSpec((rows // steps, cols), lambda b: (b, 0)))
        w_shapes.append(jax.ShapeDtypeStruct((rows, cols), BF16))
    return pl.pallas_call(
        functools.partial(_attn_ctx_kernel, n_weights=len(weights)),
        grid=(steps,),
        in_specs=[
            pl.BlockSpec(memory_space=pltpu.SMEM),
            pl.BlockSpec((seq, BRANCH_DIM), lambda b: (b, QX_QA // BRANCH_DIM)),
            pl.BlockSpec((seq, BRANCH_DIM), lambda b: (b, QX_QB // BRANCH_DIM)),
        ] + cache_specs + w_in_specs,
        out_specs=[pl.BlockSpec((seq, BRANCH_DIM), lambda b: (b, 0))] * 2 + w_out_specs,
        out_shape=[out, out] + w_shapes,
        compiler_params=_cparams(("parallel",)),
        name="attn_ctx",
    )(sink, qx, qx, *caches, *weights)


def _attn_a_lat_kernel(sink_ref, q_ref, kv_ref, ck_ref, cv_ref, o_ref,
                       qh_scr, kp_scr, vp_scr, ckb_scr, cvb_scr, mask_scr, *, n_blocks):
    seq = n_blocks * A_BLOCK
    band = 3 * A_BLOCK
    low_seq = _low_lanes((seq, LANES))
    for h in range(A_Q_HEADS):
        j, p, g = h // 2, h % 2, h // A_GROUP
        x = q_ref[:, j * LANES:(j + 1) * LANES].astype(F32) * SCALE
        if p != g:
            x = pltpu.roll(x, HEAD_DIM, 1)
        qh_scr[h] = jnp.where(low_seq if g == 0 else jnp.logical_not(low_seq), x, 0.0).astype(BF16)
    pad = jnp.zeros((A_BLOCK, LANES), BF16)
    for scr, col in ((kp_scr, 0), (vp_scr, KV_A)):
        scr[0:A_BLOCK, :] = pad
        scr[A_BLOCK:A_BLOCK + seq, :] = kv_ref[:, col:col + KV_A]
        scr[A_BLOCK + seq:2 * A_BLOCK + seq, :] = pad
    ckb_scr[...] = ck_ref[...].astype(BF16)
    cvb_scr[...] = cv_ref[...].astype(BF16)

    rows = A_GROUP * A_BLOCK
    row = lax.broadcasted_iota(jnp.int32, (rows, band), 0)
    qi = row % A_BLOCK
    kj = lax.broadcasted_iota(jnp.int32, (rows, band), 1)
    head_row = lax.broadcasted_iota(jnp.int32, (rows, 1), 0) // A_BLOCK
    low_blk = _low_lanes((A_BLOCK, LANES))
    for case, (no_prev, no_next) in enumerate(((1, 0), (0, 0), (0, 1))):
        valid = (kj >= qi + no_prev * (A_BLOCK - qi)) & (kj <= 2 * A_BLOCK + qi - no_next * (qi + 1))
        mask_scr[case] = jnp.where(valid, 0.0, NEG_INF)

    def body(n, carry):
        start = pl.multiple_of(n * A_BLOCK, A_BLOCK)
        case = jnp.where(n > 0, 1, 0) + jnp.where(n < n_blocks - 1, 0, 1)
        scores = []
        for g in range(A_KV_HEADS):
            q = jnp.concatenate(
                [qh_scr[A_GROUP * g + i, pl.ds(start, A_BLOCK), :] for i in range(A_GROUP)], axis=0)
            s_band = _dot_t(q, kp_scr[pl.ds(start, band), :]) + mask_scr[case]
            scores.append((s_band, _dot_t(q, ckb_scr[...])))
        outs = []
        for g, (s_band, s_ctx) in enumerate(scores):
            sink = jnp.full((rows, 1), sink_ref[A_GROUP * g], F32)
            for i in range(1, A_GROUP):
                sink = jnp.where(head_row == i, sink_ref[A_GROUP * g + i], sink)
            outs.append(_softmax_pv([(s_band, vp_scr[pl.ds(start, band), :]), (s_ctx, cvb_scr[...])],
                                    sink=sink))
        for j in range(A_Q_HEADS // 2):
            halves = []
            for p in range(2):
                h = 2 * j + p
                g, i = h // A_GROUP, h % A_GROUP
                t = outs[g][i * A_BLOCK:(i + 1) * A_BLOCK]
                halves.append(t if p == g else pltpu.roll(t, HEAD_DIM, 1))
            o_ref[pl.ds(start, A_BLOCK), j * LANES:(j + 1) * LANES] = (
                jnp.where(low_blk, halves[0], halves[1]).astype(o_ref.dtype))
        return carry

    lax.fori_loop(0, n_blocks, body, 0, unroll=2)


def _attn_a_lat(main, kv, cache_k, cache_v, sink, l, seq):
    m = main.shape[0]
    nb = seq // A_BLOCK
    assert nb >= 2
    n_seq = m // seq
    past = cache_k.shape[2]
    cache_spec = pl.BlockSpec((None, None, past, KV_A), lambda b: (b, l, 0, 0))
    return pl.pallas_call(
        functools.partial(_attn_a_lat_kernel, n_blocks=nb),
        grid=(n_seq,),
        in_specs=[
            pl.BlockSpec(memory_space=pltpu.SMEM),
            pl.BlockSpec((seq, BRANCH_DIM), lambda b: (b, QX_QA // BRANCH_DIM)),
            pl.BlockSpec((seq, 2 * KV_A), lambda b: (b, QX_KVA // (2 * KV_A))),
            cache_spec, cache_spec,
        ],
        out_specs=pl.BlockSpec((seq, BRANCH_DIM), lambda b: (b, 0)),
        out_shape=jax.ShapeDtypeStruct((m, BRANCH_DIM), BF16),
        scratch_shapes=[
            pltpu.VMEM((A_Q_HEADS, seq, LANES), BF16),
            pltpu.VMEM((seq + 2 * A_BLOCK, LANES), BF16),
            pltpu.VMEM((seq + 2 * A_BLOCK, LANES), BF16),
            pltpu.VMEM((past, LANES), BF16),
            pltpu.VMEM((past, LANES), BF16),
            pltpu.VMEM((3, A_GROUP * A_BLOCK, 3 * A_BLOCK), F32),
        ],
        compiler_params=_cparams(("parallel",)),
        name="attn_a_lat",
    )(sink, main, kv, cache_k, cache_v)


def _attn_b_lat_kernel(q_ref, k_ref, v_ref, ck_ref, cv_ref, rpb_ref, o_ref, ckb_scr, vx_scr, cvx_scr,
                       bias_ref, *, rows, kr):
    @pl.when(pl.program_id(0) == 0)
    def _():
        _fill_neighbourhood_bias(rpb_ref, bias_ref, kr=kr)

    half = kr // 2
    n_loc = kr * GRID_W
    low_q = _low_lanes((GRID_W, LANES))
    chunks = [slice(j * LANES, (j + 1) * LANES) for j in range(B_HEADS // 2)]
    ckb_scr[...] = ck_ref[...].astype(BF16)
    for j, cols in enumerate(chunks):
        vx_scr[j] = _with_ones(v_ref[:, cols])
        cvx_scr[j] = _with_ones(cv_ref[:, cols].astype(BF16))
    rows_per_iter = 4
    assert rows % rows_per_iter == 0

    def body(it, carry):
        work = []
        for u in range(rows_per_iter):
            r = it * rows_per_iter + u
            r0 = jnp.clip(r - half, 0, rows - kr)
            q0 = pl.multiple_of(r * GRID_W, GRID_W)
            start = pl.multiple_of(r0 * GRID_W, GRID_W)
            for j, cols in enumerate(chunks):
                qc = _scaled(q_ref[pl.ds(q0, GRID_W), cols])
                zero = jnp.zeros_like(qc)
                qs = jnp.concatenate([jnp.where(low_q, qc, zero), jnp.where(low_q, zero, qc)], axis=0)
                s_loc = _dot_t(qs, k_ref[pl.ds(start, n_loc), cols]) + bias_ref[r - r0, j]
                work.append((q0, start, j, cols, s_loc, _dot_t(qs, ckb_scr[:, cols])))
        for q0, start, j, cols, s_loc, s_ctx in work:
            o2 = _softmax_pv([(s_loc, vx_scr[j, pl.ds(start, n_loc), :]), (s_ctx, cvx_scr[j])])
            o = jnp.where(low_q, o2[:GRID_W], o2[GRID_W:])
            o_ref[pl.ds(q0, GRID_W), cols] = o.astype(o_ref.dtype)
        return carry

    lax.fori_loop(0, rows // rows_per_iter, body, 0)


def _attn_b_lat(main, kv, cache_k, cache_v, rpb_pad, l, seq):
    m = main.shape[0]
    rows = seq // GRID_W
    kr = min(B_WIN_ROWS, rows)
    assert kr % 2 == 0 and 2 * GRID_W == LANES
    n_seq = m // seq
    past = cache_k.shape[2]
    cache_spec = pl.BlockSpec((None, None, past, BRANCH_DIM), lambda b: (b, l, 0, 0))
    return pl.pallas_call(
        functools.partial(_attn_b_lat_kernel, rows=rows, kr=kr),
        grid=(n_seq,),
        in_specs=[
            pl.BlockSpec((seq, BRANCH_DIM), lambda b: (b, QX_QB // BRANCH_DIM)),
            pl.BlockSpec((seq, BRANCH_DIM), lambda b: (b, KV_KB // BRANCH_DIM)),
            pl.BlockSpec((seq, BRANCH_DIM), lambda b: (b, KV_VB // BRANCH_DIM)),
            cache_spec, cache_spec,
            _resident((None,) + rpb_pad.shape[1:], lambda b: (l, 0, 0, 0)),
        ],
        out_specs=pl.BlockSpec((seq, BRANCH_DIM), lambda b: (b, 0)),
        out_shape=jax.ShapeDtypeStruct((m, BRANCH_DIM), BF16),
        scratch_shapes=[
            pltpu.VMEM((past, BRANCH_DIM), BF16),
            pltpu.VMEM((B_HEADS // 2, seq, 2 * LANES), BF16),
            pltpu.VMEM((B_HEADS // 2, past, 2 * LANES), BF16),
            pltpu.VMEM((kr, B_HEADS // 2, 2 * GRID_W, kr * GRID_W), F32),
        ],
        compiler_params=_cparams(("arbitrary",)),
        name="attn_b_lat",
    )(main, kv, kv, cache_k, cache_v, rpb_pad)


def _fill_neighbourhood_bias(rpb_ref, bias_scr, *, kr):
    shape = (GRID_W, LANES)
    c = lax.broadcasted_iota(jnp.int32, shape, 0)
    lane = lax.broadcasted_iota(jnp.int32, shape, 1)
    c2 = lane % GRID_W
    ws = jnp.clip(c - B_WIN_COLS // 2, 0, GRID_W - B_WIN_COLS)
    ok = (c2 >= ws) & (c2 < ws + B_WIN_COLS)
    low = lane < GRID_W

    def toeplitz(h, dr, lane0):
        row = jnp.broadcast_to(rpb_ref[h, dr:dr + 1, :], shape)
        return pltpu.roll(row, (lane0 - (B_WIN_COLS - 1)) % LANES, 1, stride=1, stride_axis=0)

    for h in range(B_HEADS):
        rows = slice(h % 2 * GRID_W, (h % 2 + 1) * GRID_W)
        pairs = {}
        for v in range(kr):
            for m in range(0, kr, 2):
                dr = m - v + B_WIN_ROWS - 1
                if dr not in pairs:
                    pair = jnp.where(low, toeplitz(h, dr, 0), toeplitz(h, dr + 1, GRID_W))
                    pairs[dr] = jnp.where(ok, pair, NEG_INF)
                bias_scr[v, h // 2, rows, m * GRID_W:(m + 2) * GRID_W] = pairs[dr]


def _dft_tables(seq):
    cd = C_GROUP_DIM
    kc = (np.arange(cd)[:, None] * np.arange(cd)[None, :]) % cd
    ang_c = 2.0 * np.pi * kc / cd
    eye2 = np.eye(2)
    bd_cos = np.kron(eye2, np.cos(ang_c))
    bd_sin = np.kron(eye2, np.sin(ang_c))
    kt = (np.arange(seq)[:, None] * np.arange(seq)[None, :]) % seq
    ang_t = 2.0 * np.pi * kt / seq
    norm = 1.0 / np.sqrt(float(seq * cd))
    pos = np.concatenate([np.cos(ang_t), -np.sin(ang_t)], axis=1) * norm
    return (jnp.asarray(bd_cos, F32).astype(BF16), jnp.asarray(bd_sin, F32).astype(BF16),
            jnp.asarray(pos, F32).astype(BF16))


def _fourier_mix(u, bc_ref, bs_ref, pos_ref, o_ref):
    pair = 2 * C_GROUP_DIM
    seq = pos_ref.shape[0]
    uc, us = [], []
    for p in range(BRANCH_DIM // pair):
        up = u[:, p * pair:(p + 1) * pair]
        uc.append(_dot(up, bc_ref[...]))
        us.append(_dot(up, bs_ref[...]))
    zc = jnp.concatenate(uc, axis=1).astype(BF16)
    zs = jnp.concatenate(us, axis=1).astype(BF16)
    for s in range(u.shape[0] // seq):
        rows = slice(s * seq, (s + 1) * seq)
        z = jnp.concatenate([zc[rows], zs[rows]], axis=0)
        o_ref[rows, :] = _dot(pos_ref[...], z).astype(o_ref.dtype)


def _merge_kernel(oa_ref, ob_ref, oc_ref, x_ref, mod_ref, gpre0_ref, gpost_ref, gpre1_ref,
                  wg_ref, wb_ref, wo_ref, x1_ref, h2_ref):
    x = x_ref[...]
    h = (_rms(x, gpre0_ref[...]) * (1.0 + mod_ref[1:2, :]) + mod_ref[0:1, :]).astype(BF16)
    gates = _dot(h, wg_ref[...])
    branches = [_dot(o_ref[...], wb_ref[i]) for i, o_ref in enumerate((oa_ref, ob_ref, oc_ref))]
    mix = None
    for i, branch in enumerate(branches):
        term = _sigmoid(gates[:, i * D_MODEL:(i + 1) * D_MODEL]) * branch
        mix = term if mix is None else mix + term
    y = _dot(mix.astype(BF16), wo_ref[...])
    x1 = x + mod_ref[2:3, :] * _rms(y, gpost_ref[...])
    x1_ref[...] = x1
    h2 = _rms(x1, gpre1_ref[...]) * (1.0 + mod_ref[4:5, :]) + mod_ref[3:4, :]
    h2_ref[...] = h2.astype(BF16)


def _merge(oa, ob, oc, x, mods, l, g_post, g_pre, w_in, w_branch, w_out, *, tm, mod_row):
    m = x.shape[0]
    o_spec = pl.BlockSpec((tm, BRANCH_DIM), lambda i: (i, 0))
    x_spec = pl.BlockSpec((tm, D_MODEL), lambda i: (i, 0))
    gain_spec = lambda which: pl.BlockSpec((None, None, 1, D_MODEL), lambda i: (l, which, 0, 0))
    return pl.pallas_call(
        _merge_kernel,
        grid=(m // tm,),
        in_specs=[
            o_spec, o_spec, o_spec,
            x_spec,
            pl.BlockSpec((None, None, 6, D_MODEL), lambda i: (l, mod_row(i), 0, 0)),
            gain_spec(0), gain_spec(0), gain_spec(1),
            _resident((None, D_MODEL, N_GATES), lambda i: (l, 0, 0)),
            _resident((N_BRANCH, BRANCH_DIM, D_MODEL), lambda i: (0, 0, 0)),
            _resident((D_MODEL, D_MODEL), lambda i: (0, 0)),
        ],
        out_specs=[x_spec, x_spec],
        out_shape=[jax.ShapeDtypeStruct((m, D_MODEL), F32), jax.ShapeDtypeStruct((m, D_MODEL), BF16)],
        compiler_params=_cparams(("parallel",)),
        name="merge",
    )(oa, ob, oc, x, mods, g_pre, g_post, g_pre, w_in, w_branch, w_out)


def _ffn_kernel(h_ref, wg_ref, wu_ref, wo_ref, x_ref, mod_ref, gpost_ref, o_ref, acc_ref):
    k = pl.program_id(1)

    @pl.when(k == 0)
    def _():
        acc_ref[...] = jnp.zeros_like(acc_ref)

    h = h_ref[...]
    act = _silu(_dot(h, wg_ref[...])) * _dot(h, wu_ref[...])
    acc_ref[...] += _dot(act.astype(BF16), wo_ref[...])

    @pl.when(k == pl.num_programs(1) - 1)
    def _():
        o_ref[...] = x_ref[...] + mod_ref[5:6, :] * _rms(acc_ref[...], gpost_ref[...])


def _ffn(h2, x1, mods, l, g_post, w_ffn_in, w_ffn_out, *, tm, tf, mod_row):
    m = x1.shape[0]
    nk = D_FF // tf
    x_spec = pl.BlockSpec((tm, D_MODEL), lambda i, k: (i, 0))
    w_spec = _resident if nk == 1 else pl.BlockSpec
    return pl.pallas_call(
        _ffn_kernel,
        grid=(m // tm, nk),
        in_specs=[
            x_spec,
            w_spec((D_MODEL, tf), lambda i, k: (0, k)),
            w_spec((D_MODEL, tf), lambda i, k: (0, k + nk)),
            w_spec((tf, D_MODEL), lambda i, k: (k, 0)),
            x_spec,
            pl.BlockSpec((None, None, 6, D_MODEL), lambda i, k: (l, mod_row(i), 0, 0)),
            pl.BlockSpec((None, None, 1, D_MODEL), lambda i, k: (l, 1, 0, 0)),
        ],
        out_specs=x_spec,
        out_shape=jax.ShapeDtypeStruct((m, D_MODEL), F32),
        scratch_shapes=[pltpu.VMEM((tm, D_MODEL), F32)],
        compiler_params=_cparams(("parallel", "arbitrary")),
        name="ffn",
    )(h2, w_ffn_in, w_ffn_in, w_ffn_out, x1, mods, g_post)


def _rope_tables(seq):
    t = jnp.arange(seq, dtype=jnp.int32)
    row = (t // GRID_W).astype(F32)
    col = (t % GRID_W).astype(F32)
    n_pairs_axis = HEAD_DIM // 4
    inv = ROPE_BASE ** (-jnp.arange(n_pairs_axis, dtype=F32) / n_pairs_axis)
    ang = jnp.concatenate([row[:, None] * inv, col[:, None] * inv], axis=-1)
    cos, sin = jnp.cos(ang), jnp.sin(ang)
    cos_l = jnp.tile(cos, (1, LANES // cos.shape[1]))
    sin_l = jnp.tile(jnp.concatenate([-sin, sin], axis=-1), (1, LANES // HEAD_DIM))
    return cos_l, sin_l


def kernel(x_prompt, x_sample, cache_a_k, cache_a_v, cache_b_k, cache_b_v, c, c_ctx, w_ada, b_ada,
           norm_pre, norm_post, w_in, a_sink, b_rpb, w_branch, w_out, w_ffn_in, w_ffn_out):
    batch, seq, _ = x_prompt.shape
    dec_batch, dec_seq, _ = x_sample.shape
    past = cache_a_k.shape[2]
    assert dec_batch <= CTX_MOD_ROW and seq % A_BLOCK == 0 and dec_seq % A_BLOCK == 0

    cvec = jnp.concatenate(
        [c, c_ctx[None, :], jnp.zeros((MOD_ROWS - dec_batch - 1, D_MODEL), F32)], axis=0)
    assert w_in.shape[-1] == D_IN
    mods, w_proj_b, w_gates_b = _modulation(cvec, w_ada, b_ada, w_in)
    mods = mods.reshape(DEPTH, MOD_ROWS, 6, D_MODEL)
    w_branch_2d = w_branch.reshape(DEPTH, N_BRANCH * BRANCH_DIM, D_MODEL)
    layer_weights = []
    g_pre = norm_pre.reshape(DEPTH, 2, 1, D_MODEL)
    g_post = norm_post.reshape(DEPTH, 2, 1, D_MODEL)
    rope_tabs = _rope_tables(dec_seq)
    cak = cache_a_k.reshape(dec_batch, DEPTH, past, KV_A)
    cav = cache_a_v.reshape(dec_batch, DEPTH, past, KV_A)
    cbk = cache_b_k.reshape(dec_batch, DEPTH, past, BRANCH_DIM)
    cbv = cache_b_v.reshape(dec_batch, DEPTH, past, BRANCH_DIM)

    tm_proj, tm_merge, tm_ffn, tf = 1024, 512, 512, D_FF

    def layer_tail(l, x, oa, ob, oc, mod_row_fn):
        w_branch_b, w_out_b, w_ffn_in_b, w_ffn_out_b = layer_weights[l]
        x1, h2 = _merge(oa, ob, oc, x, mods, l, g_post, g_pre, w_gates_b,
                        w_branch_b.reshape(N_BRANCH, BRANCH_DIM, D_MODEL), w_out_b,
                        tm=tm_merge, mod_row=mod_row_fn(tm_merge))
        return _ffn(h2, x1, mods, l, g_post, w_ffn_in_b, w_ffn_out_b,
                    tm=tm_ffn, tf=tf, mod_row=mod_row_fn(tm_ffn))

    ctx_row = lambda tm: (lambda i: CTX_MOD_ROW)
    x = x_prompt.reshape(batch * seq, D_MODEL)
    caches = None
    for l in range(DEPTH):
        qx, oc, *rest = _proj(x, mods, l, g_pre, w_proj_b, tm=tm_proj, mod_row=ctx_row(tm_proj),
                              seq=seq, caches=caches, cast_weights=(w_ffn_in,))
        caches, (w_ffn_in_b,) = rest[:4], rest[4:]
        oa, ob, w_branch_b, w_out_b, w_ffn_out_b = _attn_ctx(
            qx, caches, a_sink[l], l, seq, (w_branch_2d, w_out, w_ffn_out))
        layer_weights.append((w_branch_b, w_out_b, w_ffn_in_b, w_ffn_out_b))
        x = layer_tail(l, x, oa, ob, oc, ctx_row)
    y_prompt = x.reshape(batch, seq, D_MODEL)
    nak, nav, nbk, nbv = caches
    new_a_k = nak.reshape(batch, DEPTH, seq, A_KV_HEADS, HEAD_DIM)
    new_a_v = nav.reshape(batch, DEPTH, seq, A_KV_HEADS, HEAD_DIM)
    new_b_k = nbk.reshape(batch, DEPTH, seq, B_HEADS, HEAD_DIM)
    new_b_v = nbv.reshape(batch, DEPTH, seq, B_HEADS, HEAD_DIM)

    lat_row = lambda tm: (lambda i: (i * tm) // dec_seq)
    x = x_sample.reshape(dec_batch * dec_seq, D_MODEL)
    n_dr, n_dc = b_rpb.shape[2:]
    rpb_pad = jnp.pad(b_rpb.astype(F32), ((0, 0), (0, 0), (0, (-n_dr) % 8), (0, LANES - n_dc)))
    for l in range(DEPTH):
        qx, kv, oc = _proj(x, mods, l, g_pre, w_proj_b, tm=tm_proj,
                           mod_row=lat_row(tm_proj), seq=dec_seq, rope_tabs=rope_tabs)
        oa = _attn_a_lat(qx, qx, cak, cav, a_sink[l], l, dec_seq)
        ob = _attn_b_lat(qx, kv, cbk, cbv, rpb_pad, l, dec_seq)
        x = layer_tail(l, x, oa, ob, oc, lat_row)
    y_sample = x.reshape(dec_batch, dec_seq, D_MODEL)
    return (y_prompt, y_sample, new_a_k, new_a_v, new_b_k, new_b_v)
```

```python
import functools

import numpy as np
import jax
import jax.numpy as jnp
from jax import lax
from jax.experimental import pallas as pl
from jax.experimental.pallas import tpu as pltpu

F32 = jnp.float32
BF16 = jnp.bfloat16

D_MODEL = 1024
DEPTH = 2
GRID_W = 64
HEAD_DIM = 64
BRANCH_DIM = D_MODEL // 2
A_Q_HEADS = BRANCH_DIM // HEAD_DIM
A_KV_HEADS = A_Q_HEADS // 4
A_GROUP = A_Q_HEADS // A_KV_HEADS
A_BLOCK = 128
B_HEADS = BRANCH_DIM // HEAD_DIM
B_WIN_ROWS = 8
B_WIN_COLS = 16
C_GROUPS = 4
C_GROUP_DIM = BRANCH_DIM // C_GROUPS
N_BRANCH = 3
D_FF = -(-8 * D_MODEL // (3 * 256)) * 256
ROPE_BASE = 10000.0
NORM_EPS = 1e-6
NEG_INF = -1e30
SCALE = HEAD_DIM ** -0.5

LANES = 128
KV_A = A_KV_HEADS * HEAD_DIM
N_GATES = N_BRANCH * D_MODEL
W_QA = 0
W_KA = W_QA + BRANCH_DIM
W_VA = W_KA + KV_A
W_QB = W_VA + KV_A
W_KB = W_QB + BRANCH_DIM
W_VB = W_KB + BRANCH_DIM
W_UC = W_VB + BRANCH_DIM
W_GATES = W_UC + BRANCH_DIM
D_IN = W_GATES + N_GATES
QX_QA, QX_QB, QX_KVA = 0, BRANCH_DIM, 2 * BRANCH_DIM
KV_KB, KV_VB = 0, BRANCH_DIM
MOD_ROWS = 8
CTX_MOD_ROW = 4
VMEM_LIMIT = 56 * 1024 * 1024
BF16_SUBLANES = 16


def _cparams(sem):
    return pltpu.CompilerParams(dimension_semantics=sem, vmem_limit_bytes=VMEM_LIMIT)


def _resident(shape, index_map):
    return pl.BlockSpec(shape, index_map, pipeline_mode=pl.Buffered(1))


def _rms(x, g):
    return x * lax.rsqrt(jnp.mean(x * x, axis=-1, keepdims=True) + NORM_EPS) * g


def _sigmoid(x):
    return 0.5 * jnp.tanh(0.5 * x) + 0.5


def _silu(x):
    half = 0.5 * x
    return half + half * jnp.tanh(half)


def _dot_t(a, b):
    return lax.dot_general(a, b, (((1,), (1,)), ((), ())), preferred_element_type=F32)


def _dot(a, b):
    return jnp.dot(a, b, preferred_element_type=F32)


def _scaled(q):
    assert np.log2(SCALE) == round(np.log2(SCALE))
    return q * jnp.asarray(SCALE, q.dtype)


def _low_lanes(shape):
    return lax.broadcasted_iota(jnp.int32, shape, len(shape) - 1) < HEAD_DIM


def _mod_kernel(cv_ref, w_ref, b_ref, win_ref, o_ref, wproj_ref, wgates_ref):
    a, b = _silu(cv_ref[...]), w_ref[...]
    a_hi, b_hi = a.astype(BF16), b.astype(BF16)
    a_lo = (a - a_hi.astype(F32)).astype(BF16)
    b_lo = (b - b_hi.astype(F32)).astype(BF16)
    o_ref[...] = _dot(a_hi, b_hi) + (_dot(a_lo, b_hi) + _dot(a_hi, b_lo)) + b_ref[...]
    wproj_ref[...] = win_ref[:, :W_GATES].astype(wproj_ref.dtype)
    wgates_ref[...] = win_ref[:, W_GATES:].astype(wgates_ref.dtype)


def _modulation(cvec, w_ada, b_ada, w_in):
    tn = 1536
    n = 6 * D_MODEL
    steps = n // tn
    rows = w_in.shape[1] // steps
    return pl.pallas_call(
        _mod_kernel,
        grid=(DEPTH, steps),
        in_specs=[
            pl.BlockSpec((MOD_ROWS, D_MODEL), lambda l, j: (0, 0)),
            pl.BlockSpec((None, D_MODEL, tn), lambda l, j: (l, 0, j)),
            pl.BlockSpec((None, 1, tn), lambda l, j: (l, 0, j)),
            pl.BlockSpec((None, rows, D_IN), lambda l, j: (l, j, 0)),
        ],
        out_specs=[pl.BlockSpec((None, MOD_ROWS, tn), lambda l, j: (l, 0, j)),
                   pl.BlockSpec((None, rows, W_GATES), lambda l, j: (l, j, 0)),
                   pl.BlockSpec((None, rows, N_GATES), lambda l, j: (l, j, 0))],
        out_shape=[jax.ShapeDtypeStruct((DEPTH, MOD_ROWS, n), F32),
                   jax.ShapeDtypeStruct((DEPTH, D_MODEL, W_GATES), BF16),
                   jax.ShapeDtypeStruct((DEPTH, D_MODEL, N_GATES), BF16)],
        compiler_params=_cparams(("parallel", "parallel")),
        name="modulation",
    )(cvec, w_ada, b_ada.reshape(DEPTH, 1, n), w_in)


def _rope_cols(v, cos, sin):
    first = (lax.broadcasted_iota(jnp.int32, (v.shape[0], LANES), 1) % HEAD_DIM) < HEAD_DIM // 2
    outs = []
    for c in range(v.shape[1] // LANES):
        vc = v[:, c * LANES:(c + 1) * LANES]
        partner = jnp.where(first, pltpu.roll(vc, LANES - HEAD_DIM // 2, 1),
                            pltpu.roll(vc, HEAD_DIM // 2, 1))
        outs.append(vc * cos + partner * sin)
    return outs[0] if len(outs) == 1 else jnp.concatenate(outs, axis=1)


def _with_rope(acc, lo, hi, cos_ref, sin_ref):
    parts = []
    if lo > 0:
        parts.append(acc[:, :lo])
    parts.append(_rope_cols(acc[:, lo:hi], cos_ref[...], sin_ref[...]))
    if hi < acc.shape[1]:
        parts.append(acc[:, hi:])
    return parts[0] if len(parts) == 1 else jnp.concatenate(parts, axis=1)


def _proj_kernel(*refs, latent, n_alias, n_cast):
    x_ref, mod_ref, g_ref, w_ref, bc_ref, bs_ref, pos_ref = refs[:7]
    if latent:
        cos_ref, sin_ref, qx_ref, kv_ref, oc_ref, h_scr = refs[7:]
    else:
        rest = refs[7 + n_alias:]
        cast_src, rest = rest[:n_cast], rest[n_cast:]
        qx_ref, oc_ref, ka_ref, va_ref, kb_ref, vb_ref = rest[:6]
        cast_dst, h_scr = rest[6:6 + n_cast], rest[6 + n_cast]
        for src, dst in zip(cast_src, cast_dst):
            dst[...] = src[...].astype(dst.dtype)
    j = pl.program_id(1)

    @pl.when(j == 0)
    def _():
        h = _rms(x_ref[...], g_ref[...]) * (1.0 + mod_ref[1:2, :]) + mod_ref[0:1, :]
        h_scr[...] = h.astype(BF16)
        acc = _dot(h_scr[...], w_ref[:, W_QA:W_KB])
        qa = acc[:, W_QA:W_QA + BRANCH_DIM]
        qb = acc[:, W_QB:W_QB + BRANCH_DIM]
        kva = acc[:, W_KA:W_KA + 2 * KV_A]
        if latent:
            qa = _rope_cols(qa, cos_ref[...], sin_ref[...])
            kva = _with_rope(kva, 0, KV_A, cos_ref, sin_ref)
            qx_ref[:, QX_KVA:QX_KVA + 2 * KV_A] = kva.astype(qx_ref.dtype)
        else:
            ka_ref[...] = kva[:, :KV_A].reshape(ka_ref.shape)
            va_ref[...] = kva[:, KV_A:].reshape(va_ref.shape)
        qx_ref[:, QX_QA:QX_QA + BRANCH_DIM] = qa.astype(qx_ref.dtype)
        qx_ref[:, QX_QB:QX_QB + BRANCH_DIM] = qb.astype(qx_ref.dtype)

    @pl.when(j == 1)
    def _():
        acc = _dot(h_scr[...], w_ref[:, W_KB:W_GATES])
        kb = acc[:, 0:BRANCH_DIM]
        vb = acc[:, W_VB - W_KB:W_VB - W_KB + BRANCH_DIM]
        uc = acc[:, W_UC - W_KB:W_UC - W_KB + BRANCH_DIM]
        if latent:
            kv_ref[:, KV_KB:KV_KB + BRANCH_DIM] = kb.astype(kv_ref.dtype)
            kv_ref[:, KV_VB:KV_VB + BRANCH_DIM] = vb.astype(kv_ref.dtype)
        else:
            kb_ref[...] = kb.reshape(kb_ref.shape)
            vb_ref[...] = vb.reshape(vb_ref.shape)
        _fourier_mix(uc.astype(BF16), bc_ref, bs_ref, pos_ref, oc_ref)


def _proj(x, mods, l, g_pre, w_in, *, tm, mod_row, seq, rope_tabs=None, caches=None, cast_weights=()):
    m = x.shape[0]
    assert tm % seq == 0
    latent = rope_tabs is not None
    bc, bs, pos = _dft_tables(seq)
    pair = 2 * C_GROUP_DIM
    in_specs = [
        pl.BlockSpec((tm, D_MODEL), lambda i, j: (i, 0)),
        pl.BlockSpec((None, None, 6, D_MODEL), lambda i, j: (l, mod_row(i), 0, 0)),
        pl.BlockSpec((None, None, 1, D_MODEL), lambda i, j: (l, 0, 0, 0)),
        _resident((None, D_MODEL, W_GATES), lambda i, j: (l, 0, 0)),
        _resident((pair, pair), lambda i, j: (0, 0)),
        _resident((pair, pair), lambda i, j: (0, 0)),
        _resident((seq, 2 * seq), lambda i, j: (0, 0)),
    ]
    args = [x, mods, g_pre, w_in, bc, bs, pos]
    aliases = {}
    if latent:
        seq_tiles = seq // tm
        tab_spec = pl.BlockSpec((tm, LANES), lambda i, j: (i % seq_tiles, 0))
        in_specs += [tab_spec, tab_spec]
        args += list(rope_tabs)
        widths = (2 * BRANCH_DIM + 2 * KV_A, 2 * BRANCH_DIM, BRANCH_DIM)
        cache_specs, cache_shapes = [], []
    else:
        widths = (2 * BRANCH_DIM, BRANCH_DIM)
        cache_widths = (KV_A, KV_A, BRANCH_DIM, BRANCH_DIM)
        if caches is not None:
            in_specs += [pl.BlockSpec(memory_space=pl.ANY)] * len(caches)
            aliases = {len(args) + k: len(widths) + k for k in range(len(caches))}
            args += list(caches)
        cache_specs = [pl.BlockSpec((tm // seq, None, seq, w), lambda i, j: (i, l, 0, 0))
                       for w in cache_widths]
        cache_shapes = [jax.ShapeDtypeStruct((m // seq, DEPTH, seq, w), F32) for w in cache_widths]
        steps = 2 * (m // tm)
        for w in cast_weights:
            _, rows, cols = w.shape
            assert rows % (steps * BF16_SUBLANES) == 0
            in_specs.append(pl.BlockSpec((None, rows // steps, cols), lambda i, j: (l, 2 * i + j, 0)))
            cache_specs.append(pl.BlockSpec((rows // steps, cols), lambda i, j: (2 * i + j, 0)))
            cache_shapes.append(jax.ShapeDtypeStruct((rows, cols), BF16))
            args.append(w)
    return pl.pallas_call(
        functools.partial(_proj_kernel, latent=latent, n_alias=len(aliases), n_cast=len(cast_weights)),
        grid=(m // tm, 2),
        in_specs=in_specs,
        out_specs=[pl.BlockSpec((tm, w), lambda i, j: (i, 0)) for w in widths] + cache_specs,
        out_shape=[jax.ShapeDtypeStruct((m, w), BF16) for w in widths] + cache_shapes,
        input_output_aliases=aliases,
        scratch_shapes=[pltpu.VMEM((tm, D_MODEL), BF16)],
        compiler_params=_cparams(("parallel", "arbitrary")),
        name="proj_lat" if latent else "proj_ctx",
    )(*args)


def _with_ones(v):
    return jnp.concatenate([v, jnp.ones_like(v)], axis=1)


def _softmax_pv(parts, sink=None):
    m = parts[0][0].max(axis=-1, keepdims=True)
    for s, _ in parts[1:]:
        m = jnp.maximum(m, s.max(axis=-1, keepdims=True))
    if sink is not None:
        m = jnp.maximum(m, sink)
    ones_half = parts[0][1].shape[1] == 2 * LANES
    acc = None
    den = None
    for s, v in parts:
        e = jnp.exp(s - m)
        if not ones_half:
            d = e.sum(axis=-1, keepdims=True)
            den = d if den is None else den + d
        o = _dot(e.astype(BF16), v)
        acc = o if acc is None else acc + o
    if ones_half:
        acc, den = acc[:, :LANES], acc[:, LANES:]
    if sink is not None:
        den = den + jnp.exp(sink - m)
    return acc * (1.0 / den)


def _kv_head_variants(x2):
    low = _low_lanes(x2.shape)
    xr = pltpu.roll(x2, HEAD_DIM, 1)
    zero = jnp.zeros_like(x2)
    return [
        [jnp.where(low, x2, zero).astype(BF16), jnp.where(low, zero, xr).astype(BF16)],
        [jnp.where(low, xr, zero).astype(BF16), jnp.where(low, zero, x2).astype(BF16)],
    ]


def _attn_ctx_kernel(sink_ref, qa_ref, qb_ref, ka_ref, va_ref, kb_ref, vb_ref, *rest, n_weights):
    w_src, (oa_ref, ob_ref), w_dst = rest[:n_weights], rest[n_weights:n_weights + 2], rest[n_weights + 2:]
    for src, dst in zip(w_src, w_dst):
        dst[...] = src[...].astype(dst.dtype)
    ka = [jnp.concatenate(v, axis=0) for v in _kv_head_variants(ka_ref[...])]
    va = [jnp.concatenate(v, axis=0) for v in _kv_head_variants(va_ref[...])]
    chunks = [slice(j * LANES, (j + 1) * LANES) for j in range(A_Q_HEADS // 2)]
    work = []
    for j, cols in enumerate(chunks):
        g = (2 * j) // A_GROUP
        s2 = _dot_t(_scaled(qa_ref[:, cols]), ka[g])
        work.append((oa_ref, cols, s2, va[g], (sink_ref[2 * j], sink_ref[2 * j + 1])))
    low = _low_lanes((kb_ref.shape[0], LANES))
    for cols in chunks:
        kc, vc = kb_ref[:, cols], vb_ref[:, cols]
        zero = jnp.zeros_like(kc)
        k2 = jnp.concatenate([jnp.where(low, kc, zero), jnp.where(low, zero, kc)], axis=0).astype(BF16)
        v2 = jnp.concatenate([jnp.where(low, vc, zero), jnp.where(low, zero, vc)], axis=0).astype(BF16)
        work.append((ob_ref, cols, _dot_t(_scaled(qb_ref[:, cols]), k2), v2, None))
    for o_ref, cols, s2, v2, sinks in work:
        o_ref[:, cols] = _pair_softmax_pv(s2, v2, sinks).astype(o_ref.dtype)


def _pair_softmax_pv(s2, v2, sinks=None):
    tk = s2.shape[1] // 2
    es, extra = [], []
    for p in range(2):
        s = s2[:, p * tk:(p + 1) * tk]
        m = s.max(axis=-1, keepdims=True)
        if sinks is not None:
            m = jnp.maximum(m, sinks[p])
            extra.append(jnp.exp(sinks[p] - m))
        es.append(jnp.exp(s - m).astype(BF16))
    lane_half = lax.broadcasted_iota(jnp.int32, v2.shape, 1) // HEAD_DIM
    row_half = lax.broadcasted_iota(jnp.int32, v2.shape, 0) // tk
    ones2 = jnp.where(lane_half == row_half, 1.0, 0.0).astype(v2.dtype)
    acc = _dot(jnp.concatenate(es, axis=1), jnp.concatenate([v2, ones2], axis=1))
    den = acc[:, LANES:]
    if sinks is not None:
        den = den + jnp.where(_low_lanes(den.shape), extra[0], extra[1])
    return acc[:, :LANES] * (1.0 / den)


def _attn_ctx(qx, caches, sink, l, seq, weights):
    m = qx.shape[0]
    steps = m // seq
    out = jax.ShapeDtypeStruct((m, BRANCH_DIM), BF16)
    cache_specs = [pl.BlockSpec((None, None, seq, c.shape[-1]), lambda b: (b, l, 0, 0)) for c in caches]
    w_in_specs, w_out_specs, w_shapes = [], [], []
    for w in weights:
        _, rows, cols = w.shape
        assert rows % (steps * BF16_SUBLANES) == 0
        w_in_specs.append(pl.BlockSpec((None, rows // steps, cols), lambda b: (l, b, 0)))
        w_out_specs.append(pl.BlockSpec((rows // steps, cols), lambda b: (b, 0)))
        w_shapes.append(jax.ShapeDtypeStruct((rows, cols), BF16))
    return pl.pallas_call(
        functools.partial(_attn_ctx_kernel, n_weights=len(weights)),
        grid=(steps,),
        in_specs=[
            pl.BlockSpec(memory_space=pltpu.SMEM),
            pl.BlockSpec((seq, BRANCH_DIM), lambda b: (b, QX_QA // BRANCH_DIM)),
            pl.BlockSpec((seq, BRANCH_DIM), lambda b: (b, QX_QB // BRANCH_DIM)),
        ] + cache_specs + w_in_specs,
        out_specs=[pl.BlockSpec((seq, BRANCH_DIM), lambda b: (b, 0))] * 2 + w_out_specs,
        out_shape=[out, out] + w_shapes,
        compiler_params=_cparams(("parallel",)),
        name="attn_ctx",
    )(sink, qx, qx, *caches, *weights)


def _attn_a_lat_kernel(sink_ref, q_ref, kv_ref, ck_ref, cv_ref, o_ref,
                       qh_scr, kp_scr, vp_scr, ckb_scr, cvb_scr, mask_scr, *, n_blocks):
    seq = n_blocks * A_BLOCK
    band = 3 * A_BLOCK
    low_seq = _low_lanes((seq, LANES))
    for h in range(A_Q_HEADS):
        j, p, g = h // 2, h % 2, h // A_GROUP
        x = q_ref[:, j * LANES:(j + 1) * LANES].astype(F32) * SCALE
        if p != g:
            x = pltpu.roll(x, HEAD_DIM, 1)
        qh_scr[h] = jnp.where(low_seq if g == 0 else jnp.logical_not(low_seq), x, 0.0).astype(BF16)
    pad = jnp.zeros((A_BLOCK, LANES), BF16)
    for scr, col in ((kp_scr, 0), (vp_scr, KV_A)):
        scr[0:A_BLOCK, :] = pad
        scr[A_BLOCK:A_BLOCK + seq, :] = kv_ref[:, col:col + KV_A]
        scr[A_BLOCK + seq:2 * A_BLOCK + seq, :] = pad
    ckb_scr[...] = ck_ref[...].astype(BF16)
    cvb_scr[...] = cv_ref[...].astype(BF16)

    rows = A_GROUP * A_BLOCK
    row = lax.broadcasted_iota(jnp.int32, (rows, band), 0)
    qi = row % A_BLOCK
    kj = lax.broadcasted_iota(jnp.int32, (rows, band), 1)
    head_row = lax.broadcasted_iota(jnp.int32, (rows, 1), 0) // A_BLOCK
    low_blk = _low_lanes((A_BLOCK, LANES))
    for case, (no_prev, no_next) in enumerate(((1, 0), (0, 0), (0, 1))):
        valid = (kj >= qi + no_prev * (A_BLOCK - qi)) & (kj <= 2 * A_BLOCK + qi - no_next * (qi + 1))
        mask_scr[case] = jnp.where(valid, 0.0, NEG_INF)

    def body(n, carry):
        start = pl.multiple_of(n * A_BLOCK, A_BLOCK)
        case = jnp.where(n > 0, 1, 0) + jnp.where(n < n_blocks - 1, 0, 1)
        scores = []
        for g in range(A_KV_HEADS):
            q = jnp.concatenate(
                [qh_scr[A_GROUP * g + i, pl.ds(start, A_BLOCK), :] for i in range(A_GROUP)], axis=0)
            s_band = _dot_t(q, kp_scr[pl.ds(start, band), :]) + mask_scr[case]
            scores.append((s_band, _dot_t(q, ckb_scr[...])))
        outs = []
        for g, (s_band, s_ctx) in enumerate(scores):
            sink = jnp.full((rows, 1), sink_ref[A_GROUP * g], F32)
            for i in range(1, A_GROUP):
                sink = jnp.where(head_row == i, sink_ref[A_GROUP * g + i], sink)
            outs.append(_softmax_pv([(s_band, vp_scr[pl.ds(start, band), :]), (s_ctx, cvb_scr[...])],
                                    sink=sink))
        for j in range(A_Q_HEADS // 2):
            halves = []
            for p in range(2):
                h = 2 * j + p
                g, i = h // A_GROUP, h % A_GROUP
                t = outs[g][i * A_BLOCK:(i + 1) * A_BLOCK]
                halves.append(t if p == g else pltpu.roll(t, HEAD_DIM, 1))
            o_ref[pl.ds(start, A_BLOCK), j * LANES:(j + 1) * LANES] = (
                jnp.where(low_blk, halves[0], halves[1]).astype(o_ref.dtype))
        return carry

    lax.fori_loop(0, n_blocks, body, 0, unroll=2)


def _attn_a_lat(main, kv, cache_k, cache_v, sink, l, seq):
    m = main.shape[0]
    nb = seq // A_BLOCK
    assert nb >= 2
    n_seq = m // seq
    past = cache_k.shape[2]
    cache_spec = pl.BlockSpec((None, None, past, KV_A), lambda b: (b, l, 0, 0))
    return pl.pallas_call(
        functools.partial(_attn_a_lat_kernel, n_blocks=nb),
        grid=(n_seq,),
        in_specs=[
            pl.BlockSpec(memory_space=pltpu.SMEM),
            pl.BlockSpec((seq, BRANCH_DIM), lambda b: (b, QX_QA // BRANCH_DIM)),
            pl.BlockSpec((seq, 2 * KV_A), lambda b: (b, QX_KVA // (2 * KV_A))),
            cache_spec, cache_spec,
        ],
        out_specs=pl.BlockSpec((seq, BRANCH_DIM), lambda b: (b, 0)),
        out_shape=jax.ShapeDtypeStruct((m, BRANCH_DIM), BF16),
        scratch_shapes=[
            pltpu.VMEM((A_Q_HEADS, seq, LANES), BF16),
            pltpu.VMEM((seq + 2 * A_BLOCK, LANES), BF16),
            pltpu.VMEM((seq + 2 * A_BLOCK, LANES), BF16),
            pltpu.VMEM((past, LANES), BF16),
            pltpu.VMEM((past, LANES), BF16),
            pltpu.VMEM((3, A_GROUP * A_BLOCK, 3 * A_BLOCK), F32),
        ],
        compiler_params=_cparams(("parallel",)),
        name="attn_a_lat",
    )(sink, main, kv, cache_k, cache_v)


def _attn_b_lat_kernel(q_ref, k_ref, v_ref, ck_ref, cv_ref, rpb_ref, o_ref, ckb_scr, vx_scr, cvx_scr,
                       bias_ref, *, rows, kr):
    @pl.when(pl.program_id(0) == 0)
    def _():
        _fill_neighbourhood_bias(rpb_ref, bias_ref, kr=kr)

    half = kr // 2
    n_loc = kr * GRID_W
    low_q = _low_lanes((GRID_W, LANES))
    chunks = [slice(j * LANES, (j + 1) * LANES) for j in range(B_HEADS // 2)]
    ckb_scr[...] = ck_ref[...].astype(BF16)
    for j, cols in enumerate(chunks):
        vx_scr[j] = _with_ones(v_ref[:, cols])
        cvx_scr[j] = _with_ones(cv_ref[:, cols].astype(BF16))
    rows_per_iter = 4
    assert rows % rows_per_iter == 0

    def body(it, carry):
        work = []
        for u in range(rows_per_iter):
            r = it * rows_per_iter + u
            r0 = jnp.clip(r - half, 0, rows - kr)
            q0 = pl.multiple_of(r * GRID_W, GRID_W)
            start = pl.multiple_of(r0 * GRID_W, GRID_W)
            for j, cols in enumerate(chunks):
                qc = _scaled(q_ref[pl.ds(q0, GRID_W), cols])
                zero = jnp.zeros_like(qc)
                qs = jnp.concatenate([jnp.where(low_q, qc, zero), jnp.where(low_q, zero, qc)], axis=0)
                s_loc = _dot_t(qs, k_ref[pl.ds(start, n_loc), cols]) + bias_ref[r - r0, j]
                work.append((q0, start, j, cols, s_loc, _dot_t(qs, ckb_scr[:, cols])))
        for q0, start, j, cols, s_loc, s_ctx in work:
            o2 = _softmax_pv([(s_loc, vx_scr[j, pl.ds(start, n_loc), :]), (s_ctx, cvx_scr[j])])
            o = jnp.where(low_q, o2[:GRID_W], o2[GRID_W:])
            o_ref[pl.ds(q0, GRID_W), cols] = o.astype(o_ref.dtype)
        return carry

    lax.fori_loop(0, rows // rows_per_iter, body, 0)


def _attn_b_lat(main, kv, cache_k, cache_v, rpb_pad, l, seq):
    m = main.shape[0]
    rows = seq // GRID_W
    kr = min(B_WIN_ROWS, rows)
    assert kr % 2 == 0 and 2 * GRID_W == LANES
    n_seq = m // seq
    past = cache_k.shape[2]
    cache_spec = pl.BlockSpec((None, None, past, BRANCH_DIM), lambda b: (b, l, 0, 0))
    return pl.pallas_call(
        functools.partial(_attn_b_lat_kernel, rows=rows, kr=kr),
        grid=(n_seq,),
        in_specs=[
            pl.BlockSpec((seq, BRANCH_DIM), lambda b: (b, QX_QB // BRANCH_DIM)),
            pl.BlockSpec((seq, BRANCH_DIM), lambda b: (b, KV_KB // BRANCH_DIM)),
            pl.BlockSpec((seq, BRANCH_DIM), lambda b: (b, KV_VB // BRANCH_DIM)),
            cache_spec, cache_spec,
            _resident((None,) + rpb_pad.shape[1:], lambda b: (l, 0, 0, 0)),
        ],
        out_specs=pl.BlockSpec((seq, BRANCH_DIM), lambda b: (b, 0)),
        out_shape=jax.ShapeDtypeStruct((m, BRANCH_DIM), BF16),
        scratch_shapes=[
            pltpu.VMEM((past, BRANCH_DIM), BF16),
            pltpu.VMEM((B_HEADS // 2, seq, 2 * LANES), BF16),
            pltpu.VMEM((B_HEADS // 2, past, 2 * LANES), BF16),
            pltpu.VMEM((kr, B_HEADS // 2, 2 * GRID_W, kr * GRID_W), F32),
        ],
        compiler_params=_cparams(("arbitrary",)),
        name="attn_b_lat",
    )(main, kv, kv, cache_k, cache_v, rpb_pad)


def _fill_neighbourhood_bias(rpb_ref, bias_scr, *, kr):
    shape = (GRID_W, LANES)
    c = lax.broadcasted_iota(jnp.int32, shape, 0)
    lane = lax.broadcasted_iota(jnp.int32, shape, 1)
    c2 = lane % GRID_W
    ws = jnp.clip(c - B_WIN_COLS // 2, 0, GRID_W - B_WIN_COLS)
    ok = (c2 >= ws) & (c2 < ws + B_WIN_COLS)
    low = lane < GRID_W

    def toeplitz(h, dr, lane0):
        row = jnp.broadcast_to(rpb_ref[h, dr:dr + 1, :], shape)
        return pltpu.roll(row, (lane0 - (B_WIN_COLS - 1)) % LANES, 1, stride=1, stride_axis=0)

    for h in range(B_HEADS):
        rows = slice(h % 2 * GRID_W, (h % 2 + 1) * GRID_W)
        pairs = {}
        for v in range(kr):
            for m in range(0, kr, 2):
                dr = m - v + B_WIN_ROWS - 1
                if dr not in pairs:
                    pair = jnp.where(low, toeplitz(h, dr, 0), toeplitz(h, dr + 1, GRID_W))
                    pairs[dr] = jnp.where(ok, pair, NEG_INF)
                bias_scr[v, h // 2, rows, m * GRID_W:(m + 2) * GRID_W] = pairs[dr]


def _dft_tables(seq):
    cd = C_GROUP_DIM
    kc = (np.arange(cd)[:, None] * np.arange(cd)[None, :]) % cd
    ang_c = 2.0 * np.pi * kc / cd
    eye2 = np.eye(2)
    bd_cos = np.kron(eye2, np.cos(ang_c))
    bd_sin = np.kron(eye2, np.sin(ang_c))
    kt = (np.arange(seq)[:, None] * np.arange(seq)[None, :]) % seq
    ang_t = 2.0 * np.pi * kt / seq
    norm = 1.0 / np.sqrt(float(seq * cd))
    pos = np.concatenate([np.cos(ang_t), -np.sin(ang_t)], axis=1) * norm
    return (jnp.asarray(bd_cos, F32).astype(BF16), jnp.asarray(bd_sin, F32).astype(BF16),
            jnp.asarray(pos, F32).astype(BF16))


def _fourier_mix(u, bc_ref, bs_ref, pos_ref, o_ref):
    pair = 2 * C_GROUP_DIM
    seq = pos_ref.shape[0]
    uc, us = [], []
    for p in range(BRANCH_DIM // pair):
        up = u[:, p * pair:(p + 1) * pair]
        uc.append(_dot(up, bc_ref[...]))
        us.append(_dot(up, bs_ref[...]))
    zc = jnp.concatenate(uc, axis=1).astype(BF16)
    zs = jnp.concatenate(us, axis=1).astype(BF16)
    for s in range(u.shape[0] // seq):
        rows = slice(s * seq, (s + 1) * seq)
        z = jnp.concatenate([zc[rows], zs[rows]], axis=0)
        o_ref[rows, :] = _dot(pos_ref[...], z).astype(o_ref.dtype)


def _tail_kernel(oa_ref, ob_ref, oc_ref, x_ref, mod_ref, gpre0_ref, gpost0_ref, gpre1_ref, gpost1_ref,
                 wg_ref, wb_ref, wo_ref, wfg_ref, wfu_ref, wfo_ref, o_ref):
    x = x_ref[...]
    h = (_rms(x, gpre0_ref[...]) * (1.0 + mod_ref[1:2, :]) + mod_ref[0:1, :]).astype(BF16)
    gates = _dot(h, wg_ref[...])
    branches = [_dot(b_ref[...], wb_ref[i]) for i, b_ref in enumerate((oa_ref, ob_ref, oc_ref))]
    mix = None
    for i, branch in enumerate(branches):
        term = _sigmoid(gates[:, i * D_MODEL:(i + 1) * D_MODEL]) * branch
        mix = term if mix is None else mix + term
    y = _dot(mix.astype(BF16), wo_ref[...])
    x1 = x + mod_ref[2:3, :] * _rms(y, gpost0_ref[...])
    h2 = (_rms(x1, gpre1_ref[...]) * (1.0 + mod_ref[4:5, :]) + mod_ref[3:4, :]).astype(BF16)
    act = _silu(_dot(h2, wfg_ref[...])) * _dot(h2, wfu_ref[...])
    f = _dot(act.astype(BF16), wfo_ref[...])
    o_ref[...] = x1 + mod_ref[5:6, :] * _rms(f, gpost1_ref[...])


def _tail(oa, ob, oc, x, mods, l, g_pre, g_post, w_gates, w_branch, w_out, w_ffn_in, w_ffn_out,
          *, tm, mod_row):
    m = x.shape[0]
    o_spec = pl.BlockSpec((tm, BRANCH_DIM), lambda i: (i, 0))
    x_spec = pl.BlockSpec((tm, D_MODEL), lambda i: (i, 0))
    gain_spec = lambda which: pl.BlockSpec((None, None, 1, D_MODEL), lambda i: (l, which, 0, 0))
    return pl.pallas_call(
        _tail_kernel,
        grid=(m // tm,),
        in_specs=[
            o_spec, o_spec, o_spec,
            x_spec,
            pl.BlockSpec((None, None, 6, D_MODEL), lambda i: (l, mod_row(i), 0, 0)),
            gain_spec(0), gain_spec(0), gain_spec(1), gain_spec(1),
            _resident((None, D_MODEL, N_GATES), lambda i: (l, 0, 0)),
            _resident((N_BRANCH, BRANCH_DIM, D_MODEL), lambda i: (0, 0, 0)),
            _resident((D_MODEL, D_MODEL), lambda i: (0, 0)),
            _resident((D_MODEL, D_FF), lambda i: (0, 0)),
            _resident((D_MODEL, D_FF), lambda i: (0, 1)),
            _resident((D_FF, D_MODEL), lambda i: (0, 0)),
        ],
        out_specs=x_spec,
        out_shape=jax.ShapeDtypeStruct((m, D_MODEL), F32),
        compiler_params=_cparams(("parallel",)),
        name="tail",
    )(oa, ob, oc, x, mods, g_pre, g_post, g_pre, g_post, w_gates, w_branch, w_out,
      w_ffn_in, w_ffn_in, w_ffn_out)


def _rope_tables(seq):
    t = jnp.arange(seq, dtype=jnp.int32)
    row = (t // GRID_W).astype(F32)
    col = (t % GRID_W).astype(F32)
    n_pairs_axis = HEAD_DIM // 4
    inv = ROPE_BASE ** (-jnp.arange(n_pairs_axis, dtype=F32) / n_pairs_axis)
    ang = jnp.concatenate([row[:, None] * inv, col[:, None] * inv], axis=-1)
    cos, sin = jnp.cos(ang), jnp.sin(ang)
    cos_l = jnp.tile(cos, (1, LANES // cos.shape[1]))
    sin_l = jnp.tile(jnp.concatenate([-sin, sin], axis=-1), (1, LANES // HEAD_DIM))
    return cos_l, sin_l


def kernel(x_prompt, x_sample, cache_a_k, cache_a_v, cache_b_k, cache_b_v, c, c_ctx, w_ada, b_ada,
           norm_pre, norm_post, w_in, a_sink, b_rpb, w_branch, w_out, w_ffn_in, w_ffn_out):
    batch, seq, _ = x_prompt.shape
    dec_batch, dec_seq, _ = x_sample.shape
    past = cache_a_k.shape[2]
    assert dec_batch <= CTX_MOD_ROW and seq % A_BLOCK == 0 and dec_seq % A_BLOCK == 0

    cvec = jnp.concatenate(
        [c, c_ctx[None, :], jnp.zeros((MOD_ROWS - dec_batch - 1, D_MODEL), F32)], axis=0)
    assert w_in.shape[-1] == D_IN
    mods, w_proj_b, w_gates_b = _modulation(cvec, w_ada, b_ada, w_in)
    mods = mods.reshape(DEPTH, MOD_ROWS, 6, D_MODEL)
    w_branch_2d = w_branch.reshape(DEPTH, N_BRANCH * BRANCH_DIM, D_MODEL)
    layer_weights = []
    g_pre = norm_pre.reshape(DEPTH, 2, 1, D_MODEL)
    g_post = norm_post.reshape(DEPTH, 2, 1, D_MODEL)
    rope_tabs = _rope_tables(dec_seq)
    cak = cache_a_k.reshape(dec_batch, DEPTH, past, KV_A)
    cav = cache_a_v.reshape(dec_batch, DEPTH, past, KV_A)
    cbk = cache_b_k.reshape(dec_batch, DEPTH, past, BRANCH_DIM)
    cbv = cache_b_v.reshape(dec_batch, DEPTH, past, BRANCH_DIM)

    tm_proj, tm_tail = 1024, 512

    def layer_tail(l, x, oa, ob, oc, mod_row_fn):
        w_branch_b, w_out_b, w_ffn_in_b, w_ffn_out_b = layer_weights[l]
        return _tail(oa, ob, oc, x, mods, l, g_pre, g_post, w_gates_b,
                     w_branch_b.reshape(N_BRANCH, BRANCH_DIM, D_MODEL), w_out_b,
                     w_ffn_in_b, w_ffn_out_b, tm=tm_tail, mod_row=mod_row_fn(tm_tail))

    ctx_row = lambda tm: (lambda i: CTX_MOD_ROW)
    x = x_prompt.reshape(batch * seq, D_MODEL)
    caches = None
    for l in range(DEPTH):
        qx, oc, *rest = _proj(x, mods, l, g_pre, w_proj_b, tm=tm_proj, mod_row=ctx_row(tm_proj),
                              seq=seq, caches=caches, cast_weights=(w_ffn_in,))
        caches, (w_ffn_in_b,) = rest[:4], rest[4:]
        oa, ob, w_branch_b, w_out_b, w_ffn_out_b = _attn_ctx(
            qx, caches, a_sink[l], l, seq, (w_branch_2d, w_out, w_ffn_out))
        layer_weights.append((w_branch_b, w_out_b, w_ffn_in_b, w_ffn_out_b))
        x = layer_tail(l, x, oa, ob, oc, ctx_row)
    y_prompt = x.reshape(batch, seq, D_MODEL)
    nak, nav, nbk, nbv = caches
    new_a_k = nak.reshape(batch, DEPTH, seq, A_KV_HEADS, HEAD_DIM)
    new_a_v = nav.reshape(batch, DEPTH, seq, A_KV_HEADS, HEAD_DIM)
    new_b_k = nbk.reshape(batch, DEPTH, seq, B_HEADS, HEAD_DIM)
    new_b_v = nbv.reshape(batch, DEPTH, seq, B_HEADS, HEAD_DIM)

    lat_row = lambda tm: (lambda i: (i * tm) // dec_seq)
    x = x_sample.reshape(dec_batch * dec_seq, D_MODEL)
    n_dr, n_dc = b_rpb.shape[2:]
    rpb_pad = jnp.pad(b_rpb.astype(F32), ((0, 0), (0, 0), (0, (-n_dr) % 8), (0, LANES - n_dc)))
    for l in range(DEPTH):
        qx, kv, oc = _proj(x, mods, l, g_pre, w_proj_b, tm=tm_proj,
                           mod_row=lat_row(tm_proj), seq=dec_seq, rope_tabs=rope_tabs)
        oa = _attn_a_lat(qx, qx, cak, cav, a_sink[l], l, dec_seq)
        ob = _attn_b_lat(qx, kv, cbk, cbv, rpb_pad, l, dec_seq)
        x = layer_tail(l, x, oa, ob, oc, lat_row)
    y_sample = x.reshape(dec_batch, dec_seq, D_MODEL)
    return (y_prompt, y_sample, new_a_k, new_a_v, new_b_k, new_b_v)
```

```python
import functools

import numpy as np
import jax
import jax.numpy as jnp
from jax import lax
from jax.experimental import pallas as pl
from jax.experimental.pallas import tpu as pltpu

F32 = jnp.float32
BF16 = jnp.bfloat16

D_MODEL = 1024
DEPTH = 2
GRID_W = 64
HEAD_DIM = 64
BRANCH_DIM = D_MODEL // 2
A_Q_HEADS = BRANCH_DIM // HEAD_DIM
A_KV_HEADS = A_Q_HEADS // 4
A_GROUP = A_Q_HEADS // A_KV_HEADS
A_BLOCK = 128
B_HEADS = BRANCH_DIM // HEAD_DIM
B_WIN_ROWS = 8
B_WIN_COLS = 16
C_GROUPS = 4
C_GROUP_DIM = BRANCH_DIM // C_GROUPS
N_BRANCH = 3
D_FF = -(-8 * D_MODEL // (3 * 256)) * 256
ROPE_BASE = 10000.0
NORM_EPS = 1e-6
NEG_INF = -1e30
SCALE = HEAD_DIM ** -0.5

LANES = 128
KV_A = A_KV_HEADS * HEAD_DIM
N_GATES = N_BRANCH * D_MODEL
W_QA = 0
W_KA = W_QA + BRANCH_DIM
W_VA = W_KA + KV_A
W_QB = W_VA + KV_A
W_KB = W_QB + BRANCH_DIM
W_VB = W_KB + BRANCH_DIM
W_UC = W_VB + BRANCH_DIM
W_GATES = W_UC + BRANCH_DIM
D_IN = W_GATES + N_GATES
QX_QA, QX_QB, QX_KVA = 0, BRANCH_DIM, 2 * BRANCH_DIM
KV_KB, KV_VB = 0, BRANCH_DIM
MOD_ROWS = 8
CTX_MOD_ROW = 4
VMEM_LIMIT = 56 * 1024 * 1024
BF16_SUBLANES = 16


def _cparams(sem):
    return pltpu.CompilerParams(dimension_semantics=sem, vmem_limit_bytes=VMEM_LIMIT)


def _resident(shape, index_map):
    return pl.BlockSpec(shape, index_map, pipeline_mode=pl.Buffered(1))


def _rms(x, g):
    return x * lax.rsqrt(jnp.mean(x * x, axis=-1, keepdims=True) + NORM_EPS) * g


def _sigmoid(x):
    return 0.5 * jnp.tanh(0.5 * x) + 0.5


def _silu(x):
    half = 0.5 * x
    return half + half * jnp.tanh(half)


def _dot_t(a, b):
    return lax.dot_general(a, b, (((1,), (1,)), ((), ())), preferred_element_type=F32)


def _dot(a, b):
    return jnp.dot(a, b, preferred_element_type=F32)


def _scaled(q):
    assert np.log2(SCALE) == round(np.log2(SCALE))
    return q * jnp.asarray(SCALE, q.dtype)


def _low_lanes(shape):
    return lax.broadcasted_iota(jnp.int32, shape, len(shape) - 1) < HEAD_DIM


def _mod_kernel(cv_ref, w_ref, b_ref, win_ref, o_ref, wproj_ref, wgates_ref):
    a, b = _silu(cv_ref[...]), w_ref[...]
    a_hi, b_hi = a.astype(BF16), b.astype(BF16)
    a_lo = (a - a_hi.astype(F32)).astype(BF16)
    b_lo = (b - b_hi.astype(F32)).astype(BF16)
    o_ref[...] = _dot(a_hi, b_hi) + (_dot(a_lo, b_hi) + _dot(a_hi, b_lo)) + b_ref[...]
    wproj_ref[...] = win_ref[:, :W_GATES].astype(wproj_ref.dtype)
    wgates_ref[...] = win_ref[:, W_GATES:].astype(wgates_ref.dtype)


def _modulation(cvec, w_ada, b_ada, w_in):
    tn = 1536
    n = 6 * D_MODEL
    steps = n // tn
    rows = w_in.shape[1] // steps
    return pl.pallas_call(
        _mod_kernel,
        grid=(DEPTH, steps),
        in_specs=[
            pl.BlockSpec((MOD_ROWS, D_MODEL), lambda l, j: (0, 0)),
            pl.BlockSpec((None, D_MODEL, tn), lambda l, j: (l, 0, j)),
            pl.BlockSpec((None, 1, tn), lambda l, j: (l, 0, j)),
            pl.BlockSpec((None, rows, D_IN), lambda l, j: (l, j, 0)),
        ],
        out_specs=[pl.BlockSpec((None, MOD_ROWS, tn), lambda l, j: (l, 0, j)),
                   pl.BlockSpec((None, rows, W_GATES), lambda l, j: (l, j, 0)),
                   pl.BlockSpec((None, rows, N_GATES), lambda l, j: (l, j, 0))],
        out_shape=[jax.ShapeDtypeStruct((DEPTH, MOD_ROWS, n), F32),
                   jax.ShapeDtypeStruct((DEPTH, D_MODEL, W_GATES), BF16),
                   jax.ShapeDtypeStruct((DEPTH, D_MODEL, N_GATES), BF16)],
        compiler_params=_cparams(("parallel", "parallel")),
        name="modulation",
    )(cvec, w_ada, b_ada.reshape(DEPTH, 1, n), w_in)


def _rope_cols(v, cos, sin):
    first = (lax.broadcasted_iota(jnp.int32, (v.shape[0], LANES), 1) % HEAD_DIM) < HEAD_DIM // 2
    outs = []
    for c in range(v.shape[1] // LANES):
        vc = v[:, c * LANES:(c + 1) * LANES]
        partner = jnp.where(first, pltpu.roll(vc, LANES - HEAD_DIM // 2, 1),
                            pltpu.roll(vc, HEAD_DIM // 2, 1))
        outs.append(vc * cos + partner * sin)
    return outs[0] if len(outs) == 1 else jnp.concatenate(outs, axis=1)


def _with_rope(acc, lo, hi, cos_ref, sin_ref):
    parts = []
    if lo > 0:
        parts.append(acc[:, :lo])
    parts.append(_rope_cols(acc[:, lo:hi], cos_ref[...], sin_ref[...]))
    if hi < acc.shape[1]:
        parts.append(acc[:, hi:])
    return parts[0] if len(parts) == 1 else jnp.concatenate(parts, axis=1)


def _proj_kernel(*refs, latent, n_alias, n_cast):
    x_ref, mod_ref, g_ref, w_ref, bc_ref, bs_ref, pos_ref = refs[:7]
    if latent:
        cos_ref, sin_ref, qx_ref, kv_ref, oc_ref, h_scr = refs[7:]
    else:
        rest = refs[7 + n_alias:]
        cast_src, rest = rest[:n_cast], rest[n_cast:]
        qx_ref, oc_ref, ka_ref, va_ref, kb_ref, vb_ref = rest[:6]
        cast_dst, h_scr = rest[6:6 + n_cast], rest[6 + n_cast]
        for src, dst in zip(cast_src, cast_dst):
            dst[...] = src[...].astype(dst.dtype)
    j = pl.program_id(1)

    @pl.when(j == 0)
    def _():
        h = _rms(x_ref[...], g_ref[...]) * (1.0 + mod_ref[1:2, :]) + mod_ref[0:1, :]
        h_scr[...] = h.astype(BF16)
        acc = _dot(h_scr[...], w_ref[:, W_QA:W_KB])
        qa = acc[:, W_QA:W_QA + BRANCH_DIM]
        qb = acc[:, W_QB:W_QB + BRANCH_DIM]
        kva = acc[:, W_KA:W_KA + 2 * KV_A]
        if latent:
            qa = _rope_cols(qa, cos_ref[...], sin_ref[...])
            kva = _with_rope(kva, 0, KV_A, cos_ref, sin_ref)
            qx_ref[:, QX_KVA:QX_KVA + 2 * KV_A] = kva.astype(qx_ref.dtype)
        else:
            ka_ref[...] = kva[:, :KV_A].reshape(ka_ref.shape)
            va_ref[...] = kva[:, KV_A:].reshape(va_ref.shape)
        qx_ref[:, QX_QA:QX_QA + BRANCH_DIM] = qa.astype(qx_ref.dtype)
        qx_ref[:, QX_QB:QX_QB + BRANCH_DIM] = qb.astype(qx_ref.dtype)

    @pl.when(j == 1)
    def _():
        acc = _dot(h_scr[...], w_ref[:, W_KB:W_GATES])
        kb = acc[:, 0:BRANCH_DIM]
        vb = acc[:, W_VB - W_KB:W_VB - W_KB + BRANCH_DIM]
        uc = acc[:, W_UC - W_KB:W_UC - W_KB + BRANCH_DIM]
        if latent:
            kv_ref[:, KV_KB:KV_KB + BRANCH_DIM] = kb.astype(kv_ref.dtype)
            kv_ref[:, KV_VB:KV_VB + BRANCH_DIM] = vb.astype(kv_ref.dtype)
        else:
            kb_ref[...] = kb.reshape(kb_ref.shape)
            vb_ref[...] = vb.reshape(vb_ref.shape)
        _fourier_mix(uc.astype(BF16), bc_ref, bs_ref, pos_ref, oc_ref)


def _proj(x, mods, l, g_pre, w_in, *, tm, mod_row, seq, rope_tabs=None, caches=None, cast_weights=()):
    m = x.shape[0]
    assert tm % seq == 0
    latent = rope_tabs is not None
    bc, bs, pos = _dft_tables(seq)
    pair = 2 * C_GROUP_DIM
    in_specs = [
        pl.BlockSpec((tm, D_MODEL), lambda i, j: (i, 0)),
        pl.BlockSpec((None, None, 6, D_MODEL), lambda i, j: (l, mod_row(i), 0, 0)),
        pl.BlockSpec((None, None, 1, D_MODEL), lambda i, j: (l, 0, 0, 0)),
        _resident((None, D_MODEL, W_GATES), lambda i, j: (l, 0, 0)),
        _resident((pair, pair), lambda i, j: (0, 0)),
        _resident((pair, pair), lambda i, j: (0, 0)),
        _resident((seq, 2 * seq), lambda i, j: (0, 0)),
    ]
    args = [x, mods, g_pre, w_in, bc, bs, pos]
    aliases = {}
    if latent:
        seq_tiles = seq // tm
        tab_spec = pl.BlockSpec((tm, LANES), lambda i, j: (i % seq_tiles, 0))
        in_specs += [tab_spec, tab_spec]
        args += list(rope_tabs)
        widths = (2 * BRANCH_DIM + 2 * KV_A, 2 * BRANCH_DIM, BRANCH_DIM)
        cache_specs, cache_shapes = [], []
    else:
        widths = (2 * BRANCH_DIM, BRANCH_DIM)
        cache_widths = (KV_A, KV_A, BRANCH_DIM, BRANCH_DIM)
        if caches is not None:
            in_specs += [pl.BlockSpec(memory_space=pl.ANY)] * len(caches)
            aliases = {len(args) + k: len(widths) + k for k in range(len(caches))}
            args += list(caches)
        cache_specs = [pl.BlockSpec((tm // seq, None, seq, w), lambda i, j: (i, l, 0, 0))
                       for w in cache_widths]
        cache_shapes = [jax.ShapeDtypeStruct((m // seq, DEPTH, seq, w), F32) for w in cache_widths]
        steps = 2 * (m // tm)
        for w in cast_weights:
            _, rows, cols = w.shape
            assert rows % (steps * BF16_SUBLANES) == 0
            in_specs.append(pl.BlockSpec((None, rows // steps, cols), lambda i, j: (l, 2 * i + j, 0)))
            cache_specs.append(pl.BlockSpec((rows // steps, cols), lambda i, j: (2 * i + j, 0)))
            cache_shapes.append(jax.ShapeDtypeStruct((rows, cols), BF16))
            args.append(w)
    return pl.pallas_call(
        functools.partial(_proj_kernel, latent=latent, n_alias=len(aliases), n_cast=len(cast_weights)),
        grid=(m // tm, 2),
        in_specs=in_specs,
        out_specs=[pl.BlockSpec((tm, w), lambda i, j: (i, 0)) for w in widths] + cache_specs,
        out_shape=[jax.ShapeDtypeStruct((m, w), BF16) for w in widths] + cache_shapes,
        input_output_aliases=aliases,
        scratch_shapes=[pltpu.VMEM((tm, D_MODEL), BF16)],
        compiler_params=_cparams(("parallel", "arbitrary")),
        name="proj_lat" if latent else "proj_ctx",
    )(*args)


def _with_ones(v):
    return jnp.concatenate([v, jnp.ones_like(v)], axis=1)


def _softmax_pv(parts, sink=None):
    m = parts[0][0].max(axis=-1, keepdims=True)
    for s, _ in parts[1:]:
        m = jnp.maximum(m, s.max(axis=-1, keepdims=True))
    if sink is not None:
        m = jnp.maximum(m, sink)
    ones_half = parts[0][1].shape[1] == 2 * LANES
    acc = None
    den = None
    for s, v in parts:
        e = jnp.exp(s - m)
        if not ones_half:
            d = e.sum(axis=-1, keepdims=True)
            den = d if den is None else den + d
        o = _dot(e.astype(BF16), v)
        acc = o if acc is None else acc + o
    if ones_half:
        acc, den = acc[:, :LANES], acc[:, LANES:]
    if sink is not None:
        den = den + jnp.exp(sink - m)
    return acc * (1.0 / den)


def _kv_head_variants(x2):
    low = _low_lanes(x2.shape)
    xr = pltpu.roll(x2, HEAD_DIM, 1)
    zero = jnp.zeros_like(x2)
    return [
        [jnp.where(low, x2, zero).astype(BF16), jnp.where(low, zero, xr).astype(BF16)],
        [jnp.where(low, xr, zero).astype(BF16), jnp.where(low, zero, x2).astype(BF16)],
    ]


def _attn_ctx_kernel(sink_ref, qa_ref, qb_ref, ka_ref, va_ref, kb_ref, vb_ref, *rest, n_weights):
    w_src, (oa_ref, ob_ref), w_dst = rest[:n_weights], rest[n_weights:n_weights + 2], rest[n_weights + 2:]
    for src, dst in zip(w_src, w_dst):
        dst[...] = src[...].astype(dst.dtype)
    ka = [jnp.concatenate(v, axis=0) for v in _kv_head_variants(ka_ref[...])]
    va = [jnp.concatenate(v, axis=0) for v in _kv_head_variants(va_ref[...])]
    chunks = [slice(j * LANES, (j + 1) * LANES) for j in range(A_Q_HEADS // 2)]
    work = []
    for j, cols in enumerate(chunks):
        g = (2 * j) // A_GROUP
        s2 = _dot_t(_scaled(qa_ref[:, cols]), ka[g])
        work.append((oa_ref, cols, s2, va[g], (sink_ref[2 * j], sink_ref[2 * j + 1])))
    low = _low_lanes((kb_ref.shape[0], LANES))
    for cols in chunks:
        kc, vc = kb_ref[:, cols], vb_ref[:, cols]
        zero = jnp.zeros_like(kc)
        k2 = jnp.concatenate([jnp.where(low, kc, zero), jnp.where(low, zero, kc)], axis=0).astype(BF16)
        v2 = jnp.concatenate([jnp.where(low, vc, zero), jnp.where(low, zero, vc)], axis=0).astype(BF16)
        work.append((ob_ref, cols, _dot_t(_scaled(qb_ref[:, cols]), k2), v2, None))
    for o_ref, cols, s2, v2, sinks in work:
        o_ref[:, cols] = _pair_softmax_pv(s2, v2, sinks).astype(o_ref.dtype)


def _pair_softmax_pv(s2, v2, sinks=None):
    tk = s2.shape[1] // 2
    es, extra = [], []
    for p in range(2):
        s = s2[:, p * tk:(p + 1) * tk]
        m = s.max(axis=-1, keepdims=True)
        if sinks is not None:
            m = jnp.maximum(m, sinks[p])
            extra.append(jnp.exp(sinks[p] - m))
        es.append(jnp.exp(s - m).astype(BF16))
    lane_half = lax.broadcasted_iota(jnp.int32, v2.shape, 1) // HEAD_DIM
    row_half = lax.broadcasted_iota(jnp.int32, v2.shape, 0) // tk
    ones2 = jnp.where(lane_half == row_half, 1.0, 0.0).astype(v2.dtype)
    acc = _dot(jnp.concatenate(es, axis=1), jnp.concatenate([v2, ones2], axis=1))
    den = acc[:, LANES:]
    if sinks is not None:
        den = den + jnp.where(_low_lanes(den.shape), extra[0], extra[1])
    return acc[:, :LANES] * (1.0 / den)


def _attn_ctx(qx, caches, sink, l, seq, weights):
    m = qx.shape[0]
    steps = m // seq
    out = jax.ShapeDtypeStruct((m, BRANCH_DIM), BF16)
    cache_specs = [pl.BlockSpec((None, None, seq, c.shape[-1]), lambda b: (b, l, 0, 0)) for c in caches]
    w_in_specs, w_out_specs, w_shapes = [], [], []
    for w in weights:
        _, rows, cols = w.shape
        assert rows % (steps * BF16_SUBLANES) == 0
        w_in_specs.append(pl.BlockSpec((None, rows // steps, cols), lambda b: (l, b, 0)))
        w_out_specs.append(pl.BlockSpec((rows // steps, cols), lambda b: (b, 0)))
        w_shapes.append(jax.ShapeDtypeStruct((rows, cols), BF16))
    return pl.pallas_call(
        functools.partial(_attn_ctx_kernel, n_weights=len(weights)),
        grid=(steps,),
        in_specs=[
            pl.BlockSpec(memory_space=pltpu.SMEM),
            pl.BlockSpec((seq, BRANCH_DIM), lambda b: (b, QX_QA // BRANCH_DIM)),
            pl.BlockSpec((seq, BRANCH_DIM), lambda b: (b, QX_QB // BRANCH_DIM)),
        ] + cache_specs + w_in_specs,
        out_specs=[pl.BlockSpec((seq, BRANCH_DIM), lambda b: (b, 0))] * 2 + w_out_specs,
        out_shape=[out, out] + w_shapes,
        compiler_params=_cparams(("parallel",)),
        name="attn_ctx",
    )(sink, qx, qx, *caches, *weights)


def _attn_a_lat_kernel(sink_ref, q_ref, kv_ref, ck_ref, cv_ref, o_ref,
                       qh_scr, kp_scr, vp_scr, ckb_scr, cvb_scr, mask_scr, *, n_blocks):
    seq = n_blocks * A_BLOCK
    band = 3 * A_BLOCK
    low_seq = _low_lanes((seq, LANES))
    for h in range(A_Q_HEADS):
        j, p, g = h // 2, h % 2, h // A_GROUP
        x = q_ref[:, j * LANES:(j + 1) * LANES].astype(F32) * SCALE
        if p != g:
            x = pltpu.roll(x, HEAD_DIM, 1)
        qh_scr[h] = jnp.where(low_seq if g == 0 else jnp.logical_not(low_seq), x, 0.0).astype(BF16)
    pad = jnp.zeros((A_BLOCK, LANES), BF16)
    for scr, col in ((kp_scr, 0), (vp_scr, KV_A)):
        scr[0:A_BLOCK, :] = pad
        scr[A_BLOCK:A_BLOCK + seq, :] = kv_ref[:, col:col + KV_A]
        scr[A_BLOCK + seq:2 * A_BLOCK + seq, :] = pad
    ckb_scr[...] = ck_ref[...].astype(BF16)
    cvb_scr[...] = cv_ref[...].astype(BF16)

    rows = A_GROUP * A_BLOCK
    row = lax.broadcasted_iota(jnp.int32, (rows, band), 0)
    qi = row % A_BLOCK
    kj = lax.broadcasted_iota(jnp.int32, (rows, band), 1)
    head_row = lax.broadcasted_iota(jnp.int32, (rows, 1), 0) // A_BLOCK
    low_blk = _low_lanes((A_BLOCK, LANES))
    for case, (no_prev, no_next) in enumerate(((1, 0), (0, 0), (0, 1))):
        valid = (kj >= qi + no_prev * (A_BLOCK - qi)) & (kj <= 2 * A_BLOCK + qi - no_next * (qi + 1))
        mask_scr[case] = jnp.where(valid, 0.0, NEG_INF)

    def body(n, carry):
        start = pl.multiple_of(n * A_BLOCK, A_BLOCK)
        case = jnp.where(n > 0, 1, 0) + jnp.where(n < n_blocks - 1, 0, 1)
        scores = []
        for g in range(A_KV_HEADS):
            q = jnp.concatenate(
                [qh_scr[A_GROUP * g + i, pl.ds(start, A_BLOCK), :] for i in range(A_GROUP)], axis=0)
            s_band = _dot_t(q, kp_scr[pl.ds(start, band), :]) + mask_scr[case]
            scores.append((s_band, _dot_t(q, ckb_scr[...])))
        outs = []
        for g, (s_band, s_ctx) in enumerate(scores):
            sink = jnp.full((rows, 1), sink_ref[A_GROUP * g], F32)
            for i in range(1, A_GROUP):
                sink = jnp.where(head_row == i, sink_ref[A_GROUP * g + i], sink)
            outs.append(_softmax_pv([(s_band, vp_scr[pl.ds(start, band), :]), (s_ctx, cvb_scr[...])],
                                    sink=sink))
        for j in range(A_Q_HEADS // 2):
            halves = []
            for p in range(2):
                h = 2 * j + p
                g, i = h // A_GROUP, h % A_GROUP
                t = outs[g][i * A_BLOCK:(i + 1) * A_BLOCK]
                halves.append(t if p == g else pltpu.roll(t, HEAD_DIM, 1))
            o_ref[pl.ds(start, A_BLOCK), j * LANES:(j + 1) * LANES] = (
                jnp.where(low_blk, halves[0], halves[1]).astype(o_ref.dtype))
        return carry

    lax.fori_loop(0, n_blocks, body, 0, unroll=2)


def _attn_a_lat(main, kv, cache_k, cache_v, sink, l, seq):
    m = main.shape[0]
    nb = seq // A_BLOCK
    assert nb >= 2
    n_seq = m // seq
    past = cache_k.shape[2]
    cache_spec = pl.BlockSpec((None, None, past, KV_A), lambda b: (b, l, 0, 0))
    return pl.pallas_call(
        functools.partial(_attn_a_lat_kernel, n_blocks=nb),
        grid=(n_seq,),
        in_specs=[
            pl.BlockSpec(memory_space=pltpu.SMEM),
            pl.BlockSpec((seq, BRANCH_DIM), lambda b: (b, QX_QA // BRANCH_DIM)),
            pl.BlockSpec((seq, 2 * KV_A), lambda b: (b, QX_KVA // (2 * KV_A))),
            cache_spec, cache_spec,
        ],
        out_specs=pl.BlockSpec((seq, BRANCH_DIM), lambda b: (b, 0)),
        out_shape=jax.ShapeDtypeStruct((m, BRANCH_DIM), BF16),
        scratch_shapes=[
            pltpu.VMEM((A_Q_HEADS, seq, LANES), BF16),
            pltpu.VMEM((seq + 2 * A_BLOCK, LANES), BF16),
            pltpu.VMEM((seq + 2 * A_BLOCK, LANES), BF16),
            pltpu.VMEM((past, LANES), BF16),
            pltpu.VMEM((past, LANES), BF16),
            pltpu.VMEM((3, A_GROUP * A_BLOCK, 3 * A_BLOCK), F32),
        ],
        compiler_params=_cparams(("parallel",)),
        name="attn_a_lat",
    )(sink, main, kv, cache_k, cache_v)


def _attn_b_lat_kernel(q_ref, k_ref, v_ref, ck_ref, cv_ref, rpb_ref, o_ref, ckb_scr, vx_scr, cvx_scr,
                       bias_ref, *, rows, kr):
    @pl.when(pl.program_id(0) == 0)
    def _():
        _fill_neighbourhood_bias(rpb_ref, bias_ref, kr=kr)

    half = kr // 2
    n_loc = kr * GRID_W
    low_q = _low_lanes((GRID_W, LANES))
    chunks = [slice(j * LANES, (j + 1) * LANES) for j in range(B_HEADS // 2)]
    ckb_scr[...] = ck_ref[...].astype(BF16)
    for j, cols in enumerate(chunks):
        vx_scr[j] = _with_ones(v_ref[:, cols])
        cvx_scr[j] = _with_ones(cv_ref[:, cols].astype(BF16))
    rows_per_iter = 4
    assert rows % rows_per_iter == 0

    def body(it, carry):
        work = []
        for u in range(rows_per_iter):
            r = it * rows_per_iter + u
            r0 = jnp.clip(r - half, 0, rows - kr)
            q0 = pl.multiple_of(r * GRID_W, GRID_W)
            start = pl.multiple_of(r0 * GRID_W, GRID_W)
            for j, cols in enumerate(chunks):
                qc = _scaled(q_ref[pl.ds(q0, GRID_W), cols])
                zero = jnp.zeros_like(qc)
                qs = jnp.concatenate([jnp.where(low_q, qc, zero), jnp.where(low_q, zero, qc)], axis=0)
                s_loc = _dot_t(qs, k_ref[pl.ds(start, n_loc), cols]) + bias_ref[r - r0, j]
                work.append((q0, start, j, cols, s_loc, _dot_t(qs, ckb_scr[:, cols])))
        for q0, start, j, cols, s_loc, s_ctx in work:
            o2 = _softmax_pv([(s_loc, vx_scr[j, pl.ds(start, n_loc), :]), (s_ctx, cvx_scr[j])])
            o = jnp.where(low_q, o2[:GRID_W], o2[GRID_W:])
            o_ref[pl.ds(q0, GRID_W), cols] = o.astype(o_ref.dtype)
        return carry

    lax.fori_loop(0, rows // rows_per_iter, body, 0)


def _attn_b_lat(main, kv, cache_k, cache_v, rpb_pad, l, seq):
    m = main.shape[0]
    rows = seq // GRID_W
    kr = min(B_WIN_ROWS, rows)
    assert kr % 2 == 0 and 2 * GRID_W == LANES
    n_seq = m // seq
    past = cache_k.shape[2]
    cache_spec = pl.BlockSpec((None, None, past, BRANCH_DIM), lambda b: (b, l, 0, 0))
    return pl.pallas_call(
        functools.partial(_attn_b_lat_kernel, rows=rows, kr=kr),
        grid=(n_seq,),
        in_specs=[
            pl.BlockSpec((seq, BRANCH_DIM), lambda b: (b, QX_QB // BRANCH_DIM)),
            pl.BlockSpec((seq, BRANCH_DIM), lambda b: (b, KV_KB // BRANCH_DIM)),
            pl.BlockSpec((seq, BRANCH_DIM), lambda b: (b, KV_VB // BRANCH_DIM)),
            cache_spec, cache_spec,
            _resident((None,) + rpb_pad.shape[1:], lambda b: (l, 0, 0, 0)),
        ],
        out_specs=pl.BlockSpec((seq, BRANCH_DIM), lambda b: (b, 0)),
        out_shape=jax.ShapeDtypeStruct((m, BRANCH_DIM), BF16),
        scratch_shapes=[
            pltpu.VMEM((past, BRANCH_DIM), BF16),
            pltpu.VMEM((B_HEADS // 2, seq, 2 * LANES), BF16),
            pltpu.VMEM((B_HEADS // 2, past, 2 * LANES), BF16),
            pltpu.VMEM((kr, B_HEADS // 2, 2 * GRID_W, kr * GRID_W), F32),
        ],
        compiler_params=_cparams(("arbitrary",)),
        name="attn_b_lat",
    )(main, kv, kv, cache_k, cache_v, rpb_pad)


def _fill_neighbourhood_bias(rpb_ref, bias_scr, *, kr):
    shape = (GRID_W, LANES)
    c = lax.broadcasted_iota(jnp.int32, shape, 0)
    lane = lax.broadcasted_iota(jnp.int32, shape, 1)
    c2 = lane % GRID_W
    ws = jnp.clip(c - B_WIN_COLS // 2, 0, GRID_W - B_WIN_COLS)
    ok = (c2 >= ws) & (c2 < ws + B_WIN_COLS)
    low = lane < GRID_W

    def toeplitz(h, dr, lane0):
        row = jnp.broadcast_to(rpb_ref[h, dr:dr + 1, :], shape)
        return pltpu.roll(row, (lane0 - (B_WIN_COLS - 1)) % LANES, 1, stride=1, stride_axis=0)

    for h in range(B_HEADS):
        rows = slice(h % 2 * GRID_W, (h % 2 + 1) * GRID_W)
        pairs = {}
        for v in range(kr):
            for m in range(0, kr, 2):
                dr = m - v + B_WIN_ROWS - 1
                if dr not in pairs:
                    pair = jnp.where(low, toeplitz(h, dr, 0), toeplitz(h, dr + 1, GRID_W))
                    pairs[dr] = jnp.where(ok, pair, NEG_INF)
                bias_scr[v, h // 2, rows, m * GRID_W:(m + 2) * GRID_W] = pairs[dr]


def _dft_tables(seq):
    cd = C_GROUP_DIM
    kc = (np.arange(cd)[:, None] * np.arange(cd)[None, :]) % cd
    ang_c = 2.0 * np.pi * kc / cd
    eye2 = np.eye(2)
    bd_cos = np.kron(eye2, np.cos(ang_c))
    bd_sin = np.kron(eye2, np.sin(ang_c))
    kt = (np.arange(seq)[:, None] * np.arange(seq)[None, :]) % seq
    ang_t = 2.0 * np.pi * kt / seq
    norm = 1.0 / np.sqrt(float(seq * cd))
    pos = np.concatenate([np.cos(ang_t), -np.sin(ang_t)], axis=1) * norm
    return (jnp.asarray(bd_cos, F32).astype(BF16), jnp.asarray(bd_sin, F32).astype(BF16),
            jnp.asarray(pos, F32).astype(BF16))


def _fourier_mix(u, bc_ref, bs_ref, pos_ref, o_ref):
    pair = 2 * C_GROUP_DIM
    seq = pos_ref.shape[0]
    uc, us = [], []
    for p in range(BRANCH_DIM // pair):
        up = u[:, p * pair:(p + 1) * pair]
        uc.append(_dot(up, bc_ref[...]))
        us.append(_dot(up, bs_ref[...]))
    zc = jnp.concatenate(uc, axis=1).astype(BF16)
    zs = jnp.concatenate(us, axis=1).astype(BF16)
    for s in range(u.shape[0] // seq):
        rows = slice(s * seq, (s + 1) * seq)
        z = jnp.concatenate([zc[rows], zs[rows]], axis=0)
        o_ref[rows, :] = _dot(pos_ref[...], z).astype(o_ref.dtype)


def _ffn_weight_copies(wfi_hbm, wfo_hbm, wfg_ref, wfu_ref, wfo_ref, sems):
    srcs = (wfi_hbm.at[:, 0:D_FF], wfi_hbm.at[:, D_FF:2 * D_FF], wfo_hbm)
    dsts = (wfg_ref, wfu_ref, wfo_ref)
    return [pltpu.make_async_copy(src, dst, sems.at[k]) for k, (src, dst) in enumerate(zip(srcs, dsts))]


def _tail_kernel(oa_ref, ob_ref, oc_ref, x_ref, mod_ref, gpre0_ref, gpost0_ref, gpre1_ref, gpost1_ref,
                 wg_ref, wb_ref, wo_ref, wfi_hbm, wfo_hbm, o_ref, wfg_ref, wfu_ref, wfo_ref, sems):
    first = pl.program_id(0) == 0
    copies = _ffn_weight_copies(wfi_hbm, wfo_hbm, wfg_ref, wfu_ref, wfo_ref, sems)

    @pl.when(first)
    def _():
        for copy in copies:
            copy.start()

    x = x_ref[...]
    h = (_rms(x, gpre0_ref[...]) * (1.0 + mod_ref[1:2, :]) + mod_ref[0:1, :]).astype(BF16)
    gates = _dot(h, wg_ref[...])
    branches = [_dot(b_ref[...], wb_ref[i]) for i, b_ref in enumerate((oa_ref, ob_ref, oc_ref))]
    mix = None
    for i, branch in enumerate(branches):
        term = _sigmoid(gates[:, i * D_MODEL:(i + 1) * D_MODEL]) * branch
        mix = term if mix is None else mix + term
    y = _dot(mix.astype(BF16), wo_ref[...])
    x1 = x + mod_ref[2:3, :] * _rms(y, gpost0_ref[...])
    h2 = (_rms(x1, gpre1_ref[...]) * (1.0 + mod_ref[4:5, :]) + mod_ref[3:4, :]).astype(BF16)

    @pl.when(first)
    def _():
        for copy in copies:
            copy.wait()

    act = _silu(_dot(h2, wfg_ref[...])) * _dot(h2, wfu_ref[...])
    f = _dot(act.astype(BF16), wfo_ref[...])
    o_ref[...] = x1 + mod_ref[5:6, :] * _rms(f, gpost1_ref[...])


def _tail(oa, ob, oc, x, mods, l, g_pre, g_post, w_gates, w_branch, w_out, w_ffn_in, w_ffn_out,
          *, tm, mod_row):
    m = x.shape[0]
    o_spec = pl.BlockSpec((tm, BRANCH_DIM), lambda i: (i, 0))
    x_spec = pl.BlockSpec((tm, D_MODEL), lambda i: (i, 0))
    gain_spec = lambda which: pl.BlockSpec((None, None, 1, D_MODEL), lambda i: (l, which, 0, 0))
    return pl.pallas_call(
        _tail_kernel,
        grid=(m // tm,),
        in_specs=[
            o_spec, o_spec, o_spec,
            x_spec,
            pl.BlockSpec((None, None, 6, D_MODEL), lambda i: (l, mod_row(i), 0, 0)),
            gain_spec(0), gain_spec(0), gain_spec(1), gain_spec(1),
            _resident((None, D_MODEL, N_GATES), lambda i: (l, 0, 0)),
            _resident((N_BRANCH, BRANCH_DIM, D_MODEL), lambda i: (0, 0, 0)),
            _resident((D_MODEL, D_MODEL), lambda i: (0, 0)),
            pl.BlockSpec(memory_space=pl.ANY),
            pl.BlockSpec(memory_space=pl.ANY),
        ],
        out_specs=x_spec,
        out_shape=jax.ShapeDtypeStruct((m, D_MODEL), F32),
        scratch_shapes=[
            pltpu.VMEM((D_MODEL, D_FF), BF16),
            pltpu.VMEM((D_MODEL, D_FF), BF16),
            pltpu.VMEM((D_FF, D_MODEL), BF16),
            pltpu.SemaphoreType.DMA((3,)),
        ],
        compiler_params=_cparams(("arbitrary",)),
        name="tail",
    )(oa, ob, oc, x, mods, g_pre, g_post, g_pre, g_post, w_gates, w_branch, w_out,
      w_ffn_in, w_ffn_out)


def _rope_tables(seq):
    t = jnp.arange(seq, dtype=jnp.int32)
    row = (t // GRID_W).astype(F32)
    col = (t % GRID_W).astype(F32)
    n_pairs_axis = HEAD_DIM // 4
    inv = ROPE_BASE ** (-jnp.arange(n_pairs_axis, dtype=F32) / n_pairs_axis)
    ang = jnp.concatenate([row[:, None] * inv, col[:, None] * inv], axis=-1)
    cos, sin = jnp.cos(ang), jnp.sin(ang)
    cos_l = jnp.tile(cos, (1, LANES // cos.shape[1]))
    sin_l = jnp.tile(jnp.concatenate([-sin, sin], axis=-1), (1, LANES // HEAD_DIM))
    return cos_l, sin_l


def kernel(x_prompt, x_sample, cache_a_k, cache_a_v, cache_b_k, cache_b_v, c, c_ctx, w_ada, b_ada,
           norm_pre, norm_post, w_in, a_sink, b_rpb, w_branch, w_out, w_ffn_in, w_ffn_out):
    batch, seq, _ = x_prompt.shape
    dec_batch, dec_seq, _ = x_sample.shape
    past = cache_a_k.shape[2]
    assert dec_batch <= CTX_MOD_ROW and seq % A_BLOCK == 0 and dec_seq % A_BLOCK == 0

    cvec = jnp.concatenate(
        [c, c_ctx[None, :], jnp.zeros((MOD_ROWS - dec_batch - 1, D_MODEL), F32)], axis=0)
    assert w_in.shape[-1] == D_IN
    mods, w_proj_b, w_gates_b = _modulation(cvec, w_ada, b_ada, w_in)
    mods = mods.reshape(DEPTH, MOD_ROWS, 6, D_MODEL)
    w_branch_2d = w_branch.reshape(DEPTH, N_BRANCH * BRANCH_DIM, D_MODEL)
    layer_weights = []
    g_pre = norm_pre.reshape(DEPTH, 2, 1, D_MODEL)
    g_post = norm_post.reshape(DEPTH, 2, 1, D_MODEL)
    rope_tabs = _rope_tables(dec_seq)
    cak = cache_a_k.reshape(dec_batch, DEPTH, past, KV_A)
    cav = cache_a_v.reshape(dec_batch, DEPTH, past, KV_A)
    cbk = cache_b_k.reshape(dec_batch, DEPTH, past, BRANCH_DIM)
    cbv = cache_b_v.reshape(dec_batch, DEPTH, past, BRANCH_DIM)

    tm_proj, tm_tail = 1024, 512

    def layer_tail(l, x, oa, ob, oc, mod_row_fn):
        w_branch_b, w_out_b, w_ffn_in_b, w_ffn_out_b = layer_weights[l]
        return _tail(oa, ob, oc, x, mods, l, g_pre, g_post, w_gates_b,
                     w_branch_b.reshape(N_BRANCH, BRANCH_DIM, D_MODEL), w_out_b,
                     w_ffn_in_b, w_ffn_out_b, tm=tm_tail, mod_row=mod_row_fn(tm_tail))

    ctx_row = lambda tm: (lambda i: CTX_MOD_ROW)
    x = x_prompt.reshape(batch * seq, D_MODEL)
    caches = None
    for l in range(DEPTH):
        qx, oc, *rest = _proj(x, mods, l, g_pre, w_proj_b, tm=tm_proj, mod_row=ctx_row(tm_proj),
                              seq=seq, caches=caches, cast_weights=(w_ffn_in,))
        caches, (w_ffn_in_b,) = rest[:4], rest[4:]
        oa, ob, w_branch_b, w_out_b, w_ffn_out_b = _attn_ctx(
            qx, caches, a_sink[l], l, seq, (w_branch_2d, w_out, w_ffn_out))
        layer_weights.append((w_branch_b, w_out_b, w_ffn_in_b, w_ffn_out_b))
        x = layer_tail(l, x, oa, ob, oc, ctx_row)
    y_prompt = x.reshape(batch, seq, D_MODEL)
    nak, nav, nbk, nbv = caches
    new_a_k = nak.reshape(batch, DEPTH, seq, A_KV_HEADS, HEAD_DIM)
    new_a_v = nav.reshape(batch, DEPTH, seq, A_KV_HEADS, HEAD_DIM)
    new_b_k = nbk.reshape(batch, DEPTH, seq, B_HEADS, HEAD_DIM)
    new_b_v = nbv.reshape(batch, DEPTH, seq, B_HEADS, HEAD_DIM)

    lat_row = lambda tm: (lambda i: (i * tm) // dec_seq)
    x = x_sample.reshape(dec_batch * dec_seq, D_MODEL)
    n_dr, n_dc = b_rpb.shape[2:]
    rpb_pad = jnp.pad(b_rpb.astype(F32), ((0, 0), (0, 0), (0, (-n_dr) % 8), (0, LANES - n_dc)))
    for l in range(DEPTH):
        qx, kv, oc = _proj(x, mods, l, g_pre, w_proj_b, tm=tm_proj,
                           mod_row=lat_row(tm_proj), seq=dec_seq, rope_tabs=rope_tabs)
        oa = _attn_a_lat(qx, qx, cak, cav, a_sink[l], l, dec_seq)
        ob = _attn_b_lat(qx, kv, cbk, cbv, rpb_pad, l, dec_seq)
        x = layer_tail(l, x, oa, ob, oc, lat_row)
    y_sample = x.reshape(dec_batch, dec_seq, D_MODEL)
    return (y_prompt, y_sample, new_a_k, new_a_v, new_b_k, new_b_v)
```

```python
import functools

import numpy as np
import jax
import jax.numpy as jnp
from jax import lax
from jax.experimental import pallas as pl
from jax.experimental.pallas import tpu as pltpu

F32 = jnp.float32
BF16 = jnp.bfloat16

D_MODEL = 1024
DEPTH = 2
GRID_W = 64
HEAD_DIM = 64
BRANCH_DIM = D_MODEL // 2
A_Q_HEADS = BRANCH_DIM // HEAD_DIM
A_KV_HEADS = A_Q_HEADS // 4
A_GROUP = A_Q_HEADS // A_KV_HEADS
A_BLOCK = 128
B_HEADS = BRANCH_DIM // HEAD_DIM
B_WIN_ROWS = 8
B_WIN_COLS = 16
C_GROUPS = 4
C_GROUP_DIM = BRANCH_DIM // C_GROUPS
N_BRANCH = 3
D_FF = -(-8 * D_MODEL // (3 * 256)) * 256
ROPE_BASE = 10000.0
NORM_EPS = 1e-6
NEG_INF = -1e30
SCALE = HEAD_DIM ** -0.5

LANES = 128
KV_A = A_KV_HEADS * HEAD_DIM
N_GATES = N_BRANCH * D_MODEL
W_QA = 0
W_KA = W_QA + BRANCH_DIM
W_VA = W_KA + KV_A
W_QB = W_VA + KV_A
W_KB = W_QB + BRANCH_DIM
W_VB = W_KB + BRANCH_DIM
W_UC = W_VB + BRANCH_DIM
W_GATES = W_UC + BRANCH_DIM
D_IN = W_GATES + N_GATES
QX_QA, QX_QB, QX_KVA = 0, BRANCH_DIM, 2 * BRANCH_DIM
KV_KB, KV_VB = 0, BRANCH_DIM
MOD_ROWS = 8
CTX_MOD_ROW = 4
VMEM_LIMIT = 56 * 1024 * 1024
BF16_SUBLANES = 16


def _cparams(sem):
    return pltpu.CompilerParams(dimension_semantics=sem, vmem_limit_bytes=VMEM_LIMIT)


def _resident(shape, index_map):
    return pl.BlockSpec(shape, index_map, pipeline_mode=pl.Buffered(1))


def _rms(x, g):
    return x * lax.rsqrt(jnp.mean(x * x, axis=-1, keepdims=True) + NORM_EPS) * g


def _sigmoid(x):
    return 0.5 * jnp.tanh(0.5 * x) + 0.5


def _silu(x):
    half = 0.5 * x
    return half + half * jnp.tanh(half)


def _dot_t(a, b):
    return lax.dot_general(a, b, (((1,), (1,)), ((), ())), preferred_element_type=F32)


def _dot(a, b):
    return jnp.dot(a, b, preferred_element_type=F32)


def _scaled(q):
    assert np.log2(SCALE) == round(np.log2(SCALE))
    return q * jnp.asarray(SCALE, q.dtype)


def _low_lanes(shape):
    return lax.broadcasted_iota(jnp.int32, shape, len(shape) - 1) < HEAD_DIM


def _mod_kernel(cv_ref, w_ref, b_ref, win_ref, o_ref, wproj_ref, wgates_ref):
    a, b = _silu(cv_ref[...]), w_ref[...]
    a_hi, b_hi = a.astype(BF16), b.astype(BF16)
    a_lo = (a - a_hi.astype(F32)).astype(BF16)
    b_lo = (b - b_hi.astype(F32)).astype(BF16)
    o_ref[...] = _dot(a_hi, b_hi) + (_dot(a_lo, b_hi) + _dot(a_hi, b_lo)) + b_ref[...]
    wproj_ref[...] = win_ref[:, :W_GATES].astype(wproj_ref.dtype)
    wgates_ref[...] = win_ref[:, W_GATES:].astype(wgates_ref.dtype)


def _modulation(cvec, w_ada, b_ada, w_in):
    tn = 1536
    n = 6 * D_MODEL
    steps = n // tn
    rows = w_in.shape[1] // steps
    return pl.pallas_call(
        _mod_kernel,
        grid=(DEPTH, steps),
        in_specs=[
            pl.BlockSpec((MOD_ROWS, D_MODEL), lambda l, j: (0, 0)),
            pl.BlockSpec((None, D_MODEL, tn), lambda l, j: (l, 0, j)),
            pl.BlockSpec((None, 1, tn), lambda l, j: (l, 0, j)),
            pl.BlockSpec((None, rows, D_IN), lambda l, j: (l, j, 0)),
        ],
        out_specs=[pl.BlockSpec((None, MOD_ROWS, tn), lambda l, j: (l, 0, j)),
                   pl.BlockSpec((None, rows, W_GATES), lambda l, j: (l, j, 0)),
                   pl.BlockSpec((None, rows, N_GATES), lambda l, j: (l, j, 0))],
        out_shape=[jax.ShapeDtypeStruct((DEPTH, MOD_ROWS, n), F32),
                   jax.ShapeDtypeStruct((DEPTH, D_MODEL, W_GATES), BF16),
                   jax.ShapeDtypeStruct((DEPTH, D_MODEL, N_GATES), BF16)],
        compiler_params=_cparams(("parallel", "parallel")),
        name="modulation",
    )(cvec, w_ada, b_ada.reshape(DEPTH, 1, n), w_in)


def _rope_cols(v, cos, sin):
    first = (lax.broadcasted_iota(jnp.int32, (v.shape[0], LANES), 1) % HEAD_DIM) < HEAD_DIM // 2
    outs = []
    for c in range(v.shape[1] // LANES):
        vc = v[:, c * LANES:(c + 1) * LANES]
        partner = jnp.where(first, pltpu.roll(vc, LANES - HEAD_DIM // 2, 1),
                            pltpu.roll(vc, HEAD_DIM // 2, 1))
        outs.append(vc * cos + partner * sin)
    return outs[0] if len(outs) == 1 else jnp.concatenate(outs, axis=1)


def _with_rope(acc, lo, hi, cos_ref, sin_ref):
    parts = []
    if lo > 0:
        parts.append(acc[:, :lo])
    parts.append(_rope_cols(acc[:, lo:hi], cos_ref[...], sin_ref[...]))
    if hi < acc.shape[1]:
        parts.append(acc[:, hi:])
    return parts[0] if len(parts) == 1 else jnp.concatenate(parts, axis=1)


def _proj_kernel(*refs, latent, n_alias, n_cast):
    x_ref, mod_ref, g_ref, w_ref, bc_ref, bs_ref, pos_ref = refs[:7]
    if latent:
        cos_ref, sin_ref, _, qx_ref, kv_ref, oc_ref, h_scr = refs[7:]
    else:
        rest = refs[7 + n_alias:]
        cast_src, rest = rest[:n_cast], rest[n_cast:]
        qx_ref, oc_ref, ka_ref, va_ref, kb_ref, vb_ref = rest[:6]
        cast_dst, h_scr = rest[6:6 + n_cast], rest[6 + n_cast]
        for src, dst in zip(cast_src, cast_dst):
            dst[...] = src[...].astype(dst.dtype)
    j = pl.program_id(1)

    @pl.when(j == 0)
    def _():
        h = _rms(x_ref[...], g_ref[...]) * (1.0 + mod_ref[1:2, :]) + mod_ref[0:1, :]
        h_scr[...] = h.astype(BF16)
        acc = _dot(h_scr[...], w_ref[:, W_QA:W_KB])
        qa = acc[:, W_QA:W_QA + BRANCH_DIM]
        qb = acc[:, W_QB:W_QB + BRANCH_DIM]
        kva = acc[:, W_KA:W_KA + 2 * KV_A]
        if latent:
            qa = _rope_cols(qa, cos_ref[...], sin_ref[...])
            kva = _with_rope(kva, 0, KV_A, cos_ref, sin_ref)
            qx_ref[:, QX_KVA:QX_KVA + 2 * KV_A] = kva.astype(qx_ref.dtype)
        else:
            ka_ref[...] = kva[:, :KV_A].reshape(ka_ref.shape)
            va_ref[...] = kva[:, KV_A:].reshape(va_ref.shape)
        qx_ref[:, QX_QA:QX_QA + BRANCH_DIM] = qa.astype(qx_ref.dtype)
        qx_ref[:, QX_QB:QX_QB + BRANCH_DIM] = qb.astype(qx_ref.dtype)

    @pl.when(j == 1)
    def _():
        acc = _dot(h_scr[...], w_ref[:, W_KB:W_GATES])
        kb = acc[:, 0:BRANCH_DIM]
        vb = acc[:, W_VB - W_KB:W_VB - W_KB + BRANCH_DIM]
        uc = acc[:, W_UC - W_KB:W_UC - W_KB + BRANCH_DIM]
        if latent:
            kv_ref[:, KV_KB:KV_KB + BRANCH_DIM] = kb.astype(kv_ref.dtype)
            kv_ref[:, KV_VB:KV_VB + BRANCH_DIM] = vb.astype(kv_ref.dtype)
        else:
            kb_ref[...] = kb.reshape(kb_ref.shape)
            vb_ref[...] = vb.reshape(vb_ref.shape)
        _fourier_mix(uc.astype(BF16), bc_ref, bs_ref, pos_ref, oc_ref)


def _proj(x, mods, l, g_pre, w_in, *, m, x_tile0, tm, mod_row, seq, oc_rows=None, oc_all=None,
          rope_tabs=None, caches=None, cast_weights=()):
    assert tm % seq == 0
    latent = rope_tabs is not None
    bc, bs, pos = _dft_tables(seq)
    pair = 2 * C_GROUP_DIM
    in_specs = [
        pl.BlockSpec((tm, D_MODEL), lambda i, j: (i + x_tile0, 0)),
        pl.BlockSpec((None, None, 6, D_MODEL), lambda i, j: (l, mod_row(i), 0, 0)),
        pl.BlockSpec((None, None, 1, D_MODEL), lambda i, j: (l, 0, 0, 0)),
        _resident((None, D_MODEL, W_GATES), lambda i, j: (l, 0, 0)),
        _resident((pair, pair), lambda i, j: (0, 0)),
        _resident((pair, pair), lambda i, j: (0, 0)),
        _resident((seq, 2 * seq), lambda i, j: (0, 0)),
    ]
    args = [x, mods, g_pre, w_in, bc, bs, pos]
    aliases = {}
    if latent:
        seq_tiles = seq // tm
        tab_spec = pl.BlockSpec((tm, LANES), lambda i, j: (i % seq_tiles, 0))
        in_specs += [tab_spec, tab_spec, pl.BlockSpec(memory_space=pl.ANY)]
        args += list(rope_tabs)
        widths = (2 * BRANCH_DIM + 2 * KV_A, 2 * BRANCH_DIM)
        aliases = {len(args): len(widths)}
        args.append(oc_all)
        oc_rows, oc_tile0 = oc_all.shape[0], (oc_all.shape[0] - m) // tm
        cache_specs, cache_shapes = [], []
    else:
        widths = (2 * BRANCH_DIM,)
        oc_tile0 = 0
        cache_widths = (KV_A, KV_A, BRANCH_DIM, BRANCH_DIM)
        if caches is not None:
            in_specs += [pl.BlockSpec(memory_space=pl.ANY)] * len(caches)
            aliases = {len(args) + k: len(widths) + 1 + k for k in range(len(caches))}
            args += list(caches)
        cache_specs = [pl.BlockSpec((tm // seq, None, seq, w), lambda i, j: (i, l, 0, 0))
                       for w in cache_widths]
        cache_shapes = [jax.ShapeDtypeStruct((m // seq, DEPTH, seq, w), F32) for w in cache_widths]
        steps = 2 * (m // tm)
        for w in cast_weights:
            _, rows, cols = w.shape
            assert rows % (steps * BF16_SUBLANES) == 0
            in_specs.append(pl.BlockSpec((None, rows // steps, cols), lambda i, j: (l, 2 * i + j, 0)))
            cache_specs.append(pl.BlockSpec((rows // steps, cols), lambda i, j: (2 * i + j, 0)))
            cache_shapes.append(jax.ShapeDtypeStruct((rows, cols), BF16))
            args.append(w)
    return pl.pallas_call(
        functools.partial(_proj_kernel, latent=latent, n_alias=len(aliases), n_cast=len(cast_weights)),
        grid=(m // tm, 2),
        in_specs=in_specs,
        out_specs=([pl.BlockSpec((tm, w), lambda i, j: (i, 0)) for w in widths]
                   + [pl.BlockSpec((tm, BRANCH_DIM), lambda i, j: (i + oc_tile0, 0))] + cache_specs),
        out_shape=([jax.ShapeDtypeStruct((m, w), BF16) for w in widths]
                   + [jax.ShapeDtypeStruct((oc_rows, BRANCH_DIM), BF16)] + cache_shapes),
        input_output_aliases=aliases,
        scratch_shapes=[pltpu.VMEM((tm, D_MODEL), BF16)],
        compiler_params=_cparams(("parallel", "arbitrary")),
        name="proj_lat" if latent else "proj_ctx",
    )(*args)


def _with_ones(v):
    return jnp.concatenate([v, jnp.ones_like(v)], axis=1)


def _softmax_pv(parts, sink=None):
    m = parts[0][0].max(axis=-1, keepdims=True)
    for s, _ in parts[1:]:
        m = jnp.maximum(m, s.max(axis=-1, keepdims=True))
    if sink is not None:
        m = jnp.maximum(m, sink)
    ones_half = parts[0][1].shape[1] == 2 * LANES
    acc = None
    den = None
    for s, v in parts:
        e = jnp.exp(s - m)
        if not ones_half:
            d = e.sum(axis=-1, keepdims=True)
            den = d if den is None else den + d
        o = _dot(e.astype(BF16), v)
        acc = o if acc is None else acc + o
    if ones_half:
        acc, den = acc[:, :LANES], acc[:, LANES:]
    if sink is not None:
        den = den + jnp.exp(sink - m)
    return acc * (1.0 / den)


def _kv_head_variants(x2):
    low = _low_lanes(x2.shape)
    xr = pltpu.roll(x2, HEAD_DIM, 1)
    zero = jnp.zeros_like(x2)
    return [
        [jnp.where(low, x2, zero).astype(BF16), jnp.where(low, zero, xr).astype(BF16)],
        [jnp.where(low, xr, zero).astype(BF16), jnp.where(low, zero, x2).astype(BF16)],
    ]


def _attn_ctx_kernel(sink_ref, qa_ref, qb_ref, ka_ref, va_ref, kb_ref, vb_ref, *rest, n_weights):
    w_src, (oa_ref, ob_ref), w_dst = rest[:n_weights], rest[n_weights:n_weights + 2], rest[n_weights + 2:]
    for src, dst in zip(w_src, w_dst):
        dst[...] = src[...].astype(dst.dtype)
    ka = [jnp.concatenate(v, axis=0) for v in _kv_head_variants(ka_ref[...])]
    va = [jnp.concatenate(v, axis=0) for v in _kv_head_variants(va_ref[...])]
    chunks = [slice(j * LANES, (j + 1) * LANES) for j in range(A_Q_HEADS // 2)]
    work = []
    for j, cols in enumerate(chunks):
        g = (2 * j) // A_GROUP
        s2 = _dot_t(_scaled(qa_ref[:, cols]), ka[g])
        work.append((oa_ref, cols, s2, va[g], (sink_ref[2 * j], sink_ref[2 * j + 1])))
    low = _low_lanes((kb_ref.shape[0], LANES))
    for cols in chunks:
        kc, vc = kb_ref[:, cols], vb_ref[:, cols]
        zero = jnp.zeros_like(kc)
        k2 = jnp.concatenate([jnp.where(low, kc, zero), jnp.where(low, zero, kc)], axis=0).astype(BF16)
        v2 = jnp.concatenate([jnp.where(low, vc, zero), jnp.where(low, zero, vc)], axis=0).astype(BF16)
        work.append((ob_ref, cols, _dot_t(_scaled(qb_ref[:, cols]), k2), v2, None))
    for o_ref, cols, s2, v2, sinks in work:
        o_ref[:, cols] = _pair_softmax_pv(s2, v2, sinks).astype(o_ref.dtype)


def _pair_softmax_pv(s2, v2, sinks=None):
    tk = s2.shape[1] // 2
    es, extra = [], []
    for p in range(2):
        s = s2[:, p * tk:(p + 1) * tk]
        m = s.max(axis=-1, keepdims=True)
        if sinks is not None:
            m = jnp.maximum(m, sinks[p])
            extra.append(jnp.exp(sinks[p] - m))
        es.append(jnp.exp(s - m).astype(BF16))
    lane_half = lax.broadcasted_iota(jnp.int32, v2.shape, 1) // HEAD_DIM
    row_half = lax.broadcasted_iota(jnp.int32, v2.shape, 0) // tk
    ones2 = jnp.where(lane_half == row_half, 1.0, 0.0).astype(v2.dtype)
    acc = _dot(jnp.concatenate(es, axis=1), jnp.concatenate([v2, ones2], axis=1))
    den = acc[:, LANES:]
    if sinks is not None:
        den = den + jnp.where(_low_lanes(den.shape), extra[0], extra[1])
    return acc[:, :LANES] * (1.0 / den)


def _attn_ctx(qx, caches, sink, l, seq, weights, out_rows):
    m = qx.shape[0]
    steps = m // seq
    out = jax.ShapeDtypeStruct((out_rows, BRANCH_DIM), BF16)
    cache_specs = [pl.BlockSpec((None, None, seq, c.shape[-1]), lambda b: (b, l, 0, 0)) for c in caches]
    w_in_specs, w_out_specs, w_shapes = [], [], []
    for w in weights:
        _, rows, cols = w.shape
        assert rows % (steps * BF16_SUBLANES) == 0
        w_in_specs.append(pl.BlockSpec((None, rows // steps, cols), lambda b: (l, b, 0)))
        w_out_specs.append(pl.BlockSpec((rows // steps, cols), lambda b: (b, 0)))
        w_shapes.append(jax.ShapeDtypeStruct((rows, cols), BF16))
    return pl.pallas_call(
        functools.partial(_attn_ctx_kernel, n_weights=len(weights)),
        grid=(steps,),
        in_specs=[
            pl.BlockSpec(memory_space=pltpu.SMEM),
            pl.BlockSpec((seq, BRANCH_DIM), lambda b: (b, QX_QA // BRANCH_DIM)),
            pl.BlockSpec((seq, BRANCH_DIM), lambda b: (b, QX_QB // BRANCH_DIM)),
        ] + cache_specs + w_in_specs,
        out_specs=[pl.BlockSpec((seq, BRANCH_DIM), lambda b: (b, 0))] * 2 + w_out_specs,
        out_shape=[out, out] + w_shapes,
        compiler_params=_cparams(("parallel",)),
        name="attn_ctx",
    )(sink, qx, qx, *caches, *weights)


def _attn_a_lat_kernel(sink_ref, q_ref, kv_ref, ck_ref, cv_ref, _, o_ref,
                       qh_scr, kp_scr, vp_scr, ckb_scr, cvb_scr, mask_scr, *, n_blocks):
    seq = n_blocks * A_BLOCK
    band = 3 * A_BLOCK
    low_seq = _low_lanes((seq, LANES))
    for h in range(A_Q_HEADS):
        j, p, g = h // 2, h % 2, h // A_GROUP
        x = q_ref[:, j * LANES:(j + 1) * LANES].astype(F32) * SCALE
        if p != g:
            x = pltpu.roll(x, HEAD_DIM, 1)
        qh_scr[h] = jnp.where(low_seq if g == 0 else jnp.logical_not(low_seq), x, 0.0).astype(BF16)
    pad = jnp.zeros((A_BLOCK, LANES), BF16)
    for scr, col in ((kp_scr, 0), (vp_scr, KV_A)):
        scr[0:A_BLOCK, :] = pad
        scr[A_BLOCK:A_BLOCK + seq, :] = kv_ref[:, col:col + KV_A]
        scr[A_BLOCK + seq:2 * A_BLOCK + seq, :] = pad
    ckb_scr[...] = ck_ref[...].astype(BF16)
    cvb_scr[...] = cv_ref[...].astype(BF16)

    rows = A_GROUP * A_BLOCK
    row = lax.broadcasted_iota(jnp.int32, (rows, band), 0)
    qi = row % A_BLOCK
    kj = lax.broadcasted_iota(jnp.int32, (rows, band), 1)
    head_row = lax.broadcasted_iota(jnp.int32, (rows, 1), 0) // A_BLOCK
    low_blk = _low_lanes((A_BLOCK, LANES))
    @pl.when(pl.program_id(0) == 0)
    def _():
        for case, (no_prev, no_next) in enumerate(((1, 0), (0, 0), (0, 1))):
            valid = (kj >= qi + no_prev * (A_BLOCK - qi)) & (kj <= 2 * A_BLOCK + qi - no_next * (qi + 1))
            mask_scr[case] = jnp.where(valid, 0.0, NEG_INF)

    def body(n, carry):
        start = pl.multiple_of(n * A_BLOCK, A_BLOCK)
        case = jnp.where(n > 0, 1, 0) + jnp.where(n < n_blocks - 1, 0, 1)
        scores = []
        for g in range(A_KV_HEADS):
            q = jnp.concatenate(
                [qh_scr[A_GROUP * g + i, pl.ds(start, A_BLOCK), :] for i in range(A_GROUP)], axis=0)
            s_band = _dot_t(q, kp_scr[pl.ds(start, band), :]) + mask_scr[case]
            scores.append((s_band, _dot_t(q, ckb_scr[...])))
        outs = []
        for g, (s_band, s_ctx) in enumerate(scores):
            sink = jnp.full((rows, 1), sink_ref[A_GROUP * g], F32)
            for i in range(1, A_GROUP):
                sink = jnp.where(head_row == i, sink_ref[A_GROUP * g + i], sink)
            outs.append(_softmax_pv([(s_band, vp_scr[pl.ds(start, band), :]), (s_ctx, cvb_scr[...])],
                                    sink=sink))
        for j in range(A_Q_HEADS // 2):
            halves = []
            for p in range(2):
                h = 2 * j + p
                g, i = h // A_GROUP, h % A_GROUP
                t = outs[g][i * A_BLOCK:(i + 1) * A_BLOCK]
                halves.append(t if p == g else pltpu.roll(t, HEAD_DIM, 1))
            o_ref[pl.ds(start, A_BLOCK), j * LANES:(j + 1) * LANES] = (
                jnp.where(low_blk, halves[0], halves[1]).astype(o_ref.dtype))
        return carry

    lax.fori_loop(0, n_blocks, body, 0, unroll=2)


def _attn_a_lat(main, kv, cache_k, cache_v, sink, l, seq, oa_all):
    m = main.shape[0]
    nb = seq // A_BLOCK
    assert nb >= 2
    n_seq = m // seq
    past = cache_k.shape[2]
    first = (oa_all.shape[0] - m) // seq
    cache_spec = pl.BlockSpec((None, None, past, KV_A), lambda b: (b, l, 0, 0))
    return pl.pallas_call(
        functools.partial(_attn_a_lat_kernel, n_blocks=nb),
        grid=(n_seq,),
        in_specs=[
            pl.BlockSpec(memory_space=pltpu.SMEM),
            pl.BlockSpec((seq, BRANCH_DIM), lambda b: (b, QX_QA // BRANCH_DIM)),
            pl.BlockSpec((seq, 2 * KV_A), lambda b: (b, QX_KVA // (2 * KV_A))),
            cache_spec, cache_spec,
            pl.BlockSpec(memory_space=pl.ANY),
        ],
        out_specs=pl.BlockSpec((seq, BRANCH_DIM), lambda b: (b + first, 0)),
        out_shape=jax.ShapeDtypeStruct(oa_all.shape, BF16),
        input_output_aliases={5: 0},
        scratch_shapes=[
            pltpu.VMEM((A_Q_HEADS, seq, LANES), BF16),
            pltpu.VMEM((seq + 2 * A_BLOCK, LANES), BF16),
            pltpu.VMEM((seq + 2 * A_BLOCK, LANES), BF16),
            pltpu.VMEM((past, LANES), BF16),
            pltpu.VMEM((past, LANES), BF16),
            pltpu.VMEM((3, A_GROUP * A_BLOCK, 3 * A_BLOCK), F32),
        ],
        compiler_params=_cparams(("arbitrary",)),
        name="attn_a_lat",
    )(sink, main, kv, cache_k, cache_v, oa_all)


def _attn_b_lat_kernel(q_ref, k_ref, v_ref, ck_ref, cv_ref, rpb_ref, _, o_ref, ckb_scr, vx_scr, cvx_scr,
                       bias_ref, *, rows, kr):
    @pl.when(pl.program_id(0) == 0)
    def _():
        _fill_neighbourhood_bias(rpb_ref, bias_ref, kr=kr)

    half = kr // 2
    n_loc = kr * GRID_W
    low_q = _low_lanes((GRID_W, LANES))
    chunks = [slice(j * LANES, (j + 1) * LANES) for j in range(B_HEADS // 2)]
    ckb_scr[...] = ck_ref[...].astype(BF16)
    for j, cols in enumerate(chunks):
        vx_scr[j] = _with_ones(v_ref[:, cols])
        cvx_scr[j] = _with_ones(cv_ref[:, cols].astype(BF16))
    rows_per_iter = 4
    assert rows % rows_per_iter == 0

    def body(it, carry):
        work = []
        for u in range(rows_per_iter):
            r = it * rows_per_iter + u
            r0 = jnp.clip(r - half, 0, rows - kr)
            q0 = pl.multiple_of(r * GRID_W, GRID_W)
            start = pl.multiple_of(r0 * GRID_W, GRID_W)
            for j, cols in enumerate(chunks):
                qc = _scaled(q_ref[pl.ds(q0, GRID_W), cols])
                zero = jnp.zeros_like(qc)
                qs = jnp.concatenate([jnp.where(low_q, qc, zero), jnp.where(low_q, zero, qc)], axis=0)
                s_loc = _dot_t(qs, k_ref[pl.ds(start, n_loc), cols]) + bias_ref[r - r0, j]
                work.append((q0, start, j, cols, s_loc, _dot_t(qs, ckb_scr[:, cols])))
        for q0, start, j, cols, s_loc, s_ctx in work:
            o2 = _softmax_pv([(s_loc, vx_scr[j, pl.ds(start, n_loc), :]), (s_ctx, cvx_scr[j])])
            o = jnp.where(low_q, o2[:GRID_W], o2[GRID_W:])
            o_ref[pl.ds(q0, GRID_W), cols] = o.astype(o_ref.dtype)
        return carry

    lax.fori_loop(0, rows // rows_per_iter, body, 0)


def _attn_b_lat(main, kv, cache_k, cache_v, rpb_pad, l, seq, ob_all):
    m = main.shape[0]
    rows = seq // GRID_W
    kr = min(B_WIN_ROWS, rows)
    assert kr % 2 == 0 and 2 * GRID_W == LANES
    n_seq = m // seq
    past = cache_k.shape[2]
    first = (ob_all.shape[0] - m) // seq
    cache_spec = pl.BlockSpec((None, None, past, BRANCH_DIM), lambda b: (b, l, 0, 0))
    return pl.pallas_call(
        functools.partial(_attn_b_lat_kernel, rows=rows, kr=kr),
        grid=(n_seq,),
        in_specs=[
            pl.BlockSpec((seq, BRANCH_DIM), lambda b: (b, QX_QB // BRANCH_DIM)),
            pl.BlockSpec((seq, BRANCH_DIM), lambda b: (b, KV_KB // BRANCH_DIM)),
            pl.BlockSpec((seq, BRANCH_DIM), lambda b: (b, KV_VB // BRANCH_DIM)),
            cache_spec, cache_spec,
            _resident((None,) + rpb_pad.shape[1:], lambda b: (l, 0, 0, 0)),
            pl.BlockSpec(memory_space=pl.ANY),
        ],
        out_specs=pl.BlockSpec((seq, BRANCH_DIM), lambda b: (b + first, 0)),
        out_shape=jax.ShapeDtypeStruct(ob_all.shape, BF16),
        input_output_aliases={6: 0},
        scratch_shapes=[
            pltpu.VMEM((past, BRANCH_DIM), BF16),
            pltpu.VMEM((B_HEADS // 2, seq, 2 * LANES), BF16),
            pltpu.VMEM((B_HEADS // 2, past, 2 * LANES), BF16),
            pltpu.VMEM((kr, B_HEADS // 2, 2 * GRID_W, kr * GRID_W), F32),
        ],
        compiler_params=_cparams(("arbitrary",)),
        name="attn_b_lat",
    )(main, kv, kv, cache_k, cache_v, rpb_pad, ob_all)


def _fill_neighbourhood_bias(rpb_ref, bias_scr, *, kr):
    shape = (GRID_W, LANES)
    c = lax.broadcasted_iota(jnp.int32, shape, 0)
    lane = lax.broadcasted_iota(jnp.int32, shape, 1)
    c2 = lane % GRID_W
    ws = jnp.clip(c - B_WIN_COLS // 2, 0, GRID_W - B_WIN_COLS)
    ok = (c2 >= ws) & (c2 < ws + B_WIN_COLS)
    low = lane < GRID_W

    def toeplitz(h, dr, lane0):
        row = jnp.broadcast_to(rpb_ref[h, dr:dr + 1, :], shape)
        return pltpu.roll(row, (lane0 - (B_WIN_COLS - 1)) % LANES, 1, stride=1, stride_axis=0)

    for h in range(B_HEADS):
        rows = slice(h % 2 * GRID_W, (h % 2 + 1) * GRID_W)
        pairs = {}
        for v in range(kr):
            for m in range(0, kr, 2):
                dr = m - v + B_WIN_ROWS - 1
                if dr not in pairs:
                    pair = jnp.where(low, toeplitz(h, dr, 0), toeplitz(h, dr + 1, GRID_W))
                    pairs[dr] = jnp.where(ok, pair, NEG_INF)
                bias_scr[v, h // 2, rows, m * GRID_W:(m + 2) * GRID_W] = pairs[dr]


def _dft_tables(seq):
    cd = C_GROUP_DIM
    kc = (np.arange(cd)[:, None] * np.arange(cd)[None, :]) % cd
    ang_c = 2.0 * np.pi * kc / cd
    eye2 = np.eye(2)
    bd_cos = np.kron(eye2, np.cos(ang_c))
    bd_sin = np.kron(eye2, np.sin(ang_c))
    kt = (np.arange(seq)[:, None] * np.arange(seq)[None, :]) % seq
    ang_t = 2.0 * np.pi * kt / seq
    norm = 1.0 / np.sqrt(float(seq * cd))
    pos = np.concatenate([np.cos(ang_t), -np.sin(ang_t)], axis=1) * norm
    return (jnp.asarray(bd_cos, F32).astype(BF16), jnp.asarray(bd_sin, F32).astype(BF16),
            jnp.asarray(pos, F32).astype(BF16))


def _fourier_mix(u, bc_ref, bs_ref, pos_ref, o_ref):
    pair = 2 * C_GROUP_DIM
    seq = pos_ref.shape[0]
    uc, us = [], []
    for p in range(BRANCH_DIM // pair):
        up = u[:, p * pair:(p + 1) * pair]
        uc.append(_dot(up, bc_ref[...]))
        us.append(_dot(up, bs_ref[...]))
    zc = jnp.concatenate(uc, axis=1).astype(BF16)
    zs = jnp.concatenate(us, axis=1).astype(BF16)
    for s in range(u.shape[0] // seq):
        rows = slice(s * seq, (s + 1) * seq)
        z = jnp.concatenate([zc[rows], zs[rows]], axis=0)
        o_ref[rows, :] = _dot(pos_ref[...], z).astype(o_ref.dtype)


def _ffn_weight_copies(wfi_hbm, wfo_hbm, wfg_ref, wfu_ref, wfo_ref, sems):
    srcs = (wfi_hbm.at[:, 0:D_FF], wfi_hbm.at[:, D_FF:2 * D_FF], wfo_hbm)
    dsts = (wfg_ref, wfu_ref, wfo_ref)
    return [pltpu.make_async_copy(src, dst, sems.at[k]) for k, (src, dst) in enumerate(zip(srcs, dsts))]


def _tail_kernel(*refs, n_x, n_out, n_ctx_tiles):
    oa_ref, ob_ref, oc_ref = refs[:3]
    x_refs, refs = refs[3:3 + n_x], refs[3 + n_x:]
    (mod_ref, gpre0_ref, gpost0_ref, gpre1_ref, gpost1_ref,
     wg_ref, wb_ref, wo_ref, wfi_hbm, wfo_hbm) = refs[:10]
    o_refs, refs = refs[10:10 + n_out], refs[10 + n_out:]
    wfg_ref, wfu_ref, wfo_ref, sems = refs[:4]
    i = pl.program_id(0)
    first = i == 0
    is_ctx = i < n_ctx_tiles
    copies = _ffn_weight_copies(wfi_hbm, wfo_hbm, wfg_ref, wfu_ref, wfo_ref, sems)

    @pl.when(first)
    def _():
        for copy in copies:
            copy.start()

    if n_x == 1:
        x = x_refs[0][...]
    else:
        pick_ctx = jnp.full(x_refs[0].shape, jnp.where(is_ctx, 1, 0), jnp.int32) > 0
        x = jnp.where(pick_ctx, x_refs[0][...], x_refs[1][...])
    h = (_rms(x, gpre0_ref[...]) * (1.0 + mod_ref[1:2, :]) + mod_ref[0:1, :]).astype(BF16)
    gates = _dot(h, wg_ref[...])
    branches = [_dot(b_ref[...], wb_ref[k]) for k, b_ref in enumerate((oa_ref, ob_ref, oc_ref))]
    mix = None
    for k, branch in enumerate(branches):
        term = _sigmoid(gates[:, k * D_MODEL:(k + 1) * D_MODEL]) * branch
        mix = term if mix is None else mix + term
    y = _dot(mix.astype(BF16), wo_ref[...])
    x1 = x + mod_ref[2:3, :] * _rms(y, gpost0_ref[...])
    h2 = (_rms(x1, gpre1_ref[...]) * (1.0 + mod_ref[4:5, :]) + mod_ref[3:4, :]).astype(BF16)

    @pl.when(first)
    def _():
        for copy in copies:
            copy.wait()

    act = _silu(_dot(h2, wfg_ref[...])) * _dot(h2, wfu_ref[...])
    f = _dot(act.astype(BF16), wfo_ref[...])
    out = x1 + mod_ref[5:6, :] * _rms(f, gpost1_ref[...])
    if n_out == 1:
        o_refs[0][...] = out
    else:
        @pl.when(is_ctx)
        def _():
            o_refs[0][...] = out

        @pl.when(jnp.logical_not(is_ctx))
        def _():
            o_refs[1][...] = out


def _tail(oa, ob, oc, xs, mods, l, g_pre, g_post, w_gates, w_branch, w_out, w_ffn_in, w_ffn_out,
          *, tm, m_ctx, mod_row, split_out):
    m = oa.shape[0]
    n_ctx = m_ctx // tm
    o_spec = pl.BlockSpec((tm, BRANCH_DIM), lambda i: (i, 0))
    whole = pl.BlockSpec((tm, D_MODEL), lambda i: (i, 0))
    halves = [pl.BlockSpec((tm, D_MODEL), lambda i: (jnp.minimum(i, n_ctx - 1), 0)),
              pl.BlockSpec((tm, D_MODEL), lambda i: (jnp.maximum(i - n_ctx, 0), 0))]
    half_shapes = [jax.ShapeDtypeStruct((m_ctx, D_MODEL), F32),
                   jax.ShapeDtypeStruct((m - m_ctx, D_MODEL), F32)]
    gain_spec = lambda which: pl.BlockSpec((None, None, 1, D_MODEL), lambda i: (l, which, 0, 0))
    scratch = [
        pltpu.VMEM((D_MODEL, D_FF), BF16),
        pltpu.VMEM((D_MODEL, D_FF), BF16),
        pltpu.VMEM((D_FF, D_MODEL), BF16),
        pltpu.SemaphoreType.DMA((3,)),
    ]
    return pl.pallas_call(
        functools.partial(_tail_kernel, n_x=len(xs), n_out=2 if split_out else 1, n_ctx_tiles=n_ctx),
        grid=(m // tm,),
        in_specs=[o_spec, o_spec, o_spec] + (halves if len(xs) == 2 else [whole]) + [
            pl.BlockSpec((None, None, 6, D_MODEL), lambda i: (l, mod_row(i), 0, 0)),
            gain_spec(0), gain_spec(0), gain_spec(1), gain_spec(1),
            _resident((None, D_MODEL, N_GATES), lambda i: (l, 0, 0)),
            _resident((N_BRANCH, BRANCH_DIM, D_MODEL), lambda i: (0, 0, 0)),
            _resident((D_MODEL, D_MODEL), lambda i: (0, 0)),
            pl.BlockSpec(memory_space=pl.ANY),
            pl.BlockSpec(memory_space=pl.ANY),
        ],
        out_specs=halves if split_out else whole,
        out_shape=half_shapes if split_out else jax.ShapeDtypeStruct((m, D_MODEL), F32),
        scratch_shapes=scratch,
        compiler_params=_cparams(("arbitrary",)),
        name="tail",
    )(oa, ob, oc, *xs, mods, g_pre, g_post, g_pre, g_post, w_gates, w_branch, w_out,
      w_ffn_in, w_ffn_out)


def _rope_tables(seq):
    t = jnp.arange(seq, dtype=jnp.int32)
    row = (t // GRID_W).astype(F32)
    col = (t % GRID_W).astype(F32)
    n_pairs_axis = HEAD_DIM // 4
    inv = ROPE_BASE ** (-jnp.arange(n_pairs_axis, dtype=F32) / n_pairs_axis)
    ang = jnp.concatenate([row[:, None] * inv, col[:, None] * inv], axis=-1)
    cos, sin = jnp.cos(ang), jnp.sin(ang)
    cos_l = jnp.tile(cos, (1, LANES // cos.shape[1]))
    sin_l = jnp.tile(jnp.concatenate([-sin, sin], axis=-1), (1, LANES // HEAD_DIM))
    return cos_l, sin_l


def kernel(x_prompt, x_sample, cache_a_k, cache_a_v, cache_b_k, cache_b_v, c, c_ctx, w_ada, b_ada,
           norm_pre, norm_post, w_in, a_sink, b_rpb, w_branch, w_out, w_ffn_in, w_ffn_out):
    batch, seq, _ = x_prompt.shape
    dec_batch, dec_seq, _ = x_sample.shape
    past = cache_a_k.shape[2]
    assert dec_batch <= CTX_MOD_ROW and seq % A_BLOCK == 0 and dec_seq % A_BLOCK == 0

    cvec = jnp.concatenate(
        [c, c_ctx[None, :], jnp.zeros((MOD_ROWS - dec_batch - 1, D_MODEL), F32)], axis=0)
    assert w_in.shape[-1] == D_IN
    mods, w_proj_b, w_gates_b = _modulation(cvec, w_ada, b_ada, w_in)
    mods = mods.reshape(DEPTH, MOD_ROWS, 6, D_MODEL)
    w_branch_2d = w_branch.reshape(DEPTH, N_BRANCH * BRANCH_DIM, D_MODEL)
    g_pre = norm_pre.reshape(DEPTH, 2, 1, D_MODEL)
    g_post = norm_post.reshape(DEPTH, 2, 1, D_MODEL)
    rope_tabs = _rope_tables(dec_seq)
    cak = cache_a_k.reshape(dec_batch, DEPTH, past, KV_A)
    cav = cache_a_v.reshape(dec_batch, DEPTH, past, KV_A)
    cbk = cache_b_k.reshape(dec_batch, DEPTH, past, BRANCH_DIM)
    cbv = cache_b_v.reshape(dec_batch, DEPTH, past, BRANCH_DIM)

    n_dr, n_dc = b_rpb.shape[2:]
    rpb_pad = jnp.pad(b_rpb.astype(F32), ((0, 0), (0, 0), (0, (-n_dr) % 8), (0, LANES - n_dc)))

    tm_proj, tm_tail = 1024, 512
    m_ctx, m_lat = batch * seq, dec_batch * dec_seq
    m_all = m_ctx + m_lat
    assert m_ctx % tm_proj == 0 and m_lat % tm_proj == 0
    ctx_row = lambda i: CTX_MOD_ROW
    lat_row = lambda i: (i * tm_proj) // dec_seq
    tail_row = lambda i: jnp.where(i < m_ctx // tm_tail, CTX_MOD_ROW, (i * tm_tail - m_ctx) // dec_seq)

    xs = (x_prompt.reshape(m_ctx, D_MODEL), x_sample.reshape(m_lat, D_MODEL))
    caches = None
    for l in range(DEPTH):
        x_ctx, x_lat = (xs[0], xs[1]) if len(xs) == 2 else (xs[0], xs[0])
        lat_tile0 = 0 if len(xs) == 2 else m_ctx // tm_proj
        qx, oc, *rest = _proj(x_ctx, mods, l, g_pre, w_proj_b, m=m_ctx, x_tile0=0, tm=tm_proj,
                              mod_row=ctx_row, seq=seq, oc_rows=m_all, caches=caches,
                              cast_weights=(w_ffn_in,))
        caches, (w_ffn_in_b,) = rest[:4], rest[4:]
        oa, ob, w_branch_b, w_out_b, w_ffn_out_b = _attn_ctx(
            qx, caches, a_sink[l], l, seq, (w_branch_2d, w_out, w_ffn_out), m_all)
        qx, kv, oc = _proj(x_lat, mods, l, g_pre, w_proj_b, m=m_lat, x_tile0=lat_tile0, tm=tm_proj,
                           mod_row=lat_row, seq=dec_seq, oc_all=oc, rope_tabs=rope_tabs)
        oa = _attn_a_lat(qx, qx, cak, cav, a_sink[l], l, dec_seq, oa)
        ob = _attn_b_lat(qx, kv, cbk, cbv, rpb_pad, l, dec_seq, ob)
        last = l == DEPTH - 1
        out = _tail(oa, ob, oc, xs, mods, l, g_pre, g_post, w_gates_b,
                    w_branch_b.reshape(N_BRANCH, BRANCH_DIM, D_MODEL), w_out_b, w_ffn_in_b, w_ffn_out_b,
                    tm=tm_tail, m_ctx=m_ctx, mod_row=tail_row, split_out=last)
        xs = tuple(out) if last else (out,)
    y_prompt = xs[0].reshape(batch, seq, D_MODEL)
    y_sample = xs[1].reshape(dec_batch, dec_seq, D_MODEL)
    nak, nav, nbk, nbv = caches
    new_a_k = nak.reshape(batch, DEPTH, seq, A_KV_HEADS, HEAD_DIM)
    new_a_v = nav.reshape(batch, DEPTH, seq, A_KV_HEADS, HEAD_DIM)
    new_b_k = nbk.reshape(batch, DEPTH, seq, B_HEADS, HEAD_DIM)
    new_b_v = nbv.reshape(batch, DEPTH, seq, B_HEADS, HEAD_DIM)
    return (y_prompt, y_sample, new_a_k, new_a_v, new_b_k, new_b_v)
```

```python
import functools

import numpy as np
import jax
import jax.numpy as jnp
from jax import lax
from jax.experimental import pallas as pl
from jax.experimental.pallas import tpu as pltpu

F32 = jnp.float32
BF16 = jnp.bfloat16

D_MODEL = 1024
DEPTH = 2
GRID_W = 64
HEAD_DIM = 64
BRANCH_DIM = D_MODEL // 2
A_Q_HEADS = BRANCH_DIM // HEAD_DIM
A_KV_HEADS = A_Q_HEADS // 4
A_GROUP = A_Q_HEADS // A_KV_HEADS
A_BLOCK = 128
B_HEADS = BRANCH_DIM // HEAD_DIM
B_WIN_ROWS = 8
B_WIN_COLS = 16
C_GROUPS = 4
C_GROUP_DIM = BRANCH_DIM // C_GROUPS
N_BRANCH = 3
D_FF = -(-8 * D_MODEL // (3 * 256)) * 256
ROPE_BASE = 10000.0
NORM_EPS = 1e-6
NEG_INF = -1e30
SCALE = HEAD_DIM ** -0.5

LANES = 128
KV_A = A_KV_HEADS * HEAD_DIM
N_GATES = N_BRANCH * D_MODEL
W_QA = 0
W_KA = W_QA + BRANCH_DIM
W_VA = W_KA + KV_A
W_QB = W_VA + KV_A
W_KB = W_QB + BRANCH_DIM
W_VB = W_KB + BRANCH_DIM
W_UC = W_VB + BRANCH_DIM
W_GATES = W_UC + BRANCH_DIM
D_IN = W_GATES + N_GATES
QX_QA, QX_QB, QX_KVA = 0, BRANCH_DIM, 2 * BRANCH_DIM
KV_KB, KV_VB = 0, BRANCH_DIM
MOD_ROWS = 8
CTX_MOD_ROW = 4
VMEM_LIMIT = 56 * 1024 * 1024
BF16_SUBLANES = 16


def _cparams(sem):
    return pltpu.CompilerParams(dimension_semantics=sem, vmem_limit_bytes=VMEM_LIMIT)


def _resident(shape, index_map):
    return pl.BlockSpec(shape, index_map, pipeline_mode=pl.Buffered(1))


def _rms(x, g):
    return x * lax.rsqrt(jnp.mean(x * x, axis=-1, keepdims=True) + NORM_EPS) * g


def _sigmoid(x):
    return 0.5 * jnp.tanh(0.5 * x) + 0.5


def _silu(x):
    half = 0.5 * x
    return half + half * jnp.tanh(half)


def _dot_t(a, b):
    return lax.dot_general(a, b, (((1,), (1,)), ((), ())), preferred_element_type=F32)


def _dot(a, b):
    return jnp.dot(a, b, preferred_element_type=F32)


def _scaled(q):
    assert np.log2(SCALE) == round(np.log2(SCALE))
    return q * jnp.asarray(SCALE, q.dtype)


def _low_lanes(shape):
    return lax.broadcasted_iota(jnp.int32, shape, len(shape) - 1) < HEAD_DIM


def _mod_kernel(cv_ref, w_ref, b_ref, win_ref, o_ref, wproj_ref, wgates_ref):
    a, b = _silu(cv_ref[...]), w_ref[...]
    a_hi, b_hi = a.astype(BF16), b.astype(BF16)
    a_lo = (a - a_hi.astype(F32)).astype(BF16)
    b_lo = (b - b_hi.astype(F32)).astype(BF16)
    o_ref[...] = _dot(a_hi, b_hi) + (_dot(a_lo, b_hi) + _dot(a_hi, b_lo)) + b_ref[...]
    wproj_ref[...] = win_ref[:, :W_GATES].astype(wproj_ref.dtype)
    wgates_ref[...] = win_ref[:, W_GATES:].astype(wgates_ref.dtype)


def _modulation(cvec, w_ada, b_ada, w_in):
    tn = 1536
    n = 6 * D_MODEL
    steps = n // tn
    rows = w_in.shape[1] // steps
    return pl.pallas_call(
        _mod_kernel,
        grid=(DEPTH, steps),
        in_specs=[
            pl.BlockSpec((MOD_ROWS, D_MODEL), lambda l, j: (0, 0)),
            pl.BlockSpec((None, D_MODEL, tn), lambda l, j: (l, 0, j)),
            pl.BlockSpec((None, 1, tn), lambda l, j: (l, 0, j)),
            pl.BlockSpec((None, rows, D_IN), lambda l, j: (l, j, 0)),
        ],
        out_specs=[pl.BlockSpec((None, MOD_ROWS, tn), lambda l, j: (l, 0, j)),
                   pl.BlockSpec((None, rows, W_GATES), lambda l, j: (l, j, 0)),
                   pl.BlockSpec((None, rows, N_GATES), lambda l, j: (l, j, 0))],
        out_shape=[jax.ShapeDtypeStruct((DEPTH, MOD_ROWS, n), F32),
                   jax.ShapeDtypeStruct((DEPTH, D_MODEL, W_GATES), BF16),
                   jax.ShapeDtypeStruct((DEPTH, D_MODEL, N_GATES), BF16)],
        compiler_params=_cparams(("parallel", "parallel")),
        name="modulation",
    )(cvec, w_ada, b_ada.reshape(DEPTH, 1, n), w_in)


def _rope_cols(v, cos, sin):
    first = (lax.broadcasted_iota(jnp.int32, (v.shape[0], LANES), 1) % HEAD_DIM) < HEAD_DIM // 2
    outs = []
    for c in range(v.shape[1] // LANES):
        vc = v[:, c * LANES:(c + 1) * LANES]
        partner = jnp.where(first, pltpu.roll(vc, LANES - HEAD_DIM // 2, 1),
                            pltpu.roll(vc, HEAD_DIM // 2, 1))
        outs.append(vc * cos + partner * sin)
    return outs[0] if len(outs) == 1 else jnp.concatenate(outs, axis=1)


def _with_rope(acc, lo, hi, cos_ref, sin_ref):
    parts = []
    if lo > 0:
        parts.append(acc[:, :lo])
    parts.append(_rope_cols(acc[:, lo:hi], cos_ref[...], sin_ref[...]))
    if hi < acc.shape[1]:
        parts.append(acc[:, hi:])
    return parts[0] if len(parts) == 1 else jnp.concatenate(parts, axis=1)


def _proj_kernel(*refs, latent, n_alias, n_cast):
    x_ref, mod_ref, g_ref, w_ref, bc_ref, bs_ref, pos_ref = refs[:7]
    if latent:
        cos_ref, sin_ref, _, qx_ref, kv_ref, oc_ref, h_scr = refs[7:]
    else:
        rest = refs[7 + n_alias:]
        cast_src, rest = rest[:n_cast], rest[n_cast:]
        qx_ref, oc_ref, ka_ref, va_ref, kb_ref, vb_ref = rest[:6]
        cast_dst, h_scr = rest[6:6 + n_cast], rest[6 + n_cast]
        for src, dst in zip(cast_src, cast_dst):
            dst[...] = src[...].astype(dst.dtype)
    j = pl.program_id(1)

    @pl.when(j == 0)
    def _():
        h = _rms(x_ref[...], g_ref[...]) * (1.0 + mod_ref[1:2, :]) + mod_ref[0:1, :]
        h_scr[...] = h.astype(BF16)
        acc = _dot(h_scr[...], w_ref[:, W_QA:W_KB])
        qa = acc[:, W_QA:W_QA + BRANCH_DIM]
        qb = acc[:, W_QB:W_QB + BRANCH_DIM]
        kva = acc[:, W_KA:W_KA + 2 * KV_A]
        if latent:
            qa = _rope_cols(qa, cos_ref[...], sin_ref[...])
            kva = _with_rope(kva, 0, KV_A, cos_ref, sin_ref)
            qx_ref[:, QX_KVA:QX_KVA + 2 * KV_A] = kva.astype(qx_ref.dtype)
        else:
            ka_ref[...] = kva[:, :KV_A].reshape(ka_ref.shape)
            va_ref[...] = kva[:, KV_A:].reshape(va_ref.shape)
        qx_ref[:, QX_QA:QX_QA + BRANCH_DIM] = qa.astype(qx_ref.dtype)
        qx_ref[:, QX_QB:QX_QB + BRANCH_DIM] = qb.astype(qx_ref.dtype)

    @pl.when(j == 1)
    def _():
        acc = _dot(h_scr[...], w_ref[:, W_KB:W_GATES])
        kb = acc[:, 0:BRANCH_DIM]
        vb = acc[:, W_VB - W_KB:W_VB - W_KB + BRANCH_DIM]
        uc = acc[:, W_UC - W_KB:W_UC - W_KB + BRANCH_DIM]
        if latent:
            kv_ref[:, KV_KB:KV_KB + BRANCH_DIM] = kb.astype(kv_ref.dtype)
            kv_ref[:, KV_VB:KV_VB + BRANCH_DIM] = vb.astype(kv_ref.dtype)
        else:
            kb_ref[...] = kb.reshape(kb_ref.shape)
            vb_ref[...] = vb.reshape(vb_ref.shape)
        _fourier_mix(uc.astype(BF16), bc_ref, bs_ref, pos_ref, oc_ref)


def _proj(x, mods, l, g_pre, w_in, *, m, x_tile0, tm, mod_row, seq, oc_rows=None, oc_all=None,
          rope_tabs=None, caches=None, cast_weights=()):
    assert tm % seq == 0
    latent = rope_tabs is not None
    bc, bs, pos = _dft_tables(seq)
    pair = 2 * C_GROUP_DIM
    in_specs = [
        pl.BlockSpec((tm, D_MODEL), lambda i, j: (i + x_tile0, 0)),
        pl.BlockSpec((None, None, 6, D_MODEL), lambda i, j: (l, mod_row(i), 0, 0)),
        pl.BlockSpec((None, None, 1, D_MODEL), lambda i, j: (l, 0, 0, 0)),
        _resident((None, D_MODEL, W_GATES), lambda i, j: (l, 0, 0)),
        _resident((pair, pair), lambda i, j: (0, 0)),
        _resident((pair, pair), lambda i, j: (0, 0)),
        _resident((seq, 2 * seq), lambda i, j: (0, 0)),
    ]
    args = [x, mods, g_pre, w_in, bc, bs, pos]
    aliases = {}
    if latent:
        seq_tiles = seq // tm
        tab_spec = pl.BlockSpec((tm, LANES), lambda i, j: (i % seq_tiles, 0))
        in_specs += [tab_spec, tab_spec, pl.BlockSpec(memory_space=pl.ANY)]
        args += list(rope_tabs)
        widths = (2 * BRANCH_DIM + 2 * KV_A, 2 * BRANCH_DIM)
        aliases = {len(args): len(widths)}
        args.append(oc_all)
        oc_rows, oc_tile0 = oc_all.shape[0], (oc_all.shape[0] - m) // tm
        cache_specs, cache_shapes = [], []
    else:
        widths = (2 * BRANCH_DIM,)
        oc_tile0 = 0
        cache_widths = (KV_A, KV_A, BRANCH_DIM, BRANCH_DIM)
        if caches is not None:
            in_specs += [pl.BlockSpec(memory_space=pl.ANY)] * len(caches)
            aliases = {len(args) + k: len(widths) + 1 + k for k in range(len(caches))}
            args += list(caches)
        cache_specs = [pl.BlockSpec((tm // seq, None, seq, w), lambda i, j: (i, l, 0, 0))
                       for w in cache_widths]
        cache_shapes = [jax.ShapeDtypeStruct((m // seq, DEPTH, seq, w), F32) for w in cache_widths]
        steps = 2 * (m // tm)
        for w in cast_weights:
            _, rows, cols = w.shape
            assert rows % (steps * BF16_SUBLANES) == 0
            in_specs.append(pl.BlockSpec((None, rows // steps, cols), lambda i, j: (l, 2 * i + j, 0)))
            cache_specs.append(pl.BlockSpec((rows // steps, cols), lambda i, j: (2 * i + j, 0)))
            cache_shapes.append(jax.ShapeDtypeStruct((rows, cols), BF16))
            args.append(w)
    return pl.pallas_call(
        functools.partial(_proj_kernel, latent=latent, n_alias=len(aliases), n_cast=len(cast_weights)),
        grid=(m // tm, 2),
        in_specs=in_specs,
        out_specs=([pl.BlockSpec((tm, w), lambda i, j: (i, 0)) for w in widths]
                   + [pl.BlockSpec((tm, BRANCH_DIM), lambda i, j: (i + oc_tile0, 0))] + cache_specs),
        out_shape=([jax.ShapeDtypeStruct((m, w), BF16) for w in widths]
                   + [jax.ShapeDtypeStruct((oc_rows, BRANCH_DIM), BF16)] + cache_shapes),
        input_output_aliases=aliases,
        scratch_shapes=[pltpu.VMEM((tm, D_MODEL), BF16)],
        compiler_params=_cparams(("parallel", "arbitrary")),
        name="proj_lat" if latent else "proj_ctx",
    )(*args)


def _with_ones(v):
    return jnp.concatenate([v, jnp.ones_like(v)], axis=1)


def _softmax_pv(parts, sink=None):
    m = parts[0][0].max(axis=-1, keepdims=True)
    for s, _ in parts[1:]:
        m = jnp.maximum(m, s.max(axis=-1, keepdims=True))
    if sink is not None:
        m = jnp.maximum(m, sink)
    ones_half = parts[0][1].shape[1] == 2 * LANES
    acc = None
    den = None
    for s, v in parts:
        e = jnp.exp(s - m)
        if not ones_half:
            d = e.sum(axis=-1, keepdims=True)
            den = d if den is None else den + d
        o = _dot(e.astype(BF16), v)
        acc = o if acc is None else acc + o
    if ones_half:
        acc, den = acc[:, :LANES], acc[:, LANES:]
    if sink is not None:
        den = den + jnp.exp(sink - m)
    return acc * (1.0 / den)


def _kv_head_variants(x2):
    low = _low_lanes(x2.shape)
    xr = pltpu.roll(x2, HEAD_DIM, 1)
    zero = jnp.zeros_like(x2)
    return [
        [jnp.where(low, x2, zero).astype(BF16), jnp.where(low, zero, xr).astype(BF16)],
        [jnp.where(low, xr, zero).astype(BF16), jnp.where(low, zero, x2).astype(BF16)],
    ]


def _attn_ctx_kernel(sink_ref, qa_ref, qb_ref, ka_ref, va_ref, kb_ref, vb_ref, *rest, n_weights):
    w_src, (oa_ref, ob_ref), w_dst = rest[:n_weights], rest[n_weights:n_weights + 2], rest[n_weights + 2:]
    for src, dst in zip(w_src, w_dst):
        dst[...] = src[...].astype(dst.dtype)
    ka = [jnp.concatenate(v, axis=0) for v in _kv_head_variants(ka_ref[...])]
    va = [jnp.concatenate(v, axis=0) for v in _kv_head_variants(va_ref[...])]
    chunks = [slice(j * LANES, (j + 1) * LANES) for j in range(A_Q_HEADS // 2)]
    work = []
    for j, cols in enumerate(chunks):
        g = (2 * j) // A_GROUP
        s2 = _dot_t(_scaled(qa_ref[:, cols]), ka[g])
        work.append((oa_ref, cols, s2, va[g], (sink_ref[2 * j], sink_ref[2 * j + 1])))
    low = _low_lanes((kb_ref.shape[0], LANES))
    for cols in chunks:
        kc, vc = kb_ref[:, cols], vb_ref[:, cols]
        zero = jnp.zeros_like(kc)
        k2 = jnp.concatenate([jnp.where(low, kc, zero), jnp.where(low, zero, kc)], axis=0).astype(BF16)
        v2 = jnp.concatenate([jnp.where(low, vc, zero), jnp.where(low, zero, vc)], axis=0).astype(BF16)
        work.append((ob_ref, cols, _dot_t(_scaled(qb_ref[:, cols]), k2), v2, None))
    for o_ref, cols, s2, v2, sinks in work:
        o_ref[:, cols] = _pair_softmax_pv(s2, v2, sinks).astype(o_ref.dtype)


def _pair_softmax_pv(s2, v2, sinks=None):
    tk = s2.shape[1] // 2
    es, extra = [], []
    for p in range(2):
        s = s2[:, p * tk:(p + 1) * tk]
        m = s.max(axis=-1, keepdims=True)
        if sinks is not None:
            m = jnp.maximum(m, sinks[p])
            extra.append(jnp.exp(sinks[p] - m))
        es.append(jnp.exp(s - m).astype(BF16))
    lane_half = lax.broadcasted_iota(jnp.int32, v2.shape, 1) // HEAD_DIM
    row_half = lax.broadcasted_iota(jnp.int32, v2.shape, 0) // tk
    ones2 = jnp.where(lane_half == row_half, 1.0, 0.0).astype(v2.dtype)
    acc = _dot(jnp.concatenate(es, axis=1), jnp.concatenate([v2, ones2], axis=1))
    den = acc[:, LANES:]
    if sinks is not None:
        den = den + jnp.where(_low_lanes(den.shape), extra[0], extra[1])
    return acc[:, :LANES] * (1.0 / den)


def _attn_ctx(qx, caches, sink, l, seq, weights, out_rows):
    m = qx.shape[0]
    steps = m // seq
    out = jax.ShapeDtypeStruct((out_rows, BRANCH_DIM), BF16)
    cache_specs = [pl.BlockSpec((None, None, seq, c.shape[-1]), lambda b: (b, l, 0, 0)) for c in caches]
    w_in_specs, w_out_specs, w_shapes = [], [], []
    for w in weights:
        _, rows, cols = w.shape
        assert rows % (steps * BF16_SUBLANES) == 0
        w_in_specs.append(pl.BlockSpec((None, rows // steps, cols), lambda b: (l, b, 0)))
        w_out_specs.append(pl.BlockSpec((rows // steps, cols), lambda b: (b, 0)))
        w_shapes.append(jax.ShapeDtypeStruct((rows, cols), BF16))
    return pl.pallas_call(
        functools.partial(_attn_ctx_kernel, n_weights=len(weights)),
        grid=(steps,),
        in_specs=[
            pl.BlockSpec(memory_space=pltpu.SMEM),
            pl.BlockSpec((seq, BRANCH_DIM), lambda b: (b, QX_QA // BRANCH_DIM)),
            pl.BlockSpec((seq, BRANCH_DIM), lambda b: (b, QX_QB // BRANCH_DIM)),
        ] + cache_specs + w_in_specs,
        out_specs=[pl.BlockSpec((seq, BRANCH_DIM), lambda b: (b, 0))] * 2 + w_out_specs,
        out_shape=[out, out] + w_shapes,
        compiler_params=_cparams(("parallel",)),
        name="attn_ctx",
    )(sink, qx, qx, *caches, *weights)


def _attn_a_lat_kernel(sink_ref, q_ref, kv_ref, ck_ref, cv_ref, _, o_ref,
                       qh_scr, kp_scr, vp_scr, ckb_scr, cvb_scr, mask_scr, *, n_blocks):
    seq = n_blocks * A_BLOCK
    band = 3 * A_BLOCK
    low_seq = _low_lanes((seq, LANES))
    for h in range(A_Q_HEADS):
        j, p, g = h // 2, h % 2, h // A_GROUP
        x = q_ref[:, j * LANES:(j + 1) * LANES].astype(F32) * SCALE
        if p != g:
            x = pltpu.roll(x, HEAD_DIM, 1)
        qh_scr[h] = jnp.where(low_seq if g == 0 else jnp.logical_not(low_seq), x, 0.0).astype(BF16)
    pad = jnp.zeros((A_BLOCK, LANES), BF16)
    for scr, col in ((kp_scr, 0), (vp_scr, KV_A)):
        scr[0:A_BLOCK, :] = pad
        scr[A_BLOCK:A_BLOCK + seq, :] = kv_ref[:, col:col + KV_A]
        scr[A_BLOCK + seq:2 * A_BLOCK + seq, :] = pad
    ckb_scr[...] = ck_ref[...].astype(BF16)
    cvb_scr[...] = cv_ref[...].astype(BF16)

    rows = A_GROUP * A_BLOCK
    row = lax.broadcasted_iota(jnp.int32, (rows, band), 0)
    qi = row % A_BLOCK
    kj = lax.broadcasted_iota(jnp.int32, (rows, band), 1)
    head_row = lax.broadcasted_iota(jnp.int32, (rows, 1), 0) // A_BLOCK
    low_blk = _low_lanes((A_BLOCK, LANES))
    @pl.when(pl.program_id(0) == 0)
    def _():
        for case, (no_prev, no_next) in enumerate(((1, 0), (0, 0), (0, 1))):
            valid = (kj >= qi + no_prev * (A_BLOCK - qi)) & (kj <= 2 * A_BLOCK + qi - no_next * (qi + 1))
            mask_scr[case] = jnp.where(valid, 0.0, NEG_INF)

    def body(n, carry):
        start = pl.multiple_of(n * A_BLOCK, A_BLOCK)
        case = jnp.where(n > 0, 1, 0) + jnp.where(n < n_blocks - 1, 0, 1)
        scores = []
        for g in range(A_KV_HEADS):
            q = jnp.concatenate(
                [qh_scr[A_GROUP * g + i, pl.ds(start, A_BLOCK), :] for i in range(A_GROUP)], axis=0)
            s_band = _dot_t(q, kp_scr[pl.ds(start, band), :]) + mask_scr[case]
            scores.append((s_band, _dot_t(q, ckb_scr[...])))
        outs = []
        for g, (s_band, s_ctx) in enumerate(scores):
            sink = jnp.full((rows, 1), sink_ref[A_GROUP * g], F32)
            for i in range(1, A_GROUP):
                sink = jnp.where(head_row == i, sink_ref[A_GROUP * g + i], sink)
            outs.append(_softmax_pv([(s_band, vp_scr[pl.ds(start, band), :]), (s_ctx, cvb_scr[...])],
                                    sink=sink))
        for j in range(A_Q_HEADS // 2):
            halves = []
            for p in range(2):
                h = 2 * j + p
                g, i = h // A_GROUP, h % A_GROUP
                t = outs[g][i * A_BLOCK:(i + 1) * A_BLOCK]
                halves.append(t if p == g else pltpu.roll(t, HEAD_DIM, 1))
            o_ref[pl.ds(start, A_BLOCK), j * LANES:(j + 1) * LANES] = (
                jnp.where(low_blk, halves[0], halves[1]).astype(o_ref.dtype))
        return carry

    lax.fori_loop(0, n_blocks, body, 0, unroll=2)


def _attn_a_lat(main, kv, cache_k, cache_v, sink, l, seq, oa_all):
    m = main.shape[0]
    nb = seq // A_BLOCK
    assert nb >= 2
    n_seq = m // seq
    past = cache_k.shape[2]
    first = (oa_all.shape[0] - m) // seq
    cache_spec = pl.BlockSpec((None, None, past, KV_A), lambda b: (b, l, 0, 0))
    return pl.pallas_call(
        functools.partial(_attn_a_lat_kernel, n_blocks=nb),
        grid=(n_seq,),
        in_specs=[
            pl.BlockSpec(memory_space=pltpu.SMEM),
            pl.BlockSpec((seq, BRANCH_DIM), lambda b: (b, QX_QA // BRANCH_DIM)),
            pl.BlockSpec((seq, 2 * KV_A), lambda b: (b, QX_KVA // (2 * KV_A))),
            cache_spec, cache_spec,
            pl.BlockSpec(memory_space=pl.ANY),
        ],
        out_specs=pl.BlockSpec((seq, BRANCH_DIM), lambda b: (b + first, 0)),
        out_shape=jax.ShapeDtypeStruct(oa_all.shape, BF16),
        input_output_aliases={5: 0},
        scratch_shapes=[
            pltpu.VMEM((A_Q_HEADS, seq, LANES), BF16),
            pltpu.VMEM((seq + 2 * A_BLOCK, LANES), BF16),
            pltpu.VMEM((seq + 2 * A_BLOCK, LANES), BF16),
            pltpu.VMEM((past, LANES), BF16),
            pltpu.VMEM((past, LANES), BF16),
            pltpu.VMEM((3, A_GROUP * A_BLOCK, 3 * A_BLOCK), F32),
        ],
        compiler_params=_cparams(("arbitrary",)),
        name="attn_a_lat",
    )(sink, main, kv, cache_k, cache_v, oa_all)


def _attn_b_lat_kernel(q_ref, k_ref, v_ref, ck_ref, cv_ref, rpb_ref, _, o_ref, ckb_scr, vx_scr, cvx_scr,
                       bias_ref, *, rows, kr):
    @pl.when(pl.program_id(0) == 0)
    def _():
        _fill_neighbourhood_bias(rpb_ref, bias_ref, kr=kr)

    half = kr // 2
    n_loc = kr * GRID_W
    low_q = _low_lanes((GRID_W, LANES))
    chunks = [slice(j * LANES, (j + 1) * LANES) for j in range(B_HEADS // 2)]
    ckb_scr[...] = ck_ref[...].astype(BF16)
    for j, cols in enumerate(chunks):
        vx_scr[j] = _with_ones(v_ref[:, cols])
        cvx_scr[j] = _with_ones(cv_ref[:, cols].astype(BF16))
    rows_per_iter = 4
    assert rows % rows_per_iter == 0

    def body(it, carry):
        work = []
        for u in range(rows_per_iter):
            r = it * rows_per_iter + u
            r0 = jnp.clip(r - half, 0, rows - kr)
            q0 = pl.multiple_of(r * GRID_W, GRID_W)
            start = pl.multiple_of(r0 * GRID_W, GRID_W)
            for j, cols in enumerate(chunks):
                qc = _scaled(q_ref[pl.ds(q0, GRID_W), cols])
                zero = jnp.zeros_like(qc)
                qs = jnp.concatenate([jnp.where(low_q, qc, zero), jnp.where(low_q, zero, qc)], axis=0)
                s_loc = _dot_t(qs, k_ref[pl.ds(start, n_loc), cols]) + bias_ref[r - r0, j]
                work.append((q0, start, j, cols, s_loc, _dot_t(qs, ckb_scr[:, cols])))
        for q0, start, j, cols, s_loc, s_ctx in work:
            o2 = _softmax_pv([(s_loc, vx_scr[j, pl.ds(start, n_loc), :]), (s_ctx, cvx_scr[j])])
            o = jnp.where(low_q, o2[:GRID_W], o2[GRID_W:])
            o_ref[pl.ds(q0, GRID_W), cols] = o.astype(o_ref.dtype)
        return carry

    lax.fori_loop(0, rows // rows_per_iter, body, 0)


def _attn_b_lat(main, kv, cache_k, cache_v, rpb_pad, l, seq, ob_all):
    m = main.shape[0]
    rows = seq // GRID_W
    kr = min(B_WIN_ROWS, rows)
    assert kr % 2 == 0 and 2 * GRID_W == LANES
    n_seq = m // seq
    past = cache_k.shape[2]
    first = (ob_all.shape[0] - m) // seq
    cache_spec = pl.BlockSpec((None, None, past, BRANCH_DIM), lambda b: (b, l, 0, 0))
    return pl.pallas_call(
        functools.partial(_attn_b_lat_kernel, rows=rows, kr=kr),
        grid=(n_seq,),
        in_specs=[
            pl.BlockSpec((seq, BRANCH_DIM), lambda b: (b, QX_QB // BRANCH_DIM)),
            pl.BlockSpec((seq, BRANCH_DIM), lambda b: (b, KV_KB // BRANCH_DIM)),
            pl.BlockSpec((seq, BRANCH_DIM), lambda b: (b, KV_VB // BRANCH_DIM)),
            cache_spec, cache_spec,
            _resident((None,) + rpb_pad.shape[1:], lambda b: (l, 0, 0, 0)),
            pl.BlockSpec(memory_space=pl.ANY),
        ],
        out_specs=pl.BlockSpec((seq, BRANCH_DIM), lambda b: (b + first, 0)),
        out_shape=jax.ShapeDtypeStruct(ob_all.shape, BF16),
        input_output_aliases={6: 0},
        scratch_shapes=[
            pltpu.VMEM((past, BRANCH_DIM), BF16),
            pltpu.VMEM((B_HEADS // 2, seq, 2 * LANES), BF16),
            pltpu.VMEM((B_HEADS // 2, past, 2 * LANES), BF16),
            pltpu.VMEM((kr, B_HEADS // 2, 2 * GRID_W, kr * GRID_W), F32),
        ],
        compiler_params=_cparams(("arbitrary",)),
        name="attn_b_lat",
    )(main, kv, kv, cache_k, cache_v, rpb_pad, ob_all)


def _fill_neighbourhood_bias(rpb_ref, bias_scr, *, kr):
    shape = (GRID_W, LANES)
    c = lax.broadcasted_iota(jnp.int32, shape, 0)
    lane = lax.broadcasted_iota(jnp.int32, shape, 1)
    c2 = lane % GRID_W
    ws = jnp.clip(c - B_WIN_COLS // 2, 0, GRID_W - B_WIN_COLS)
    ok = (c2 >= ws) & (c2 < ws + B_WIN_COLS)
    low = lane < GRID_W

    def toeplitz(h, dr, lane0):
        row = jnp.broadcast_to(rpb_ref[h, dr:dr + 1, :], shape)
        return pltpu.roll(row, (lane0 - (B_WIN_COLS - 1)) % LANES, 1, stride=1, stride_axis=0)

    for h in range(B_HEADS):
        rows = slice(h % 2 * GRID_W, (h % 2 + 1) * GRID_W)
        pairs = {}
        for v in range(kr):
            for m in range(0, kr, 2):
                dr = m - v + B_WIN_ROWS - 1
                if dr not in pairs:
                    pair = jnp.where(low, toeplitz(h, dr, 0), toeplitz(h, dr + 1, GRID_W))
                    pairs[dr] = jnp.where(ok, pair, NEG_INF)
                bias_scr[v, h // 2, rows, m * GRID_W:(m + 2) * GRID_W] = pairs[dr]


def _dft_tables(seq):
    cd = C_GROUP_DIM
    kc = (np.arange(cd)[:, None] * np.arange(cd)[None, :]) % cd
    ang_c = 2.0 * np.pi * kc / cd
    eye2 = np.eye(2)
    bd_cos = np.kron(eye2, np.cos(ang_c))
    bd_sin = np.kron(eye2, np.sin(ang_c))
    kt = (np.arange(seq)[:, None] * np.arange(seq)[None, :]) % seq
    ang_t = 2.0 * np.pi * kt / seq
    norm = 1.0 / np.sqrt(float(seq * cd))
    pos = np.concatenate([np.cos(ang_t), -np.sin(ang_t)], axis=1) * norm
    return (jnp.asarray(bd_cos, F32).astype(BF16), jnp.asarray(bd_sin, F32).astype(BF16),
            jnp.asarray(pos, F32).astype(BF16))


def _fourier_mix(u, bc_ref, bs_ref, pos_ref, o_ref):
    pair = 2 * C_GROUP_DIM
    seq = pos_ref.shape[0]
    uc, us = [], []
    for p in range(BRANCH_DIM // pair):
        up = u[:, p * pair:(p + 1) * pair]
        uc.append(_dot(up, bc_ref[...]))
        us.append(_dot(up, bs_ref[...]))
    zc = jnp.concatenate(uc, axis=1).astype(BF16)
    zs = jnp.concatenate(us, axis=1).astype(BF16)
    for s in range(u.shape[0] // seq):
        rows = slice(s * seq, (s + 1) * seq)
        z = jnp.concatenate([zc[rows], zs[rows]], axis=0)
        o_ref[rows, :] = _dot(pos_ref[...], z).astype(o_ref.dtype)


def _ffn_weight_copies(wfi_hbm, wfo_hbm, wfg_ref, wfu_ref, wfo_ref, sems):
    srcs = (wfi_hbm.at[:, 0:D_FF], wfi_hbm.at[:, D_FF:2 * D_FF], wfo_hbm)
    dsts = (wfg_ref, wfu_ref, wfo_ref)
    return [pltpu.make_async_copy(src, dst, sems.at[k]) for k, (src, dst) in enumerate(zip(srcs, dsts))]


def _tail_kernel(*refs, n_x, n_out, n_ctx_tiles):
    oa_ref, ob_ref, oc_ref = refs[:3]
    x_refs, refs = refs[3:3 + n_x], refs[3 + n_x:]
    (mod_ref, gpre0_ref, gpost0_ref, gpre1_ref, gpost1_ref,
     wg_ref, wb_ref, wo_ref, wfi_hbm, wfo_hbm) = refs[:10]
    o_refs, refs = refs[10:10 + n_out], refs[10 + n_out:]
    wfg_ref, wfu_ref, wfo_ref, sems = refs[:4]
    i = pl.program_id(0)
    first = i == 0
    is_ctx = i < n_ctx_tiles
    copies = _ffn_weight_copies(wfi_hbm, wfo_hbm, wfg_ref, wfu_ref, wfo_ref, sems)

    @pl.when(first)
    def _():
        for copy in copies:
            copy.start()

    if n_x == 1:
        x = x_refs[0][...]
    else:
        pick_ctx = jnp.full(x_refs[0].shape, jnp.where(is_ctx, 1, 0), jnp.int32) > 0
        x = jnp.where(pick_ctx, x_refs[0][...], x_refs[1][...])
    h = (_rms(x, gpre0_ref[...]) * (1.0 + mod_ref[1:2, :]) + mod_ref[0:1, :]).astype(BF16)
    gates = _dot(h, wg_ref[...])
    branches = [_dot(b_ref[...], wb_ref[k]) for k, b_ref in enumerate((oa_ref, ob_ref, oc_ref))]
    mix = None
    for k, branch in enumerate(branches):
        term = _sigmoid(gates[:, k * D_MODEL:(k + 1) * D_MODEL]) * branch
        mix = term if mix is None else mix + term
    y = _dot(mix.astype(BF16), wo_ref[...])
    x1 = x + mod_ref[2:3, :] * _rms(y, gpost0_ref[...])
    h2 = (_rms(x1, gpre1_ref[...]) * (1.0 + mod_ref[4:5, :]) + mod_ref[3:4, :]).astype(BF16)

    @pl.when(first)
    def _():
        for copy in copies:
            copy.wait()

    act = _silu(_dot(h2, wfg_ref[...])) * _dot(h2, wfu_ref[...])
    f = _dot(act.astype(BF16), wfo_ref[...])
    out = x1 + mod_ref[5:6, :] * _rms(f, gpost1_ref[...])
    if n_out == 1:
        o_refs[0][...] = out
    else:
        @pl.when(is_ctx)
        def _():
            o_refs[0][...] = out

        @pl.when(jnp.logical_not(is_ctx))
        def _():
            o_refs[1][...] = out


def _tail(oa, ob, oc, xs, mods, l, g_pre, g_post, w_gates, w_branch, w_out, w_ffn_in, w_ffn_out,
          *, tm, m_ctx, mod_row, split_out):
    m = oa.shape[0]
    n_ctx = m_ctx // tm
    o_spec = pl.BlockSpec((tm, BRANCH_DIM), lambda i: (i, 0))
    whole = pl.BlockSpec((tm, D_MODEL), lambda i: (i, 0))
    halves = [pl.BlockSpec((tm, D_MODEL), lambda i: (jnp.minimum(i, n_ctx - 1), 0)),
              pl.BlockSpec((tm, D_MODEL), lambda i: (jnp.maximum(i - n_ctx, 0), 0))]
    half_shapes = [jax.ShapeDtypeStruct((m_ctx, D_MODEL), F32),
                   jax.ShapeDtypeStruct((m - m_ctx, D_MODEL), F32)]
    gain_spec = lambda which: pl.BlockSpec((None, None, 1, D_MODEL), lambda i: (l, which, 0, 0))
    scratch = [
        pltpu.VMEM((D_MODEL, D_FF), BF16),
        pltpu.VMEM((D_MODEL, D_FF), BF16),
        pltpu.VMEM((D_FF, D_MODEL), BF16),
        pltpu.SemaphoreType.DMA((3,)),
    ]
    return pl.pallas_call(
        functools.partial(_tail_kernel, n_x=len(xs), n_out=2 if split_out else 1, n_ctx_tiles=n_ctx),
        grid=(m // tm,),
        in_specs=[o_spec, o_spec, o_spec] + (halves if len(xs) == 2 else [whole]) + [
            pl.BlockSpec((None, None, 6, D_MODEL), lambda i: (l, mod_row(i), 0, 0)),
            gain_spec(0), gain_spec(0), gain_spec(1), gain_spec(1),
            _resident((None, D_MODEL, N_GATES), lambda i: (l, 0, 0)),
            _resident((N_BRANCH, BRANCH_DIM, D_MODEL), lambda i: (0, 0, 0)),
            _resident((D_MODEL, D_MODEL), lambda i: (0, 0)),
            pl.BlockSpec(memory_space=pl.ANY),
            pl.BlockSpec(memory_space=pl.ANY),
        ],
        out_specs=halves if split_out else whole,
        out_shape=half_shapes if split_out else jax.ShapeDtypeStruct((m, D_MODEL), F32),
        scratch_shapes=scratch,
        compiler_params=_cparams(("arbitrary",)),
        name="tail",
    )(oa, ob, oc, *xs, mods, g_pre, g_post, g_pre, g_post, w_gates, w_branch, w_out,
      w_ffn_in, w_ffn_out)


def _rope_tables(seq):
    t = np.arange(seq)
    row = (t // GRID_W).astype(np.float32)
    col = (t % GRID_W).astype(np.float32)
    n_pairs_axis = HEAD_DIM // 4
    inv = (np.float32(ROPE_BASE) ** (-np.arange(n_pairs_axis, dtype=np.float32) / n_pairs_axis)
           ).astype(np.float32)
    ang = np.concatenate([row[:, None] * inv, col[:, None] * inv], axis=-1).astype(np.float64)
    cos, sin = np.cos(ang), np.sin(ang)
    cos_l = np.tile(cos, (1, LANES // cos.shape[1]))
    sin_l = np.tile(np.concatenate([-sin, sin], axis=-1), (1, LANES // HEAD_DIM))
    return jnp.asarray(cos_l, F32), jnp.asarray(sin_l, F32)


def kernel(x_prompt, x_sample, cache_a_k, cache_a_v, cache_b_k, cache_b_v, c, c_ctx, w_ada, b_ada,
           norm_pre, norm_post, w_in, a_sink, b_rpb, w_branch, w_out, w_ffn_in, w_ffn_out):
    batch, seq, _ = x_prompt.shape
    dec_batch, dec_seq, _ = x_sample.shape
    past = cache_a_k.shape[2]
    assert dec_batch <= CTX_MOD_ROW and seq % A_BLOCK == 0 and dec_seq % A_BLOCK == 0

    cvec = jnp.concatenate(
        [c, c_ctx[None, :], jnp.zeros((MOD_ROWS - dec_batch - 1, D_MODEL), F32)], axis=0)
    assert w_in.shape[-1] == D_IN
    mods, w_proj_b, w_gates_b = _modulation(cvec, w_ada, b_ada, w_in)
    mods = mods.reshape(DEPTH, MOD_ROWS, 6, D_MODEL)
    w_branch_2d = w_branch.reshape(DEPTH, N_BRANCH * BRANCH_DIM, D_MODEL)
    g_pre = norm_pre.reshape(DEPTH, 2, 1, D_MODEL)
    g_post = norm_post.reshape(DEPTH, 2, 1, D_MODEL)
    rope_tabs = _rope_tables(dec_seq)
    cak = cache_a_k.reshape(dec_batch, DEPTH, past, KV_A)
    cav = cache_a_v.reshape(dec_batch, DEPTH, past, KV_A)
    cbk = cache_b_k.reshape(dec_batch, DEPTH, past, BRANCH_DIM)
    cbv = cache_b_v.reshape(dec_batch, DEPTH, past, BRANCH_DIM)

    n_dr, n_dc = b_rpb.shape[2:]
    rpb_pad = jnp.pad(b_rpb.astype(F32), ((0, 0), (0, 0), (0, (-n_dr) % 8), (0, LANES - n_dc)))

    tm_proj, tm_tail = 1024, 512
    m_ctx, m_lat = batch * seq, dec_batch * dec_seq
    m_all = m_ctx + m_lat
    assert m_ctx % tm_proj == 0 and m_lat % tm_proj == 0
    ctx_row = lambda i: CTX_MOD_ROW
    lat_row = lambda i: (i * tm_proj) // dec_seq
    tail_row = lambda i: jnp.where(i < m_ctx // tm_tail, CTX_MOD_ROW, (i * tm_tail - m_ctx) // dec_seq)

    xs = (x_prompt.reshape(m_ctx, D_MODEL), x_sample.reshape(m_lat, D_MODEL))
    caches = None
    for l in range(DEPTH):
        x_ctx, x_lat = (xs[0], xs[1]) if len(xs) == 2 else (xs[0], xs[0])
        lat_tile0 = 0 if len(xs) == 2 else m_ctx // tm_proj
        qx, oc, *rest = _proj(x_ctx, mods, l, g_pre, w_proj_b, m=m_ctx, x_tile0=0, tm=tm_proj,
                              mod_row=ctx_row, seq=seq, oc_rows=m_all, caches=caches,
                              cast_weights=(w_ffn_in,))
        caches, (w_ffn_in_b,) = rest[:4], rest[4:]
        oa, ob, w_branch_b, w_out_b, w_ffn_out_b = _attn_ctx(
            qx, caches, a_sink[l], l, seq, (w_branch_2d, w_out, w_ffn_out), m_all)
        qx, kv, oc = _proj(x_lat, mods, l, g_pre, w_proj_b, m=m_lat, x_tile0=lat_tile0, tm=tm_proj,
                           mod_row=lat_row, seq=dec_seq, oc_all=oc, rope_tabs=rope_tabs)
        oa = _attn_a_lat(qx, qx, cak, cav, a_sink[l], l, dec_seq, oa)
        ob = _attn_b_lat(qx, kv, cbk, cbv, rpb_pad, l, dec_seq, ob)
        last = l == DEPTH - 1
        out = _tail(oa, ob, oc, xs, mods, l, g_pre, g_post, w_gates_b,
                    w_branch_b.reshape(N_BRANCH, BRANCH_DIM, D_MODEL), w_out_b, w_ffn_in_b, w_ffn_out_b,
                    tm=tm_tail, m_ctx=m_ctx, mod_row=tail_row, split_out=last)
        xs = tuple(out) if last else (out,)
    y_prompt = xs[0].reshape(batch, seq, D_MODEL)
    y_sample = xs[1].reshape(dec_batch, dec_seq, D_MODEL)
    nak, nav, nbk, nbv = caches
    new_a_k = nak.reshape(batch, DEPTH, seq, A_KV_HEADS, HEAD_DIM)
    new_a_v = nav.reshape(batch, DEPTH, seq, A_KV_HEADS, HEAD_DIM)
    new_b_k = nbk.reshape(batch, DEPTH, seq, B_HEADS, HEAD_DIM)
    new_b_v = nbv.reshape(batch, DEPTH, seq, B_HEADS, HEAD_DIM)
    return (y_prompt, y_sample, new_a_k, new_a_v, new_b_k, new_b_v)
```

```python
import functools

import numpy as np
import jax
import jax.numpy as jnp
from jax import lax
from jax.experimental import pallas as pl
from jax.experimental.pallas import tpu as pltpu

F32 = jnp.float32
BF16 = jnp.bfloat16

D_MODEL = 1024
DEPTH = 2
GRID_W = 64
HEAD_DIM = 64
BRANCH_DIM = D_MODEL // 2
A_Q_HEADS = BRANCH_DIM // HEAD_DIM
A_KV_HEADS = A_Q_HEADS // 4
A_GROUP = A_Q_HEADS // A_KV_HEADS
A_BLOCK = 128
B_HEADS = BRANCH_DIM // HEAD_DIM
B_WIN_ROWS = 8
B_WIN_COLS = 16
C_GROUPS = 4
C_GROUP_DIM = BRANCH_DIM // C_GROUPS
N_BRANCH = 3
D_FF = -(-8 * D_MODEL // (3 * 256)) * 256
ROPE_BASE = 10000.0
NORM_EPS = 1e-6
NEG_INF = -1e30
SCALE = HEAD_DIM ** -0.5

LANES = 128
KV_A = A_KV_HEADS * HEAD_DIM
N_GATES = N_BRANCH * D_MODEL
W_QA = 0
W_KA = W_QA + BRANCH_DIM
W_VA = W_KA + KV_A
W_QB = W_VA + KV_A
W_KB = W_QB + BRANCH_DIM
W_VB = W_KB + BRANCH_DIM
W_UC = W_VB + BRANCH_DIM
W_GATES = W_UC + BRANCH_DIM
D_IN = W_GATES + N_GATES
QX_QA, QX_QB, QX_KVA = 0, BRANCH_DIM, 2 * BRANCH_DIM
KV_KB, KV_VB = 0, BRANCH_DIM
MOD_ROWS = 8
CTX_MOD_ROW = 4
VMEM_LIMIT = 56 * 1024 * 1024
BF16_SUBLANES = 16


def _cparams(sem):
    return pltpu.CompilerParams(dimension_semantics=sem, vmem_limit_bytes=VMEM_LIMIT)


def _resident(shape, index_map):
    return pl.BlockSpec(shape, index_map, pipeline_mode=pl.Buffered(1))


def _rms(x, g):
    return x * lax.rsqrt(jnp.mean(x * x, axis=-1, keepdims=True) + NORM_EPS) * g


def _sigmoid(x):
    return 0.5 * jnp.tanh(0.5 * x) + 0.5


def _silu(x):
    half = 0.5 * x
    return half + half * jnp.tanh(half)


def _dot_t(a, b):
    return lax.dot_general(a, b, (((1,), (1,)), ((), ())), preferred_element_type=F32)


def _dot(a, b):
    return jnp.dot(a, b, preferred_element_type=F32)


def _scaled(q):
    assert np.log2(SCALE) == round(np.log2(SCALE))
    return q * jnp.asarray(SCALE, q.dtype)


def _low_lanes(shape):
    return lax.broadcasted_iota(jnp.int32, shape, len(shape) - 1) < HEAD_DIM


def _mod_kernel(cv_ref, w_ref, b_ref, win_ref, o_ref, wproj_ref, wgates_ref):
    a, b = _silu(cv_ref[...]), w_ref[...]
    a_hi, b_hi = a.astype(BF16), b.astype(BF16)
    a_lo = (a - a_hi.astype(F32)).astype(BF16)
    b_lo = (b - b_hi.astype(F32)).astype(BF16)
    o_ref[...] = _dot(a_hi, b_hi) + (_dot(a_lo, b_hi) + _dot(a_hi, b_lo)) + b_ref[...]
    @pl.when(pl.program_id(0) == 0)
    def _():
        _split_cast_w_in(win_ref, wproj_ref, wgates_ref)


def _split_cast_w_in(win_ref, wproj_ref, wgates_ref):
    wproj_ref[...] = win_ref[:, :W_GATES].astype(wproj_ref.dtype)
    wgates_ref[...] = win_ref[:, W_GATES:].astype(wgates_ref.dtype)


def _modulation(cvec, w_ada, b_ada, w_in):
    tn = 1536
    n = 6 * D_MODEL
    steps = n // tn
    rows = w_in.shape[1] // steps
    slab = lambda l, j: jnp.where(l == 0, j, steps - 1)
    return pl.pallas_call(
        _mod_kernel,
        grid=(DEPTH, steps),
        in_specs=[
            pl.BlockSpec((MOD_ROWS, D_MODEL), lambda l, j: (0, 0)),
            pl.BlockSpec((None, D_MODEL, tn), lambda l, j: (l, 0, j)),
            pl.BlockSpec((None, 1, tn), lambda l, j: (l, 0, j)),
            pl.BlockSpec((None, rows, D_IN), lambda l, j: (0, slab(l, j), 0)),
        ],
        out_specs=[pl.BlockSpec((None, MOD_ROWS, tn), lambda l, j: (l, 0, j)),
                   pl.BlockSpec((rows, W_GATES), lambda l, j: (slab(l, j), 0)),
                   pl.BlockSpec((rows, N_GATES), lambda l, j: (slab(l, j), 0))],
        out_shape=[jax.ShapeDtypeStruct((DEPTH, MOD_ROWS, n), F32),
                   jax.ShapeDtypeStruct((D_MODEL, W_GATES), BF16),
                   jax.ShapeDtypeStruct((D_MODEL, N_GATES), BF16)],
        compiler_params=_cparams(("arbitrary", "arbitrary")),
        name="modulation",
    )(cvec, w_ada, b_ada.reshape(DEPTH, 1, n), w_in)


def _rope_cols(v, cos, sin):
    first = (lax.broadcasted_iota(jnp.int32, (v.shape[0], LANES), 1) % HEAD_DIM) < HEAD_DIM // 2
    outs = []
    for c in range(v.shape[1] // LANES):
        vc = v[:, c * LANES:(c + 1) * LANES]
        partner = jnp.where(first, pltpu.roll(vc, LANES - HEAD_DIM // 2, 1),
                            pltpu.roll(vc, HEAD_DIM // 2, 1))
        outs.append(vc * cos + partner * sin)
    return outs[0] if len(outs) == 1 else jnp.concatenate(outs, axis=1)


def _with_rope(acc, lo, hi, cos_ref, sin_ref):
    parts = []
    if lo > 0:
        parts.append(acc[:, :lo])
    parts.append(_rope_cols(acc[:, lo:hi], cos_ref[...], sin_ref[...]))
    if hi < acc.shape[1]:
        parts.append(acc[:, hi:])
    return parts[0] if len(parts) == 1 else jnp.concatenate(parts, axis=1)


def _proj_kernel(*refs, latent, n_alias, n_cast):
    x_ref, mod_ref, g_ref, w_ref, bc_ref, bs_ref, pos_ref = refs[:7]
    if latent:
        cos_ref, sin_ref, _, qx_ref, kv_ref, oc_ref, h_scr = refs[7:]
    else:
        rest = refs[7 + n_alias:]
        cast_src, rest = rest[:n_cast], rest[n_cast:]
        qx_ref, oc_ref, ka_ref, va_ref, kb_ref, vb_ref = rest[:6]
        cast_dst, h_scr = rest[6:6 + n_cast], rest[6 + n_cast]
        for src, dst in zip(cast_src, cast_dst):
            dst[...] = src[...].astype(dst.dtype)
    j = pl.program_id(1)

    @pl.when(j == 0)
    def _():
        h = _rms(x_ref[...], g_ref[...]) * (1.0 + mod_ref[1:2, :]) + mod_ref[0:1, :]
        h_scr[...] = h.astype(BF16)
        acc = _dot(h_scr[...], w_ref[:, W_QA:W_KB])
        qa = acc[:, W_QA:W_QA + BRANCH_DIM]
        qb = acc[:, W_QB:W_QB + BRANCH_DIM]
        kva = acc[:, W_KA:W_KA + 2 * KV_A]
        if latent:
            qa = _rope_cols(qa, cos_ref[...], sin_ref[...])
            kva = _with_rope(kva, 0, KV_A, cos_ref, sin_ref)
            qx_ref[:, QX_KVA:QX_KVA + 2 * KV_A] = kva.astype(qx_ref.dtype)
        else:
            ka_ref[...] = kva[:, :KV_A].reshape(ka_ref.shape)
            va_ref[...] = kva[:, KV_A:].reshape(va_ref.shape)
        qx_ref[:, QX_QA:QX_QA + BRANCH_DIM] = qa.astype(qx_ref.dtype)
        qx_ref[:, QX_QB:QX_QB + BRANCH_DIM] = qb.astype(qx_ref.dtype)

    @pl.when(j == 1)
    def _():
        acc = _dot(h_scr[...], w_ref[:, W_KB:W_GATES])
        kb = acc[:, 0:BRANCH_DIM]
        vb = acc[:, W_VB - W_KB:W_VB - W_KB + BRANCH_DIM]
        uc = acc[:, W_UC - W_KB:W_UC - W_KB + BRANCH_DIM]
        if latent:
            kv_ref[:, KV_KB:KV_KB + BRANCH_DIM] = kb.astype(kv_ref.dtype)
            kv_ref[:, KV_VB:KV_VB + BRANCH_DIM] = vb.astype(kv_ref.dtype)
        else:
            kb_ref[...] = kb.reshape(kb_ref.shape)
            vb_ref[...] = vb.reshape(vb_ref.shape)
        _fourier_mix(uc.astype(BF16), bc_ref, bs_ref, pos_ref, oc_ref)


def _proj(x, mods, l, g_pre, w_in, *, m, x_tile0, tm, mod_row, seq, oc_rows=None, oc_all=None,
          rope_tabs=None, caches=None, cast_weights=()):
    assert tm % seq == 0
    latent = rope_tabs is not None
    bc, bs, pos = _dft_tables(seq)
    pair = 2 * C_GROUP_DIM
    in_specs = [
        pl.BlockSpec((tm, D_MODEL), lambda i, j: (i + x_tile0, 0)),
        pl.BlockSpec((None, None, 6, D_MODEL), lambda i, j: (l, mod_row(i), 0, 0)),
        pl.BlockSpec((None, None, 1, D_MODEL), lambda i, j: (l, 0, 0, 0)),
        _resident((D_MODEL, W_GATES), lambda i, j: (0, 0)),
        _resident((pair, pair), lambda i, j: (0, 0)),
        _resident((pair, pair), lambda i, j: (0, 0)),
        _resident((seq, 2 * seq), lambda i, j: (0, 0)),
    ]
    args = [x, mods, g_pre, w_in, bc, bs, pos]
    aliases = {}
    if latent:
        seq_tiles = seq // tm
        tab_spec = pl.BlockSpec((tm, LANES), lambda i, j: (i % seq_tiles, 0))
        in_specs += [tab_spec, tab_spec, pl.BlockSpec(memory_space=pl.ANY)]
        args += list(rope_tabs)
        widths = (2 * BRANCH_DIM + 2 * KV_A, 2 * BRANCH_DIM)
        aliases = {len(args): len(widths)}
        args.append(oc_all)
        oc_rows, oc_tile0 = oc_all.shape[0], (oc_all.shape[0] - m) // tm
        cache_specs, cache_shapes = [], []
    else:
        widths = (2 * BRANCH_DIM,)
        oc_tile0 = 0
        cache_widths = (KV_A, KV_A, BRANCH_DIM, BRANCH_DIM)
        if caches is not None:
            in_specs += [pl.BlockSpec(memory_space=pl.ANY)] * len(caches)
            aliases = {len(args) + k: len(widths) + 1 + k for k in range(len(caches))}
            args += list(caches)
        cache_specs = [pl.BlockSpec((tm // seq, None, seq, w), lambda i, j: (i, l, 0, 0))
                       for w in cache_widths]
        cache_shapes = [jax.ShapeDtypeStruct((m // seq, DEPTH, seq, w), F32) for w in cache_widths]
        steps = 2 * (m // tm)
        for w in cast_weights:
            _, rows, cols = w.shape
            assert rows % (steps * BF16_SUBLANES) == 0
            in_specs.append(pl.BlockSpec((None, rows // steps, cols), lambda i, j: (l, 2 * i + j, 0)))
            cache_specs.append(pl.BlockSpec((rows // steps, cols), lambda i, j: (2 * i + j, 0)))
            cache_shapes.append(jax.ShapeDtypeStruct((rows, cols), BF16))
            args.append(w)
    return pl.pallas_call(
        functools.partial(_proj_kernel, latent=latent, n_alias=len(aliases), n_cast=len(cast_weights)),
        grid=(m // tm, 2),
        in_specs=in_specs,
        out_specs=([pl.BlockSpec((tm, w), lambda i, j: (i, 0)) for w in widths]
                   + [pl.BlockSpec((tm, BRANCH_DIM), lambda i, j: (i + oc_tile0, 0))] + cache_specs),
        out_shape=([jax.ShapeDtypeStruct((m, w), BF16) for w in widths]
                   + [jax.ShapeDtypeStruct((oc_rows, BRANCH_DIM), BF16)] + cache_shapes),
        input_output_aliases=aliases,
        scratch_shapes=[pltpu.VMEM((tm, D_MODEL), BF16)],
        compiler_params=_cparams(("parallel", "arbitrary")),
        name="proj_lat" if latent else "proj_ctx",
    )(*args)


def _with_ones(v):
    return jnp.concatenate([v, jnp.ones_like(v)], axis=1)


def _softmax_pv(parts, sink=None):
    m = parts[0][0].max(axis=-1, keepdims=True)
    for s, _ in parts[1:]:
        m = jnp.maximum(m, s.max(axis=-1, keepdims=True))
    if sink is not None:
        m = jnp.maximum(m, sink)
    ones_half = parts[0][1].shape[1] == 2 * LANES
    acc = None
    den = None
    for s, v in parts:
        e = jnp.exp(s - m)
        if not ones_half:
            d = e.sum(axis=-1, keepdims=True)
            den = d if den is None else den + d
        o = _dot(e.astype(BF16), v)
        acc = o if acc is None else acc + o
    if ones_half:
        acc, den = acc[:, :LANES], acc[:, LANES:]
    if sink is not None:
        den = den + jnp.exp(sink - m)
    return acc * (1.0 / den)


def _kv_head_variants(x2):
    low = _low_lanes(x2.shape)
    xr = pltpu.roll(x2, HEAD_DIM, 1)
    zero = jnp.zeros_like(x2)
    return [
        [jnp.where(low, x2, zero).astype(BF16), jnp.where(low, zero, xr).astype(BF16)],
        [jnp.where(low, xr, zero).astype(BF16), jnp.where(low, zero, x2).astype(BF16)],
    ]


def _attn_ctx_kernel(sink_ref, qa_ref, qb_ref, ka_ref, va_ref, kb_ref, vb_ref, *rest, n_weights):
    w_src, (oa_ref, ob_ref), w_dst = rest[:n_weights], rest[n_weights:n_weights + 2], rest[n_weights + 2:]
    for src, dst in zip(w_src, w_dst):
        dst[...] = src[...].astype(dst.dtype)
    ka = [jnp.concatenate(v, axis=0) for v in _kv_head_variants(ka_ref[...])]
    va = [jnp.concatenate(v, axis=0) for v in _kv_head_variants(va_ref[...])]
    chunks = [slice(j * LANES, (j + 1) * LANES) for j in range(A_Q_HEADS // 2)]
    work = []
    for j, cols in enumerate(chunks):
        g = (2 * j) // A_GROUP
        s2 = _dot_t(_scaled(qa_ref[:, cols]), ka[g])
        work.append((oa_ref, cols, s2, va[g], (sink_ref[2 * j], sink_ref[2 * j + 1])))
    low = _low_lanes((kb_ref.shape[0], LANES))
    for cols in chunks:
        kc, vc = kb_ref[:, cols], vb_ref[:, cols]
        zero = jnp.zeros_like(kc)
        k2 = jnp.concatenate([jnp.where(low, kc, zero), jnp.where(low, zero, kc)], axis=0).astype(BF16)
        v2 = jnp.concatenate([jnp.where(low, vc, zero), jnp.where(low, zero, vc)], axis=0).astype(BF16)
        work.append((ob_ref, cols, _dot_t(_scaled(qb_ref[:, cols]), k2), v2, None))
    for o_ref, cols, s2, v2, sinks in work:
        o_ref[:, cols] = _pair_softmax_pv(s2, v2, sinks).astype(o_ref.dtype)


def _pair_softmax_pv(s2, v2, sinks=None):
    tk = s2.shape[1] // 2
    es, extra = [], []
    for p in range(2):
        s = s2[:, p * tk:(p + 1) * tk]
        m = s.max(axis=-1, keepdims=True)
        if sinks is not None:
            m = jnp.maximum(m, sinks[p])
            extra.append(jnp.exp(sinks[p] - m))
        es.append(jnp.exp(s - m).astype(BF16))
    lane_half = lax.broadcasted_iota(jnp.int32, v2.shape, 1) // HEAD_DIM
    row_half = lax.broadcasted_iota(jnp.int32, v2.shape, 0) // tk
    ones2 = jnp.where(lane_half == row_half, 1.0, 0.0).astype(v2.dtype)
    acc = _dot(jnp.concatenate(es, axis=1), jnp.concatenate([v2, ones2], axis=1))
    den = acc[:, LANES:]
    if sinks is not None:
        den = den + jnp.where(_low_lanes(den.shape), extra[0], extra[1])
    return acc[:, :LANES] * (1.0 / den)


def _attn_ctx(qx, caches, sink, l, seq, weights, out_rows):
    m = qx.shape[0]
    steps = m // seq
    out = jax.ShapeDtypeStruct((out_rows, BRANCH_DIM), BF16)
    cache_specs = [pl.BlockSpec((None, None, seq, c.shape[-1]), lambda b: (b, l, 0, 0)) for c in caches]
    w_in_specs, w_out_specs, w_shapes = [], [], []
    for w in weights:
        _, rows, cols = w.shape
        assert rows % (steps * BF16_SUBLANES) == 0
        w_in_specs.append(pl.BlockSpec((None, rows // steps, cols), lambda b: (l, b, 0)))
        w_out_specs.append(pl.BlockSpec((rows // steps, cols), lambda b: (b, 0)))
        w_shapes.append(jax.ShapeDtypeStruct((rows, cols), BF16))
    return pl.pallas_call(
        functools.partial(_attn_ctx_kernel, n_weights=len(weights)),
        grid=(steps,),
        in_specs=[
            pl.BlockSpec(memory_space=pltpu.SMEM),
            pl.BlockSpec((seq, BRANCH_DIM), lambda b: (b, QX_QA // BRANCH_DIM)),
            pl.BlockSpec((seq, BRANCH_DIM), lambda b: (b, QX_QB // BRANCH_DIM)),
        ] + cache_specs + w_in_specs,
        out_specs=[pl.BlockSpec((seq, BRANCH_DIM), lambda b: (b, 0))] * 2 + w_out_specs,
        out_shape=[out, out] + w_shapes,
        compiler_params=_cparams(("parallel",)),
        name="attn_ctx",
    )(sink, qx, qx, *caches, *weights)


def _attn_a_lat_kernel(sink_ref, q_ref, kv_ref, ck_ref, cv_ref, _, *rest, n_blocks, cast_w_in):
    if cast_w_in:
        win_ref, o_ref, wproj_ref, wgates_ref = rest[:4]
        _split_cast_w_in(win_ref, wproj_ref, wgates_ref)
        rest = rest[4:]
    else:
        o_ref, rest = rest[0], rest[1:]
    qh_scr, kp_scr, vp_scr, ckb_scr, cvb_scr, mask_scr = rest
    seq = n_blocks * A_BLOCK
    band = 3 * A_BLOCK
    low_seq = _low_lanes((seq, LANES))
    for h in range(A_Q_HEADS):
        j, p, g = h // 2, h % 2, h // A_GROUP
        x = q_ref[:, j * LANES:(j + 1) * LANES].astype(F32) * SCALE
        if p != g:
            x = pltpu.roll(x, HEAD_DIM, 1)
        qh_scr[h] = jnp.where(low_seq if g == 0 else jnp.logical_not(low_seq), x, 0.0).astype(BF16)
    pad = jnp.zeros((A_BLOCK, LANES), BF16)
    for scr, col in ((kp_scr, 0), (vp_scr, KV_A)):
        scr[0:A_BLOCK, :] = pad
        scr[A_BLOCK:A_BLOCK + seq, :] = kv_ref[:, col:col + KV_A]
        scr[A_BLOCK + seq:2 * A_BLOCK + seq, :] = pad
    ckb_scr[...] = ck_ref[...].astype(BF16)
    cvb_scr[...] = cv_ref[...].astype(BF16)

    rows = A_GROUP * A_BLOCK
    row = lax.broadcasted_iota(jnp.int32, (rows, band), 0)
    qi = row % A_BLOCK
    kj = lax.broadcasted_iota(jnp.int32, (rows, band), 1)
    head_row = lax.broadcasted_iota(jnp.int32, (rows, 1), 0) // A_BLOCK
    low_blk = _low_lanes((A_BLOCK, LANES))
    @pl.when(pl.program_id(0) == 0)
    def _():
        for case, (no_prev, no_next) in enumerate(((1, 0), (0, 0), (0, 1))):
            valid = (kj >= qi + no_prev * (A_BLOCK - qi)) & (kj <= 2 * A_BLOCK + qi - no_next * (qi + 1))
            mask_scr[case] = jnp.where(valid, 0.0, NEG_INF)

    def body(n, carry):
        start = pl.multiple_of(n * A_BLOCK, A_BLOCK)
        case = jnp.where(n > 0, 1, 0) + jnp.where(n < n_blocks - 1, 0, 1)
        scores = []
        for g in range(A_KV_HEADS):
            q = jnp.concatenate(
                [qh_scr[A_GROUP * g + i, pl.ds(start, A_BLOCK), :] for i in range(A_GROUP)], axis=0)
            s_band = _dot_t(q, kp_scr[pl.ds(start, band), :]) + mask_scr[case]
            scores.append((s_band, _dot_t(q, ckb_scr[...])))
        outs = []
        for g, (s_band, s_ctx) in enumerate(scores):
            sink = jnp.full((rows, 1), sink_ref[A_GROUP * g], F32)
            for i in range(1, A_GROUP):
                sink = jnp.where(head_row == i, sink_ref[A_GROUP * g + i], sink)
            outs.append(_softmax_pv([(s_band, vp_scr[pl.ds(start, band), :]), (s_ctx, cvb_scr[...])],
                                    sink=sink))
        for j in range(A_Q_HEADS // 2):
            halves = []
            for p in range(2):
                h = 2 * j + p
                g, i = h // A_GROUP, h % A_GROUP
                t = outs[g][i * A_BLOCK:(i + 1) * A_BLOCK]
                halves.append(t if p == g else pltpu.roll(t, HEAD_DIM, 1))
            o_ref[pl.ds(start, A_BLOCK), j * LANES:(j + 1) * LANES] = (
                jnp.where(low_blk, halves[0], halves[1]).astype(o_ref.dtype))
        return carry

    lax.fori_loop(0, n_blocks, body, 0, unroll=2)


def _attn_a_lat(main, kv, cache_k, cache_v, sink, l, seq, oa_all, w_in_next=None):
    m = main.shape[0]
    nb = seq // A_BLOCK
    assert nb >= 2
    n_seq = m // seq
    past = cache_k.shape[2]
    first = (oa_all.shape[0] - m) // seq
    cache_spec = pl.BlockSpec((None, None, past, KV_A), lambda b: (b, l, 0, 0))
    in_specs = [
        pl.BlockSpec(memory_space=pltpu.SMEM),
        pl.BlockSpec((seq, BRANCH_DIM), lambda b: (b, QX_QA // BRANCH_DIM)),
        pl.BlockSpec((seq, 2 * KV_A), lambda b: (b, QX_KVA // (2 * KV_A))),
        cache_spec, cache_spec,
        pl.BlockSpec(memory_space=pl.ANY),
    ]
    args = [sink, main, kv, cache_k, cache_v, oa_all]
    out_specs = [pl.BlockSpec((seq, BRANCH_DIM), lambda b: (b + first, 0))]
    out_shape = [jax.ShapeDtypeStruct(oa_all.shape, BF16)]
    if w_in_next is not None:
        w_in, layer = w_in_next
        rows = w_in.shape[1] // n_seq
        assert w_in.shape[1] % (n_seq * BF16_SUBLANES) == 0
        in_specs.append(pl.BlockSpec((None, rows, D_IN), lambda b: (layer, b, 0)))
        args.append(w_in)
        out_specs += [pl.BlockSpec((rows, W_GATES), lambda b: (b, 0)),
                      pl.BlockSpec((rows, N_GATES), lambda b: (b, 0))]
        out_shape += [jax.ShapeDtypeStruct((D_MODEL, W_GATES), BF16),
                      jax.ShapeDtypeStruct((D_MODEL, N_GATES), BF16)]
    return pl.pallas_call(
        functools.partial(_attn_a_lat_kernel, n_blocks=nb, cast_w_in=w_in_next is not None),
        grid=(n_seq,),
        in_specs=in_specs,
        out_specs=out_specs,
        out_shape=out_shape,
        input_output_aliases={5: 0},
        scratch_shapes=[
            pltpu.VMEM((A_Q_HEADS, seq, LANES), BF16),
            pltpu.VMEM((seq + 2 * A_BLOCK, LANES), BF16),
            pltpu.VMEM((seq + 2 * A_BLOCK, LANES), BF16),
            pltpu.VMEM((past, LANES), BF16),
            pltpu.VMEM((past, LANES), BF16),
            pltpu.VMEM((3, A_GROUP * A_BLOCK, 3 * A_BLOCK), F32),
        ],
        compiler_params=_cparams(("arbitrary",)),
        name="attn_a_lat",
    )(*args)


def _attn_b_lat_kernel(q_ref, k_ref, v_ref, ck_ref, cv_ref, rpb_ref, _, o_ref, ckb_scr, vx_scr, cvx_scr,
                       bias_ref, *, rows, kr):
    @pl.when(pl.program_id(0) == 0)
    def _():
        _fill_neighbourhood_bias(rpb_ref, bias_ref, kr=kr)

    half = kr // 2
    n_loc = kr * GRID_W
    low_q = _low_lanes((GRID_W, LANES))
    chunks = [slice(j * LANES, (j + 1) * LANES) for j in range(B_HEADS // 2)]
    ckb_scr[...] = ck_ref[...].astype(BF16)
    for j, cols in enumerate(chunks):
        vx_scr[j] = _with_ones(v_ref[:, cols])
        cvx_scr[j] = _with_ones(cv_ref[:, cols].astype(BF16))
    rows_per_iter = 4
    assert rows % rows_per_iter == 0

    def body(it, carry):
        work = []
        for u in range(rows_per_iter):
            r = it * rows_per_iter + u
            r0 = jnp.clip(r - half, 0, rows - kr)
            q0 = pl.multiple_of(r * GRID_W, GRID_W)
            start = pl.multiple_of(r0 * GRID_W, GRID_W)
            for j, cols in enumerate(chunks):
                qc = _scaled(q_ref[pl.ds(q0, GRID_W), cols])
                zero = jnp.zeros_like(qc)
                qs = jnp.concatenate([jnp.where(low_q, qc, zero), jnp.where(low_q, zero, qc)], axis=0)
                s_loc = _dot_t(qs, k_ref[pl.ds(start, n_loc), cols]) + bias_ref[r - r0, j]
                work.append((q0, start, j, cols, s_loc, _dot_t(qs, ckb_scr[:, cols])))
        for q0, start, j, cols, s_loc, s_ctx in work:
            o2 = _softmax_pv([(s_loc, vx_scr[j, pl.ds(start, n_loc), :]), (s_ctx, cvx_scr[j])])
            o = jnp.where(low_q, o2[:GRID_W], o2[GRID_W:])
            o_ref[pl.ds(q0, GRID_W), cols] = o.astype(o_ref.dtype)
        return carry

    lax.fori_loop(0, rows // rows_per_iter, body, 0)


def _attn_b_lat(main, kv, cache_k, cache_v, rpb_pad, l, seq, ob_all):
    m = main.shape[0]
    rows = seq // GRID_W
    kr = min(B_WIN_ROWS, rows)
    assert kr % 2 == 0 and 2 * GRID_W == LANES
    n_seq = m // seq
    past = cache_k.shape[2]
    first = (ob_all.shape[0] - m) // seq
    cache_spec = pl.BlockSpec((None, None, past, BRANCH_DIM), lambda b: (b, l, 0, 0))
    return pl.pallas_call(
        functools.partial(_attn_b_lat_kernel, rows=rows, kr=kr),
        grid=(n_seq,),
        in_specs=[
            pl.BlockSpec((seq, BRANCH_DIM), lambda b: (b, QX_QB // BRANCH_DIM)),
            pl.BlockSpec((seq, BRANCH_DIM), lambda b: (b, KV_KB // BRANCH_DIM)),
            pl.BlockSpec((seq, BRANCH_DIM), lambda b: (b, KV_VB // BRANCH_DIM)),
            cache_spec, cache_spec,
            _resident((None,) + rpb_pad.shape[1:], lambda b: (l, 0, 0, 0)),
            pl.BlockSpec(memory_space=pl.ANY),
        ],
        out_specs=pl.BlockSpec((seq, BRANCH_DIM), lambda b: (b + first, 0)),
        out_shape=jax.ShapeDtypeStruct(ob_all.shape, BF16),
        input_output_aliases={6: 0},
        scratch_shapes=[
            pltpu.VMEM((past, BRANCH_DIM), BF16),
            pltpu.VMEM((B_HEADS // 2, seq, 2 * LANES), BF16),
            pltpu.VMEM((B_HEADS // 2, past, 2 * LANES), BF16),
            pltpu.VMEM((kr, B_HEADS // 2, 2 * GRID_W, kr * GRID_W), F32),
        ],
        compiler_params=_cparams(("arbitrary",)),
        name="attn_b_lat",
    )(main, kv, kv, cache_k, cache_v, rpb_pad, ob_all)


def _fill_neighbourhood_bias(rpb_ref, bias_scr, *, kr):
    shape = (GRID_W, LANES)
    c = lax.broadcasted_iota(jnp.int32, shape, 0)
    lane = lax.broadcasted_iota(jnp.int32, shape, 1)
    c2 = lane % GRID_W
    ws = jnp.clip(c - B_WIN_COLS // 2, 0, GRID_W - B_WIN_COLS)
    ok = (c2 >= ws) & (c2 < ws + B_WIN_COLS)
    low = lane < GRID_W

    def toeplitz(h, dr, lane0):
        row = jnp.broadcast_to(rpb_ref[h, dr:dr + 1, :], shape)
        return pltpu.roll(row, (lane0 - (B_WIN_COLS - 1)) % LANES, 1, stride=1, stride_axis=0)

    for h in range(B_HEADS):
        rows = slice(h % 2 * GRID_W, (h % 2 + 1) * GRID_W)
        pairs = {}
        for v in range(kr):
            for m in range(0, kr, 2):
                dr = m - v + B_WIN_ROWS - 1
                if dr not in pairs:
                    pair = jnp.where(low, toeplitz(h, dr, 0), toeplitz(h, dr + 1, GRID_W))
                    pairs[dr] = jnp.where(ok, pair, NEG_INF)
                bias_scr[v, h // 2, rows, m * GRID_W:(m + 2) * GRID_W] = pairs[dr]


def _dft_tables(seq):
    cd = C_GROUP_DIM
    kc = (np.arange(cd)[:, None] * np.arange(cd)[None, :]) % cd
    ang_c = 2.0 * np.pi * kc / cd
    eye2 = np.eye(2)
    bd_cos = np.kron(eye2, np.cos(ang_c))
    bd_sin = np.kron(eye2, np.sin(ang_c))
    kt = (np.arange(seq)[:, None] * np.arange(seq)[None, :]) % seq
    ang_t = 2.0 * np.pi * kt / seq
    norm = 1.0 / np.sqrt(float(seq * cd))
    pos = np.concatenate([np.cos(ang_t), -np.sin(ang_t)], axis=1) * norm
    return (jnp.asarray(bd_cos, F32).astype(BF16), jnp.asarray(bd_sin, F32).astype(BF16),
            jnp.asarray(pos, F32).astype(BF16))


def _fourier_mix(u, bc_ref, bs_ref, pos_ref, o_ref):
    pair = 2 * C_GROUP_DIM
    seq = pos_ref.shape[0]
    uc, us = [], []
    for p in range(BRANCH_DIM // pair):
        up = u[:, p * pair:(p + 1) * pair]
        uc.append(_dot(up, bc_ref[...]))
        us.append(_dot(up, bs_ref[...]))
    zc = jnp.concatenate(uc, axis=1).astype(BF16)
    zs = jnp.concatenate(us, axis=1).astype(BF16)
    for s in range(u.shape[0] // seq):
        rows = slice(s * seq, (s + 1) * seq)
        z = jnp.concatenate([zc[rows], zs[rows]], axis=0)
        o_ref[rows, :] = _dot(pos_ref[...], z).astype(o_ref.dtype)


def _ffn_weight_copies(wfi_hbm, wfo_hbm, wfg_ref, wfu_ref, wfo_ref, sems):
    srcs = (wfi_hbm.at[:, 0:D_FF], wfi_hbm.at[:, D_FF:2 * D_FF], wfo_hbm)
    dsts = (wfg_ref, wfu_ref, wfo_ref)
    return [pltpu.make_async_copy(src, dst, sems.at[k]) for k, (src, dst) in enumerate(zip(srcs, dsts))]


def _tail_kernel(*refs, n_x, n_out, n_ctx_tiles):
    oa_ref, ob_ref, oc_ref = refs[:3]
    x_refs, refs = refs[3:3 + n_x], refs[3 + n_x:]
    (mod_ref, gpre0_ref, gpost0_ref, gpre1_ref, gpost1_ref,
     wg_ref, wb_ref, wo_ref, wfi_hbm, wfo_hbm) = refs[:10]
    o_refs, refs = refs[10:10 + n_out], refs[10 + n_out:]
    wfg_ref, wfu_ref, wfo_ref, sems = refs[:4]
    i = pl.program_id(0)
    first = i == 0
    is_ctx = i < n_ctx_tiles
    copies = _ffn_weight_copies(wfi_hbm, wfo_hbm, wfg_ref, wfu_ref, wfo_ref, sems)

    @pl.when(first)
    def _():
        for copy in copies:
            copy.start()

    if n_x == 1:
        x = x_refs[0][...]
    else:
        pick_ctx = jnp.full(x_refs[0].shape, jnp.where(is_ctx, 1, 0), jnp.int32) > 0
        x = jnp.where(pick_ctx, x_refs[0][...], x_refs[1][...])
    h = (_rms(x, gpre0_ref[...]) * (1.0 + mod_ref[1:2, :]) + mod_ref[0:1, :]).astype(BF16)
    gates = _dot(h, wg_ref[...])
    branches = [_dot(b_ref[...], wb_ref[k]) for k, b_ref in enumerate((oa_ref, ob_ref, oc_ref))]
    mix = None
    for k, branch in enumerate(branches):
        term = _sigmoid(gates[:, k * D_MODEL:(k + 1) * D_MODEL]) * branch
        mix = term if mix is None else mix + term
    y = _dot(mix.astype(BF16), wo_ref[...])
    x1 = x + mod_ref[2:3, :] * _rms(y, gpost0_ref[...])
    h2 = (_rms(x1, gpre1_ref[...]) * (1.0 + mod_ref[4:5, :]) + mod_ref[3:4, :]).astype(BF16)

    @pl.when(first)
    def _():
        for copy in copies:
            copy.wait()

    act = _silu(_dot(h2, wfg_ref[...])) * _dot(h2, wfu_ref[...])
    f = _dot(act.astype(BF16), wfo_ref[...])
    out = x1 + mod_ref[5:6, :] * _rms(f, gpost1_ref[...])
    if n_out == 1:
        o_refs[0][...] = out
    else:
        @pl.when(is_ctx)
        def _():
            o_refs[0][...] = out

        @pl.when(jnp.logical_not(is_ctx))
        def _():
            o_refs[1][...] = out


def _tail(oa, ob, oc, xs, mods, l, g_pre, g_post, w_gates, w_branch, w_out, w_ffn_in, w_ffn_out,
          *, tm, m_ctx, mod_row, split_out):
    m = oa.shape[0]
    n_ctx = m_ctx // tm
    o_spec = pl.BlockSpec((tm, BRANCH_DIM), lambda i: (i, 0))
    whole = pl.BlockSpec((tm, D_MODEL), lambda i: (i, 0))
    halves = [pl.BlockSpec((tm, D_MODEL), lambda i: (jnp.minimum(i, n_ctx - 1), 0)),
              pl.BlockSpec((tm, D_MODEL), lambda i: (jnp.maximum(i - n_ctx, 0), 0))]
    half_shapes = [jax.ShapeDtypeStruct((m_ctx, D_MODEL), F32),
                   jax.ShapeDtypeStruct((m - m_ctx, D_MODEL), F32)]
    gain_spec = lambda which: pl.BlockSpec((None, None, 1, D_MODEL), lambda i: (l, which, 0, 0))
    scratch = [
        pltpu.VMEM((D_MODEL, D_FF), BF16),
        pltpu.VMEM((D_MODEL, D_FF), BF16),
        pltpu.VMEM((D_FF, D_MODEL), BF16),
        pltpu.SemaphoreType.DMA((3,)),
    ]
    return pl.pallas_call(
        functools.partial(_tail_kernel, n_x=len(xs), n_out=2 if split_out else 1, n_ctx_tiles=n_ctx),
        grid=(m // tm,),
        in_specs=[o_spec, o_spec, o_spec] + (halves if len(xs) == 2 else [whole]) + [
            pl.BlockSpec((None, None, 6, D_MODEL), lambda i: (l, mod_row(i), 0, 0)),
            gain_spec(0), gain_spec(0), gain_spec(1), gain_spec(1),
            _resident((D_MODEL, N_GATES), lambda i: (0, 0)),
            _resident((N_BRANCH, BRANCH_DIM, D_MODEL), lambda i: (0, 0, 0)),
            _resident((D_MODEL, D_MODEL), lambda i: (0, 0)),
            pl.BlockSpec(memory_space=pl.ANY),
            pl.BlockSpec(memory_space=pl.ANY),
        ],
        out_specs=halves if split_out else whole,
        out_shape=half_shapes if split_out else jax.ShapeDtypeStruct((m, D_MODEL), F32),
        scratch_shapes=scratch,
        compiler_params=_cparams(("arbitrary",)),
        name="tail",
    )(oa, ob, oc, *xs, mods, g_pre, g_post, g_pre, g_post, w_gates, w_branch, w_out,
      w_ffn_in, w_ffn_out)


def _rope_tables(seq):
    t = np.arange(seq)
    row = (t // GRID_W).astype(np.float32)
    col = (t % GRID_W).astype(np.float32)
    n_pairs_axis = HEAD_DIM // 4
    inv = (np.float32(ROPE_BASE) ** (-np.arange(n_pairs_axis, dtype=np.float32) / n_pairs_axis)
           ).astype(np.float32)
    ang = np.concatenate([row[:, None] * inv, col[:, None] * inv], axis=-1).astype(np.float64)
    cos, sin = np.cos(ang), np.sin(ang)
    cos_l = np.tile(cos, (1, LANES // cos.shape[1]))
    sin_l = np.tile(np.concatenate([-sin, sin], axis=-1), (1, LANES // HEAD_DIM))
    return jnp.asarray(cos_l, F32), jnp.asarray(sin_l, F32)


def kernel(x_prompt, x_sample, cache_a_k, cache_a_v, cache_b_k, cache_b_v, c, c_ctx, w_ada, b_ada,
           norm_pre, norm_post, w_in, a_sink, b_rpb, w_branch, w_out, w_ffn_in, w_ffn_out):
    batch, seq, _ = x_prompt.shape
    dec_batch, dec_seq, _ = x_sample.shape
    past = cache_a_k.shape[2]
    assert dec_batch <= CTX_MOD_ROW and seq % A_BLOCK == 0 and dec_seq % A_BLOCK == 0

    cvec = jnp.concatenate(
        [c, c_ctx[None, :], jnp.zeros((MOD_ROWS - dec_batch - 1, D_MODEL), F32)], axis=0)
    assert w_in.shape[-1] == D_IN
    mods, w_proj_b, w_gates_b = _modulation(cvec, w_ada, b_ada, w_in)
    mods = mods.reshape(DEPTH, MOD_ROWS, 6, D_MODEL)
    w_branch_2d = w_branch.reshape(DEPTH, N_BRANCH * BRANCH_DIM, D_MODEL)
    g_pre = norm_pre.reshape(DEPTH, 2, 1, D_MODEL)
    g_post = norm_post.reshape(DEPTH, 2, 1, D_MODEL)
    rope_tabs = _rope_tables(dec_seq)
    cak = cache_a_k.reshape(dec_batch, DEPTH, past, KV_A)
    cav = cache_a_v.reshape(dec_batch, DEPTH, past, KV_A)
    cbk = cache_b_k.reshape(dec_batch, DEPTH, past, BRANCH_DIM)
    cbv = cache_b_v.reshape(dec_batch, DEPTH, past, BRANCH_DIM)

    n_dr, n_dc = b_rpb.shape[2:]
    rpb_pad = jnp.pad(b_rpb.astype(F32), ((0, 0), (0, 0), (0, (-n_dr) % 8), (0, LANES - n_dc)))

    tm_proj, tm_tail = 1024, 512
    m_ctx, m_lat = batch * seq, dec_batch * dec_seq
    m_all = m_ctx + m_lat
    assert m_ctx % tm_proj == 0 and m_lat % tm_proj == 0
    ctx_row = lambda i: CTX_MOD_ROW
    lat_row = lambda i: (i * tm_proj) // dec_seq
    tail_row = lambda i: jnp.where(i < m_ctx // tm_tail, CTX_MOD_ROW, (i * tm_tail - m_ctx) // dec_seq)

    xs = (x_prompt.reshape(m_ctx, D_MODEL), x_sample.reshape(m_lat, D_MODEL))
    caches = None
    for l in range(DEPTH):
        x_ctx, x_lat = (xs[0], xs[1]) if len(xs) == 2 else (xs[0], xs[0])
        lat_tile0 = 0 if len(xs) == 2 else m_ctx // tm_proj
        qx, oc, *rest = _proj(x_ctx, mods, l, g_pre, w_proj_b, m=m_ctx, x_tile0=0, tm=tm_proj,
                              mod_row=ctx_row, seq=seq, oc_rows=m_all, caches=caches,
                              cast_weights=(w_ffn_in,))
        caches, (w_ffn_in_b,) = rest[:4], rest[4:]
        oa, ob, w_branch_b, w_out_b, w_ffn_out_b = _attn_ctx(
            qx, caches, a_sink[l], l, seq, (w_branch_2d, w_out, w_ffn_out), m_all)
        qx, kv, oc = _proj(x_lat, mods, l, g_pre, w_proj_b, m=m_lat, x_tile0=lat_tile0, tm=tm_proj,
                           mod_row=lat_row, seq=dec_seq, oc_all=oc, rope_tabs=rope_tabs)
        last = l == DEPTH - 1
        w_gates_l = w_gates_b
        if last:
            oa, = _attn_a_lat(qx, qx, cak, cav, a_sink[l], l, dec_seq, oa)
        else:
            oa, w_proj_b, w_gates_b = _attn_a_lat(qx, qx, cak, cav, a_sink[l], l, dec_seq, oa,
                                                  w_in_next=(w_in, l + 1))
        ob = _attn_b_lat(qx, kv, cbk, cbv, rpb_pad, l, dec_seq, ob)
        out = _tail(oa, ob, oc, xs, mods, l, g_pre, g_post, w_gates_l,
                    w_branch_b.reshape(N_BRANCH, BRANCH_DIM, D_MODEL), w_out_b, w_ffn_in_b, w_ffn_out_b,
                    tm=tm_tail, m_ctx=m_ctx, mod_row=tail_row, split_out=last)
        xs = tuple(out) if last else (out,)
    y_prompt = xs[0].reshape(batch, seq, D_MODEL)
    y_sample = xs[1].reshape(dec_batch, dec_seq, D_MODEL)
    nak, nav, nbk, nbv = caches
    new_a_k = nak.reshape(batch, DEPTH, seq, A_KV_HEADS, HEAD_DIM)
    new_a_v = nav.reshape(batch, DEPTH, seq, A_KV_HEADS, HEAD_DIM)
    new_b_k = nbk.reshape(batch, DEPTH, seq, B_HEADS, HEAD_DIM)
    new_b_v = nbv.reshape(batch, DEPTH, seq, B_HEADS, HEAD_DIM)
    return (y_prompt, y_sample, new_a_k, new_a_v, new_b_k, new_b_v)
```

```python
import functools

import numpy as np
import jax
import jax.numpy as jnp
from jax import lax
from jax.experimental import pallas as pl
from jax.experimental.pallas import tpu as pltpu

F32 = jnp.float32
BF16 = jnp.bfloat16

D_MODEL = 1024
DEPTH = 2
GRID_W = 64
HEAD_DIM = 64
BRANCH_DIM = D_MODEL // 2
A_Q_HEADS = BRANCH_DIM // HEAD_DIM
A_KV_HEADS = A_Q_HEADS // 4
A_GROUP = A_Q_HEADS // A_KV_HEADS
A_BLOCK = 128
B_HEADS = BRANCH_DIM // HEAD_DIM
B_WIN_ROWS = 8
B_WIN_COLS = 16
C_GROUPS = 4
C_GROUP_DIM = BRANCH_DIM // C_GROUPS
N_BRANCH = 3
D_FF = -(-8 * D_MODEL // (3 * 256)) * 256
ROPE_BASE = 10000.0
NORM_EPS = 1e-6
NEG_INF = -1e30
SCALE = HEAD_DIM ** -0.5

LANES = 128
KV_A = A_KV_HEADS * HEAD_DIM
N_GATES = N_BRANCH * D_MODEL
W_QA = 0
W_KA = W_QA + BRANCH_DIM
W_VA = W_KA + KV_A
W_QB = W_VA + KV_A
W_KB = W_QB + BRANCH_DIM
W_VB = W_KB + BRANCH_DIM
W_UC = W_VB + BRANCH_DIM
W_GATES = W_UC + BRANCH_DIM
D_IN = W_GATES + N_GATES
QX_QA, QX_QB, QX_KVA = 0, BRANCH_DIM, 2 * BRANCH_DIM
KV_KB, KV_VB = 0, BRANCH_DIM
MOD_ROWS = 8
CTX_MOD_ROW = 4
VMEM_LIMIT = 56 * 1024 * 1024
BF16_SUBLANES = 16
CTX_SEQS_PER_STEP = 2


def _cparams(sem):
    return pltpu.CompilerParams(dimension_semantics=sem, vmem_limit_bytes=VMEM_LIMIT)


def _resident(shape, index_map):
    return pl.BlockSpec(shape, index_map, pipeline_mode=pl.Buffered(1))


def _rms(x, g):
    return x * lax.rsqrt(jnp.mean(x * x, axis=-1, keepdims=True) + NORM_EPS) * g


def _sigmoid(x):
    return 0.5 * jnp.tanh(0.5 * x) + 0.5


def _silu(x):
    half = 0.5 * x
    return half + half * jnp.tanh(half)


def _dot_t(a, b):
    return lax.dot_general(a, b, (((1,), (1,)), ((), ())), preferred_element_type=F32)


def _dot(a, b):
    return jnp.dot(a, b, preferred_element_type=F32)


def _scaled(q):
    assert np.log2(SCALE) == round(np.log2(SCALE))
    return q * jnp.asarray(SCALE, q.dtype)


def _low_lanes(shape):
    return lax.broadcasted_iota(jnp.int32, shape, len(shape) - 1) < HEAD_DIM


def _mod_kernel(cv_ref, w_ref, b_ref, win_ref, o_ref, wproj_ref, wgates_ref):
    a, b = _silu(cv_ref[...]), w_ref[...]
    a_hi, b_hi = a.astype(BF16), b.astype(BF16)
    a_lo = (a - a_hi.astype(F32)).astype(BF16)
    b_lo = (b - b_hi.astype(F32)).astype(BF16)
    o_ref[...] = _dot(a_hi, b_hi) + (_dot(a_lo, b_hi) + _dot(a_hi, b_lo)) + b_ref[...]
    @pl.when(pl.program_id(0) == 0)
    def _():
        _split_cast_w_in(win_ref, wproj_ref, wgates_ref)


def _split_cast_w_in(win_ref, wproj_ref, wgates_ref):
    wproj_ref[...] = win_ref[:, :W_GATES].astype(wproj_ref.dtype)
    wgates_ref[...] = win_ref[:, W_GATES:].astype(wgates_ref.dtype)


def _modulation(cvec, w_ada, b_ada, w_in):
    tn = 1536
    n = 6 * D_MODEL
    steps = n // tn
    rows = w_in.shape[1] // steps
    slab = lambda l, j: jnp.where(l == 0, j, steps - 1)
    return pl.pallas_call(
        _mod_kernel,
        grid=(DEPTH, steps),
        in_specs=[
            pl.BlockSpec((MOD_ROWS, D_MODEL), lambda l, j: (0, 0)),
            pl.BlockSpec((None, D_MODEL, tn), lambda l, j: (l, 0, j)),
            pl.BlockSpec((None, 1, tn), lambda l, j: (l, 0, j)),
            pl.BlockSpec((None, rows, D_IN), lambda l, j: (0, slab(l, j), 0)),
        ],
        out_specs=[pl.BlockSpec((None, MOD_ROWS, tn), lambda l, j: (l, 0, j)),
                   pl.BlockSpec((rows, W_GATES), lambda l, j: (slab(l, j), 0)),
                   pl.BlockSpec((rows, N_GATES), lambda l, j: (slab(l, j), 0))],
        out_shape=[jax.ShapeDtypeStruct((DEPTH, MOD_ROWS, n), F32),
                   jax.ShapeDtypeStruct((D_MODEL, W_GATES), BF16),
                   jax.ShapeDtypeStruct((D_MODEL, N_GATES), BF16)],
        compiler_params=_cparams(("arbitrary", "arbitrary")),
        name="modulation",
    )(cvec, w_ada, b_ada.reshape(DEPTH, 1, n), w_in)


def _rope_cols(v, cos, sin):
    first = (lax.broadcasted_iota(jnp.int32, (v.shape[0], LANES), 1) % HEAD_DIM) < HEAD_DIM // 2
    outs = []
    for c in range(v.shape[1] // LANES):
        vc = v[:, c * LANES:(c + 1) * LANES]
        partner = jnp.where(first, pltpu.roll(vc, LANES - HEAD_DIM // 2, 1),
                            pltpu.roll(vc, HEAD_DIM // 2, 1))
        outs.append(vc * cos + partner * sin)
    return outs[0] if len(outs) == 1 else jnp.concatenate(outs, axis=1)


def _with_rope(acc, lo, hi, cos_ref, sin_ref):
    parts = []
    if lo > 0:
        parts.append(acc[:, :lo])
    parts.append(_rope_cols(acc[:, lo:hi], cos_ref[...], sin_ref[...]))
    if hi < acc.shape[1]:
        parts.append(acc[:, hi:])
    return parts[0] if len(parts) == 1 else jnp.concatenate(parts, axis=1)


def _proj_kernel(*refs, latent, n_alias, n_cast):
    x_ref, mod_ref, g_ref, w_ref, bc_ref, bs_ref, pos_ref = refs[:7]
    if latent:
        cos_ref, sin_ref, _, qx_ref, kv_ref, oc_ref, h_scr = refs[7:]
    else:
        rest = refs[7 + n_alias:]
        cast_src, rest = rest[:n_cast], rest[n_cast:]
        qx_ref, oc_ref, ka_ref, va_ref, kb_ref, vb_ref = rest[:6]
        cast_dst, h_scr = rest[6:6 + n_cast], rest[6 + n_cast]
        for src, dst in zip(cast_src, cast_dst):
            dst[...] = src[...].astype(dst.dtype)
    j = pl.program_id(1)

    @pl.when(j == 0)
    def _():
        h = _rms(x_ref[...], g_ref[...]) * (1.0 + mod_ref[1:2, :]) + mod_ref[0:1, :]
        h_scr[...] = h.astype(BF16)
        acc = _dot(h_scr[...], w_ref[:, W_QA:W_KB])
        qa = acc[:, W_QA:W_QA + BRANCH_DIM]
        qb = acc[:, W_QB:W_QB + BRANCH_DIM]
        kva = acc[:, W_KA:W_KA + 2 * KV_A]
        if latent:
            qa = _rope_cols(qa, cos_ref[...], sin_ref[...])
            kva = _with_rope(kva, 0, KV_A, cos_ref, sin_ref)
            qx_ref[:, QX_KVA:QX_KVA + 2 * KV_A] = kva.astype(qx_ref.dtype)
        else:
            ka_ref[...] = kva[:, :KV_A].reshape(ka_ref.shape)
            va_ref[...] = kva[:, KV_A:].reshape(va_ref.shape)
        qx_ref[:, QX_QA:QX_QA + BRANCH_DIM] = qa.astype(qx_ref.dtype)
        qx_ref[:, QX_QB:QX_QB + BRANCH_DIM] = qb.astype(qx_ref.dtype)

    @pl.when(j == 1)
    def _():
        acc = _dot(h_scr[...], w_ref[:, W_KB:W_GATES])
        kb = acc[:, 0:BRANCH_DIM]
        vb = acc[:, W_VB - W_KB:W_VB - W_KB + BRANCH_DIM]
        uc = acc[:, W_UC - W_KB:W_UC - W_KB + BRANCH_DIM]
        if latent:
            kv_ref[:, KV_KB:KV_KB + BRANCH_DIM] = kb.astype(kv_ref.dtype)
            kv_ref[:, KV_VB:KV_VB + BRANCH_DIM] = vb.astype(kv_ref.dtype)
        else:
            kb_ref[...] = kb.reshape(kb_ref.shape)
            vb_ref[...] = vb.reshape(vb_ref.shape)
        _fourier_mix(uc.astype(BF16), bc_ref, bs_ref, pos_ref, oc_ref)


def _proj(x, mods, l, g_pre, w_in, *, m, x_tile0, tm, mod_row, seq, oc_rows=None, oc_all=None,
          rope_tabs=None, caches=None, cast_weights=()):
    assert tm % seq == 0
    latent = rope_tabs is not None
    bc, bs, pos = _dft_tables(seq)
    pair = 2 * C_GROUP_DIM
    in_specs = [
        pl.BlockSpec((tm, D_MODEL), lambda i, j: (i + x_tile0, 0)),
        pl.BlockSpec((None, None, 6, D_MODEL), lambda i, j: (l, mod_row(i), 0, 0)),
        pl.BlockSpec((None, None, 1, D_MODEL), lambda i, j: (l, 0, 0, 0)),
        _resident((D_MODEL, W_GATES), lambda i, j: (0, 0)),
        _resident((pair, pair), lambda i, j: (0, 0)),
        _resident((pair, pair), lambda i, j: (0, 0)),
        _resident((seq, 2 * seq), lambda i, j: (0, 0)),
    ]
    args = [x, mods, g_pre, w_in, bc, bs, pos]
    aliases = {}
    if latent:
        seq_tiles = seq // tm
        tab_spec = pl.BlockSpec((tm, LANES), lambda i, j: (i % seq_tiles, 0))
        in_specs += [tab_spec, tab_spec, pl.BlockSpec(memory_space=pl.ANY)]
        args += list(rope_tabs)
        widths = (2 * BRANCH_DIM + 2 * KV_A, 2 * BRANCH_DIM)
        aliases = {len(args): len(widths)}
        args.append(oc_all)
        oc_rows, oc_tile0 = oc_all.shape[0], (oc_all.shape[0] - m) // tm
        cache_specs, cache_shapes = [], []
    else:
        widths = (2 * BRANCH_DIM,)
        oc_tile0 = 0
        cache_widths = (KV_A, KV_A, BRANCH_DIM, BRANCH_DIM)
        if caches is not None:
            in_specs += [pl.BlockSpec(memory_space=pl.ANY)] * len(caches)
            aliases = {len(args) + k: len(widths) + 1 + k for k in range(len(caches))}
            args += list(caches)
        cache_specs = [pl.BlockSpec((tm // seq, None, seq, w), lambda i, j: (i, l, 0, 0))
                       for w in cache_widths]
        cache_shapes = [jax.ShapeDtypeStruct((m // seq, DEPTH, seq, w), F32) for w in cache_widths]
        steps = 2 * (m // tm)
        for w in cast_weights:
            _, rows, cols = w.shape
            assert rows % (steps * BF16_SUBLANES) == 0
            in_specs.append(pl.BlockSpec((None, rows // steps, cols), lambda i, j: (l, 2 * i + j, 0)))
            cache_specs.append(pl.BlockSpec((rows // steps, cols), lambda i, j: (2 * i + j, 0)))
            cache_shapes.append(jax.ShapeDtypeStruct((rows, cols), BF16))
            args.append(w)
    return pl.pallas_call(
        functools.partial(_proj_kernel, latent=latent, n_alias=len(aliases), n_cast=len(cast_weights)),
        grid=(m // tm, 2),
        in_specs=in_specs,
        out_specs=([pl.BlockSpec((tm, w), lambda i, j: (i, 0)) for w in widths]
                   + [pl.BlockSpec((tm, BRANCH_DIM), lambda i, j: (i + oc_tile0, 0))] + cache_specs),
        out_shape=([jax.ShapeDtypeStruct((m, w), BF16) for w in widths]
                   + [jax.ShapeDtypeStruct((oc_rows, BRANCH_DIM), BF16)] + cache_shapes),
        input_output_aliases=aliases,
        scratch_shapes=[pltpu.VMEM((tm, D_MODEL), BF16)],
        compiler_params=_cparams(("parallel", "arbitrary")),
        name="proj_lat" if latent else "proj_ctx",
    )(*args)


def _with_ones(v):
    return jnp.concatenate([v, jnp.ones_like(v)], axis=1)


def _softmax_pv(parts, sink=None):
    m = parts[0][0].max(axis=-1, keepdims=True)
    for s, _ in parts[1:]:
        m = jnp.maximum(m, s.max(axis=-1, keepdims=True))
    if sink is not None:
        m = jnp.maximum(m, sink)
    ones_half = parts[0][1].shape[1] == 2 * LANES
    acc = None
    den = None
    for s, v in parts:
        e = jnp.exp(s - m)
        if not ones_half:
            d = e.sum(axis=-1, keepdims=True)
            den = d if den is None else den + d
        o = _dot(e.astype(BF16), v)
        acc = o if acc is None else acc + o
    if ones_half:
        acc, den = acc[:, :LANES], acc[:, LANES:]
    if sink is not None:
        den = den + jnp.exp(sink - m)
    return acc * (1.0 / den)


def _kv_head_variants(x2):
    low = _low_lanes(x2.shape)
    xr = pltpu.roll(x2, HEAD_DIM, 1)
    zero = jnp.zeros_like(x2)
    return [
        [jnp.where(low, x2, zero).astype(BF16), jnp.where(low, zero, xr).astype(BF16)],
        [jnp.where(low, xr, zero).astype(BF16), jnp.where(low, zero, x2).astype(BF16)],
    ]


def _attn_ctx_kernel(sink_ref, qa_ref, qb_ref, ka_ref, va_ref, kb_ref, vb_ref, *rest, n_weights):
    w_src, (oa_ref, ob_ref), w_dst = rest[:n_weights], rest[n_weights:n_weights + 2], rest[n_weights + 2:]
    for src, dst in zip(w_src, w_dst):
        dst[...] = src[...].astype(dst.dtype)
    seq = ka_ref.shape[1]
    chunks = [slice(j * LANES, (j + 1) * LANES) for j in range(A_Q_HEADS // 2)]
    low = _low_lanes((seq, LANES))
    for s in range(ka_ref.shape[0]):
        rows = slice(s * seq, (s + 1) * seq)
        ka = [jnp.concatenate(v, axis=0) for v in _kv_head_variants(ka_ref[s])]
        va = [jnp.concatenate(v, axis=0) for v in _kv_head_variants(va_ref[s])]
        work = []
        for j, cols in enumerate(chunks):
            g = (2 * j) // A_GROUP
            s2 = _dot_t(_scaled(qa_ref[rows, cols]), ka[g])
            work.append((oa_ref, cols, s2, va[g], (sink_ref[2 * j], sink_ref[2 * j + 1])))
        for cols in chunks:
            kc, vc = kb_ref[s, :, cols], vb_ref[s, :, cols]
            zero = jnp.zeros_like(kc)
            k2 = jnp.concatenate([jnp.where(low, kc, zero), jnp.where(low, zero, kc)], axis=0).astype(BF16)
            v2 = jnp.concatenate([jnp.where(low, vc, zero), jnp.where(low, zero, vc)], axis=0).astype(BF16)
            work.append((ob_ref, cols, _dot_t(_scaled(qb_ref[rows, cols]), k2), v2, None))
        for o_ref, cols, s2, v2, sinks in work:
            o_ref[rows, cols] = _pair_softmax_pv(s2, v2, sinks).astype(o_ref.dtype)


def _pair_softmax_pv(s2, v2, sinks=None):
    tk = s2.shape[1] // 2
    es, extra = [], []
    for p in range(2):
        s = s2[:, p * tk:(p + 1) * tk]
        m = s.max(axis=-1, keepdims=True)
        if sinks is not None:
            m = jnp.maximum(m, sinks[p])
            extra.append(jnp.exp(sinks[p] - m))
        es.append(jnp.exp(s - m).astype(BF16))
    lane_half = lax.broadcasted_iota(jnp.int32, v2.shape, 1) // HEAD_DIM
    row_half = lax.broadcasted_iota(jnp.int32, v2.shape, 0) // tk
    ones2 = jnp.where(lane_half == row_half, 1.0, 0.0).astype(v2.dtype)
    acc = _dot(jnp.concatenate(es, axis=1), jnp.concatenate([v2, ones2], axis=1))
    den = acc[:, LANES:]
    if sinks is not None:
        den = den + jnp.where(_low_lanes(den.shape), extra[0], extra[1])
    return acc[:, :LANES] * (1.0 / den)


def _attn_ctx(qx, caches, sink, l, seq, weights, out_rows):
    m = qx.shape[0]
    per_step = CTX_SEQS_PER_STEP
    rows_step = per_step * seq
    steps = m // rows_step
    out = jax.ShapeDtypeStruct((out_rows, BRANCH_DIM), BF16)
    cache_specs = [pl.BlockSpec((per_step, None, seq, c.shape[-1]), lambda b: (b, l, 0, 0)) for c in caches]
    w_in_specs, w_out_specs, w_shapes = [], [], []
    for w in weights:
        _, rows, cols = w.shape
        assert rows % (steps * BF16_SUBLANES) == 0
        w_in_specs.append(pl.BlockSpec((None, rows // steps, cols), lambda b: (l, b, 0)))
        w_out_specs.append(pl.BlockSpec((rows // steps, cols), lambda b: (b, 0)))
        w_shapes.append(jax.ShapeDtypeStruct((rows, cols), BF16))
    return pl.pallas_call(
        functools.partial(_attn_ctx_kernel, n_weights=len(weights)),
        grid=(steps,),
        in_specs=[
            pl.BlockSpec(memory_space=pltpu.SMEM),
            pl.BlockSpec((rows_step, BRANCH_DIM), lambda b: (b, QX_QA // BRANCH_DIM)),
            pl.BlockSpec((rows_step, BRANCH_DIM), lambda b: (b, QX_QB // BRANCH_DIM)),
        ] + cache_specs + w_in_specs,
        out_specs=[pl.BlockSpec((rows_step, BRANCH_DIM), lambda b: (b, 0))] * 2 + w_out_specs,
        out_shape=[out, out] + w_shapes,
        compiler_params=_cparams(("parallel",)),
        name="attn_ctx",
    )(sink, qx, qx, *caches, *weights)


def _attn_a_lat_kernel(sink_ref, q_ref, kv_ref, ck_ref, cv_ref, _, *rest, n_blocks, cast_w_in):
    if cast_w_in:
        win_ref, o_ref, wproj_ref, wgates_ref = rest[:4]
        _split_cast_w_in(win_ref, wproj_ref, wgates_ref)
        rest = rest[4:]
    else:
        o_ref, rest = rest[0], rest[1:]
    qh_scr, kp_scr, vp_scr, ckb_scr, cvb_scr, mask_scr = rest
    seq = n_blocks * A_BLOCK
    band = 3 * A_BLOCK
    low_seq = _low_lanes((seq, LANES))
    for h in range(A_Q_HEADS):
        j, p, g = h // 2, h % 2, h // A_GROUP
        x = q_ref[:, j * LANES:(j + 1) * LANES].astype(F32) * SCALE
        if p != g:
            x = pltpu.roll(x, HEAD_DIM, 1)
        qh_scr[h] = jnp.where(low_seq if g == 0 else jnp.logical_not(low_seq), x, 0.0).astype(BF16)
    pad = jnp.zeros((A_BLOCK, LANES), BF16)
    for scr, col in ((kp_scr, 0), (vp_scr, KV_A)):
        scr[0:A_BLOCK, :] = pad
        scr[A_BLOCK:A_BLOCK + seq, :] = kv_ref[:, col:col + KV_A]
        scr[A_BLOCK + seq:2 * A_BLOCK + seq, :] = pad
    ckb_scr[...] = ck_ref[...].astype(BF16)
    cvb_scr[...] = cv_ref[...].astype(BF16)

    rows = A_GROUP * A_BLOCK
    row = lax.broadcasted_iota(jnp.int32, (rows, band), 0)
    qi = row % A_BLOCK
    kj = lax.broadcasted_iota(jnp.int32, (rows, band), 1)
    head_row = lax.broadcasted_iota(jnp.int32, (rows, 1), 0) // A_BLOCK
    low_blk = _low_lanes((A_BLOCK, LANES))
    @pl.when(pl.program_id(0) == 0)
    def _():
        for case, (no_prev, no_next) in enumerate(((1, 0), (0, 0), (0, 1))):
            valid = (kj >= qi + no_prev * (A_BLOCK - qi)) & (kj <= 2 * A_BLOCK + qi - no_next * (qi + 1))
            mask_scr[case] = jnp.where(valid, 0.0, NEG_INF)

    def body(n, carry):
        start = pl.multiple_of(n * A_BLOCK, A_BLOCK)
        case = jnp.where(n > 0, 1, 0) + jnp.where(n < n_blocks - 1, 0, 1)
        scores = []
        for g in range(A_KV_HEADS):
            q = jnp.concatenate(
                [qh_scr[A_GROUP * g + i, pl.ds(start, A_BLOCK), :] for i in range(A_GROUP)], axis=0)
            s_band = _dot_t(q, kp_scr[pl.ds(start, band), :]) + mask_scr[case]
            scores.append((s_band, _dot_t(q, ckb_scr[...])))
        outs = []
        for g, (s_band, s_ctx) in enumerate(scores):
            sink = jnp.full((rows, 1), sink_ref[A_GROUP * g], F32)
            for i in range(1, A_GROUP):
                sink = jnp.where(head_row == i, sink_ref[A_GROUP * g + i], sink)
            outs.append(_softmax_pv([(s_band, vp_scr[pl.ds(start, band), :]), (s_ctx, cvb_scr[...])],
                                    sink=sink))
        for j in range(A_Q_HEADS // 2):
            halves = []
            for p in range(2):
                h = 2 * j + p
                g, i = h // A_GROUP, h % A_GROUP
                t = outs[g][i * A_BLOCK:(i + 1) * A_BLOCK]
                halves.append(t if p == g else pltpu.roll(t, HEAD_DIM, 1))
            o_ref[pl.ds(start, A_BLOCK), j * LANES:(j + 1) * LANES] = (
                jnp.where(low_blk, halves[0], halves[1]).astype(o_ref.dtype))
        return carry

    lax.fori_loop(0, n_blocks, body, 0, unroll=2)


def _attn_a_lat(main, kv, cache_k, cache_v, sink, l, seq, oa_all, w_in_next=None):
    m = main.shape[0]
    nb = seq // A_BLOCK
    assert nb >= 2
    n_seq = m // seq
    past = cache_k.shape[2]
    first = (oa_all.shape[0] - m) // seq
    cache_spec = pl.BlockSpec((None, None, past, KV_A), lambda b: (b, l, 0, 0))
    in_specs = [
        pl.BlockSpec(memory_space=pltpu.SMEM),
        pl.BlockSpec((seq, BRANCH_DIM), lambda b: (b, QX_QA // BRANCH_DIM)),
        pl.BlockSpec((seq, 2 * KV_A), lambda b: (b, QX_KVA // (2 * KV_A))),
        cache_spec, cache_spec,
        pl.BlockSpec(memory_space=pl.ANY),
    ]
    args = [sink, main, kv, cache_k, cache_v, oa_all]
    out_specs = [pl.BlockSpec((seq, BRANCH_DIM), lambda b: (b + first, 0))]
    out_shape = [jax.ShapeDtypeStruct(oa_all.shape, BF16)]
    if w_in_next is not None:
        w_in, layer = w_in_next
        rows = w_in.shape[1] // n_seq
        assert w_in.shape[1] % (n_seq * BF16_SUBLANES) == 0
        in_specs.append(pl.BlockSpec((None, rows, D_IN), lambda b: (layer, b, 0)))
        args.append(w_in)
        out_specs += [pl.BlockSpec((rows, W_GATES), lambda b: (b, 0)),
                      pl.BlockSpec((rows, N_GATES), lambda b: (b, 0))]
        out_shape += [jax.ShapeDtypeStruct((D_MODEL, W_GATES), BF16),
                      jax.ShapeDtypeStruct((D_MODEL, N_GATES), BF16)]
    return pl.pallas_call(
        functools.partial(_attn_a_lat_kernel, n_blocks=nb, cast_w_in=w_in_next is not None),
        grid=(n_seq,),
        in_specs=in_specs,
        out_specs=out_specs,
        out_shape=out_shape,
        input_output_aliases={5: 0},
        scratch_shapes=[
            pltpu.VMEM((A_Q_HEADS, seq, LANES), BF16),
            pltpu.VMEM((seq + 2 * A_BLOCK, LANES), BF16),
            pltpu.VMEM((seq + 2 * A_BLOCK, LANES), BF16),
            pltpu.VMEM((past, LANES), BF16),
            pltpu.VMEM((past, LANES), BF16),
            pltpu.VMEM((3, A_GROUP * A_BLOCK, 3 * A_BLOCK), F32),
        ],
        compiler_params=_cparams(("arbitrary",)),
        name="attn_a_lat",
    )(*args)


def _attn_b_lat_kernel(q_ref, k_ref, v_ref, ck_ref, cv_ref, rpb_ref, _, o_ref, ckb_scr, vx_scr, cvx_scr,
                       bias_ref, *, rows, kr):
    @pl.when(pl.program_id(0) == 0)
    def _():
        _fill_neighbourhood_bias(rpb_ref, bias_ref, kr=kr)

    half = kr // 2
    n_loc = kr * GRID_W
    low_q = _low_lanes((GRID_W, LANES))
    chunks = [slice(j * LANES, (j + 1) * LANES) for j in range(B_HEADS // 2)]
    ckb_scr[...] = ck_ref[...].astype(BF16)
    for j, cols in enumerate(chunks):
        vx_scr[j] = _with_ones(v_ref[:, cols])
        cvx_scr[j] = _with_ones(cv_ref[:, cols].astype(BF16))
    rows_per_iter = 4
    assert rows % rows_per_iter == 0

    def body(it, carry):
        work = []
        for u in range(rows_per_iter):
            r = it * rows_per_iter + u
            r0 = jnp.clip(r - half, 0, rows - kr)
            q0 = pl.multiple_of(r * GRID_W, GRID_W)
            start = pl.multiple_of(r0 * GRID_W, GRID_W)
            for j, cols in enumerate(chunks):
                qc = _scaled(q_ref[pl.ds(q0, GRID_W), cols])
                zero = jnp.zeros_like(qc)
                qs = jnp.concatenate([jnp.where(low_q, qc, zero), jnp.where(low_q, zero, qc)], axis=0)
                s_loc = _dot_t(qs, k_ref[pl.ds(start, n_loc), cols]) + bias_ref[r - r0, j]
                work.append((q0, start, j, cols, s_loc, _dot_t(qs, ckb_scr[:, cols])))
        for q0, start, j, cols, s_loc, s_ctx in work:
            o2 = _softmax_pv([(s_loc, vx_scr[j, pl.ds(start, n_loc), :]), (s_ctx, cvx_scr[j])])
            o = jnp.where(low_q, o2[:GRID_W], o2[GRID_W:])
            o_ref[pl.ds(q0, GRID_W), cols] = o.astype(o_ref.dtype)
        return carry

    lax.fori_loop(0, rows // rows_per_iter, body, 0)


def _attn_b_lat(main, kv, cache_k, cache_v, rpb_pad, l, seq, ob_all):
    m = main.shape[0]
    rows = seq // GRID_W
    kr = min(B_WIN_ROWS, rows)
    assert kr % 2 == 0 and 2 * GRID_W == LANES
    n_seq = m // seq
    past = cache_k.shape[2]
    first = (ob_all.shape[0] - m) // seq
    cache_spec = pl.BlockSpec((None, None, past, BRANCH_DIM), lambda b: (b, l, 0, 0))
    return pl.pallas_call(
        functools.partial(_attn_b_lat_kernel, rows=rows, kr=kr),
        grid=(n_seq,),
        in_specs=[
            pl.BlockSpec((seq, BRANCH_DIM), lambda b: (b, QX_QB // BRANCH_DIM)),
            pl.BlockSpec((seq, BRANCH_DIM), lambda b: (b, KV_KB // BRANCH_DIM)),
            pl.BlockSpec((seq, BRANCH_DIM), lambda b: (b, KV_VB // BRANCH_DIM)),
            cache_spec, cache_spec,
            _resident((None,) + rpb_pad.shape[1:], lambda b: (l, 0, 0, 0)),
            pl.BlockSpec(memory_space=pl.ANY),
        ],
        out_specs=pl.BlockSpec((seq, BRANCH_DIM), lambda b: (b + first, 0)),
        out_shape=jax.ShapeDtypeStruct(ob_all.shape, BF16),
        input_output_aliases={6: 0},
        scratch_shapes=[
            pltpu.VMEM((past, BRANCH_DIM), BF16),
            pltpu.VMEM((B_HEADS // 2, seq, 2 * LANES), BF16),
            pltpu.VMEM((B_HEADS // 2, past, 2 * LANES), BF16),
            pltpu.VMEM((kr, B_HEADS // 2, 2 * GRID_W, kr * GRID_W), F32),
        ],
        compiler_params=_cparams(("arbitrary",)),
        name="attn_b_lat",
    )(main, kv, kv, cache_k, cache_v, rpb_pad, ob_all)


def _fill_neighbourhood_bias(rpb_ref, bias_scr, *, kr):
    shape = (GRID_W, LANES)
    c = lax.broadcasted_iota(jnp.int32, shape, 0)
    lane = lax.broadcasted_iota(jnp.int32, shape, 1)
    c2 = lane % GRID_W
    ws = jnp.clip(c - B_WIN_COLS // 2, 0, GRID_W - B_WIN_COLS)
    ok = (c2 >= ws) & (c2 < ws + B_WIN_COLS)
    low = lane < GRID_W

    def toeplitz(h, dr, lane0):
        row = jnp.broadcast_to(rpb_ref[h, dr:dr + 1, :], shape)
        return pltpu.roll(row, (lane0 - (B_WIN_COLS - 1)) % LANES, 1, stride=1, stride_axis=0)

    for h in range(B_HEADS):
        rows = slice(h % 2 * GRID_W, (h % 2 + 1) * GRID_W)
        pairs = {}
        for v in range(kr):
            for m in range(0, kr, 2):
                dr = m - v + B_WIN_ROWS - 1
                if dr not in pairs:
                    pair = jnp.where(low, toeplitz(h, dr, 0), toeplitz(h, dr + 1, GRID_W))
                    pairs[dr] = jnp.where(ok, pair, NEG_INF)
                bias_scr[v, h // 2, rows, m * GRID_W:(m + 2) * GRID_W] = pairs[dr]


def _dft_tables(seq):
    cd = C_GROUP_DIM
    kc = (np.arange(cd)[:, None] * np.arange(cd)[None, :]) % cd
    ang_c = 2.0 * np.pi * kc / cd
    eye2 = np.eye(2)
    bd_cos = np.kron(eye2, np.cos(ang_c))
    bd_sin = np.kron(eye2, np.sin(ang_c))
    kt = (np.arange(seq)[:, None] * np.arange(seq)[None, :]) % seq
    ang_t = 2.0 * np.pi * kt / seq
    norm = 1.0 / np.sqrt(float(seq * cd))
    pos = np.concatenate([np.cos(ang_t), -np.sin(ang_t)], axis=1) * norm
    return (jnp.asarray(bd_cos, F32).astype(BF16), jnp.asarray(bd_sin, F32).astype(BF16),
            jnp.asarray(pos, F32).astype(BF16))


def _fourier_mix(u, bc_ref, bs_ref, pos_ref, o_ref):
    pair = 2 * C_GROUP_DIM
    seq = pos_ref.shape[0]
    uc, us = [], []
    for p in range(BRANCH_DIM // pair):
        up = u[:, p * pair:(p + 1) * pair]
        uc.append(_dot(up, bc_ref[...]))
        us.append(_dot(up, bs_ref[...]))
    zc = jnp.concatenate(uc, axis=1).astype(BF16)
    zs = jnp.concatenate(us, axis=1).astype(BF16)
    for s in range(u.shape[0] // seq):
        rows = slice(s * seq, (s + 1) * seq)
        z = jnp.concatenate([zc[rows], zs[rows]], axis=0)
        o_ref[rows, :] = _dot(pos_ref[...], z).astype(o_ref.dtype)


def _ffn_weight_copies(wfi_hbm, wfo_hbm, wfg_ref, wfu_ref, wfo_ref, sems):
    srcs = (wfi_hbm.at[:, 0:D_FF], wfi_hbm.at[:, D_FF:2 * D_FF], wfo_hbm)
    dsts = (wfg_ref, wfu_ref, wfo_ref)
    return [pltpu.make_async_copy(src, dst, sems.at[k]) for k, (src, dst) in enumerate(zip(srcs, dsts))]


def _tail_kernel(*refs, n_x, n_out, n_ctx_tiles):
    oa_ref, ob_ref, oc_ref = refs[:3]
    x_refs, refs = refs[3:3 + n_x], refs[3 + n_x:]
    (mod_ref, gpre0_ref, gpost0_ref, gpre1_ref, gpost1_ref,
     wg_ref, wb_ref, wo_ref, wfi_hbm, wfo_hbm) = refs[:10]
    o_refs, refs = refs[10:10 + n_out], refs[10 + n_out:]
    wfg_ref, wfu_ref, wfo_ref, sems = refs[:4]
    i = pl.program_id(0)
    first = i == 0
    is_ctx = i < n_ctx_tiles
    copies = _ffn_weight_copies(wfi_hbm, wfo_hbm, wfg_ref, wfu_ref, wfo_ref, sems)

    @pl.when(first)
    def _():
        for copy in copies:
            copy.start()

    if n_x == 1:
        x = x_refs[0][...]
    else:
        pick_ctx = jnp.full(x_refs[0].shape, jnp.where(is_ctx, 1, 0), jnp.int32) > 0
        x = jnp.where(pick_ctx, x_refs[0][...], x_refs[1][...])
    h = (_rms(x, gpre0_ref[...]) * (1.0 + mod_ref[1:2, :]) + mod_ref[0:1, :]).astype(BF16)
    gates = _dot(h, wg_ref[...])
    branches = [_dot(b_ref[...], wb_ref[k]) for k, b_ref in enumerate((oa_ref, ob_ref, oc_ref))]
    mix = None
    for k, branch in enumerate(branches):
        term = _sigmoid(gates[:, k * D_MODEL:(k + 1) * D_MODEL]) * branch
        mix = term if mix is None else mix + term
    y = _dot(mix.astype(BF16), wo_ref[...])
    x1 = x + mod_ref[2:3, :] * _rms(y, gpost0_ref[...])
    h2 = (_rms(x1, gpre1_ref[...]) * (1.0 + mod_ref[4:5, :]) + mod_ref[3:4, :]).astype(BF16)

    @pl.when(first)
    def _():
        for copy in copies:
            copy.wait()

    act = _silu(_dot(h2, wfg_ref[...])) * _dot(h2, wfu_ref[...])
    f = _dot(act.astype(BF16), wfo_ref[...])
    out = x1 + mod_ref[5:6, :] * _rms(f, gpost1_ref[...])
    if n_out == 1:
        o_refs[0][...] = out
    else:
        @pl.when(is_ctx)
        def _():
            o_refs[0][...] = out

        @pl.when(jnp.logical_not(is_ctx))
        def _():
            o_refs[1][...] = out


def _tail(oa, ob, oc, xs, mods, l, g_pre, g_post, w_gates, w_branch, w_out, w_ffn_in, w_ffn_out,
          *, tm, m_ctx, mod_row, split_out):
    m = oa.shape[0]
    n_ctx = m_ctx // tm
    o_spec = pl.BlockSpec((tm, BRANCH_DIM), lambda i: (i, 0))
    whole = pl.BlockSpec((tm, D_MODEL), lambda i: (i, 0))
    halves = [pl.BlockSpec((tm, D_MODEL), lambda i: (jnp.minimum(i, n_ctx - 1), 0)),
              pl.BlockSpec((tm, D_MODEL), lambda i: (jnp.maximum(i - n_ctx, 0), 0))]
    half_shapes = [jax.ShapeDtypeStruct((m_ctx, D_MODEL), F32),
                   jax.ShapeDtypeStruct((m - m_ctx, D_MODEL), F32)]
    gain_spec = lambda which: pl.BlockSpec((None, None, 1, D_MODEL), lambda i: (l, which, 0, 0))
    scratch = [
        pltpu.VMEM((D_MODEL, D_FF), BF16),
        pltpu.VMEM((D_MODEL, D_FF), BF16),
        pltpu.VMEM((D_FF, D_MODEL), BF16),
        pltpu.SemaphoreType.DMA((3,)),
    ]
    return pl.pallas_call(
        functools.partial(_tail_kernel, n_x=len(xs), n_out=2 if split_out else 1, n_ctx_tiles=n_ctx),
        grid=(m // tm,),
        in_specs=[o_spec, o_spec, o_spec] + (halves if len(xs) == 2 else [whole]) + [
            pl.BlockSpec((None, None, 6, D_MODEL), lambda i: (l, mod_row(i), 0, 0)),
            gain_spec(0), gain_spec(0), gain_spec(1), gain_spec(1),
            _resident((D_MODEL, N_GATES), lambda i: (0, 0)),
            _resident((N_BRANCH, BRANCH_DIM, D_MODEL), lambda i: (0, 0, 0)),
            _resident((D_MODEL, D_MODEL), lambda i: (0, 0)),
            pl.BlockSpec(memory_space=pl.ANY),
            pl.BlockSpec(memory_space=pl.ANY),
        ],
        out_specs=halves if split_out else whole,
        out_shape=half_shapes if split_out else jax.ShapeDtypeStruct((m, D_MODEL), F32),
        scratch_shapes=scratch,
        compiler_params=_cparams(("arbitrary",)),
        name="tail",
    )(oa, ob, oc, *xs, mods, g_pre, g_post, g_pre, g_post, w_gates, w_branch, w_out,
      w_ffn_in, w_ffn_out)


def _rope_tables(seq):
    t = np.arange(seq)
    row = (t // GRID_W).astype(np.float32)
    col = (t % GRID_W).astype(np.float32)
    n_pairs_axis = HEAD_DIM // 4
    inv = (np.float32(ROPE_BASE) ** (-np.arange(n_pairs_axis, dtype=np.float32) / n_pairs_axis)
           ).astype(np.float32)
    ang = np.concatenate([row[:, None] * inv, col[:, None] * inv], axis=-1).astype(np.float64)
    cos, sin = np.cos(ang), np.sin(ang)
    cos_l = np.tile(cos, (1, LANES // cos.shape[1]))
    sin_l = np.tile(np.concatenate([-sin, sin], axis=-1), (1, LANES // HEAD_DIM))
    return jnp.asarray(cos_l, F32), jnp.asarray(sin_l, F32)


def kernel(x_prompt, x_sample, cache_a_k, cache_a_v, cache_b_k, cache_b_v, c, c_ctx, w_ada, b_ada,
           norm_pre, norm_post, w_in, a_sink, b_rpb, w_branch, w_out, w_ffn_in, w_ffn_out):
    batch, seq, _ = x_prompt.shape
    dec_batch, dec_seq, _ = x_sample.shape
    past = cache_a_k.shape[2]
    assert dec_batch <= CTX_MOD_ROW and seq % A_BLOCK == 0 and dec_seq % A_BLOCK == 0

    cvec = jnp.concatenate(
        [c, c_ctx[None, :], jnp.zeros((MOD_ROWS - dec_batch - 1, D_MODEL), F32)], axis=0)
    assert w_in.shape[-1] == D_IN
    mods, w_proj_b, w_gates_b = _modulation(cvec, w_ada, b_ada, w_in)
    mods = mods.reshape(DEPTH, MOD_ROWS, 6, D_MODEL)
    w_branch_2d = w_branch.reshape(DEPTH, N_BRANCH * BRANCH_DIM, D_MODEL)
    g_pre = norm_pre.reshape(DEPTH, 2, 1, D_MODEL)
    g_post = norm_post.reshape(DEPTH, 2, 1, D_MODEL)
    rope_tabs = _rope_tables(dec_seq)
    cak = cache_a_k.reshape(dec_batch, DEPTH, past, KV_A)
    cav = cache_a_v.reshape(dec_batch, DEPTH, past, KV_A)
    cbk = cache_b_k.reshape(dec_batch, DEPTH, past, BRANCH_DIM)
    cbv = cache_b_v.reshape(dec_batch, DEPTH, past, BRANCH_DIM)

    n_dr, n_dc = b_rpb.shape[2:]
    rpb_pad = jnp.pad(b_rpb.astype(F32), ((0, 0), (0, 0), (0, (-n_dr) % 8), (0, LANES - n_dc)))

    tm_proj, tm_tail = 1024, 512
    m_ctx, m_lat = batch * seq, dec_batch * dec_seq
    m_all = m_ctx + m_lat
    assert m_ctx % tm_proj == 0 and m_lat % tm_proj == 0
    ctx_row = lambda i: CTX_MOD_ROW
    lat_row = lambda i: (i * tm_proj) // dec_seq
    tail_row = lambda i: jnp.where(i < m_ctx // tm_tail, CTX_MOD_ROW, (i * tm_tail - m_ctx) // dec_seq)

    xs = (x_prompt.reshape(m_ctx, D_MODEL), x_sample.reshape(m_lat, D_MODEL))
    caches = None
    for l in range(DEPTH):
        x_ctx, x_lat = (xs[0], xs[1]) if len(xs) == 2 else (xs[0], xs[0])
        lat_tile0 = 0 if len(xs) == 2 else m_ctx // tm_proj
        qx, oc, *rest = _proj(x_ctx, mods, l, g_pre, w_proj_b, m=m_ctx, x_tile0=0, tm=tm_proj,
                              mod_row=ctx_row, seq=seq, oc_rows=m_all, caches=caches,
                              cast_weights=(w_ffn_in,))
        caches, (w_ffn_in_b,) = rest[:4], rest[4:]
        oa, ob, w_branch_b, w_out_b, w_ffn_out_b = _attn_ctx(
            qx, caches, a_sink[l], l, seq, (w_branch_2d, w_out, w_ffn_out), m_all)
        qx, kv, oc = _proj(x_lat, mods, l, g_pre, w_proj_b, m=m_lat, x_tile0=lat_tile0, tm=tm_proj,
                           mod_row=lat_row, seq=dec_seq, oc_all=oc, rope_tabs=rope_tabs)
        last = l == DEPTH - 1
        w_gates_l = w_gates_b
        if last:
            oa, = _attn_a_lat(qx, qx, cak, cav, a_sink[l], l, dec_seq, oa)
        else:
            oa, w_proj_b, w_gates_b = _attn_a_lat(qx, qx, cak, cav, a_sink[l], l, dec_seq, oa,
                                                  w_in_next=(w_in, l + 1))
        ob = _attn_b_lat(qx, kv, cbk, cbv, rpb_pad, l, dec_seq, ob)
        out = _tail(oa, ob, oc, xs, mods, l, g_pre, g_post, w_gates_l,
                    w_branch_b.reshape(N_BRANCH, BRANCH_DIM, D_MODEL), w_out_b, w_ffn_in_b, w_ffn_out_b,
                    tm=tm_tail, m_ctx=m_ctx, mod_row=tail_row, split_out=last)
        xs = tuple(out) if last else (out,)
    y_prompt = xs[0].reshape(batch, seq, D_MODEL)
    y_sample = xs[1].reshape(dec_batch, dec_seq, D_MODEL)
    nak, nav, nbk, nbv = caches
    new_a_k = nak.reshape(batch, DEPTH, seq, A_KV_HEADS, HEAD_DIM)
    new_a_v = nav.reshape(batch, DEPTH, seq, A_KV_HEADS, HEAD_DIM)
    new_b_k = nbk.reshape(batch, DEPTH, seq, B_HEADS, HEAD_DIM)
    new_b_v = nbv.reshape(batch, DEPTH, seq, B_HEADS, HEAD_DIM)
    return (y_prompt, y_sample, new_a_k, new_a_v, new_b_k, new_b_v)
```

```python
import functools

import numpy as np
import jax
import jax.numpy as jnp
from jax import lax
from jax.experimental import pallas as pl
from jax.experimental.pallas import tpu as pltpu

F32 = jnp.float32
BF16 = jnp.bfloat16

D_MODEL = 1024
DEPTH = 2
GRID_W = 64
HEAD_DIM = 64
BRANCH_DIM = D_MODEL // 2
A_Q_HEADS = BRANCH_DIM // HEAD_DIM
A_KV_HEADS = A_Q_HEADS // 4
A_GROUP = A_Q_HEADS // A_KV_HEADS
A_BLOCK = 128
B_HEADS = BRANCH_DIM // HEAD_DIM
B_WIN_ROWS = 8
B_WIN_COLS = 16
C_GROUPS = 4
C_GROUP_DIM = BRANCH_DIM // C_GROUPS
N_BRANCH = 3
D_FF = -(-8 * D_MODEL // (3 * 256)) * 256
ROPE_BASE = 10000.0
NORM_EPS = 1e-6
NEG_INF = -1e30
SCALE = HEAD_DIM ** -0.5

LANES = 128
KV_A = A_KV_HEADS * HEAD_DIM
N_GATES = N_BRANCH * D_MODEL
W_QA = 0
W_KA = W_QA + BRANCH_DIM
W_VA = W_KA + KV_A
W_QB = W_VA + KV_A
W_KB = W_QB + BRANCH_DIM
W_VB = W_KB + BRANCH_DIM
W_UC = W_VB + BRANCH_DIM
W_GATES = W_UC + BRANCH_DIM
D_IN = W_GATES + N_GATES
QX_QA, QX_QB, QX_KVA = 0, BRANCH_DIM, 2 * BRANCH_DIM
KV_KB, KV_VB = 0, BRANCH_DIM
MOD_ROWS = 8
CTX_MOD_ROW = 4
VMEM_LIMIT = 56 * 1024 * 1024
BF16_SUBLANES = 16
CTX_SEQS_PER_STEP = 2


def _cparams(sem):
    return pltpu.CompilerParams(dimension_semantics=sem, vmem_limit_bytes=VMEM_LIMIT)


def _resident(shape, index_map):
    return pl.BlockSpec(shape, index_map, pipeline_mode=pl.Buffered(1))


def _rms(x, g):
    return x * lax.rsqrt(jnp.mean(x * x, axis=-1, keepdims=True) + NORM_EPS) * g


def _sigmoid(x):
    return 0.5 * jnp.tanh(0.5 * x) + 0.5


def _silu(x):
    half = 0.5 * x
    return half + half * jnp.tanh(half)


def _dot_t(a, b):
    return lax.dot_general(a, b, (((1,), (1,)), ((), ())), preferred_element_type=F32)


def _dot(a, b):
    return jnp.dot(a, b, preferred_element_type=F32)


def _scaled(q):
    assert np.log2(SCALE) == round(np.log2(SCALE))
    return q * jnp.asarray(SCALE, q.dtype)


def _low_lanes(shape):
    return lax.broadcasted_iota(jnp.int32, shape, len(shape) - 1) < HEAD_DIM


def _mod_kernel(cv_ref, w_ref, b_ref, win_ref, o_ref, wproj_ref, wgates_ref):
    a, b = _silu(cv_ref[...]), w_ref[...]
    a_hi, b_hi = a.astype(BF16), b.astype(BF16)
    a_lo = (a - a_hi.astype(F32)).astype(BF16)
    b_lo = (b - b_hi.astype(F32)).astype(BF16)
    o_ref[...] = _dot(a_hi, b_hi) + (_dot(a_lo, b_hi) + _dot(a_hi, b_lo)) + b_ref[...]
    @pl.when(pl.program_id(0) == 0)
    def _():
        _split_cast_w_in(win_ref, wproj_ref, wgates_ref)


def _split_cast_w_in(win_ref, wproj_ref, wgates_ref):
    wproj_ref[...] = win_ref[:, :W_GATES].astype(wproj_ref.dtype)
    wgates_ref[...] = win_ref[:, W_GATES:].astype(wgates_ref.dtype)


def _modulation(cvec, w_ada, b_ada, w_in):
    tn = 1536
    n = 6 * D_MODEL
    steps = n // tn
    rows = w_in.shape[1] // steps
    slab = lambda l, j: jnp.where(l == 0, j, steps - 1)
    return pl.pallas_call(
        _mod_kernel,
        grid=(DEPTH, steps),
        in_specs=[
            pl.BlockSpec((MOD_ROWS, D_MODEL), lambda l, j: (0, 0)),
            pl.BlockSpec((None, D_MODEL, tn), lambda l, j: (l, 0, j)),
            pl.BlockSpec((None, 1, tn), lambda l, j: (l, 0, j)),
            pl.BlockSpec((None, rows, D_IN), lambda l, j: (0, slab(l, j), 0)),
        ],
        out_specs=[pl.BlockSpec((None, MOD_ROWS, tn), lambda l, j: (l, 0, j)),
                   pl.BlockSpec((rows, W_GATES), lambda l, j: (slab(l, j), 0)),
                   pl.BlockSpec((rows, N_GATES), lambda l, j: (slab(l, j), 0))],
        out_shape=[jax.ShapeDtypeStruct((DEPTH, MOD_ROWS, n), F32),
                   jax.ShapeDtypeStruct((D_MODEL, W_GATES), BF16),
                   jax.ShapeDtypeStruct((D_MODEL, N_GATES), BF16)],
        compiler_params=_cparams(("arbitrary", "arbitrary")),
        name="modulation",
    )(cvec, w_ada, b_ada.reshape(DEPTH, 1, n), w_in)


def _rope_cols(v, cos, sin):
    first = (lax.broadcasted_iota(jnp.int32, (v.shape[0], LANES), 1) % HEAD_DIM) < HEAD_DIM // 2
    outs = []
    for c in range(v.shape[1] // LANES):
        vc = v[:, c * LANES:(c + 1) * LANES]
        partner = jnp.where(first, pltpu.roll(vc, LANES - HEAD_DIM // 2, 1),
                            pltpu.roll(vc, HEAD_DIM // 2, 1))
        outs.append(vc * cos + partner * sin)
    return outs[0] if len(outs) == 1 else jnp.concatenate(outs, axis=1)


def _with_rope(acc, lo, hi, cos_ref, sin_ref):
    parts = []
    if lo > 0:
        parts.append(acc[:, :lo])
    parts.append(_rope_cols(acc[:, lo:hi], cos_ref[...], sin_ref[...]))
    if hi < acc.shape[1]:
        parts.append(acc[:, hi:])
    return parts[0] if len(parts) == 1 else jnp.concatenate(parts, axis=1)


def _proj_kernel(*refs, latent, n_alias, n_cast):
    x_ref, mod_ref, g_ref, w_ref, bc_ref, bs_ref, pos_ref = refs[:7]
    if latent:
        cos_ref, sin_ref, _, qx_ref, kv_ref, oc_ref = refs[7:]
    else:
        rest = refs[7 + n_alias:]
        cast_src, rest = rest[:n_cast], rest[n_cast:]
        qx_ref, oc_ref, ka_ref, va_ref, kb_ref, vb_ref = rest[:6]
        for src, dst in zip(cast_src, rest[6:6 + n_cast]):
            dst[...] = src[...].astype(dst.dtype)

    h = (_rms(x_ref[...], g_ref[...]) * (1.0 + mod_ref[1:2, :]) + mod_ref[0:1, :]).astype(BF16)
    acc = _dot(h, w_ref[:, W_QA:W_KB])
    qa = acc[:, W_QA:W_QA + BRANCH_DIM]
    qb = acc[:, W_QB:W_QB + BRANCH_DIM]
    kva = acc[:, W_KA:W_KA + 2 * KV_A]
    if latent:
        qa = _rope_cols(qa, cos_ref[...], sin_ref[...])
        kva = _with_rope(kva, 0, KV_A, cos_ref, sin_ref)
        qx_ref[:, QX_KVA:QX_KVA + 2 * KV_A] = kva.astype(qx_ref.dtype)
    else:
        ka_ref[...] = kva[:, :KV_A].reshape(ka_ref.shape)
        va_ref[...] = kva[:, KV_A:].reshape(va_ref.shape)
    qx_ref[:, QX_QA:QX_QA + BRANCH_DIM] = qa.astype(qx_ref.dtype)
    qx_ref[:, QX_QB:QX_QB + BRANCH_DIM] = qb.astype(qx_ref.dtype)

    acc = _dot(h, w_ref[:, W_KB:W_GATES])
    kb = acc[:, 0:BRANCH_DIM]
    vb = acc[:, W_VB - W_KB:W_VB - W_KB + BRANCH_DIM]
    uc = acc[:, W_UC - W_KB:W_UC - W_KB + BRANCH_DIM]
    if latent:
        kv_ref[:, KV_KB:KV_KB + BRANCH_DIM] = kb.astype(kv_ref.dtype)
        kv_ref[:, KV_VB:KV_VB + BRANCH_DIM] = vb.astype(kv_ref.dtype)
    else:
        kb_ref[...] = kb.reshape(kb_ref.shape)
        vb_ref[...] = vb.reshape(vb_ref.shape)
    _fourier_mix(uc.astype(BF16), bc_ref, bs_ref, pos_ref, oc_ref)


def _proj(x, mods, l, g_pre, w_in, *, m, x_tile0, tm, mod_row, seq, oc_rows=None, oc_all=None,
          rope_tabs=None, caches=None, cast_weights=()):
    assert tm % seq == 0
    latent = rope_tabs is not None
    bc, bs, pos = _dft_tables(seq)
    pair = 2 * C_GROUP_DIM
    in_specs = [
        pl.BlockSpec((tm, D_MODEL), lambda i: (i + x_tile0, 0)),
        pl.BlockSpec((None, None, 6, D_MODEL), lambda i: (l, mod_row(i), 0, 0)),
        pl.BlockSpec((None, None, 1, D_MODEL), lambda i: (l, 0, 0, 0)),
        _resident((D_MODEL, W_GATES), lambda i: (0, 0)),
        _resident((pair, pair), lambda i: (0, 0)),
        _resident((pair, pair), lambda i: (0, 0)),
        _resident((seq, 2 * seq), lambda i: (0, 0)),
    ]
    args = [x, mods, g_pre, w_in, bc, bs, pos]
    aliases = {}
    if latent:
        seq_tiles = seq // tm
        tab_spec = pl.BlockSpec((tm, LANES), lambda i: (i % seq_tiles, 0))
        in_specs += [tab_spec, tab_spec, pl.BlockSpec(memory_space=pl.ANY)]
        args += list(rope_tabs)
        widths = (2 * BRANCH_DIM + 2 * KV_A, 2 * BRANCH_DIM)
        aliases = {len(args): len(widths)}
        args.append(oc_all)
        oc_rows, oc_tile0 = oc_all.shape[0], (oc_all.shape[0] - m) // tm
        cache_specs, cache_shapes = [], []
    else:
        widths = (2 * BRANCH_DIM,)
        oc_tile0 = 0
        cache_widths = (KV_A, KV_A, BRANCH_DIM, BRANCH_DIM)
        if caches is not None:
            in_specs += [pl.BlockSpec(memory_space=pl.ANY)] * len(caches)
            aliases = {len(args) + k: len(widths) + 1 + k for k in range(len(caches))}
            args += list(caches)
        cache_specs = [pl.BlockSpec((tm // seq, None, seq, w), lambda i: (i, l, 0, 0))
                       for w in cache_widths]
        cache_shapes = [jax.ShapeDtypeStruct((m // seq, DEPTH, seq, w), F32) for w in cache_widths]
        steps = m // tm
        for w in cast_weights:
            _, rows, cols = w.shape
            assert rows % (steps * BF16_SUBLANES) == 0
            in_specs.append(pl.BlockSpec((None, rows // steps, cols), lambda i: (l, i, 0)))
            cache_specs.append(pl.BlockSpec((rows // steps, cols), lambda i: (i, 0)))
            cache_shapes.append(jax.ShapeDtypeStruct((rows, cols), BF16))
            args.append(w)
    return pl.pallas_call(
        functools.partial(_proj_kernel, latent=latent, n_alias=len(aliases), n_cast=len(cast_weights)),
        grid=(m // tm,),
        in_specs=in_specs,
        out_specs=([pl.BlockSpec((tm, w), lambda i: (i, 0)) for w in widths]
                   + [pl.BlockSpec((tm, BRANCH_DIM), lambda i: (i + oc_tile0, 0))] + cache_specs),
        out_shape=([jax.ShapeDtypeStruct((m, w), BF16) for w in widths]
                   + [jax.ShapeDtypeStruct((oc_rows, BRANCH_DIM), BF16)] + cache_shapes),
        input_output_aliases=aliases,
        compiler_params=_cparams(("parallel",)),
        name="proj_lat" if latent else "proj_ctx",
    )(*args)


def _with_ones(v):
    return jnp.concatenate([v, jnp.ones_like(v)], axis=1)


def _softmax_pv(parts, sink=None):
    m = parts[0][0].max(axis=-1, keepdims=True)
    for s, _ in parts[1:]:
        m = jnp.maximum(m, s.max(axis=-1, keepdims=True))
    if sink is not None:
        m = jnp.maximum(m, sink)
    ones_half = parts[0][1].shape[1] == 2 * LANES
    acc = None
    den = None
    for s, v in parts:
        e = jnp.exp(s - m)
        if not ones_half:
            d = e.sum(axis=-1, keepdims=True)
            den = d if den is None else den + d
        o = _dot(e.astype(BF16), v)
        acc = o if acc is None else acc + o
    if ones_half:
        acc, den = acc[:, :LANES], acc[:, LANES:]
    if sink is not None:
        den = den + jnp.exp(sink - m)
    return acc * (1.0 / den)


def _kv_head_variants(x2):
    low = _low_lanes(x2.shape)
    xr = pltpu.roll(x2, HEAD_DIM, 1)
    zero = jnp.zeros_like(x2)
    return [
        [jnp.where(low, x2, zero).astype(BF16), jnp.where(low, zero, xr).astype(BF16)],
        [jnp.where(low, xr, zero).astype(BF16), jnp.where(low, zero, x2).astype(BF16)],
    ]


def _attn_ctx_kernel(sink_ref, qa_ref, qb_ref, ka_ref, va_ref, kb_ref, vb_ref, *rest, n_weights):
    w_src, (oa_ref, ob_ref), w_dst = rest[:n_weights], rest[n_weights:n_weights + 2], rest[n_weights + 2:]
    for src, dst in zip(w_src, w_dst):
        dst[...] = src[...].astype(dst.dtype)
    seq = ka_ref.shape[1]
    chunks = [slice(j * LANES, (j + 1) * LANES) for j in range(A_Q_HEADS // 2)]
    low = _low_lanes((seq, LANES))
    for s in range(ka_ref.shape[0]):
        rows = slice(s * seq, (s + 1) * seq)
        ka = [jnp.concatenate(v, axis=0) for v in _kv_head_variants(ka_ref[s])]
        va = [jnp.concatenate(v, axis=0) for v in _kv_head_variants(va_ref[s])]
        work = []
        for j, cols in enumerate(chunks):
            g = (2 * j) // A_GROUP
            s2 = _dot_t(_scaled(qa_ref[rows, cols]), ka[g])
            work.append((oa_ref, cols, s2, va[g], (sink_ref[2 * j], sink_ref[2 * j + 1])))
        for cols in chunks:
            kc, vc = kb_ref[s, :, cols], vb_ref[s, :, cols]
            zero = jnp.zeros_like(kc)
            k2 = jnp.concatenate([jnp.where(low, kc, zero), jnp.where(low, zero, kc)], axis=0).astype(BF16)
            v2 = jnp.concatenate([jnp.where(low, vc, zero), jnp.where(low, zero, vc)], axis=0).astype(BF16)
            work.append((ob_ref, cols, _dot_t(_scaled(qb_ref[rows, cols]), k2), v2, None))
        for o_ref, cols, s2, v2, sinks in work:
            o_ref[rows, cols] = _pair_softmax_pv(s2, v2, sinks).astype(o_ref.dtype)


def _pair_softmax_pv(s2, v2, sinks=None):
    tk = s2.shape[1] // 2
    es, extra = [], []
    for p in range(2):
        s = s2[:, p * tk:(p + 1) * tk]
        m = s.max(axis=-1, keepdims=True)
        if sinks is not None:
            m = jnp.maximum(m, sinks[p])
            extra.append(jnp.exp(sinks[p] - m))
        es.append(jnp.exp(s - m).astype(BF16))
    lane_half = lax.broadcasted_iota(jnp.int32, v2.shape, 1) // HEAD_DIM
    row_half = lax.broadcasted_iota(jnp.int32, v2.shape, 0) // tk
    ones2 = jnp.where(lane_half == row_half, 1.0, 0.0).astype(v2.dtype)
    acc = _dot(jnp.concatenate(es, axis=1), jnp.concatenate([v2, ones2], axis=1))
    den = acc[:, LANES:]
    if sinks is not None:
        den = den + jnp.where(_low_lanes(den.shape), extra[0], extra[1])
    return acc[:, :LANES] * (1.0 / den)


def _attn_ctx(qx, caches, sink, l, seq, weights, out_rows):
    m = qx.shape[0]
    per_step = CTX_SEQS_PER_STEP
    rows_step = per_step * seq
    steps = m // rows_step
    out = jax.ShapeDtypeStruct((out_rows, BRANCH_DIM), BF16)
    cache_specs = [pl.BlockSpec((per_step, None, seq, c.shape[-1]), lambda b: (b, l, 0, 0)) for c in caches]
    w_in_specs, w_out_specs, w_shapes = [], [], []
    for w in weights:
        _, rows, cols = w.shape
        assert rows % (steps * BF16_SUBLANES) == 0
        w_in_specs.append(pl.BlockSpec((None, rows // steps, cols), lambda b: (l, b, 0)))
        w_out_specs.append(pl.BlockSpec((rows // steps, cols), lambda b: (b, 0)))
        w_shapes.append(jax.ShapeDtypeStruct((rows, cols), BF16))
    return pl.pallas_call(
        functools.partial(_attn_ctx_kernel, n_weights=len(weights)),
        grid=(steps,),
        in_specs=[
            pl.BlockSpec(memory_space=pltpu.SMEM),
            pl.BlockSpec((rows_step, BRANCH_DIM), lambda b: (b, QX_QA // BRANCH_DIM)),
            pl.BlockSpec((rows_step, BRANCH_DIM), lambda b: (b, QX_QB // BRANCH_DIM)),
        ] + cache_specs + w_in_specs,
        out_specs=[pl.BlockSpec((rows_step, BRANCH_DIM), lambda b: (b, 0))] * 2 + w_out_specs,
        out_shape=[out, out] + w_shapes,
        compiler_params=_cparams(("parallel",)),
        name="attn_ctx",
    )(sink, qx, qx, *caches, *weights)


def _attn_a_lat_kernel(sink_ref, q_ref, kv_ref, ck_ref, cv_ref, _, *rest, n_blocks, cast_w_in):
    if cast_w_in:
        win_ref, o_ref, wproj_ref, wgates_ref = rest[:4]
        _split_cast_w_in(win_ref, wproj_ref, wgates_ref)
        rest = rest[4:]
    else:
        o_ref, rest = rest[0], rest[1:]
    qh_scr, kp_scr, vp_scr, ckb_scr, cvb_scr, mask_scr = rest
    seq = n_blocks * A_BLOCK
    band = 3 * A_BLOCK
    low_seq = _low_lanes((seq, LANES))
    for h in range(A_Q_HEADS):
        j, p, g = h // 2, h % 2, h // A_GROUP
        x = q_ref[:, j * LANES:(j + 1) * LANES].astype(F32) * SCALE
        if p != g:
            x = pltpu.roll(x, HEAD_DIM, 1)
        qh_scr[h] = jnp.where(low_seq if g == 0 else jnp.logical_not(low_seq), x, 0.0).astype(BF16)
    pad = jnp.zeros((A_BLOCK, LANES), BF16)
    for scr, col in ((kp_scr, 0), (vp_scr, KV_A)):
        scr[0:A_BLOCK, :] = pad
        scr[A_BLOCK:A_BLOCK + seq, :] = kv_ref[:, col:col + KV_A]
        scr[A_BLOCK + seq:2 * A_BLOCK + seq, :] = pad
    ckb_scr[...] = ck_ref[...].astype(BF16)
    cvb_scr[...] = cv_ref[...].astype(BF16)

    rows = A_GROUP * A_BLOCK
    row = lax.broadcasted_iota(jnp.int32, (rows, band), 0)
    qi = row % A_BLOCK
    kj = lax.broadcasted_iota(jnp.int32, (rows, band), 1)
    head_row = lax.broadcasted_iota(jnp.int32, (rows, 1), 0) // A_BLOCK
    low_blk = _low_lanes((A_BLOCK, LANES))
    @pl.when(pl.program_id(0) == 0)
    def _():
        for case, (no_prev, no_next) in enumerate(((1, 0), (0, 0), (0, 1))):
            valid = (kj >= qi + no_prev * (A_BLOCK - qi)) & (kj <= 2 * A_BLOCK + qi - no_next * (qi + 1))
            mask_scr[case] = jnp.where(valid, 0.0, NEG_INF)

    def body(n, carry):
        start = pl.multiple_of(n * A_BLOCK, A_BLOCK)
        case = jnp.where(n > 0, 1, 0) + jnp.where(n < n_blocks - 1, 0, 1)
        scores = []
        for g in range(A_KV_HEADS):
            q = jnp.concatenate(
                [qh_scr[A_GROUP * g + i, pl.ds(start, A_BLOCK), :] for i in range(A_GROUP)], axis=0)
            s_band = _dot_t(q, kp_scr[pl.ds(start, band), :]) + mask_scr[case]
            scores.append((s_band, _dot_t(q, ckb_scr[...])))
        outs = []
        for g, (s_band, s_ctx) in enumerate(scores):
            sink = jnp.full((rows, 1), sink_ref[A_GROUP * g], F32)
            for i in range(1, A_GROUP):
                sink = jnp.where(head_row == i, sink_ref[A_GROUP * g + i], sink)
            outs.append(_softmax_pv([(s_band, vp_scr[pl.ds(start, band), :]), (s_ctx, cvb_scr[...])],
                                    sink=sink))
        for j in range(A_Q_HEADS // 2):
            halves = []
            for p in range(2):
                h = 2 * j + p
                g, i = h // A_GROUP, h % A_GROUP
                t = outs[g][i * A_BLOCK:(i + 1) * A_BLOCK]
                halves.append(t if p == g else pltpu.roll(t, HEAD_DIM, 1))
            o_ref[pl.ds(start, A_BLOCK), j * LANES:(j + 1) * LANES] = (
                jnp.where(low_blk, halves[0], halves[1]).astype(o_ref.dtype))
        return carry

    lax.fori_loop(0, n_blocks, body, 0, unroll=2)


def _attn_a_lat(main, kv, cache_k, cache_v, sink, l, seq, oa_all, w_in_next=None):
    m = main.shape[0]
    nb = seq // A_BLOCK
    assert nb >= 2
    n_seq = m // seq
    past = cache_k.shape[2]
    first = (oa_all.shape[0] - m) // seq
    cache_spec = pl.BlockSpec((None, None, past, KV_A), lambda b: (b, l, 0, 0))
    in_specs = [
        pl.BlockSpec(memory_space=pltpu.SMEM),
        pl.BlockSpec((seq, BRANCH_DIM), lambda b: (b, QX_QA // BRANCH_DIM)),
        pl.BlockSpec((seq, 2 * KV_A), lambda b: (b, QX_KVA // (2 * KV_A))),
        cache_spec, cache_spec,
        pl.BlockSpec(memory_space=pl.ANY),
    ]
    args = [sink, main, kv, cache_k, cache_v, oa_all]
    out_specs = [pl.BlockSpec((seq, BRANCH_DIM), lambda b: (b + first, 0))]
    out_shape = [jax.ShapeDtypeStruct(oa_all.shape, BF16)]
    if w_in_next is not None:
        w_in, layer = w_in_next
        rows = w_in.shape[1] // n_seq
        assert w_in.shape[1] % (n_seq * BF16_SUBLANES) == 0
        in_specs.append(pl.BlockSpec((None, rows, D_IN), lambda b: (layer, b, 0)))
        args.append(w_in)
        out_specs += [pl.BlockSpec((rows, W_GATES), lambda b: (b, 0)),
                      pl.BlockSpec((rows, N_GATES), lambda b: (b, 0))]
        out_shape += [jax.ShapeDtypeStruct((D_MODEL, W_GATES), BF16),
                      jax.ShapeDtypeStruct((D_MODEL, N_GATES), BF16)]
    return pl.pallas_call(
        functools.partial(_attn_a_lat_kernel, n_blocks=nb, cast_w_in=w_in_next is not None),
        grid=(n_seq,),
        in_specs=in_specs,
        out_specs=out_specs,
        out_shape=out_shape,
        input_output_aliases={5: 0},
        scratch_shapes=[
            pltpu.VMEM((A_Q_HEADS, seq, LANES), BF16),
            pltpu.VMEM((seq + 2 * A_BLOCK, LANES), BF16),
            pltpu.VMEM((seq + 2 * A_BLOCK, LANES), BF16),
            pltpu.VMEM((past, LANES), BF16),
            pltpu.VMEM((past, LANES), BF16),
            pltpu.VMEM((3, A_GROUP * A_BLOCK, 3 * A_BLOCK), F32),
        ],
        compiler_params=_cparams(("arbitrary",)),
        name="attn_a_lat",
    )(*args)


def _attn_b_lat_kernel(q_ref, k_ref, v_ref, ck_ref, cv_ref, rpb_ref, _, o_ref, ckb_scr, vx_scr, cvx_scr,
                       bias_ref, *, rows, kr):
    @pl.when(pl.program_id(0) == 0)
    def _():
        _fill_neighbourhood_bias(rpb_ref, bias_ref, kr=kr)

    half = kr // 2
    n_loc = kr * GRID_W
    low_q = _low_lanes((GRID_W, LANES))
    chunks = [slice(j * LANES, (j + 1) * LANES) for j in range(B_HEADS // 2)]
    ckb_scr[...] = ck_ref[...].astype(BF16)
    for j, cols in enumerate(chunks):
        vx_scr[j] = _with_ones(v_ref[:, cols])
        cvx_scr[j] = _with_ones(cv_ref[:, cols].astype(BF16))
    rows_per_iter = 4
    assert rows % rows_per_iter == 0

    def body(it, carry):
        work = []
        for u in range(rows_per_iter):
            r = it * rows_per_iter + u
            r0 = jnp.clip(r - half, 0, rows - kr)
            q0 = pl.multiple_of(r * GRID_W, GRID_W)
            start = pl.multiple_of(r0 * GRID_W, GRID_W)
            for j, cols in enumerate(chunks):
                qc = _scaled(q_ref[pl.ds(q0, GRID_W), cols])
                zero = jnp.zeros_like(qc)
                qs = jnp.concatenate([jnp.where(low_q, qc, zero), jnp.where(low_q, zero, qc)], axis=0)
                s_loc = _dot_t(qs, k_ref[pl.ds(start, n_loc), cols]) + bias_ref[r - r0, j]
                work.append((q0, start, j, cols, s_loc, _dot_t(qs, ckb_scr[:, cols])))
        for q0, start, j, cols, s_loc, s_ctx in work:
            o2 = _softmax_pv([(s_loc, vx_scr[j, pl.ds(start, n_loc), :]), (s_ctx, cvx_scr[j])])
            o = jnp.where(low_q, o2[:GRID_W], o2[GRID_W:])
            o_ref[pl.ds(q0, GRID_W), cols] = o.astype(o_ref.dtype)
        return carry

    lax.fori_loop(0, rows // rows_per_iter, body, 0)


def _attn_b_lat(main, kv, cache_k, cache_v, rpb_pad, l, seq, ob_all):
    m = main.shape[0]
    rows = seq // GRID_W
    kr = min(B_WIN_ROWS, rows)
    assert kr % 2 == 0 and 2 * GRID_W == LANES
    n_seq = m // seq
    past = cache_k.shape[2]
    first = (ob_all.shape[0] - m) // seq
    cache_spec = pl.BlockSpec((None, None, past, BRANCH_DIM), lambda b: (b, l, 0, 0))
    return pl.pallas_call(
        functools.partial(_attn_b_lat_kernel, rows=rows, kr=kr),
        grid=(n_seq,),
        in_specs=[
            pl.BlockSpec((seq, BRANCH_DIM), lambda b: (b, QX_QB // BRANCH_DIM)),
            pl.BlockSpec((seq, BRANCH_DIM), lambda b: (b, KV_KB // BRANCH_DIM)),
            pl.BlockSpec((seq, BRANCH_DIM), lambda b: (b, KV_VB // BRANCH_DIM)),
            cache_spec, cache_spec,
            _resident((None,) + rpb_pad.shape[1:], lambda b: (l, 0, 0, 0)),
            pl.BlockSpec(memory_space=pl.ANY),
        ],
        out_specs=pl.BlockSpec((seq, BRANCH_DIM), lambda b: (b + first, 0)),
        out_shape=jax.ShapeDtypeStruct(ob_all.shape, BF16),
        input_output_aliases={6: 0},
        scratch_shapes=[
            pltpu.VMEM((past, BRANCH_DIM), BF16),
            pltpu.VMEM((B_HEADS // 2, seq, 2 * LANES), BF16),
            pltpu.VMEM((B_HEADS // 2, past, 2 * LANES), BF16),
            pltpu.VMEM((kr, B_HEADS // 2, 2 * GRID_W, kr * GRID_W), F32),
        ],
        compiler_params=_cparams(("arbitrary",)),
        name="attn_b_lat",
    )(main, kv, kv, cache_k, cache_v, rpb_pad, ob_all)


def _fill_neighbourhood_bias(rpb_ref, bias_scr, *, kr):
    shape = (GRID_W, LANES)
    c = lax.broadcasted_iota(jnp.int32, shape, 0)
    lane = lax.broadcasted_iota(jnp.int32, shape, 1)
    c2 = lane % GRID_W
    ws = jnp.clip(c - B_WIN_COLS // 2, 0, GRID_W - B_WIN_COLS)
    ok = (c2 >= ws) & (c2 < ws + B_WIN_COLS)
    low = lane < GRID_W

    def toeplitz(h, dr, lane0):
        row = jnp.broadcast_to(rpb_ref[h, dr:dr + 1, :], shape)
        return pltpu.roll(row, (lane0 - (B_WIN_COLS - 1)) % LANES, 1, stride=1, stride_axis=0)

    for h in range(B_HEADS):
        rows = slice(h % 2 * GRID_W, (h % 2 + 1) * GRID_W)
        pairs = {}
        for v in range(kr):
            for m in range(0, kr, 2):
                dr = m - v + B_WIN_ROWS - 1
                if dr not in pairs:
                    pair = jnp.where(low, toeplitz(h, dr, 0), toeplitz(h, dr + 1, GRID_W))
                    pairs[dr] = jnp.where(ok, pair, NEG_INF)
                bias_scr[v, h // 2, rows, m * GRID_W:(m + 2) * GRID_W] = pairs[dr]


def _dft_tables(seq):
    cd = C_GROUP_DIM
    kc = (np.arange(cd)[:, None] * np.arange(cd)[None, :]) % cd
    ang_c = 2.0 * np.pi * kc / cd
    eye2 = np.eye(2)
    bd_cos = np.kron(eye2, np.cos(ang_c))
    bd_sin = np.kron(eye2, np.sin(ang_c))
    kt = (np.arange(seq)[:, None] * np.arange(seq)[None, :]) % seq
    ang_t = 2.0 * np.pi * kt / seq
    norm = 1.0 / np.sqrt(float(seq * cd))
    pos = np.concatenate([np.cos(ang_t), -np.sin(ang_t)], axis=1) * norm
    return (jnp.asarray(bd_cos, F32).astype(BF16), jnp.asarray(bd_sin, F32).astype(BF16),
            jnp.asarray(pos, F32).astype(BF16))


def _fourier_mix(u, bc_ref, bs_ref, pos_ref, o_ref):
    pair = 2 * C_GROUP_DIM
    seq = pos_ref.shape[0]
    uc, us = [], []
    for p in range(BRANCH_DIM // pair):
        up = u[:, p * pair:(p + 1) * pair]
        uc.append(_dot(up, bc_ref[...]))
        us.append(_dot(up, bs_ref[...]))
    zc = jnp.concatenate(uc, axis=1).astype(BF16)
    zs = jnp.concatenate(us, axis=1).astype(BF16)
    for s in range(u.shape[0] // seq):
        rows = slice(s * seq, (s + 1) * seq)
        z = jnp.concatenate([zc[rows], zs[rows]], axis=0)
        o_ref[rows, :] = _dot(pos_ref[...], z).astype(o_ref.dtype)


def _ffn_weight_copies(wfi_hbm, wfo_hbm, wfg_ref, wfu_ref, wfo_ref, sems):
    srcs = (wfi_hbm.at[:, 0:D_FF], wfi_hbm.at[:, D_FF:2 * D_FF], wfo_hbm)
    dsts = (wfg_ref, wfu_ref, wfo_ref)
    return [pltpu.make_async_copy(src, dst, sems.at[k]) for k, (src, dst) in enumerate(zip(srcs, dsts))]


def _tail_kernel(*refs, n_x, n_out, n_ctx_tiles):
    oa_ref, ob_ref, oc_ref = refs[:3]
    x_refs, refs = refs[3:3 + n_x], refs[3 + n_x:]
    (mod_ref, gpre0_ref, gpost0_ref, gpre1_ref, gpost1_ref,
     wg_ref, wb_ref, wo_ref, wfi_hbm, wfo_hbm) = refs[:10]
    o_refs, refs = refs[10:10 + n_out], refs[10 + n_out:]
    wfg_ref, wfu_ref, wfo_ref, sems = refs[:4]
    i = pl.program_id(0)
    first = i == 0
    is_ctx = i < n_ctx_tiles
    copies = _ffn_weight_copies(wfi_hbm, wfo_hbm, wfg_ref, wfu_ref, wfo_ref, sems)

    @pl.when(first)
    def _():
        for copy in copies:
            copy.start()

    if n_x == 1:
        x = x_refs[0][...]
    else:
        pick_ctx = jnp.full(x_refs[0].shape, jnp.where(is_ctx, 1, 0), jnp.int32) > 0
        x = jnp.where(pick_ctx, x_refs[0][...], x_refs[1][...])
    h = (_rms(x, gpre0_ref[...]) * (1.0 + mod_ref[1:2, :]) + mod_ref[0:1, :]).astype(BF16)
    gates = _dot(h, wg_ref[...])
    branches = [_dot(b_ref[...], wb_ref[k]) for k, b_ref in enumerate((oa_ref, ob_ref, oc_ref))]
    mix = None
    for k, branch in enumerate(branches):
        term = _sigmoid(gates[:, k * D_MODEL:(k + 1) * D_MODEL]) * branch
        mix = term if mix is None else mix + term
    y = _dot(mix.astype(BF16), wo_ref[...])
    x1 = x + mod_ref[2:3, :] * _rms(y, gpost0_ref[...])
    h2 = (_rms(x1, gpre1_ref[...]) * (1.0 + mod_ref[4:5, :]) + mod_ref[3:4, :]).astype(BF16)

    @pl.when(first)
    def _():
        for copy in copies:
            copy.wait()

    act = _silu(_dot(h2, wfg_ref[...])) * _dot(h2, wfu_ref[...])
    f = _dot(act.astype(BF16), wfo_ref[...])
    out = x1 + mod_ref[5:6, :] * _rms(f, gpost1_ref[...])
    if n_out == 1:
        o_refs[0][...] = out
    else:
        @pl.when(is_ctx)
        def _():
            o_refs[0][...] = out

        @pl.when(jnp.logical_not(is_ctx))
        def _():
            o_refs[1][...] = out


def _tail(oa, ob, oc, xs, mods, l, g_pre, g_post, w_gates, w_branch, w_out, w_ffn_in, w_ffn_out,
          *, tm, m_ctx, mod_row, split_out):
    m = oa.shape[0]
    n_ctx = m_ctx // tm
    o_spec = pl.BlockSpec((tm, BRANCH_DIM), lambda i: (i, 0))
    whole = pl.BlockSpec((tm, D_MODEL), lambda i: (i, 0))
    halves = [pl.BlockSpec((tm, D_MODEL), lambda i: (jnp.minimum(i, n_ctx - 1), 0)),
              pl.BlockSpec((tm, D_MODEL), lambda i: (jnp.maximum(i - n_ctx, 0), 0))]
    half_shapes = [jax.ShapeDtypeStruct((m_ctx, D_MODEL), F32),
                   jax.ShapeDtypeStruct((m - m_ctx, D_MODEL), F32)]
    gain_spec = lambda which: pl.BlockSpec((None, None, 1, D_MODEL), lambda i: (l, which, 0, 0))
    scratch = [
        pltpu.VMEM((D_MODEL, D_FF), BF16),
        pltpu.VMEM((D_MODEL, D_FF), BF16),
        pltpu.VMEM((D_FF, D_MODEL), BF16),
        pltpu.SemaphoreType.DMA((3,)),
    ]
    return pl.pallas_call(
        functools.partial(_tail_kernel, n_x=len(xs), n_out=2 if split_out else 1, n_ctx_tiles=n_ctx),
        grid=(m // tm,),
        in_specs=[o_spec, o_spec, o_spec] + (halves if len(xs) == 2 else [whole]) + [
            pl.BlockSpec((None, None, 6, D_MODEL), lambda i: (l, mod_row(i), 0, 0)),
            gain_spec(0), gain_spec(0), gain_spec(1), gain_spec(1),
            _resident((D_MODEL, N_GATES), lambda i: (0, 0)),
            _resident((N_BRANCH, BRANCH_DIM, D_MODEL), lambda i: (0, 0, 0)),
            _resident((D_MODEL, D_MODEL), lambda i: (0, 0)),
            pl.BlockSpec(memory_space=pl.ANY),
            pl.BlockSpec(memory_space=pl.ANY),
        ],
        out_specs=halves if split_out else whole,
        out_shape=half_shapes if split_out else jax.ShapeDtypeStruct((m, D_MODEL), F32),
        scratch_shapes=scratch,
        compiler_params=_cparams(("arbitrary",)),
        name="tail",
    )(oa, ob, oc, *xs, mods, g_pre, g_post, g_pre, g_post, w_gates, w_branch, w_out,
      w_ffn_in, w_ffn_out)


def _rope_tables(seq):
    t = np.arange(seq)
    row = (t // GRID_W).astype(np.float32)
    col = (t % GRID_W).astype(np.float32)
    n_pairs_axis = HEAD_DIM // 4
    inv = (np.float32(ROPE_BASE) ** (-np.arange(n_pairs_axis, dtype=np.float32) / n_pairs_axis)
           ).astype(np.float32)
    ang = np.concatenate([row[:, None] * inv, col[:, None] * inv], axis=-1).astype(np.float64)
    cos, sin = np.cos(ang), np.sin(ang)
    cos_l = np.tile(cos, (1, LANES // cos.shape[1]))
    sin_l = np.tile(np.concatenate([-sin, sin], axis=-1), (1, LANES // HEAD_DIM))
    return jnp.asarray(cos_l, F32), jnp.asarray(sin_l, F32)


def kernel(x_prompt, x_sample, cache_a_k, cache_a_v, cache_b_k, cache_b_v, c, c_ctx, w_ada, b_ada,
           norm_pre, norm_post, w_in, a_sink, b_rpb, w_branch, w_out, w_ffn_in, w_ffn_out):
    batch, seq, _ = x_prompt.shape
    dec_batch, dec_seq, _ = x_sample.shape
    past = cache_a_k.shape[2]
    assert dec_batch <= CTX_MOD_ROW and seq % A_BLOCK == 0 and dec_seq % A_BLOCK == 0

    cvec = jnp.concatenate(
        [c, c_ctx[None, :], jnp.zeros((MOD_ROWS - dec_batch - 1, D_MODEL), F32)], axis=0)
    assert w_in.shape[-1] == D_IN
    mods, w_proj_b, w_gates_b = _modulation(cvec, w_ada, b_ada, w_in)
    mods = mods.reshape(DEPTH, MOD_ROWS, 6, D_MODEL)
    w_branch_2d = w_branch.reshape(DEPTH, N_BRANCH * BRANCH_DIM, D_MODEL)
    g_pre = norm_pre.reshape(DEPTH, 2, 1, D_MODEL)
    g_post = norm_post.reshape(DEPTH, 2, 1, D_MODEL)
    rope_tabs = _rope_tables(dec_seq)
    cak = cache_a_k.reshape(dec_batch, DEPTH, past, KV_A)
    cav = cache_a_v.reshape(dec_batch, DEPTH, past, KV_A)
    cbk = cache_b_k.reshape(dec_batch, DEPTH, past, BRANCH_DIM)
    cbv = cache_b_v.reshape(dec_batch, DEPTH, past, BRANCH_DIM)

    n_dr, n_dc = b_rpb.shape[2:]
    rpb_pad = jnp.pad(b_rpb.astype(F32), ((0, 0), (0, 0), (0, (-n_dr) % 8), (0, LANES - n_dc)))

    tm_proj, tm_tail = 1024, 512
    m_ctx, m_lat = batch * seq, dec_batch * dec_seq
    m_all = m_ctx + m_lat
    assert m_ctx % tm_proj == 0 and m_lat % tm_proj == 0
    ctx_row = lambda i: CTX_MOD_ROW
    lat_row = lambda i: (i * tm_proj) // dec_seq
    tail_row = lambda i: jnp.where(i < m_ctx // tm_tail, CTX_MOD_ROW, (i * tm_tail - m_ctx) // dec_seq)

    xs = (x_prompt.reshape(m_ctx, D_MODEL), x_sample.reshape(m_lat, D_MODEL))
    caches = None
    for l in range(DEPTH):
        x_ctx, x_lat = (xs[0], xs[1]) if len(xs) == 2 else (xs[0], xs[0])
        lat_tile0 = 0 if len(xs) == 2 else m_ctx // tm_proj
        qx, oc, *rest = _proj(x_ctx, mods, l, g_pre, w_proj_b, m=m_ctx, x_tile0=0, tm=tm_proj,
                              mod_row=ctx_row, seq=seq, oc_rows=m_all, caches=caches,
                              cast_weights=(w_ffn_in,))
        caches, (w_ffn_in_b,) = rest[:4], rest[4:]
        oa, ob, w_branch_b, w_out_b, w_ffn_out_b = _attn_ctx(
            qx, caches, a_sink[l], l, seq, (w_branch_2d, w_out, w_ffn_out), m_all)
        qx, kv, oc = _proj(x_lat, mods, l, g_pre, w_proj_b, m=m_lat, x_tile0=lat_tile0, tm=tm_proj,
                           mod_row=lat_row, seq=dec_seq, oc_all=oc, rope_tabs=rope_tabs)
        last = l == DEPTH - 1
        w_gates_l = w_gates_b
        if last:
            oa, = _attn_a_lat(qx, qx, cak, cav, a_sink[l], l, dec_seq, oa)
        else:
            oa, w_proj_b, w_gates_b = _attn_a_lat(qx, qx, cak, cav, a_sink[l], l, dec_seq, oa,
                                                  w_in_next=(w_in, l + 1))
        ob = _attn_b_lat(qx, kv, cbk, cbv, rpb_pad, l, dec_seq, ob)
        out = _tail(oa, ob, oc, xs, mods, l, g_pre, g_post, w_gates_l,
                    w_branch_b.reshape(N_BRANCH, BRANCH_DIM, D_MODEL), w_out_b, w_ffn_in_b, w_ffn_out_b,
                    tm=tm_tail, m_ctx=m_ctx, mod_row=tail_row, split_out=last)
        xs = tuple(out) if last else (out,)
    y_prompt = xs[0].reshape(batch, seq, D_MODEL)
    y_sample = xs[1].reshape(dec_batch, dec_seq, D_MODEL)
    nak, nav, nbk, nbv = caches
    new_a_k = nak.reshape(batch, DEPTH, seq, A_KV_HEADS, HEAD_DIM)
    new_a_v = nav.reshape(batch, DEPTH, seq, A_KV_HEADS, HEAD_DIM)
    new_b_k = nbk.reshape(batch, DEPTH, seq, B_HEADS, HEAD_DIM)
    new_b_v = nbv.reshape(batch, DEPTH, seq, B_HEADS, HEAD_DIM)
    return (y_prompt, y_sample, new_a_k, new_a_v, new_b_k, new_b_v)
```

```python
import functools

import numpy as np
import jax
import jax.numpy as jnp
from jax import lax
from jax.experimental import pallas as pl
from jax.experimental.pallas import tpu as pltpu

F32 = jnp.float32
BF16 = jnp.bfloat16

D_MODEL = 1024
DEPTH = 2
GRID_W = 64
HEAD_DIM = 64
BRANCH_DIM = D_MODEL // 2
A_Q_HEADS = BRANCH_DIM // HEAD_DIM
A_KV_HEADS = A_Q_HEADS // 4
A_GROUP = A_Q_HEADS // A_KV_HEADS
A_BLOCK = 128
B_HEADS = BRANCH_DIM // HEAD_DIM
B_WIN_ROWS = 8
B_WIN_COLS = 16
C_GROUPS = 4
C_GROUP_DIM = BRANCH_DIM // C_GROUPS
N_BRANCH = 3
D_FF = -(-8 * D_MODEL // (3 * 256)) * 256
ROPE_BASE = 10000.0
NORM_EPS = 1e-6
NEG_INF = -1e30
SCALE = HEAD_DIM ** -0.5

LANES = 128
KV_A = A_KV_HEADS * HEAD_DIM
N_GATES = N_BRANCH * D_MODEL
W_QA = 0
W_KA = W_QA + BRANCH_DIM
W_VA = W_KA + KV_A
W_QB = W_VA + KV_A
W_KB = W_QB + BRANCH_DIM
W_VB = W_KB + BRANCH_DIM
W_UC = W_VB + BRANCH_DIM
W_GATES = W_UC + BRANCH_DIM
D_IN = W_GATES + N_GATES
QX_QA, QX_QB, QX_KVA = 0, BRANCH_DIM, 2 * BRANCH_DIM
KV_KB, KV_VB = 0, BRANCH_DIM
MOD_ROWS = 8
CTX_MOD_ROW = 4
VMEM_LIMIT = 56 * 1024 * 1024
BF16_SUBLANES = 16
CTX_SEQS_PER_STEP = 2


def _cparams(sem):
    return pltpu.CompilerParams(dimension_semantics=sem, vmem_limit_bytes=VMEM_LIMIT)


def _resident(shape, index_map):
    return pl.BlockSpec(shape, index_map, pipeline_mode=pl.Buffered(1))


def _rms(x, w):
    return x * lax.rsqrt(jnp.mean(x * x, axis=-1, keepdims=True) + NORM_EPS) * w


def _sigmoid(x):
    return 0.5 * jnp.tanh(0.5 * x) + 0.5


def _silu(x):
    half = 0.5 * x
    return half + half * jnp.tanh(half)


def _dot_t(a, b):
    return lax.dot_general(a, b, (((1,), (1,)), ((), ())), preferred_element_type=F32)


def _dot(a, b):
    return jnp.dot(a, b, preferred_element_type=F32)


def _scaled(q):
    assert np.log2(SCALE) == round(np.log2(SCALE))
    return q * jnp.asarray(SCALE, q.dtype)


def _low_lanes(shape):
    return lax.broadcasted_iota(jnp.int32, shape, len(shape) - 1) < HEAD_DIM


def _mod_kernel(cv_ref, w_ref, b_ref, win_ref, o_ref, wproj_ref, wgates_ref):
    a, b = _silu(cv_ref[...]), w_ref[...]
    a_hi, b_hi = a.astype(BF16), b.astype(BF16)
    a_lo = (a - a_hi.astype(F32)).astype(BF16)
    b_lo = (b - b_hi.astype(F32)).astype(BF16)
    o_ref[...] = _dot(a_hi, b_hi) + (_dot(a_lo, b_hi) + _dot(a_hi, b_lo)) + b_ref[...]
    @pl.when(pl.program_id(0) == 0)
    def _():
        _split_cast_w_in(win_ref, wproj_ref, wgates_ref)


def _split_cast_w_in(win_ref, wproj_ref, wgates_ref):
    wproj_ref[...] = win_ref[:, :W_GATES].astype(wproj_ref.dtype)
    wgates_ref[...] = win_ref[:, W_GATES:].astype(wgates_ref.dtype)


def _modulation(cvec, w_ada, b_ada, w_in):
    tn = 1536
    n = 6 * D_MODEL
    steps = n // tn
    rows = w_in.shape[1] // steps
    slab = lambda l, j: jnp.where(l == 0, j, steps - 1)
    return pl.pallas_call(
        _mod_kernel,
        grid=(DEPTH, steps),
        in_specs=[
            pl.BlockSpec((MOD_ROWS, D_MODEL), lambda l, j: (0, 0)),
            pl.BlockSpec((None, D_MODEL, tn), lambda l, j: (l, 0, j)),
            pl.BlockSpec((None, 1, tn), lambda l, j: (l, 0, j)),
            pl.BlockSpec((None, rows, D_IN), lambda l, j: (0, slab(l, j), 0)),
        ],
        out_specs=[pl.BlockSpec((None, MOD_ROWS, tn), lambda l, j: (l, 0, j)),
                   pl.BlockSpec((rows, W_GATES), lambda l, j: (slab(l, j), 0)),
                   pl.BlockSpec((rows, N_GATES), lambda l, j: (slab(l, j), 0))],
        out_shape=[jax.ShapeDtypeStruct((DEPTH, MOD_ROWS, n), F32),
                   jax.ShapeDtypeStruct((D_MODEL, W_GATES), BF16),
                   jax.ShapeDtypeStruct((D_MODEL, N_GATES), BF16)],
        compiler_params=_cparams(("arbitrary", "arbitrary")),
        name="modulation",
    )(cvec, w_ada, b_ada.reshape(DEPTH, 1, n), w_in)


def _rope_cols(v, cos, sin):
    first = (lax.broadcasted_iota(jnp.int32, (v.shape[0], LANES), 1) % HEAD_DIM) < HEAD_DIM // 2
    outs = []
    for c in range(v.shape[1] // LANES):
        vc = v[:, c * LANES:(c + 1) * LANES]
        partner = jnp.where(first, pltpu.roll(vc, LANES - HEAD_DIM // 2, 1),
                            pltpu.roll(vc, HEAD_DIM // 2, 1))
        outs.append(vc * cos + partner * sin)
    return outs[0] if len(outs) == 1 else jnp.concatenate(outs, axis=1)


def _with_rope(acc, lo, hi, cos_ref, sin_ref):
    parts = []
    if lo > 0:
        parts.append(acc[:, :lo])
    parts.append(_rope_cols(acc[:, lo:hi], cos_ref[...], sin_ref[...]))
    if hi < acc.shape[1]:
        parts.append(acc[:, hi:])
    return parts[0] if len(parts) == 1 else jnp.concatenate(parts, axis=1)


def _proj_kernel(*refs, latent, n_alias, n_cast):
    x_ref, mod_ref, g_ref, w_ref, bc_ref, bs_ref, pos_ref = refs[:7]
    if latent:
        cos_ref, sin_ref, _, qx_ref, kv_ref, oc_ref = refs[7:]
    else:
        rest = refs[7 + n_alias:]
        cast_src, rest = rest[:n_cast], rest[n_cast:]
        qx_ref, oc_ref, ka_ref, va_ref, kb_ref, vb_ref = rest[:6]
        for src, dst in zip(cast_src, rest[6:6 + n_cast]):
            dst[...] = src[...].astype(dst.dtype)

    h = (_rms(x_ref[...], g_ref[...] * (1.0 + mod_ref[1:2, :])) + mod_ref[0:1, :]).astype(BF16)
    acc = _dot(h, w_ref[:, W_QA:W_KB])
    qa = acc[:, W_QA:W_QA + BRANCH_DIM]
    qb = acc[:, W_QB:W_QB + BRANCH_DIM]
    kva = acc[:, W_KA:W_KA + 2 * KV_A]
    if latent:
        qa = _rope_cols(qa, cos_ref[...], sin_ref[...])
        kva = _with_rope(kva, 0, KV_A, cos_ref, sin_ref)
        qx_ref[:, QX_KVA:QX_KVA + 2 * KV_A] = kva.astype(qx_ref.dtype)
    else:
        ka_ref[...] = kva[:, :KV_A].reshape(ka_ref.shape)
        va_ref[...] = kva[:, KV_A:].reshape(va_ref.shape)
    qx_ref[:, QX_QA:QX_QA + BRANCH_DIM] = qa.astype(qx_ref.dtype)
    qx_ref[:, QX_QB:QX_QB + BRANCH_DIM] = qb.astype(qx_ref.dtype)

    acc = _dot(h, w_ref[:, W_KB:W_GATES])
    kb = acc[:, 0:BRANCH_DIM]
    vb = acc[:, W_VB - W_KB:W_VB - W_KB + BRANCH_DIM]
    uc = acc[:, W_UC - W_KB:W_UC - W_KB + BRANCH_DIM]
    if latent:
        kv_ref[:, KV_KB:KV_KB + BRANCH_DIM] = kb.astype(kv_ref.dtype)
        kv_ref[:, KV_VB:KV_VB + BRANCH_DIM] = vb.astype(kv_ref.dtype)
    else:
        kb_ref[...] = kb.reshape(kb_ref.shape)
        vb_ref[...] = vb.reshape(vb_ref.shape)
    _fourier_mix(uc.astype(BF16), bc_ref, bs_ref, pos_ref, oc_ref)


def _proj(x, mods, l, g_pre, w_in, *, m, x_tile0, tm, mod_row, seq, oc_rows=None, oc_all=None,
          rope_tabs=None, caches=None, cast_weights=()):
    assert tm % seq == 0
    latent = rope_tabs is not None
    bc, bs, pos = _dft_tables(seq)
    pair = 2 * C_GROUP_DIM
    in_specs = [
        pl.BlockSpec((tm, D_MODEL), lambda i: (i + x_tile0, 0)),
        pl.BlockSpec((None, None, 6, D_MODEL), lambda i: (l, mod_row(i), 0, 0)),
        pl.BlockSpec((None, None, 1, D_MODEL), lambda i: (l, 0, 0, 0)),
        _resident((D_MODEL, W_GATES), lambda i: (0, 0)),
        _resident((pair, pair), lambda i: (0, 0)),
        _resident((pair, pair), lambda i: (0, 0)),
        _resident((seq, 2 * seq), lambda i: (0, 0)),
    ]
    args = [x, mods, g_pre, w_in, bc, bs, pos]
    aliases = {}
    if latent:
        seq_tiles = seq // tm
        tab_spec = pl.BlockSpec((tm, LANES), lambda i: (i % seq_tiles, 0))
        in_specs += [tab_spec, tab_spec, pl.BlockSpec(memory_space=pl.ANY)]
        args += list(rope_tabs)
        widths = (2 * BRANCH_DIM + 2 * KV_A, 2 * BRANCH_DIM)
        aliases = {len(args): len(widths)}
        args.append(oc_all)
        oc_rows, oc_tile0 = oc_all.shape[0], (oc_all.shape[0] - m) // tm
        cache_specs, cache_shapes = [], []
    else:
        widths = (2 * BRANCH_DIM,)
        oc_tile0 = 0
        cache_widths = (KV_A, KV_A, BRANCH_DIM, BRANCH_DIM)
        if caches is not None:
            in_specs += [pl.BlockSpec(memory_space=pl.ANY)] * len(caches)
            aliases = {len(args) + k: len(widths) + 1 + k for k in range(len(caches))}
            args += list(caches)
        cache_specs = [pl.BlockSpec((tm // seq, None, seq, w), lambda i: (i, l, 0, 0))
                       for w in cache_widths]
        cache_shapes = [jax.ShapeDtypeStruct((m // seq, DEPTH, seq, w), F32) for w in cache_widths]
        steps = m // tm
        for w in cast_weights:
            _, rows, cols = w.shape
            assert rows % (steps * BF16_SUBLANES) == 0
            in_specs.append(pl.BlockSpec((None, rows // steps, cols), lambda i: (l, i, 0)))
            cache_specs.append(pl.BlockSpec((rows // steps, cols), lambda i: (i, 0)))
            cache_shapes.append(jax.ShapeDtypeStruct((rows, cols), BF16))
            args.append(w)
    return pl.pallas_call(
        functools.partial(_proj_kernel, latent=latent, n_alias=len(aliases), n_cast=len(cast_weights)),
        grid=(m // tm,),
        in_specs=in_specs,
        out_specs=([pl.BlockSpec((tm, w), lambda i: (i, 0)) for w in widths]
                   + [pl.BlockSpec((tm, BRANCH_DIM), lambda i: (i + oc_tile0, 0))] + cache_specs),
        out_shape=([jax.ShapeDtypeStruct((m, w), BF16) for w in widths]
                   + [jax.ShapeDtypeStruct((oc_rows, BRANCH_DIM), BF16)] + cache_shapes),
        input_output_aliases=aliases,
        compiler_params=_cparams(("parallel",)),
        name="proj_lat" if latent else "proj_ctx",
    )(*args)


def _with_ones(v):
    return jnp.concatenate([v, jnp.ones_like(v)], axis=1)


def _softmax_pv(parts, sink=None):
    m = parts[0][0].max(axis=-1, keepdims=True)
    for s, _ in parts[1:]:
        m = jnp.maximum(m, s.max(axis=-1, keepdims=True))
    if sink is not None:
        m = jnp.maximum(m, sink)
    ones_half = parts[0][1].shape[1] == 2 * LANES
    acc = None
    den = None
    for s, v in parts:
        e = jnp.exp(s - m)
        if not ones_half:
            d = e.sum(axis=-1, keepdims=True)
            den = d if den is None else den + d
        o = _dot(e.astype(BF16), v)
        acc = o if acc is None else acc + o
    if ones_half:
        acc, den = acc[:, :LANES], acc[:, LANES:]
    if sink is not None:
        den = den + jnp.exp(sink - m)
    return acc * (1.0 / den)


def _kv_head_variants(x2):
    low = _low_lanes(x2.shape)
    xr = pltpu.roll(x2, HEAD_DIM, 1)
    zero = jnp.zeros_like(x2)
    return [
        [jnp.where(low, x2, zero).astype(BF16), jnp.where(low, zero, xr).astype(BF16)],
        [jnp.where(low, xr, zero).astype(BF16), jnp.where(low, zero, x2).astype(BF16)],
    ]


def _attn_ctx_kernel(sink_ref, qa_ref, qb_ref, ka_ref, va_ref, kb_ref, vb_ref, *rest, n_weights):
    w_src, (oa_ref, ob_ref), w_dst = rest[:n_weights], rest[n_weights:n_weights + 2], rest[n_weights + 2:]
    for src, dst in zip(w_src, w_dst):
        dst[...] = src[...].astype(dst.dtype)
    seq = ka_ref.shape[1]
    chunks = [slice(j * LANES, (j + 1) * LANES) for j in range(A_Q_HEADS // 2)]
    low = _low_lanes((seq, LANES))
    for s in range(ka_ref.shape[0]):
        rows = slice(s * seq, (s + 1) * seq)
        ka = [jnp.concatenate(v, axis=0) for v in _kv_head_variants(ka_ref[s])]
        va = [jnp.concatenate(v, axis=0) for v in _kv_head_variants(va_ref[s])]
        work = []
        for j, cols in enumerate(chunks):
            g = (2 * j) // A_GROUP
            s2 = _dot_t(_scaled(qa_ref[rows, cols]), ka[g])
            work.append((oa_ref, cols, s2, va[g], (sink_ref[2 * j], sink_ref[2 * j + 1])))
        for cols in chunks:
            kc, vc = kb_ref[s, :, cols], vb_ref[s, :, cols]
            zero = jnp.zeros_like(kc)
            k2 = jnp.concatenate([jnp.where(low, kc, zero), jnp.where(low, zero, kc)], axis=0).astype(BF16)
            v2 = jnp.concatenate([jnp.where(low, vc, zero), jnp.where(low, zero, vc)], axis=0).astype(BF16)
            work.append((ob_ref, cols, _dot_t(_scaled(qb_ref[rows, cols]), k2), v2, None))
        for o_ref, cols, s2, v2, sinks in work:
            o_ref[rows, cols] = _pair_softmax_pv(s2, v2, sinks).astype(o_ref.dtype)


def _pair_softmax_pv(s2, v2, sinks=None):
    tk = s2.shape[1] // 2
    es, extra = [], []
    for p in range(2):
        s = s2[:, p * tk:(p + 1) * tk]
        m = s.max(axis=-1, keepdims=True)
        if sinks is not None:
            m = jnp.maximum(m, sinks[p])
            extra.append(jnp.exp(sinks[p] - m))
        es.append(jnp.exp(s - m).astype(BF16))
    lane_half = lax.broadcasted_iota(jnp.int32, v2.shape, 1) // HEAD_DIM
    row_half = lax.broadcasted_iota(jnp.int32, v2.shape, 0) // tk
    ones2 = jnp.where(lane_half == row_half, 1.0, 0.0).astype(v2.dtype)
    acc = _dot(jnp.concatenate(es, axis=1), jnp.concatenate([v2, ones2], axis=1))
    den = acc[:, LANES:]
    if sinks is not None:
        den = den + jnp.where(_low_lanes(den.shape), extra[0], extra[1])
    return acc[:, :LANES] * (1.0 / den)


def _attn_ctx(qx, caches, sink, l, seq, weights, out_rows):
    m = qx.shape[0]
    per_step = CTX_SEQS_PER_STEP
    rows_step = per_step * seq
    steps = m // rows_step
    out = jax.ShapeDtypeStruct((out_rows, BRANCH_DIM), BF16)
    cache_specs = [pl.BlockSpec((per_step, None, seq, c.shape[-1]), lambda b: (b, l, 0, 0)) for c in caches]
    w_in_specs, w_out_specs, w_shapes = [], [], []
    for w in weights:
        _, rows, cols = w.shape
        assert rows % (steps * BF16_SUBLANES) == 0
        w_in_specs.append(pl.BlockSpec((None, rows // steps, cols), lambda b: (l, b, 0)))
        w_out_specs.append(pl.BlockSpec((rows // steps, cols), lambda b: (b, 0)))
        w_shapes.append(jax.ShapeDtypeStruct((rows, cols), BF16))
    return pl.pallas_call(
        functools.partial(_attn_ctx_kernel, n_weights=len(weights)),
        grid=(steps,),
        in_specs=[
            pl.BlockSpec(memory_space=pltpu.SMEM),
            pl.BlockSpec((rows_step, BRANCH_DIM), lambda b: (b, QX_QA // BRANCH_DIM)),
            pl.BlockSpec((rows_step, BRANCH_DIM), lambda b: (b, QX_QB // BRANCH_DIM)),
        ] + cache_specs + w_in_specs,
        out_specs=[pl.BlockSpec((rows_step, BRANCH_DIM), lambda b: (b, 0))] * 2 + w_out_specs,
        out_shape=[out, out] + w_shapes,
        compiler_params=_cparams(("parallel",)),
        name="attn_ctx",
    )(sink, qx, qx, *caches, *weights)


def _attn_a_lat_kernel(sink_ref, q_ref, kv_ref, ck_ref, cv_ref, _, *rest, n_blocks, cast_w_in):
    if cast_w_in:
        win_ref, o_ref, wproj_ref, wgates_ref = rest[:4]
        _split_cast_w_in(win_ref, wproj_ref, wgates_ref)
        rest = rest[4:]
    else:
        o_ref, rest = rest[0], rest[1:]
    qh_scr, kp_scr, vp_scr, ckb_scr, cvb_scr, mask_scr = rest
    seq = n_blocks * A_BLOCK
    band = 3 * A_BLOCK
    low_seq = _low_lanes((seq, LANES))
    for h in range(A_Q_HEADS):
        j, p, g = h // 2, h % 2, h // A_GROUP
        x = q_ref[:, j * LANES:(j + 1) * LANES].astype(F32) * SCALE
        if p != g:
            x = pltpu.roll(x, HEAD_DIM, 1)
        qh_scr[h] = jnp.where(low_seq if g == 0 else jnp.logical_not(low_seq), x, 0.0).astype(BF16)
    pad = jnp.zeros((A_BLOCK, LANES), BF16)
    for scr, col in ((kp_scr, 0), (vp_scr, KV_A)):
        scr[0:A_BLOCK, :] = pad
        scr[A_BLOCK:A_BLOCK + seq, :] = kv_ref[:, col:col + KV_A]
        scr[A_BLOCK + seq:2 * A_BLOCK + seq, :] = pad
    ckb_scr[...] = ck_ref[...].astype(BF16)
    cvb_scr[...] = cv_ref[...].astype(BF16)

    rows = A_GROUP * A_BLOCK
    row = lax.broadcasted_iota(jnp.int32, (rows, band), 0)
    qi = row % A_BLOCK
    kj = lax.broadcasted_iota(jnp.int32, (rows, band), 1)
    head_row = lax.broadcasted_iota(jnp.int32, (rows, 1), 0) // A_BLOCK
    low_blk = _low_lanes((A_BLOCK, LANES))
    @pl.when(pl.program_id(0) == 0)
    def _():
        for case, (no_prev, no_next) in enumerate(((1, 0), (0, 0), (0, 1))):
            valid = (kj >= qi + no_prev * (A_BLOCK - qi)) & (kj <= 2 * A_BLOCK + qi - no_next * (qi + 1))
            mask_scr[case] = jnp.where(valid, 0.0, NEG_INF)

    def body(n, carry):
        start = pl.multiple_of(n * A_BLOCK, A_BLOCK)
        case = jnp.where(n > 0, 1, 0) + jnp.where(n < n_blocks - 1, 0, 1)
        scores = []
        for g in range(A_KV_HEADS):
            q = jnp.concatenate(
                [qh_scr[A_GROUP * g + i, pl.ds(start, A_BLOCK), :] for i in range(A_GROUP)], axis=0)
            s_band = _dot_t(q, kp_scr[pl.ds(start, band), :]) + mask_scr[case]
            scores.append((s_band, _dot_t(q, ckb_scr[...])))
        outs = []
        for g, (s_band, s_ctx) in enumerate(scores):
            sink = jnp.full((rows, 1), sink_ref[A_GROUP * g], F32)
            for i in range(1, A_GROUP):
                sink = jnp.where(head_row == i, sink_ref[A_GROUP * g + i], sink)
            outs.append(_softmax_pv([(s_band, vp_scr[pl.ds(start, band), :]), (s_ctx, cvb_scr[...])],
                                    sink=sink))
        for j in range(A_Q_HEADS // 2):
            halves = []
            for p in range(2):
                h = 2 * j + p
                g, i = h // A_GROUP, h % A_GROUP
                t = outs[g][i * A_BLOCK:(i + 1) * A_BLOCK]
                halves.append(t if p == g else pltpu.roll(t, HEAD_DIM, 1))
            o_ref[pl.ds(start, A_BLOCK), j * LANES:(j + 1) * LANES] = (
                jnp.where(low_blk, halves[0], halves[1]).astype(o_ref.dtype))
        return carry

    lax.fori_loop(0, n_blocks, body, 0, unroll=2)


def _attn_a_lat(main, kv, cache_k, cache_v, sink, l, seq, oa_all, w_in_next=None):
    m = main.shape[0]
    nb = seq // A_BLOCK
    assert nb >= 2
    n_seq = m // seq
    past = cache_k.shape[2]
    first = (oa_all.shape[0] - m) // seq
    cache_spec = pl.BlockSpec((None, None, past, KV_A), lambda b: (b, l, 0, 0))
    in_specs = [
        pl.BlockSpec(memory_space=pltpu.SMEM),
        pl.BlockSpec((seq, BRANCH_DIM), lambda b: (b, QX_QA // BRANCH_DIM)),
        pl.BlockSpec((seq, 2 * KV_A), lambda b: (b, QX_KVA // (2 * KV_A))),
        cache_spec, cache_spec,
        pl.BlockSpec(memory_space=pl.ANY),
    ]
    args = [sink, main, kv, cache_k, cache_v, oa_all]
    out_specs = [pl.BlockSpec((seq, BRANCH_DIM), lambda b: (b + first, 0))]
    out_shape = [jax.ShapeDtypeStruct(oa_all.shape, BF16)]
    if w_in_next is not None:
        w_in, layer = w_in_next
        rows = w_in.shape[1] // n_seq
        assert w_in.shape[1] % (n_seq * BF16_SUBLANES) == 0
        in_specs.append(pl.BlockSpec((None, rows, D_IN), lambda b: (layer, b, 0)))
        args.append(w_in)
        out_specs += [pl.BlockSpec((rows, W_GATES), lambda b: (b, 0)),
                      pl.BlockSpec((rows, N_GATES), lambda b: (b, 0))]
        out_shape += [jax.ShapeDtypeStruct((D_MODEL, W_GATES), BF16),
                      jax.ShapeDtypeStruct((D_MODEL, N_GATES), BF16)]
    return pl.pallas_call(
        functools.partial(_attn_a_lat_kernel, n_blocks=nb, cast_w_in=w_in_next is not None),
        grid=(n_seq,),
        in_specs=in_specs,
        out_specs=out_specs,
        out_shape=out_shape,
        input_output_aliases={5: 0},
        scratch_shapes=[
            pltpu.VMEM((A_Q_HEADS, seq, LANES), BF16),
            pltpu.VMEM((seq + 2 * A_BLOCK, LANES), BF16),
            pltpu.VMEM((seq + 2 * A_BLOCK, LANES), BF16),
            pltpu.VMEM((past, LANES), BF16),
            pltpu.VMEM((past, LANES), BF16),
            pltpu.VMEM((3, A_GROUP * A_BLOCK, 3 * A_BLOCK), F32),
        ],
        compiler_params=_cparams(("arbitrary",)),
        name="attn_a_lat",
    )(*args)


def _attn_b_lat_kernel(q_ref, k_ref, v_ref, ck_ref, cv_ref, rpb_ref, _, o_ref, ckb_scr, vx_scr, cvx_scr,
                       bias_ref, *, rows, kr):
    @pl.when(pl.program_id(0) == 0)
    def _():
        _fill_neighbourhood_bias(rpb_ref, bias_ref, kr=kr)

    half = kr // 2
    n_loc = kr * GRID_W
    low_q = _low_lanes((GRID_W, LANES))
    chunks = [slice(j * LANES, (j + 1) * LANES) for j in range(B_HEADS // 2)]
    ckb_scr[...] = ck_ref[...].astype(BF16)
    for j, cols in enumerate(chunks):
        vx_scr[j] = _with_ones(v_ref[:, cols])
        cvx_scr[j] = _with_ones(cv_ref[:, cols].astype(BF16))
    rows_per_iter = 4
    assert rows % rows_per_iter == 0

    def body(it, carry):
        work = []
        for u in range(rows_per_iter):
            r = it * rows_per_iter + u
            r0 = jnp.clip(r - half, 0, rows - kr)
            q0 = pl.multiple_of(r * GRID_W, GRID_W)
            start = pl.multiple_of(r0 * GRID_W, GRID_W)
            for j, cols in enumerate(chunks):
                qc = _scaled(q_ref[pl.ds(q0, GRID_W), cols])
                zero = jnp.zeros_like(qc)
                qs = jnp.concatenate([jnp.where(low_q, qc, zero), jnp.where(low_q, zero, qc)], axis=0)
                s_loc = _dot_t(qs, k_ref[pl.ds(start, n_loc), cols]) + bias_ref[r - r0, j]
                work.append((q0, start, j, cols, s_loc, _dot_t(qs, ckb_scr[:, cols])))
        for q0, start, j, cols, s_loc, s_ctx in work:
            o2 = _softmax_pv([(s_loc, vx_scr[j, pl.ds(start, n_loc), :]), (s_ctx, cvx_scr[j])])
            o = jnp.where(low_q, o2[:GRID_W], o2[GRID_W:])
            o_ref[pl.ds(q0, GRID_W), cols] = o.astype(o_ref.dtype)
        return carry

    lax.fori_loop(0, rows // rows_per_iter, body, 0)


def _attn_b_lat(main, kv, cache_k, cache_v, rpb_pad, l, seq, ob_all):
    m = main.shape[0]
    rows = seq // GRID_W
    kr = min(B_WIN_ROWS, rows)
    assert kr % 2 == 0 and 2 * GRID_W == LANES
    n_seq = m // seq
    past = cache_k.shape[2]
    first = (ob_all.shape[0] - m) // seq
    cache_spec = pl.BlockSpec((None, None, past, BRANCH_DIM), lambda b: (b, l, 0, 0))
    return pl.pallas_call(
        functools.partial(_attn_b_lat_kernel, rows=rows, kr=kr),
        grid=(n_seq,),
        in_specs=[
            pl.BlockSpec((seq, BRANCH_DIM), lambda b: (b, QX_QB // BRANCH_DIM)),
            pl.BlockSpec((seq, BRANCH_DIM), lambda b: (b, KV_KB // BRANCH_DIM)),
            pl.BlockSpec((seq, BRANCH_DIM), lambda b: (b, KV_VB // BRANCH_DIM)),
            cache_spec, cache_spec,
            _resident((None,) + rpb_pad.shape[1:], lambda b: (l, 0, 0, 0)),
            pl.BlockSpec(memory_space=pl.ANY),
        ],
        out_specs=pl.BlockSpec((seq, BRANCH_DIM), lambda b: (b + first, 0)),
        out_shape=jax.ShapeDtypeStruct(ob_all.shape, BF16),
        input_output_aliases={6: 0},
        scratch_shapes=[
            pltpu.VMEM((past, BRANCH_DIM), BF16),
            pltpu.VMEM((B_HEADS // 2, seq, 2 * LANES), BF16),
            pltpu.VMEM((B_HEADS // 2, past, 2 * LANES), BF16),
            pltpu.VMEM((kr, B_HEADS // 2, 2 * GRID_W, kr * GRID_W), F32),
        ],
        compiler_params=_cparams(("arbitrary",)),
        name="attn_b_lat",
    )(main, kv, kv, cache_k, cache_v, rpb_pad, ob_all)


def _fill_neighbourhood_bias(rpb_ref, bias_scr, *, kr):
    shape = (GRID_W, LANES)
    c = lax.broadcasted_iota(jnp.int32, shape, 0)
    lane = lax.broadcasted_iota(jnp.int32, shape, 1)
    c2 = lane % GRID_W
    ws = jnp.clip(c - B_WIN_COLS // 2, 0, GRID_W - B_WIN_COLS)
    ok = (c2 >= ws) & (c2 < ws + B_WIN_COLS)
    low = lane < GRID_W

    def toeplitz(h, dr, lane0):
        row = jnp.broadcast_to(rpb_ref[h, dr:dr + 1, :], shape)
        return pltpu.roll(row, (lane0 - (B_WIN_COLS - 1)) % LANES, 1, stride=1, stride_axis=0)

    for h in range(B_HEADS):
        rows = slice(h % 2 * GRID_W, (h % 2 + 1) * GRID_W)
        pairs = {}
        for v in range(kr):
            for m in range(0, kr, 2):
                dr = m - v + B_WIN_ROWS - 1
                if dr not in pairs:
                    pair = jnp.where(low, toeplitz(h, dr, 0), toeplitz(h, dr + 1, GRID_W))
                    pairs[dr] = jnp.where(ok, pair, NEG_INF)
                bias_scr[v, h // 2, rows, m * GRID_W:(m + 2) * GRID_W] = pairs[dr]


def _dft_tables(seq):
    cd = C_GROUP_DIM
    kc = (np.arange(cd)[:, None] * np.arange(cd)[None, :]) % cd
    ang_c = 2.0 * np.pi * kc / cd
    eye2 = np.eye(2)
    bd_cos = np.kron(eye2, np.cos(ang_c))
    bd_sin = np.kron(eye2, np.sin(ang_c))
    kt = (np.arange(seq)[:, None] * np.arange(seq)[None, :]) % seq
    ang_t = 2.0 * np.pi * kt / seq
    norm = 1.0 / np.sqrt(float(seq * cd))
    pos = np.concatenate([np.cos(ang_t), -np.sin(ang_t)], axis=1) * norm
    return (jnp.asarray(bd_cos, F32).astype(BF16), jnp.asarray(bd_sin, F32).astype(BF16),
            jnp.asarray(pos, F32).astype(BF16))


def _fourier_mix(u, bc_ref, bs_ref, pos_ref, o_ref):
    pair = 2 * C_GROUP_DIM
    seq = pos_ref.shape[0]
    uc, us = [], []
    for p in range(BRANCH_DIM // pair):
        up = u[:, p * pair:(p + 1) * pair]
        uc.append(_dot(up, bc_ref[...]))
        us.append(_dot(up, bs_ref[...]))
    zc = jnp.concatenate(uc, axis=1).astype(BF16)
    zs = jnp.concatenate(us, axis=1).astype(BF16)
    for s in range(u.shape[0] // seq):
        rows = slice(s * seq, (s + 1) * seq)
        z = jnp.concatenate([zc[rows], zs[rows]], axis=0)
        o_ref[rows, :] = _dot(pos_ref[...], z).astype(o_ref.dtype)


def _ffn_weight_copies(wfi_hbm, wfo_hbm, wfg_ref, wfu_ref, wfo_ref, sems):
    srcs = (wfi_hbm.at[:, 0:D_FF], wfi_hbm.at[:, D_FF:2 * D_FF], wfo_hbm)
    dsts = (wfg_ref, wfu_ref, wfo_ref)
    return [pltpu.make_async_copy(src, dst, sems.at[k]) for k, (src, dst) in enumerate(zip(srcs, dsts))]


def _tail_kernel(*refs, n_x, n_out, n_ctx_tiles):
    oa_ref, ob_ref, oc_ref = refs[:3]
    x_refs, refs = refs[3:3 + n_x], refs[3 + n_x:]
    (mod_ref, gpre0_ref, gpost0_ref, gpre1_ref, gpost1_ref,
     wg_ref, wb_ref, wo_ref, wfi_hbm, wfo_hbm) = refs[:10]
    o_refs, refs = refs[10:10 + n_out], refs[10 + n_out:]
    wfg_ref, wfu_ref, wfo_ref, sems = refs[:4]
    i = pl.program_id(0)
    first = i == 0
    is_ctx = i < n_ctx_tiles
    copies = _ffn_weight_copies(wfi_hbm, wfo_hbm, wfg_ref, wfu_ref, wfo_ref, sems)

    @pl.when(first)
    def _():
        for copy in copies:
            copy.start()

    if n_x == 1:
        x = x_refs[0][...]
    else:
        pick_ctx = jnp.full(x_refs[0].shape, jnp.where(is_ctx, 1, 0), jnp.int32) > 0
        x = jnp.where(pick_ctx, x_refs[0][...], x_refs[1][...])
    h = (_rms(x, gpre0_ref[...] * (1.0 + mod_ref[1:2, :])) + mod_ref[0:1, :]).astype(BF16)
    gates = _dot(h, wg_ref[...])
    branches = [_dot(b_ref[...], wb_ref[k]) for k, b_ref in enumerate((oa_ref, ob_ref, oc_ref))]
    mix = None
    for k, branch in enumerate(branches):
        term = _sigmoid(gates[:, k * D_MODEL:(k + 1) * D_MODEL]) * branch
        mix = term if mix is None else mix + term
    y = _dot(mix.astype(BF16), wo_ref[...])
    x1 = x + _rms(y, mod_ref[2:3, :] * gpost0_ref[...])
    h2 = (_rms(x1, gpre1_ref[...] * (1.0 + mod_ref[4:5, :])) + mod_ref[3:4, :]).astype(BF16)

    @pl.when(first)
    def _():
        for copy in copies:
            copy.wait()

    act = _silu(_dot(h2, wfg_ref[...])) * _dot(h2, wfu_ref[...])
    f = _dot(act.astype(BF16), wfo_ref[...])
    out = x1 + _rms(f, mod_ref[5:6, :] * gpost1_ref[...])
    if n_out == 1:
        o_refs[0][...] = out
    else:
        @pl.when(is_ctx)
        def _():
            o_refs[0][...] = out

        @pl.when(jnp.logical_not(is_ctx))
        def _():
            o_refs[1][...] = out


def _tail(oa, ob, oc, xs, mods, l, g_pre, g_post, w_gates, w_branch, w_out, w_ffn_in, w_ffn_out,
          *, tm, m_ctx, mod_row, split_out):
    m = oa.shape[0]
    n_ctx = m_ctx // tm
    o_spec = pl.BlockSpec((tm, BRANCH_DIM), lambda i: (i, 0))
    whole = pl.BlockSpec((tm, D_MODEL), lambda i: (i, 0))
    halves = [pl.BlockSpec((tm, D_MODEL), lambda i: (jnp.minimum(i, n_ctx - 1), 0)),
              pl.BlockSpec((tm, D_MODEL), lambda i: (jnp.maximum(i - n_ctx, 0), 0))]
    half_shapes = [jax.ShapeDtypeStruct((m_ctx, D_MODEL), F32),
                   jax.ShapeDtypeStruct((m - m_ctx, D_MODEL), F32)]
    gain_spec = lambda which: pl.BlockSpec((None, None, 1, D_MODEL), lambda i: (l, which, 0, 0))
    scratch = [
        pltpu.VMEM((D_MODEL, D_FF), BF16),
        pltpu.VMEM((D_MODEL, D_FF), BF16),
        pltpu.VMEM((D_FF, D_MODEL), BF16),
        pltpu.SemaphoreType.DMA((3,)),
    ]
    return pl.pallas_call(
        functools.partial(_tail_kernel, n_x=len(xs), n_out=2 if split_out else 1, n_ctx_tiles=n_ctx),
        grid=(m // tm,),
        in_specs=[o_spec, o_spec, o_spec] + (halves if len(xs) == 2 else [whole]) + [
            pl.BlockSpec((None, None, 6, D_MODEL), lambda i: (l, mod_row(i), 0, 0)),
            gain_spec(0), gain_spec(0), gain_spec(1), gain_spec(1),
            _resident((D_MODEL, N_GATES), lambda i: (0, 0)),
            _resident((N_BRANCH, BRANCH_DIM, D_MODEL), lambda i: (0, 0, 0)),
            _resident((D_MODEL, D_MODEL), lambda i: (0, 0)),
            pl.BlockSpec(memory_space=pl.ANY),
            pl.BlockSpec(memory_space=pl.ANY),
        ],
        out_specs=halves if split_out else whole,
        out_shape=half_shapes if split_out else jax.ShapeDtypeStruct((m, D_MODEL), F32),
        scratch_shapes=scratch,
        compiler_params=_cparams(("arbitrary",)),
        name="tail",
    )(oa, ob, oc, *xs, mods, g_pre, g_post, g_pre, g_post, w_gates, w_branch, w_out,
      w_ffn_in, w_ffn_out)


def _rope_tables(seq):
    t = np.arange(seq)
    row = (t // GRID_W).astype(np.float32)
    col = (t % GRID_W).astype(np.float32)
    n_pairs_axis = HEAD_DIM // 4
    inv = (np.float32(ROPE_BASE) ** (-np.arange(n_pairs_axis, dtype=np.float32) / n_pairs_axis)
           ).astype(np.float32)
    ang = np.concatenate([row[:, None] * inv, col[:, None] * inv], axis=-1).astype(np.float64)
    cos, sin = np.cos(ang), np.sin(ang)
    cos_l = np.tile(cos, (1, LANES // cos.shape[1]))
    sin_l = np.tile(np.concatenate([-sin, sin], axis=-1), (1, LANES // HEAD_DIM))
    return jnp.asarray(cos_l, F32), jnp.asarray(sin_l, F32)


def kernel(x_prompt, x_sample, cache_a_k, cache_a_v, cache_b_k, cache_b_v, c, c_ctx, w_ada, b_ada,
           norm_pre, norm_post, w_in, a_sink, b_rpb, w_branch, w_out, w_ffn_in, w_ffn_out):
    batch, seq, _ = x_prompt.shape
    dec_batch, dec_seq, _ = x_sample.shape
    past = cache_a_k.shape[2]
    assert dec_batch <= CTX_MOD_ROW and seq % A_BLOCK == 0 and dec_seq % A_BLOCK == 0

    cvec = jnp.concatenate(
        [c, c_ctx[None, :], jnp.zeros((MOD_ROWS - dec_batch - 1, D_MODEL), F32)], axis=0)
    assert w_in.shape[-1] == D_IN
    mods, w_proj_b, w_gates_b = _modulation(cvec, w_ada, b_ada, w_in)
    mods = mods.reshape(DEPTH, MOD_ROWS, 6, D_MODEL)
    w_branch_2d = w_branch.reshape(DEPTH, N_BRANCH * BRANCH_DIM, D_MODEL)
    g_pre = norm_pre.reshape(DEPTH, 2, 1, D_MODEL)
    g_post = norm_post.reshape(DEPTH, 2, 1, D_MODEL)
    rope_tabs = _rope_tables(dec_seq)
    cak = cache_a_k.reshape(dec_batch, DEPTH, past, KV_A)
    cav = cache_a_v.reshape(dec_batch, DEPTH, past, KV_A)
    cbk = cache_b_k.reshape(dec_batch, DEPTH, past, BRANCH_DIM)
    cbv = cache_b_v.reshape(dec_batch, DEPTH, past, BRANCH_DIM)

    n_dr, n_dc = b_rpb.shape[2:]
    rpb_pad = jnp.pad(b_rpb.astype(F32), ((0, 0), (0, 0), (0, (-n_dr) % 8), (0, LANES - n_dc)))

    tm_proj, tm_tail = 1024, 512
    m_ctx, m_lat = batch * seq, dec_batch * dec_seq
    m_all = m_ctx + m_lat
    assert m_ctx % tm_proj == 0 and m_lat % tm_proj == 0
    ctx_row = lambda i: CTX_MOD_ROW
    lat_row = lambda i: (i * tm_proj) // dec_seq
    tail_row = lambda i: jnp.where(i < m_ctx // tm_tail, CTX_MOD_ROW, (i * tm_tail - m_ctx) // dec_seq)

    xs = (x_prompt.reshape(m_ctx, D_MODEL), x_sample.reshape(m_lat, D_MODEL))
    caches = None
    for l in range(DEPTH):
        x_ctx, x_lat = (xs[0], xs[1]) if len(xs) == 2 else (xs[0], xs[0])
        lat_tile0 = 0 if len(xs) == 2 else m_ctx // tm_proj
        qx, oc, *rest = _proj(x_ctx, mods, l, g_pre, w_proj_b, m=m_ctx, x_tile0=0, tm=tm_proj,
                              mod_row=ctx_row, seq=seq, oc_rows=m_all, caches=caches,
                              cast_weights=(w_ffn_in,))
        caches, (w_ffn_in_b,) = rest[:4], rest[4:]
        oa, ob, w_branch_b, w_out_b, w_ffn_out_b = _attn_ctx(
            qx, caches, a_sink[l], l, seq, (w_branch_2d, w_out, w_ffn_out), m_all)
        qx, kv, oc = _proj(x_lat, mods, l, g_pre, w_proj_b, m=m_lat, x_tile0=lat_tile0, tm=tm_proj,
                           mod_row=lat_row, seq=dec_seq, oc_all=oc, rope_tabs=rope_tabs)
        last = l == DEPTH - 1
        w_gates_l = w_gates_b
        if last:
            oa, = _attn_a_lat(qx, qx, cak, cav, a_sink[l], l, dec_seq, oa)
        else:
            oa, w_proj_b, w_gates_b = _attn_a_lat(qx, qx, cak, cav, a_sink[l], l, dec_seq, oa,
                                                  w_in_next=(w_in, l + 1))
        ob = _attn_b_lat(qx, kv, cbk, cbv, rpb_pad, l, dec_seq, ob)
        out = _tail(oa, ob, oc, xs, mods, l, g_pre, g_post, w_gates_l,
                    w_branch_b.reshape(N_BRANCH, BRANCH_DIM, D_MODEL), w_out_b, w_ffn_in_b, w_ffn_out_b,
                    tm=tm_tail, m_ctx=m_ctx, mod_row=tail_row, split_out=last)
        xs = tuple(out) if last else (out,)
    y_prompt = xs[0].reshape(batch, seq, D_MODEL)
    y_sample = xs[1].reshape(dec_batch, dec_seq, D_MODEL)
    nak, nav, nbk, nbv = caches
    new_a_k = nak.reshape(batch, DEPTH, seq, A_KV_HEADS, HEAD_DIM)
    new_a_v = nav.reshape(batch, DEPTH, seq, A_KV_HEADS, HEAD_DIM)
    new_b_k = nbk.reshape(batch, DEPTH, seq, B_HEADS, HEAD_DIM)
    new_b_v = nbv.reshape(batch, DEPTH, seq, B_HEADS, HEAD_DIM)
    return (y_prompt, y_sample, new_a_k, new_a_v, new_b_k, new_b_v)
```

```python
import functools

import numpy as np
import jax
import jax.numpy as jnp
from jax import lax
from jax.experimental import pallas as pl
from jax.experimental.pallas import tpu as pltpu

F32 = jnp.float32
BF16 = jnp.bfloat16

D_MODEL = 1024
DEPTH = 2
GRID_W = 64
HEAD_DIM = 64
BRANCH_DIM = D_MODEL // 2
A_Q_HEADS = BRANCH_DIM // HEAD_DIM
A_KV_HEADS = A_Q_HEADS // 4
A_GROUP = A_Q_HEADS // A_KV_HEADS
A_BLOCK = 128
B_HEADS = BRANCH_DIM // HEAD_DIM
B_WIN_ROWS = 8
B_WIN_COLS = 16
C_GROUPS = 4
C_GROUP_DIM = BRANCH_DIM // C_GROUPS
N_BRANCH = 3
D_FF = -(-8 * D_MODEL // (3 * 256)) * 256
ROPE_BASE = 10000.0
NORM_EPS = 1e-6
NEG_INF = -1e30
SCALE = HEAD_DIM ** -0.5

LANES = 128
KV_A = A_KV_HEADS * HEAD_DIM
N_GATES = N_BRANCH * D_MODEL
W_QA = 0
W_KA = W_QA + BRANCH_DIM
W_VA = W_KA + KV_A
W_QB = W_VA + KV_A
W_KB = W_QB + BRANCH_DIM
W_VB = W_KB + BRANCH_DIM
W_UC = W_VB + BRANCH_DIM
W_GATES = W_UC + BRANCH_DIM
D_IN = W_GATES + N_GATES
QX_QA, QX_QB, QX_KVA = 0, BRANCH_DIM, 2 * BRANCH_DIM
KV_KB, KV_VB = 0, BRANCH_DIM
MOD_ROWS = 8
CTX_MOD_ROW = 4
VMEM_LIMIT = 56 * 1024 * 1024
BF16_SUBLANES = 16
CTX_SEQS_PER_STEP = 2


def _cparams(sem):
    return pltpu.CompilerParams(dimension_semantics=sem, vmem_limit_bytes=VMEM_LIMIT)


def _resident(shape, index_map):
    return pl.BlockSpec(shape, index_map, pipeline_mode=pl.Buffered(1))


def _rms(x, w):
    return x * lax.rsqrt(jnp.mean(x * x, axis=-1, keepdims=True) + NORM_EPS) * w


def _sigmoid(x):
    return 0.5 * jnp.tanh(0.5 * x) + 0.5


def _silu(x):
    half = 0.5 * x
    return half + half * jnp.tanh(half)


def _dot_t(a, b):
    return lax.dot_general(a, b, (((1,), (1,)), ((), ())), preferred_element_type=F32)


def _dot(a, b):
    return jnp.dot(a, b, preferred_element_type=F32)


def _scaled(q):
    assert np.log2(SCALE) == round(np.log2(SCALE))
    return q * jnp.asarray(SCALE, q.dtype)


def _low_lanes(shape):
    return lax.broadcasted_iota(jnp.int32, shape, len(shape) - 1) < HEAD_DIM


def _mod_kernel(cv_ref, w_ref, b_ref, win_ref, o_ref, wproj_ref, wgates_ref):
    a, b = _silu(cv_ref[...]), w_ref[...]
    a_hi, b_hi = a.astype(BF16), b.astype(BF16)
    a_lo = (a - a_hi.astype(F32)).astype(BF16)
    b_lo = (b - b_hi.astype(F32)).astype(BF16)
    bias = b_ref[pl.ds(pl.program_id(0), 1), :]
    o_ref[...] = _dot(a_hi, b_hi) + (_dot(a_lo, b_hi) + _dot(a_hi, b_lo)) + bias
    @pl.when(pl.program_id(0) == 0)
    def _():
        _split_cast_w_in(win_ref, wproj_ref, wgates_ref)


def _split_cast_w_in(win_ref, wproj_ref, wgates_ref):
    wproj_ref[...] = win_ref[:, :W_GATES].astype(wproj_ref.dtype)
    wgates_ref[...] = win_ref[:, W_GATES:].astype(wgates_ref.dtype)


def _modulation(cvec, w_ada, b_ada, w_in):
    tn = 1536
    n = 6 * D_MODEL
    steps = n // tn
    rows = w_in.shape[1] // steps
    slab = lambda l, j: jnp.where(l == 0, j, steps - 1)
    return pl.pallas_call(
        _mod_kernel,
        grid=(DEPTH, steps),
        in_specs=[
            pl.BlockSpec((MOD_ROWS, D_MODEL), lambda l, j: (0, 0)),
            pl.BlockSpec((None, D_MODEL, tn), lambda l, j: (l, 0, j)),
            pl.BlockSpec((DEPTH, tn), lambda l, j: (0, j)),
            pl.BlockSpec((None, rows, D_IN), lambda l, j: (0, slab(l, j), 0)),
        ],
        out_specs=[pl.BlockSpec((None, MOD_ROWS, tn), lambda l, j: (l, 0, j)),
                   pl.BlockSpec((rows, W_GATES), lambda l, j: (slab(l, j), 0)),
                   pl.BlockSpec((rows, N_GATES), lambda l, j: (slab(l, j), 0))],
        out_shape=[jax.ShapeDtypeStruct((DEPTH, MOD_ROWS, n), F32),
                   jax.ShapeDtypeStruct((D_MODEL, W_GATES), BF16),
                   jax.ShapeDtypeStruct((D_MODEL, N_GATES), BF16)],
        compiler_params=_cparams(("arbitrary", "arbitrary")),
        name="modulation",
    )(cvec, w_ada, b_ada, w_in)


def _rope_cols(v, cos, sin):
    first = (lax.broadcasted_iota(jnp.int32, (v.shape[0], LANES), 1) % HEAD_DIM) < HEAD_DIM // 2
    outs = []
    for c in range(v.shape[1] // LANES):
        vc = v[:, c * LANES:(c + 1) * LANES]
        partner = jnp.where(first, pltpu.roll(vc, LANES - HEAD_DIM // 2, 1),
                            pltpu.roll(vc, HEAD_DIM // 2, 1))
        outs.append(vc * cos + partner * sin)
    return outs[0] if len(outs) == 1 else jnp.concatenate(outs, axis=1)


def _with_rope(acc, lo, hi, cos_ref, sin_ref):
    parts = []
    if lo > 0:
        parts.append(acc[:, :lo])
    parts.append(_rope_cols(acc[:, lo:hi], cos_ref[...], sin_ref[...]))
    if hi < acc.shape[1]:
        parts.append(acc[:, hi:])
    return parts[0] if len(parts) == 1 else jnp.concatenate(parts, axis=1)


def _proj_kernel(*refs, latent, n_alias, n_cast):
    x_ref, mod_ref, g_ref, w_ref, bc_ref, bs_ref, pos_ref = refs[:7]
    if latent:
        cos_ref, sin_ref, _, qx_ref, kv_ref, oc_ref = refs[7:]
    else:
        rest = refs[7 + n_alias:]
        cast_src, rest = rest[:n_cast], rest[n_cast:]
        qx_ref, oc_ref, ka_ref, va_ref, kb_ref, vb_ref = rest[:6]
        for src, dst in zip(cast_src, rest[6:6 + n_cast]):
            dst[...] = src[...].astype(dst.dtype)

    h = (_rms(x_ref[...], g_ref[0:1, :] * (1.0 + mod_ref[1:2, :])) + mod_ref[0:1, :]).astype(BF16)
    acc = _dot(h, w_ref[:, W_QA:W_KB])
    qa = acc[:, W_QA:W_QA + BRANCH_DIM]
    qb = acc[:, W_QB:W_QB + BRANCH_DIM]
    kva = acc[:, W_KA:W_KA + 2 * KV_A]
    if latent:
        qa = _rope_cols(qa, cos_ref[...], sin_ref[...])
        kva = _with_rope(kva, 0, KV_A, cos_ref, sin_ref)
        qx_ref[:, QX_KVA:QX_KVA + 2 * KV_A] = kva.astype(qx_ref.dtype)
    else:
        ka_ref[...] = kva[:, :KV_A].reshape(ka_ref.shape)
        va_ref[...] = kva[:, KV_A:].reshape(va_ref.shape)
    qx_ref[:, QX_QA:QX_QA + BRANCH_DIM] = qa.astype(qx_ref.dtype)
    qx_ref[:, QX_QB:QX_QB + BRANCH_DIM] = qb.astype(qx_ref.dtype)

    acc = _dot(h, w_ref[:, W_KB:W_GATES])
    kb = acc[:, 0:BRANCH_DIM]
    vb = acc[:, W_VB - W_KB:W_VB - W_KB + BRANCH_DIM]
    uc = acc[:, W_UC - W_KB:W_UC - W_KB + BRANCH_DIM]
    if latent:
        kv_ref[:, KV_KB:KV_KB + BRANCH_DIM] = kb.astype(kv_ref.dtype)
        kv_ref[:, KV_VB:KV_VB + BRANCH_DIM] = vb.astype(kv_ref.dtype)
    else:
        kb_ref[...] = kb.reshape(kb_ref.shape)
        vb_ref[...] = vb.reshape(vb_ref.shape)
    _fourier_mix(uc.astype(BF16), bc_ref, bs_ref, pos_ref, oc_ref)


def _proj(x, mods, l, g_pre, w_in, *, m, x_tile0, tm, mod_row, seq, oc_rows=None, oc_all=None,
          rope_tabs=None, caches=None, cast_weights=()):
    assert tm % seq == 0
    latent = rope_tabs is not None
    bc, bs, pos = _dft_tables(seq)
    pair = 2 * C_GROUP_DIM
    in_specs = [
        pl.BlockSpec((tm, D_MODEL), lambda i: (i + x_tile0, 0)),
        pl.BlockSpec((None, None, 6, D_MODEL), lambda i: (l, mod_row(i), 0, 0)),
        pl.BlockSpec((None, 2, D_MODEL), lambda i: (l, 0, 0)),
        _resident((D_MODEL, W_GATES), lambda i: (0, 0)),
        _resident((pair, pair), lambda i: (0, 0)),
        _resident((pair, pair), lambda i: (0, 0)),
        _resident((seq, 2 * seq), lambda i: (0, 0)),
    ]
    args = [x, mods, g_pre, w_in, bc, bs, pos]
    aliases = {}
    if latent:
        seq_tiles = seq // tm
        tab_spec = pl.BlockSpec((tm, LANES), lambda i: (i % seq_tiles, 0))
        in_specs += [tab_spec, tab_spec, pl.BlockSpec(memory_space=pl.ANY)]
        args += list(rope_tabs)
        widths = (2 * BRANCH_DIM + 2 * KV_A, 2 * BRANCH_DIM)
        aliases = {len(args): len(widths)}
        args.append(oc_all)
        oc_rows, oc_tile0 = oc_all.shape[0], (oc_all.shape[0] - m) // tm
        cache_specs, cache_shapes = [], []
    else:
        widths = (2 * BRANCH_DIM,)
        oc_tile0 = 0
        cache_widths = (KV_A, KV_A, BRANCH_DIM, BRANCH_DIM)
        if caches is not None:
            in_specs += [pl.BlockSpec(memory_space=pl.ANY)] * len(caches)
            aliases = {len(args) + k: len(widths) + 1 + k for k in range(len(caches))}
            args += list(caches)
        cache_specs = [pl.BlockSpec((tm // seq, None, seq, w), lambda i: (i, l, 0, 0))
                       for w in cache_widths]
        cache_shapes = [jax.ShapeDtypeStruct((m // seq, DEPTH, seq, w), F32) for w in cache_widths]
        steps = m // tm
        for w in cast_weights:
            _, rows, cols = w.shape
            assert rows % (steps * BF16_SUBLANES) == 0
            in_specs.append(pl.BlockSpec((None, rows // steps, cols), lambda i: (l, i, 0)))
            cache_specs.append(pl.BlockSpec((rows // steps, cols), lambda i: (i, 0)))
            cache_shapes.append(jax.ShapeDtypeStruct((rows, cols), BF16))
            args.append(w)
    return pl.pallas_call(
        functools.partial(_proj_kernel, latent=latent, n_alias=len(aliases), n_cast=len(cast_weights)),
        grid=(m // tm,),
        in_specs=in_specs,
        out_specs=([pl.BlockSpec((tm, w), lambda i: (i, 0)) for w in widths]
                   + [pl.BlockSpec((tm, BRANCH_DIM), lambda i: (i + oc_tile0, 0))] + cache_specs),
        out_shape=([jax.ShapeDtypeStruct((m, w), BF16) for w in widths]
                   + [jax.ShapeDtypeStruct((oc_rows, BRANCH_DIM), BF16)] + cache_shapes),
        input_output_aliases=aliases,
        compiler_params=_cparams(("parallel",)),
        name="proj_lat" if latent else "proj_ctx",
    )(*args)


def _with_ones(v):
    return jnp.concatenate([v, jnp.ones_like(v)], axis=1)


def _softmax_pv(parts, sink=None):
    m = parts[0][0].max(axis=-1, keepdims=True)
    for s, _ in parts[1:]:
        m = jnp.maximum(m, s.max(axis=-1, keepdims=True))
    if sink is not None:
        m = jnp.maximum(m, sink)
    ones_half = parts[0][1].shape[1] == 2 * LANES
    acc = None
    den = None
    for s, v in parts:
        e = jnp.exp(s - m)
        if not ones_half:
            d = e.sum(axis=-1, keepdims=True)
            den = d if den is None else den + d
        o = _dot(e.astype(BF16), v)
        acc = o if acc is None else acc + o
    if ones_half:
        acc, den = acc[:, :LANES], acc[:, LANES:]
    if sink is not None:
        den = den + jnp.exp(sink - m)
    return acc * (1.0 / den)


def _kv_head_variants(x2):
    low = _low_lanes(x2.shape)
    xr = pltpu.roll(x2, HEAD_DIM, 1)
    zero = jnp.zeros_like(x2)
    return [
        [jnp.where(low, x2, zero).astype(BF16), jnp.where(low, zero, xr).astype(BF16)],
        [jnp.where(low, xr, zero).astype(BF16), jnp.where(low, zero, x2).astype(BF16)],
    ]


def _attn_ctx_kernel(sink_ref, qa_ref, qb_ref, ka_ref, va_ref, kb_ref, vb_ref, *rest, n_weights, layer):
    w_src, (oa_ref, ob_ref), w_dst = rest[:n_weights], rest[n_weights:n_weights + 2], rest[n_weights + 2:]
    for src, dst in zip(w_src, w_dst):
        dst[...] = src[...].astype(dst.dtype)
    seq = ka_ref.shape[1]
    chunks = [slice(j * LANES, (j + 1) * LANES) for j in range(A_Q_HEADS // 2)]
    low = _low_lanes((seq, LANES))
    for s in range(ka_ref.shape[0]):
        rows = slice(s * seq, (s + 1) * seq)
        ka = [jnp.concatenate(v, axis=0) for v in _kv_head_variants(ka_ref[s])]
        va = [jnp.concatenate(v, axis=0) for v in _kv_head_variants(va_ref[s])]
        work = []
        for j, cols in enumerate(chunks):
            g = (2 * j) // A_GROUP
            s2 = _dot_t(_scaled(qa_ref[rows, cols]), ka[g])
            work.append((oa_ref, cols, s2, va[g], (sink_ref[layer, 2 * j], sink_ref[layer, 2 * j + 1])))
        for cols in chunks:
            kc, vc = kb_ref[s, :, cols], vb_ref[s, :, cols]
            zero = jnp.zeros_like(kc)
            k2 = jnp.concatenate([jnp.where(low, kc, zero), jnp.where(low, zero, kc)], axis=0).astype(BF16)
            v2 = jnp.concatenate([jnp.where(low, vc, zero), jnp.where(low, zero, vc)], axis=0).astype(BF16)
            work.append((ob_ref, cols, _dot_t(_scaled(qb_ref[rows, cols]), k2), v2, None))
        for o_ref, cols, s2, v2, sinks in work:
            o_ref[rows, cols] = _pair_softmax_pv(s2, v2, sinks).astype(o_ref.dtype)


def _pair_softmax_pv(s2, v2, sinks=None):
    tk = s2.shape[1] // 2
    es, extra = [], []
    for p in range(2):
        s = s2[:, p * tk:(p + 1) * tk]
        m = s.max(axis=-1, keepdims=True)
        if sinks is not None:
            m = jnp.maximum(m, sinks[p])
            extra.append(jnp.exp(sinks[p] - m))
        es.append(jnp.exp(s - m).astype(BF16))
    lane_half = lax.broadcasted_iota(jnp.int32, v2.shape, 1) // HEAD_DIM
    row_half = lax.broadcasted_iota(jnp.int32, v2.shape, 0) // tk
    ones2 = jnp.where(lane_half == row_half, 1.0, 0.0).astype(v2.dtype)
    acc = _dot(jnp.concatenate(es, axis=1), jnp.concatenate([v2, ones2], axis=1))
    den = acc[:, LANES:]
    if sinks is not None:
        den = den + jnp.where(_low_lanes(den.shape), extra[0], extra[1])
    return acc[:, :LANES] * (1.0 / den)


def _attn_ctx(qx, caches, sink, l, seq, weights, out_rows):
    m = qx.shape[0]
    per_step = CTX_SEQS_PER_STEP
    rows_step = per_step * seq
    steps = m // rows_step
    out = jax.ShapeDtypeStruct((out_rows, BRANCH_DIM), BF16)
    cache_specs = [pl.BlockSpec((per_step, None, seq, c.shape[-1]), lambda b: (b, l, 0, 0)) for c in caches]
    w_in_specs, w_out_specs, w_shapes = [], [], []
    for w in weights:
        _, rows, cols = w.shape
        assert rows % (steps * BF16_SUBLANES) == 0
        w_in_specs.append(pl.BlockSpec((None, rows // steps, cols), lambda b: (l, b, 0)))
        w_out_specs.append(pl.BlockSpec((rows // steps, cols), lambda b: (b, 0)))
        w_shapes.append(jax.ShapeDtypeStruct((rows, cols), BF16))
    return pl.pallas_call(
        functools.partial(_attn_ctx_kernel, n_weights=len(weights), layer=l),
        grid=(steps,),
        in_specs=[
            pl.BlockSpec(memory_space=pltpu.SMEM),
            pl.BlockSpec((rows_step, BRANCH_DIM), lambda b: (b, QX_QA // BRANCH_DIM)),
            pl.BlockSpec((rows_step, BRANCH_DIM), lambda b: (b, QX_QB // BRANCH_DIM)),
        ] + cache_specs + w_in_specs,
        out_specs=[pl.BlockSpec((rows_step, BRANCH_DIM), lambda b: (b, 0))] * 2 + w_out_specs,
        out_shape=[out, out] + w_shapes,
        compiler_params=_cparams(("parallel",)),
        name="attn_ctx",
    )(sink, qx, qx, *caches, *weights)


def _attn_a_lat_kernel(sink_ref, q_ref, kv_ref, ck_ref, cv_ref, _, *rest, n_blocks, cast_w_in, layer):
    if cast_w_in:
        win_ref, o_ref, wproj_ref, wgates_ref = rest[:4]
        _split_cast_w_in(win_ref, wproj_ref, wgates_ref)
        rest = rest[4:]
    else:
        o_ref, rest = rest[0], rest[1:]
    qh_scr, kp_scr, vp_scr, ckb_scr, cvb_scr, mask_scr = rest
    seq = n_blocks * A_BLOCK
    band = 3 * A_BLOCK
    low_seq = _low_lanes((seq, LANES))
    for h in range(A_Q_HEADS):
        j, p, g = h // 2, h % 2, h // A_GROUP
        x = q_ref[:, j * LANES:(j + 1) * LANES].astype(F32) * SCALE
        if p != g:
            x = pltpu.roll(x, HEAD_DIM, 1)
        qh_scr[h] = jnp.where(low_seq if g == 0 else jnp.logical_not(low_seq), x, 0.0).astype(BF16)
    pad = jnp.zeros((A_BLOCK, LANES), BF16)
    for scr, col in ((kp_scr, 0), (vp_scr, KV_A)):
        scr[0:A_BLOCK, :] = pad
        scr[A_BLOCK:A_BLOCK + seq, :] = kv_ref[:, col:col + KV_A]
        scr[A_BLOCK + seq:2 * A_BLOCK + seq, :] = pad
    ckb_scr[...] = ck_ref[...].astype(BF16)
    cvb_scr[...] = cv_ref[...].astype(BF16)

    rows = A_GROUP * A_BLOCK
    row = lax.broadcasted_iota(jnp.int32, (rows, band), 0)
    qi = row % A_BLOCK
    kj = lax.broadcasted_iota(jnp.int32, (rows, band), 1)
    head_row = lax.broadcasted_iota(jnp.int32, (rows, 1), 0) // A_BLOCK
    low_blk = _low_lanes((A_BLOCK, LANES))
    @pl.when(pl.program_id(0) == 0)
    def _():
        for case, (no_prev, no_next) in enumerate(((1, 0), (0, 0), (0, 1))):
            valid = (kj >= qi + no_prev * (A_BLOCK - qi)) & (kj <= 2 * A_BLOCK + qi - no_next * (qi + 1))
            mask_scr[case] = jnp.where(valid, 0.0, NEG_INF)

    def body(n, carry):
        start = pl.multiple_of(n * A_BLOCK, A_BLOCK)
        case = jnp.where(n > 0, 1, 0) + jnp.where(n < n_blocks - 1, 0, 1)
        scores = []
        for g in range(A_KV_HEADS):
            q = jnp.concatenate(
                [qh_scr[A_GROUP * g + i, pl.ds(start, A_BLOCK), :] for i in range(A_GROUP)], axis=0)
            s_band = _dot_t(q, kp_scr[pl.ds(start, band), :]) + mask_scr[case]
            scores.append((s_band, _dot_t(q, ckb_scr[...])))
        outs = []
        for g, (s_band, s_ctx) in enumerate(scores):
            sink = jnp.full((rows, 1), sink_ref[layer, A_GROUP * g], F32)
            for i in range(1, A_GROUP):
                sink = jnp.where(head_row == i, sink_ref[layer, A_GROUP * g + i], sink)
            outs.append(_softmax_pv([(s_band, vp_scr[pl.ds(start, band), :]), (s_ctx, cvb_scr[...])],
                                    sink=sink))
        for j in range(A_Q_HEADS // 2):
            halves = []
            for p in range(2):
                h = 2 * j + p
                g, i = h // A_GROUP, h % A_GROUP
                t = outs[g][i * A_BLOCK:(i + 1) * A_BLOCK]
                halves.append(t if p == g else pltpu.roll(t, HEAD_DIM, 1))
            o_ref[pl.ds(start, A_BLOCK), j * LANES:(j + 1) * LANES] = (
                jnp.where(low_blk, halves[0], halves[1]).astype(o_ref.dtype))
        return carry

    lax.fori_loop(0, n_blocks, body, 0, unroll=2)


def _attn_a_lat(main, kv, cache_k, cache_v, sink, l, seq, oa_all, w_in_next=None):
    m = main.shape[0]
    nb = seq // A_BLOCK
    assert nb >= 2
    n_seq = m // seq
    past = cache_k.shape[2]
    first = (oa_all.shape[0] - m) // seq
    cache_spec = pl.BlockSpec((None, None, past, KV_A), lambda b: (b, l, 0, 0))
    in_specs = [
        pl.BlockSpec(memory_space=pltpu.SMEM),
        pl.BlockSpec((seq, BRANCH_DIM), lambda b: (b, QX_QA // BRANCH_DIM)),
        pl.BlockSpec((seq, 2 * KV_A), lambda b: (b, QX_KVA // (2 * KV_A))),
        cache_spec, cache_spec,
        pl.BlockSpec(memory_space=pl.ANY),
    ]
    args = [sink, main, kv, cache_k, cache_v, oa_all]
    out_specs = [pl.BlockSpec((seq, BRANCH_DIM), lambda b: (b + first, 0))]
    out_shape = [jax.ShapeDtypeStruct(oa_all.shape, BF16)]
    if w_in_next is not None:
        w_in, layer = w_in_next
        rows = w_in.shape[1] // n_seq
        assert w_in.shape[1] % (n_seq * BF16_SUBLANES) == 0
        in_specs.append(pl.BlockSpec((None, rows, D_IN), lambda b: (layer, b, 0)))
        args.append(w_in)
        out_specs += [pl.BlockSpec((rows, W_GATES), lambda b: (b, 0)),
                      pl.BlockSpec((rows, N_GATES), lambda b: (b, 0))]
        out_shape += [jax.ShapeDtypeStruct((D_MODEL, W_GATES), BF16),
                      jax.ShapeDtypeStruct((D_MODEL, N_GATES), BF16)]
    return pl.pallas_call(
        functools.partial(_attn_a_lat_kernel, n_blocks=nb, cast_w_in=w_in_next is not None, layer=l),
        grid=(n_seq,),
        in_specs=in_specs,
        out_specs=out_specs,
        out_shape=out_shape,
        input_output_aliases={5: 0},
        scratch_shapes=[
            pltpu.VMEM((A_Q_HEADS, seq, LANES), BF16),
            pltpu.VMEM((seq + 2 * A_BLOCK, LANES), BF16),
            pltpu.VMEM((seq + 2 * A_BLOCK, LANES), BF16),
            pltpu.VMEM((past, LANES), BF16),
            pltpu.VMEM((past, LANES), BF16),
            pltpu.VMEM((3, A_GROUP * A_BLOCK, 3 * A_BLOCK), F32),
        ],
        compiler_params=_cparams(("arbitrary",)),
        name="attn_a_lat",
    )(*args)


def _attn_b_lat_kernel(q_ref, k_ref, v_ref, ck_ref, cv_ref, rpb_ref, _, o_ref, ckb_scr, vx_scr, cvx_scr,
                       bias_ref, *, rows, kr):
    @pl.when(pl.program_id(0) == 0)
    def _():
        _fill_neighbourhood_bias(rpb_ref, bias_ref, kr=kr)

    half = kr // 2
    n_loc = kr * GRID_W
    low_q = _low_lanes((GRID_W, LANES))
    chunks = [slice(j * LANES, (j + 1) * LANES) for j in range(B_HEADS // 2)]
    ckb_scr[...] = ck_ref[...].astype(BF16)
    for j, cols in enumerate(chunks):
        vx_scr[j] = _with_ones(v_ref[:, cols])
        cvx_scr[j] = _with_ones(cv_ref[:, cols].astype(BF16))
    rows_per_iter = 4
    assert rows % rows_per_iter == 0

    def body(it, carry):
        work = []
        for u in range(rows_per_iter):
            r = it * rows_per_iter + u
            r0 = jnp.clip(r - half, 0, rows - kr)
            q0 = pl.multiple_of(r * GRID_W, GRID_W)
            start = pl.multiple_of(r0 * GRID_W, GRID_W)
            for j, cols in enumerate(chunks):
                qc = _scaled(q_ref[pl.ds(q0, GRID_W), cols])
                zero = jnp.zeros_like(qc)
                qs = jnp.concatenate([jnp.where(low_q, qc, zero), jnp.where(low_q, zero, qc)], axis=0)
                s_loc = _dot_t(qs, k_ref[pl.ds(start, n_loc), cols]) + bias_ref[r - r0, j]
                work.append((q0, start, j, cols, s_loc, _dot_t(qs, ckb_scr[:, cols])))
        for q0, start, j, cols, s_loc, s_ctx in work:
            o2 = _softmax_pv([(s_loc, vx_scr[j, pl.ds(start, n_loc), :]), (s_ctx, cvx_scr[j])])
            o = jnp.where(low_q, o2[:GRID_W], o2[GRID_W:])
            o_ref[pl.ds(q0, GRID_W), cols] = o.astype(o_ref.dtype)
        return carry

    lax.fori_loop(0, rows // rows_per_iter, body, 0)


def _attn_b_lat(main, kv, cache_k, cache_v, rpb_pad, l, seq, ob_all):
    m = main.shape[0]
    rows = seq // GRID_W
    kr = min(B_WIN_ROWS, rows)
    assert kr % 2 == 0 and 2 * GRID_W == LANES
    n_seq = m // seq
    past = cache_k.shape[2]
    first = (ob_all.shape[0] - m) // seq
    cache_spec = pl.BlockSpec((None, None, past, BRANCH_DIM), lambda b: (b, l, 0, 0))
    return pl.pallas_call(
        functools.partial(_attn_b_lat_kernel, rows=rows, kr=kr),
        grid=(n_seq,),
        in_specs=[
            pl.BlockSpec((seq, BRANCH_DIM), lambda b: (b, QX_QB // BRANCH_DIM)),
            pl.BlockSpec((seq, BRANCH_DIM), lambda b: (b, KV_KB // BRANCH_DIM)),
            pl.BlockSpec((seq, BRANCH_DIM), lambda b: (b, KV_VB // BRANCH_DIM)),
            cache_spec, cache_spec,
            _resident((None,) + rpb_pad.shape[1:], lambda b: (l, 0, 0, 0)),
            pl.BlockSpec(memory_space=pl.ANY),
        ],
        out_specs=pl.BlockSpec((seq, BRANCH_DIM), lambda b: (b + first, 0)),
        out_shape=jax.ShapeDtypeStruct(ob_all.shape, BF16),
        input_output_aliases={6: 0},
        scratch_shapes=[
            pltpu.VMEM((past, BRANCH_DIM), BF16),
            pltpu.VMEM((B_HEADS // 2, seq, 2 * LANES), BF16),
            pltpu.VMEM((B_HEADS // 2, past, 2 * LANES), BF16),
            pltpu.VMEM((kr, B_HEADS // 2, 2 * GRID_W, kr * GRID_W), F32),
        ],
        compiler_params=_cparams(("arbitrary",)),
        name="attn_b_lat",
    )(main, kv, kv, cache_k, cache_v, rpb_pad, ob_all)


def _fill_neighbourhood_bias(rpb_ref, bias_scr, *, kr):
    shape = (GRID_W, LANES)
    c = lax.broadcasted_iota(jnp.int32, shape, 0)
    lane = lax.broadcasted_iota(jnp.int32, shape, 1)
    c2 = lane % GRID_W
    ws = jnp.clip(c - B_WIN_COLS // 2, 0, GRID_W - B_WIN_COLS)
    ok = (c2 >= ws) & (c2 < ws + B_WIN_COLS)
    low = lane < GRID_W

    def toeplitz(h, dr, lane0):
        row = jnp.broadcast_to(rpb_ref[h, dr:dr + 1, :], shape)
        return pltpu.roll(row, (lane0 - (B_WIN_COLS - 1)) % LANES, 1, stride=1, stride_axis=0)

    for h in range(B_HEADS):
        rows = slice(h % 2 * GRID_W, (h % 2 + 1) * GRID_W)
        pairs = {}
        for v in range(kr):
            for m in range(0, kr, 2):
                dr = m - v + B_WIN_ROWS - 1
                if dr not in pairs:
                    pair = jnp.where(low, toeplitz(h, dr, 0), toeplitz(h, dr + 1, GRID_W))
                    pairs[dr] = jnp.where(ok, pair, NEG_INF)
                bias_scr[v, h // 2, rows, m * GRID_W:(m + 2) * GRID_W] = pairs[dr]


def _dft_tables(seq):
    cd = C_GROUP_DIM
    kc = (np.arange(cd)[:, None] * np.arange(cd)[None, :]) % cd
    ang_c = 2.0 * np.pi * kc / cd
    eye2 = np.eye(2)
    bd_cos = np.kron(eye2, np.cos(ang_c))
    bd_sin = np.kron(eye2, np.sin(ang_c))
    kt = (np.arange(seq)[:, None] * np.arange(seq)[None, :]) % seq
    ang_t = 2.0 * np.pi * kt / seq
    norm = 1.0 / np.sqrt(float(seq * cd))
    pos = np.concatenate([np.cos(ang_t), -np.sin(ang_t)], axis=1) * norm
    return (jnp.asarray(bd_cos, F32).astype(BF16), jnp.asarray(bd_sin, F32).astype(BF16),
            jnp.asarray(pos, F32).astype(BF16))


def _fourier_mix(u, bc_ref, bs_ref, pos_ref, o_ref):
    pair = 2 * C_GROUP_DIM
    seq = pos_ref.shape[0]
    uc, us = [], []
    for p in range(BRANCH_DIM // pair):
        up = u[:, p * pair:(p + 1) * pair]
        uc.append(_dot(up, bc_ref[...]))
        us.append(_dot(up, bs_ref[...]))
    zc = jnp.concatenate(uc, axis=1).astype(BF16)
    zs = jnp.concatenate(us, axis=1).astype(BF16)
    for s in range(u.shape[0] // seq):
        rows = slice(s * seq, (s + 1) * seq)
        z = jnp.concatenate([zc[rows], zs[rows]], axis=0)
        o_ref[rows, :] = _dot(pos_ref[...], z).astype(o_ref.dtype)


def _ffn_weight_copies(wfi_hbm, wfo_hbm, wfg_ref, wfu_ref, wfo_ref, sems):
    srcs = (wfi_hbm.at[:, 0:D_FF], wfi_hbm.at[:, D_FF:2 * D_FF], wfo_hbm)
    dsts = (wfg_ref, wfu_ref, wfo_ref)
    return [pltpu.make_async_copy(src, dst, sems.at[k]) for k, (src, dst) in enumerate(zip(srcs, dsts))]


def _tail_kernel(*refs, n_x, n_out, n_ctx_tiles):
    oa_ref, ob_ref, oc_ref = refs[:3]
    x_refs, refs = refs[3:3 + n_x], refs[3 + n_x:]
    mod_ref, gpre_ref, gpost_ref, wg_ref, wb_ref, wo_ref, wfi_hbm, wfo_hbm = refs[:8]
    o_refs, refs = refs[8:8 + n_out], refs[8 + n_out:]
    gpre0, gpre1 = gpre_ref[0:1, :], gpre_ref[1:2, :]
    gpost0, gpost1 = gpost_ref[0:1, :], gpost_ref[1:2, :]
    wfg_ref, wfu_ref, wfo_ref, sems = refs[:4]
    i = pl.program_id(0)
    first = i == 0
    is_ctx = i < n_ctx_tiles
    copies = _ffn_weight_copies(wfi_hbm, wfo_hbm, wfg_ref, wfu_ref, wfo_ref, sems)

    @pl.when(first)
    def _():
        for copy in copies:
            copy.start()

    if n_x == 1:
        x = x_refs[0][...]
    else:
        pick_ctx = jnp.full(x_refs[0].shape, jnp.where(is_ctx, 1, 0), jnp.int32) > 0
        x = jnp.where(pick_ctx, x_refs[0][...], x_refs[1][...])
    h = (_rms(x, gpre0 * (1.0 + mod_ref[1:2, :])) + mod_ref[0:1, :]).astype(BF16)
    gates = _dot(h, wg_ref[...])
    branches = [_dot(b_ref[...], wb_ref[k]) for k, b_ref in enumerate((oa_ref, ob_ref, oc_ref))]
    mix = None
    for k, branch in enumerate(branches):
        term = _sigmoid(gates[:, k * D_MODEL:(k + 1) * D_MODEL]) * branch
        mix = term if mix is None else mix + term
    y = _dot(mix.astype(BF16), wo_ref[...])
    x1 = x + _rms(y, mod_ref[2:3, :] * gpost0)
    h2 = (_rms(x1, gpre1 * (1.0 + mod_ref[4:5, :])) + mod_ref[3:4, :]).astype(BF16)

    @pl.when(first)
    def _():
        for copy in copies:
            copy.wait()

    act = _silu(_dot(h2, wfg_ref[...])) * _dot(h2, wfu_ref[...])
    f = _dot(act.astype(BF16), wfo_ref[...])
    out = x1 + _rms(f, mod_ref[5:6, :] * gpost1)
    if n_out == 1:
        o_refs[0][...] = out
    else:
        @pl.when(is_ctx)
        def _():
            o_refs[0][...] = out

        @pl.when(jnp.logical_not(is_ctx))
        def _():
            o_refs[1][...] = out


def _tail(oa, ob, oc, xs, mods, l, g_pre, g_post, w_gates, w_branch, w_out, w_ffn_in, w_ffn_out,
          *, tm, m_ctx, mod_row, split_out):
    m = oa.shape[0]
    n_ctx = m_ctx // tm
    o_spec = pl.BlockSpec((tm, BRANCH_DIM), lambda i: (i, 0))
    whole = pl.BlockSpec((tm, D_MODEL), lambda i: (i, 0))
    halves = [pl.BlockSpec((tm, D_MODEL), lambda i: (jnp.minimum(i, n_ctx - 1), 0)),
              pl.BlockSpec((tm, D_MODEL), lambda i: (jnp.maximum(i - n_ctx, 0), 0))]
    half_shapes = [jax.ShapeDtypeStruct((m_ctx, D_MODEL), F32),
                   jax.ShapeDtypeStruct((m - m_ctx, D_MODEL), F32)]
    gain_spec = pl.BlockSpec((None, 2, D_MODEL), lambda i: (l, 0, 0))
    scratch = [
        pltpu.VMEM((D_MODEL, D_FF), BF16),
        pltpu.VMEM((D_MODEL, D_FF), BF16),
        pltpu.VMEM((D_FF, D_MODEL), BF16),
        pltpu.SemaphoreType.DMA((3,)),
    ]
    return pl.pallas_call(
        functools.partial(_tail_kernel, n_x=len(xs), n_out=2 if split_out else 1, n_ctx_tiles=n_ctx),
        grid=(m // tm,),
        in_specs=[o_spec, o_spec, o_spec] + (halves if len(xs) == 2 else [whole]) + [
            pl.BlockSpec((None, None, 6, D_MODEL), lambda i: (l, mod_row(i), 0, 0)),
            gain_spec, gain_spec,
            _resident((D_MODEL, N_GATES), lambda i: (0, 0)),
            _resident((N_BRANCH, BRANCH_DIM, D_MODEL), lambda i: (0, 0, 0)),
            _resident((D_MODEL, D_MODEL), lambda i: (0, 0)),
            pl.BlockSpec(memory_space=pl.ANY),
            pl.BlockSpec(memory_space=pl.ANY),
        ],
        out_specs=halves if split_out else whole,
        out_shape=half_shapes if split_out else jax.ShapeDtypeStruct((m, D_MODEL), F32),
        scratch_shapes=scratch,
        compiler_params=_cparams(("arbitrary",)),
        name="tail",
    )(oa, ob, oc, *xs, mods, g_pre, g_post, w_gates, w_branch, w_out, w_ffn_in, w_ffn_out)


def _rope_tables(seq):
    t = np.arange(seq)
    row = (t // GRID_W).astype(np.float32)
    col = (t % GRID_W).astype(np.float32)
    n_pairs_axis = HEAD_DIM // 4
    inv = (np.float32(ROPE_BASE) ** (-np.arange(n_pairs_axis, dtype=np.float32) / n_pairs_axis)
           ).astype(np.float32)
    ang = np.concatenate([row[:, None] * inv, col[:, None] * inv], axis=-1).astype(np.float64)
    cos, sin = np.cos(ang), np.sin(ang)
    cos_l = np.tile(cos, (1, LANES // cos.shape[1]))
    sin_l = np.tile(np.concatenate([-sin, sin], axis=-1), (1, LANES // HEAD_DIM))
    return jnp.asarray(cos_l, F32), jnp.asarray(sin_l, F32)


def kernel(x_prompt, x_sample, cache_a_k, cache_a_v, cache_b_k, cache_b_v, c, c_ctx, w_ada, b_ada,
           norm_pre, norm_post, w_in, a_sink, b_rpb, w_branch, w_out, w_ffn_in, w_ffn_out):
    batch, seq, _ = x_prompt.shape
    dec_batch, dec_seq, _ = x_sample.shape
    past = cache_a_k.shape[2]
    assert dec_batch <= CTX_MOD_ROW and seq % A_BLOCK == 0 and dec_seq % A_BLOCK == 0

    cvec = jnp.concatenate(
        [c, c_ctx[None, :], jnp.zeros((MOD_ROWS - dec_batch - 1, D_MODEL), F32)], axis=0)
    assert w_in.shape[-1] == D_IN
    mods, w_proj_b, w_gates_b = _modulation(cvec, w_ada, b_ada, w_in)
    mods = mods.reshape(DEPTH, MOD_ROWS, 6, D_MODEL)
    w_branch_2d = w_branch.reshape(DEPTH, N_BRANCH * BRANCH_DIM, D_MODEL)
    g_pre, g_post = norm_pre, norm_post
    rope_tabs = _rope_tables(dec_seq)
    cak = cache_a_k.reshape(dec_batch, DEPTH, past, KV_A)
    cav = cache_a_v.reshape(dec_batch, DEPTH, past, KV_A)
    cbk = cache_b_k.reshape(dec_batch, DEPTH, past, BRANCH_DIM)
    cbv = cache_b_v.reshape(dec_batch, DEPTH, past, BRANCH_DIM)

    n_dr, n_dc = b_rpb.shape[2:]
    rpb_pad = jnp.pad(b_rpb.astype(F32), ((0, 0), (0, 0), (0, (-n_dr) % 8), (0, LANES - n_dc)))

    tm_proj, tm_tail = 1024, 512
    m_ctx, m_lat = batch * seq, dec_batch * dec_seq
    m_all = m_ctx + m_lat
    assert m_ctx % tm_proj == 0 and m_lat % tm_proj == 0
    ctx_row = lambda i: CTX_MOD_ROW
    lat_row = lambda i: (i * tm_proj) // dec_seq
    tail_row = lambda i: jnp.where(i < m_ctx // tm_tail, CTX_MOD_ROW, (i * tm_tail - m_ctx) // dec_seq)

    xs = (x_prompt.reshape(m_ctx, D_MODEL), x_sample.reshape(m_lat, D_MODEL))
    caches = None
    for l in range(DEPTH):
        x_ctx, x_lat = (xs[0], xs[1]) if len(xs) == 2 else (xs[0], xs[0])
        lat_tile0 = 0 if len(xs) == 2 else m_ctx // tm_proj
        qx, oc, *rest = _proj(x_ctx, mods, l, g_pre, w_proj_b, m=m_ctx, x_tile0=0, tm=tm_proj,
                              mod_row=ctx_row, seq=seq, oc_rows=m_all, caches=caches,
                              cast_weights=(w_ffn_in,))
        caches, (w_ffn_in_b,) = rest[:4], rest[4:]
        oa, ob, w_branch_b, w_out_b, w_ffn_out_b = _attn_ctx(
            qx, caches, a_sink, l, seq, (w_branch_2d, w_out, w_ffn_out), m_all)
        qx, kv, oc = _proj(x_lat, mods, l, g_pre, w_proj_b, m=m_lat, x_tile0=lat_tile0, tm=tm_proj,
                           mod_row=lat_row, seq=dec_seq, oc_all=oc, rope_tabs=rope_tabs)
        last = l == DEPTH - 1
        w_gates_l = w_gates_b
        if last:
            oa, = _attn_a_lat(qx, qx, cak, cav, a_sink, l, dec_seq, oa)
        else:
            oa, w_proj_b, w_gates_b = _attn_a_lat(qx, qx, cak, cav, a_sink, l, dec_seq, oa,
                                                  w_in_next=(w_in, l + 1))
        ob = _attn_b_lat(qx, kv, cbk, cbv, rpb_pad, l, dec_seq, ob)
        out = _tail(oa, ob, oc, xs, mods, l, g_pre, g_post, w_gates_l,
                    w_branch_b.reshape(N_BRANCH, BRANCH_DIM, D_MODEL), w_out_b, w_ffn_in_b, w_ffn_out_b,
                    tm=tm_tail, m_ctx=m_ctx, mod_row=tail_row, split_out=last)
        xs = tuple(out) if last else (out,)
    y_prompt = xs[0].reshape(batch, seq, D_MODEL)
    y_sample = xs[1].reshape(dec_batch, dec_seq, D_MODEL)
    nak, nav, nbk, nbv = caches
    new_a_k = nak.reshape(batch, DEPTH, seq, A_KV_HEADS, HEAD_DIM)
    new_a_v = nav.reshape(batch, DEPTH, seq, A_KV_HEADS, HEAD_DIM)
    new_b_k = nbk.reshape(batch, DEPTH, seq, B_HEADS, HEAD_DIM)
    new_b_v = nbv.reshape(batch, DEPTH, seq, B_HEADS, HEAD_DIM)
    return (y_prompt, y_sample, new_a_k, new_a_v, new_b_k, new_b_v)
```

```python
import functools

import numpy as np
import jax
import jax.numpy as jnp
from jax import lax
from jax.experimental import pallas as pl
from jax.experimental.pallas import tpu as pltpu

F32 = jnp.float32
BF16 = jnp.bfloat16

D_MODEL = 1024
DEPTH = 2
GRID_W = 64
HEAD_DIM = 64
BRANCH_DIM = D_MODEL // 2
A_Q_HEADS = BRANCH_DIM // HEAD_DIM
A_KV_HEADS = A_Q_HEADS // 4
A_GROUP = A_Q_HEADS // A_KV_HEADS
A_BLOCK = 128
B_HEADS = BRANCH_DIM // HEAD_DIM
B_WIN_ROWS = 8
B_WIN_COLS = 16
C_GROUPS = 4
C_GROUP_DIM = BRANCH_DIM // C_GROUPS
N_BRANCH = 3
D_FF = -(-8 * D_MODEL // (3 * 256)) * 256
ROPE_BASE = 10000.0
NORM_EPS = 1e-6
NEG_INF = -1e30
SCALE = HEAD_DIM ** -0.5

LANES = 128
KV_A = A_KV_HEADS * HEAD_DIM
N_GATES = N_BRANCH * D_MODEL
W_QA = 0
W_KA = W_QA + BRANCH_DIM
W_VA = W_KA + KV_A
W_QB = W_VA + KV_A
W_KB = W_QB + BRANCH_DIM
W_VB = W_KB + BRANCH_DIM
W_UC = W_VB + BRANCH_DIM
W_GATES = W_UC + BRANCH_DIM
D_IN = W_GATES + N_GATES
QX_QA, QX_QB, QX_KVA = 0, BRANCH_DIM, 2 * BRANCH_DIM
KV_KB, KV_VB = 0, BRANCH_DIM
MOD_ROWS = 8
CTX_MOD_ROW = 4
VMEM_LIMIT = 56 * 1024 * 1024
BF16_SUBLANES = 16
CTX_SEQS_PER_STEP = 4


def _cparams(sem):
    return pltpu.CompilerParams(dimension_semantics=sem, vmem_limit_bytes=VMEM_LIMIT)


def _resident(shape, index_map):
    return pl.BlockSpec(shape, index_map, pipeline_mode=pl.Buffered(1))


def _rms(x, w):
    return x * lax.rsqrt(jnp.mean(x * x, axis=-1, keepdims=True) + NORM_EPS) * w


def _sigmoid(x):
    return 0.5 * jnp.tanh(0.5 * x) + 0.5


def _silu(x):
    half = 0.5 * x
    return half + half * jnp.tanh(half)


def _dot_t(a, b):
    return lax.dot_general(a, b, (((1,), (1,)), ((), ())), preferred_element_type=F32)


def _dot(a, b):
    return jnp.dot(a, b, preferred_element_type=F32)


def _scaled(q):
    assert np.log2(SCALE) == round(np.log2(SCALE))
    return q * jnp.asarray(SCALE, q.dtype)


def _low_lanes(shape):
    return lax.broadcasted_iota(jnp.int32, shape, len(shape) - 1) < HEAD_DIM


def _mod_kernel(cv_ref, w_ref, b_ref, win_ref, o_ref, wproj_ref, wgates_ref):
    a, b = _silu(cv_ref[...]), w_ref[...]
    a_hi, b_hi = a.astype(BF16), b.astype(BF16)
    a_lo = (a - a_hi.astype(F32)).astype(BF16)
    b_lo = (b - b_hi.astype(F32)).astype(BF16)
    bias = b_ref[pl.ds(pl.program_id(0), 1), :]
    o_ref[...] = _dot(a_hi, b_hi) + (_dot(a_lo, b_hi) + _dot(a_hi, b_lo)) + bias
    @pl.when(pl.program_id(0) == 0)
    def _():
        _split_cast_w_in(win_ref, wproj_ref, wgates_ref)


def _split_cast_w_in(win_ref, wproj_ref, wgates_ref):
    wproj_ref[...] = win_ref[:, :W_GATES].astype(wproj_ref.dtype)
    wgates_ref[...] = win_ref[:, W_GATES:].astype(wgates_ref.dtype)


def _modulation(cvec, w_ada, b_ada, w_in):
    tn = 1536
    n = 6 * D_MODEL
    steps = n // tn
    rows = w_in.shape[1] // steps
    slab = lambda l, j: jnp.where(l == 0, j, steps - 1)
    return pl.pallas_call(
        _mod_kernel,
        grid=(DEPTH, steps),
        in_specs=[
            pl.BlockSpec((MOD_ROWS, D_MODEL), lambda l, j: (0, 0)),
            pl.BlockSpec((None, D_MODEL, tn), lambda l, j: (l, 0, j)),
            pl.BlockSpec((DEPTH, tn), lambda l, j: (0, j)),
            pl.BlockSpec((None, rows, D_IN), lambda l, j: (0, slab(l, j), 0)),
        ],
        out_specs=[pl.BlockSpec((None, MOD_ROWS, tn), lambda l, j: (l, 0, j)),
                   pl.BlockSpec((rows, W_GATES), lambda l, j: (slab(l, j), 0)),
                   pl.BlockSpec((rows, N_GATES), lambda l, j: (slab(l, j), 0))],
        out_shape=[jax.ShapeDtypeStruct((DEPTH, MOD_ROWS, n), F32),
                   jax.ShapeDtypeStruct((D_MODEL, W_GATES), BF16),
                   jax.ShapeDtypeStruct((D_MODEL, N_GATES), BF16)],
        compiler_params=_cparams(("arbitrary", "arbitrary")),
        name="modulation",
    )(cvec, w_ada, b_ada, w_in)


def _rope_cols(v, cos, sin):
    first = (lax.broadcasted_iota(jnp.int32, (v.shape[0], LANES), 1) % HEAD_DIM) < HEAD_DIM // 2
    outs = []
    for c in range(v.shape[1] // LANES):
        vc = v[:, c * LANES:(c + 1) * LANES]
        partner = jnp.where(first, pltpu.roll(vc, LANES - HEAD_DIM // 2, 1),
                            pltpu.roll(vc, HEAD_DIM // 2, 1))
        outs.append(vc * cos + partner * sin)
    return outs[0] if len(outs) == 1 else jnp.concatenate(outs, axis=1)


def _with_rope(acc, lo, hi, cos_ref, sin_ref):
    parts = []
    if lo > 0:
        parts.append(acc[:, :lo])
    parts.append(_rope_cols(acc[:, lo:hi], cos_ref[...], sin_ref[...]))
    if hi < acc.shape[1]:
        parts.append(acc[:, hi:])
    return parts[0] if len(parts) == 1 else jnp.concatenate(parts, axis=1)


def _proj_kernel(*refs, latent, n_alias, n_cast):
    x_ref, mod_ref, g_ref, w_ref, bc_ref, bs_ref, pos_ref = refs[:7]
    if latent:
        cos_ref, sin_ref, _, qx_ref, kv_ref, oc_ref = refs[7:]
    else:
        rest = refs[7 + n_alias:]
        cast_src, rest = rest[:n_cast], rest[n_cast:]
        qx_ref, oc_ref, ka_ref, va_ref, kb_ref, vb_ref = rest[:6]
        for src, dst in zip(cast_src, rest[6:6 + n_cast]):
            dst[...] = src[...].astype(dst.dtype)

    h = (_rms(x_ref[...], g_ref[0:1, :] * (1.0 + mod_ref[1:2, :])) + mod_ref[0:1, :]).astype(BF16)
    acc = _dot(h, w_ref[:, W_QA:W_KB])
    qa = acc[:, W_QA:W_QA + BRANCH_DIM]
    qb = acc[:, W_QB:W_QB + BRANCH_DIM]
    kva = acc[:, W_KA:W_KA + 2 * KV_A]
    if latent:
        qa = _rope_cols(qa, cos_ref[...], sin_ref[...])
        kva = _with_rope(kva, 0, KV_A, cos_ref, sin_ref)
        qx_ref[:, QX_KVA:QX_KVA + 2 * KV_A] = kva.astype(qx_ref.dtype)
    else:
        ka_ref[...] = kva[:, :KV_A].reshape(ka_ref.shape)
        va_ref[...] = kva[:, KV_A:].reshape(va_ref.shape)
    qx_ref[:, QX_QA:QX_QA + BRANCH_DIM] = qa.astype(qx_ref.dtype)
    qx_ref[:, QX_QB:QX_QB + BRANCH_DIM] = qb.astype(qx_ref.dtype)

    acc = _dot(h, w_ref[:, W_KB:W_GATES])
    kb = acc[:, 0:BRANCH_DIM]
    vb = acc[:, W_VB - W_KB:W_VB - W_KB + BRANCH_DIM]
    uc = acc[:, W_UC - W_KB:W_UC - W_KB + BRANCH_DIM]
    if latent:
        kv_ref[:, KV_KB:KV_KB + BRANCH_DIM] = kb.astype(kv_ref.dtype)
        kv_ref[:, KV_VB:KV_VB + BRANCH_DIM] = vb.astype(kv_ref.dtype)
    else:
        kb_ref[...] = kb.reshape(kb_ref.shape)
        vb_ref[...] = vb.reshape(vb_ref.shape)
    _fourier_mix(uc.astype(BF16), bc_ref, bs_ref, pos_ref, oc_ref)


def _proj(x, mods, l, g_pre, w_in, *, m, x_tile0, tm, mod_row, seq, oc_rows=None, oc_all=None,
          rope_tabs=None, caches=None, cast_weights=()):
    assert tm % seq == 0
    latent = rope_tabs is not None
    bc, bs, pos = _dft_tables(seq)
    pair = 2 * C_GROUP_DIM
    in_specs = [
        pl.BlockSpec((tm, D_MODEL), lambda i: (i + x_tile0, 0)),
        pl.BlockSpec((None, None, 6, D_MODEL), lambda i: (l, mod_row(i), 0, 0)),
        pl.BlockSpec((None, 2, D_MODEL), lambda i: (l, 0, 0)),
        _resident((D_MODEL, W_GATES), lambda i: (0, 0)),
        _resident((pair, pair), lambda i: (0, 0)),
        _resident((pair, pair), lambda i: (0, 0)),
        _resident((seq, 2 * seq), lambda i: (0, 0)),
    ]
    args = [x, mods, g_pre, w_in, bc, bs, pos]
    aliases = {}
    if latent:
        seq_tiles = seq // tm
        tab_spec = pl.BlockSpec((tm, LANES), lambda i: (i % seq_tiles, 0))
        in_specs += [tab_spec, tab_spec, pl.BlockSpec(memory_space=pl.ANY)]
        args += list(rope_tabs)
        widths = (2 * BRANCH_DIM + 2 * KV_A, 2 * BRANCH_DIM)
        aliases = {len(args): len(widths)}
        args.append(oc_all)
        oc_rows, oc_tile0 = oc_all.shape[0], (oc_all.shape[0] - m) // tm
        cache_specs, cache_shapes = [], []
    else:
        widths = (2 * BRANCH_DIM,)
        oc_tile0 = 0
        cache_widths = (KV_A, KV_A, BRANCH_DIM, BRANCH_DIM)
        if caches is not None:
            in_specs += [pl.BlockSpec(memory_space=pl.ANY)] * len(caches)
            aliases = {len(args) + k: len(widths) + 1 + k for k in range(len(caches))}
            args += list(caches)
        cache_specs = [pl.BlockSpec((tm // seq, None, seq, w), lambda i: (i, l, 0, 0))
                       for w in cache_widths]
        cache_shapes = [jax.ShapeDtypeStruct((m // seq, DEPTH, seq, w), F32) for w in cache_widths]
        steps = m // tm
        for w in cast_weights:
            _, rows, cols = w.shape
            assert rows % (steps * BF16_SUBLANES) == 0
            in_specs.append(pl.BlockSpec((None, rows // steps, cols), lambda i: (l, i, 0)))
            cache_specs.append(pl.BlockSpec((rows // steps, cols), lambda i: (i, 0)))
            cache_shapes.append(jax.ShapeDtypeStruct((rows, cols), BF16))
            args.append(w)
    return pl.pallas_call(
        functools.partial(_proj_kernel, latent=latent, n_alias=len(aliases), n_cast=len(cast_weights)),
        grid=(m // tm,),
        in_specs=in_specs,
        out_specs=([pl.BlockSpec((tm, w), lambda i: (i, 0)) for w in widths]
                   + [pl.BlockSpec((tm, BRANCH_DIM), lambda i: (i + oc_tile0, 0))] + cache_specs),
        out_shape=([jax.ShapeDtypeStruct((m, w), BF16) for w in widths]
                   + [jax.ShapeDtypeStruct((oc_rows, BRANCH_DIM), BF16)] + cache_shapes),
        input_output_aliases=aliases,
        compiler_params=_cparams(("parallel",)),
        name="proj_lat" if latent else "proj_ctx",
    )(*args)


def _with_ones(v):
    return jnp.concatenate([v, jnp.ones_like(v)], axis=1)


def _softmax_pv(parts, sink=None):
    m = parts[0][0].max(axis=-1, keepdims=True)
    for s, _ in parts[1:]:
        m = jnp.maximum(m, s.max(axis=-1, keepdims=True))
    if sink is not None:
        m = jnp.maximum(m, sink)
    ones_half = parts[0][1].shape[1] == 2 * LANES
    acc = None
    den = None
    for s, v in parts:
        e = jnp.exp(s - m)
        if not ones_half:
            d = e.sum(axis=-1, keepdims=True)
            den = d if den is None else den + d
        o = _dot(e.astype(BF16), v)
        acc = o if acc is None else acc + o
    if ones_half:
        acc, den = acc[:, :LANES], acc[:, LANES:]
    if sink is not None:
        den = den + jnp.exp(sink - m)
    return acc * (1.0 / den)


def _kv_head_variants(x2):
    low = _low_lanes(x2.shape)
    xr = pltpu.roll(x2, HEAD_DIM, 1)
    zero = jnp.zeros_like(x2)
    return [
        [jnp.where(low, x2, zero).astype(BF16), jnp.where(low, zero, xr).astype(BF16)],
        [jnp.where(low, xr, zero).astype(BF16), jnp.where(low, zero, x2).astype(BF16)],
    ]


def _attn_ctx_kernel(sink_ref, qa_ref, qb_ref, ka_ref, va_ref, kb_ref, vb_ref, *rest, n_weights, layer):
    w_src, (oa_ref, ob_ref), w_dst = rest[:n_weights], rest[n_weights:n_weights + 2], rest[n_weights + 2:]
    for src, dst in zip(w_src, w_dst):
        dst[...] = src[...].astype(dst.dtype)
    seq = ka_ref.shape[1]
    chunks = [slice(j * LANES, (j + 1) * LANES) for j in range(A_Q_HEADS // 2)]
    low = _low_lanes((seq, LANES))
    for s in range(ka_ref.shape[0]):
        rows = slice(s * seq, (s + 1) * seq)
        ka = [jnp.concatenate(v, axis=0) for v in _kv_head_variants(ka_ref[s])]
        va = [jnp.concatenate(v, axis=0) for v in _kv_head_variants(va_ref[s])]
        work = []
        for j, cols in enumerate(chunks):
            g = (2 * j) // A_GROUP
            s2 = _dot_t(_scaled(qa_ref[rows, cols]), ka[g])
            work.append((oa_ref, cols, s2, va[g], (sink_ref[layer, 2 * j], sink_ref[layer, 2 * j + 1])))
        for cols in chunks:
            kc, vc = kb_ref[s, :, cols], vb_ref[s, :, cols]
            zero = jnp.zeros_like(kc)
            k2 = jnp.concatenate([jnp.where(low, kc, zero), jnp.where(low, zero, kc)], axis=0).astype(BF16)
            v2 = jnp.concatenate([jnp.where(low, vc, zero), jnp.where(low, zero, vc)], axis=0).astype(BF16)
            work.append((ob_ref, cols, _dot_t(_scaled(qb_ref[rows, cols]), k2), v2, None))
        for o_ref, cols, s2, v2, sinks in work:
            o_ref[rows, cols] = _pair_softmax_pv(s2, v2, sinks).astype(o_ref.dtype)


def _pair_softmax_pv(s2, v2, sinks=None):
    tk = s2.shape[1] // 2
    es, extra = [], []
    for p in range(2):
        s = s2[:, p * tk:(p + 1) * tk]
        m = s.max(axis=-1, keepdims=True)
        if sinks is not None:
            m = jnp.maximum(m, sinks[p])
            extra.append(jnp.exp(sinks[p] - m))
        es.append(jnp.exp(s - m).astype(BF16))
    lane_half = lax.broadcasted_iota(jnp.int32, v2.shape, 1) // HEAD_DIM
    row_half = lax.broadcasted_iota(jnp.int32, v2.shape, 0) // tk
    ones2 = jnp.where(lane_half == row_half, 1.0, 0.0).astype(v2.dtype)
    acc = _dot(jnp.concatenate(es, axis=1), jnp.concatenate([v2, ones2], axis=1))
    den = acc[:, LANES:]
    if sinks is not None:
        den = den + jnp.where(_low_lanes(den.shape), extra[0], extra[1])
    return acc[:, :LANES] * (1.0 / den)


def _attn_ctx(qx, caches, sink, l, seq, weights, out_rows):
    m = qx.shape[0]
    per_step = CTX_SEQS_PER_STEP
    rows_step = per_step * seq
    steps = m // rows_step
    out = jax.ShapeDtypeStruct((out_rows, BRANCH_DIM), BF16)
    cache_specs = [pl.BlockSpec((per_step, None, seq, c.shape[-1]), lambda b: (b, l, 0, 0)) for c in caches]
    w_in_specs, w_out_specs, w_shapes = [], [], []
    for w in weights:
        _, rows, cols = w.shape
        assert rows % (steps * BF16_SUBLANES) == 0
        w_in_specs.append(pl.BlockSpec((None, rows // steps, cols), lambda b: (l, b, 0)))
        w_out_specs.append(pl.BlockSpec((rows // steps, cols), lambda b: (b, 0)))
        w_shapes.append(jax.ShapeDtypeStruct((rows, cols), BF16))
    return pl.pallas_call(
        functools.partial(_attn_ctx_kernel, n_weights=len(weights), layer=l),
        grid=(steps,),
        in_specs=[
            pl.BlockSpec(memory_space=pltpu.SMEM),
            pl.BlockSpec((rows_step, BRANCH_DIM), lambda b: (b, QX_QA // BRANCH_DIM)),
            pl.BlockSpec((rows_step, BRANCH_DIM), lambda b: (b, QX_QB // BRANCH_DIM)),
        ] + cache_specs + w_in_specs,
        out_specs=[pl.BlockSpec((rows_step, BRANCH_DIM), lambda b: (b, 0))] * 2 + w_out_specs,
        out_shape=[out, out] + w_shapes,
        compiler_params=_cparams(("parallel",)),
        name="attn_ctx",
    )(sink, qx, qx, *caches, *weights)


def _attn_a_lat_kernel(sink_ref, q_ref, kv_ref, ck_ref, cv_ref, _, *rest, n_blocks, cast_w_in, layer):
    if cast_w_in:
        win_ref, o_ref, wproj_ref, wgates_ref = rest[:4]
        _split_cast_w_in(win_ref, wproj_ref, wgates_ref)
        rest = rest[4:]
    else:
        o_ref, rest = rest[0], rest[1:]
    qh_scr, kp_scr, vp_scr, ckb_scr, cvb_scr, mask_scr = rest
    seq = n_blocks * A_BLOCK
    band = 3 * A_BLOCK
    low_seq = _low_lanes((seq, LANES))
    for h in range(A_Q_HEADS):
        j, p, g = h // 2, h % 2, h // A_GROUP
        x = q_ref[:, j * LANES:(j + 1) * LANES].astype(F32) * SCALE
        if p != g:
            x = pltpu.roll(x, HEAD_DIM, 1)
        qh_scr[h] = jnp.where(low_seq if g == 0 else jnp.logical_not(low_seq), x, 0.0).astype(BF16)
    pad = jnp.zeros((A_BLOCK, LANES), BF16)
    for scr, col in ((kp_scr, 0), (vp_scr, KV_A)):
        scr[0:A_BLOCK, :] = pad
        scr[A_BLOCK:A_BLOCK + seq, :] = kv_ref[:, col:col + KV_A]
        scr[A_BLOCK + seq:2 * A_BLOCK + seq, :] = pad
    ckb_scr[...] = ck_ref[...].astype(BF16)
    cvb_scr[...] = cv_ref[...].astype(BF16)

    rows = A_GROUP * A_BLOCK
    row = lax.broadcasted_iota(jnp.int32, (rows, band), 0)
    qi = row % A_BLOCK
    kj = lax.broadcasted_iota(jnp.int32, (rows, band), 1)
    head_row = lax.broadcasted_iota(jnp.int32, (rows, 1), 0) // A_BLOCK
    low_blk = _low_lanes((A_BLOCK, LANES))
    @pl.when(pl.program_id(0) == 0)
    def _():
        for case, (no_prev, no_next) in enumerate(((1, 0), (0, 0), (0, 1))):
            valid = (kj >= qi + no_prev * (A_BLOCK - qi)) & (kj <= 2 * A_BLOCK + qi - no_next * (qi + 1))
            mask_scr[case] = jnp.where(valid, 0.0, NEG_INF)

    def body(n, carry):
        start = pl.multiple_of(n * A_BLOCK, A_BLOCK)
        case = jnp.where(n > 0, 1, 0) + jnp.where(n < n_blocks - 1, 0, 1)
        scores = []
        for g in range(A_KV_HEADS):
            q = jnp.concatenate(
                [qh_scr[A_GROUP * g + i, pl.ds(start, A_BLOCK), :] for i in range(A_GROUP)], axis=0)
            s_band = _dot_t(q, kp_scr[pl.ds(start, band), :]) + mask_scr[case]
            scores.append((s_band, _dot_t(q, ckb_scr[...])))
        outs = []
        for g, (s_band, s_ctx) in enumerate(scores):
            sink = jnp.full((rows, 1), sink_ref[layer, A_GROUP * g], F32)
            for i in range(1, A_GROUP):
                sink = jnp.where(head_row == i, sink_ref[layer, A_GROUP * g + i], sink)
            outs.append(_softmax_pv([(s_band, vp_scr[pl.ds(start, band), :]), (s_ctx, cvb_scr[...])],
                                    sink=sink))
        for j in range(A_Q_HEADS // 2):
            halves = []
            for p in range(2):
                h = 2 * j + p
                g, i = h // A_GROUP, h % A_GROUP
                t = outs[g][i * A_BLOCK:(i + 1) * A_BLOCK]
                halves.append(t if p == g else pltpu.roll(t, HEAD_DIM, 1))
            o_ref[pl.ds(start, A_BLOCK), j * LANES:(j + 1) * LANES] = (
                jnp.where(low_blk, halves[0], halves[1]).astype(o_ref.dtype))
        return carry

    lax.fori_loop(0, n_blocks, body, 0, unroll=2)


def _attn_a_lat(main, kv, cache_k, cache_v, sink, l, seq, oa_all, w_in_next=None):
    m = main.shape[0]
    nb = seq // A_BLOCK
    assert nb >= 2
    n_seq = m // seq
    past = cache_k.shape[2]
    first = (oa_all.shape[0] - m) // seq
    cache_spec = pl.BlockSpec((None, None, past, KV_A), lambda b: (b, l, 0, 0))
    in_specs = [
        pl.BlockSpec(memory_space=pltpu.SMEM),
        pl.BlockSpec((seq, BRANCH_DIM), lambda b: (b, QX_QA // BRANCH_DIM)),
        pl.BlockSpec((seq, 2 * KV_A), lambda b: (b, QX_KVA // (2 * KV_A))),
        cache_spec, cache_spec,
        pl.BlockSpec(memory_space=pl.ANY),
    ]
    args = [sink, main, kv, cache_k, cache_v, oa_all]
    out_specs = [pl.BlockSpec((seq, BRANCH_DIM), lambda b: (b + first, 0))]
    out_shape = [jax.ShapeDtypeStruct(oa_all.shape, BF16)]
    if w_in_next is not None:
        w_in, layer = w_in_next
        rows = w_in.shape[1] // n_seq
        assert w_in.shape[1] % (n_seq * BF16_SUBLANES) == 0
        in_specs.append(pl.BlockSpec((None, rows, D_IN), lambda b: (layer, b, 0)))
        args.append(w_in)
        out_specs += [pl.BlockSpec((rows, W_GATES), lambda b: (b, 0)),
                      pl.BlockSpec((rows, N_GATES), lambda b: (b, 0))]
        out_shape += [jax.ShapeDtypeStruct((D_MODEL, W_GATES), BF16),
                      jax.ShapeDtypeStruct((D_MODEL, N_GATES), BF16)]
    return pl.pallas_call(
        functools.partial(_attn_a_lat_kernel, n_blocks=nb, cast_w_in=w_in_next is not None, layer=l),
        grid=(n_seq,),
        in_specs=in_specs,
        out_specs=out_specs,
        out_shape=out_shape,
        input_output_aliases={5: 0},
        scratch_shapes=[
            pltpu.VMEM((A_Q_HEADS, seq, LANES), BF16),
            pltpu.VMEM((seq + 2 * A_BLOCK, LANES), BF16),
            pltpu.VMEM((seq + 2 * A_BLOCK, LANES), BF16),
            pltpu.VMEM((past, LANES), BF16),
            pltpu.VMEM((past, LANES), BF16),
            pltpu.VMEM((3, A_GROUP * A_BLOCK, 3 * A_BLOCK), F32),
        ],
        compiler_params=_cparams(("arbitrary",)),
        name="attn_a_lat",
    )(*args)


def _attn_b_lat_kernel(q_ref, k_ref, v_ref, ck_ref, cv_ref, rpb_ref, _, o_ref, ckb_scr, vx_scr, cvx_scr,
                       bias_ref, *, rows, kr):
    @pl.when(pl.program_id(0) == 0)
    def _():
        _fill_neighbourhood_bias(rpb_ref, bias_ref, kr=kr)

    half = kr // 2
    n_loc = kr * GRID_W
    low_q = _low_lanes((GRID_W, LANES))
    chunks = [slice(j * LANES, (j + 1) * LANES) for j in range(B_HEADS // 2)]
    ckb_scr[...] = ck_ref[...].astype(BF16)
    for j, cols in enumerate(chunks):
        vx_scr[j] = _with_ones(v_ref[:, cols])
        cvx_scr[j] = _with_ones(cv_ref[:, cols].astype(BF16))
    rows_per_iter = 4
    assert rows % rows_per_iter == 0

    def body(it, carry):
        work = []
        for u in range(rows_per_iter):
            r = it * rows_per_iter + u
            r0 = jnp.clip(r - half, 0, rows - kr)
            q0 = pl.multiple_of(r * GRID_W, GRID_W)
            start = pl.multiple_of(r0 * GRID_W, GRID_W)
            for j, cols in enumerate(chunks):
                qc = _scaled(q_ref[pl.ds(q0, GRID_W), cols])
                zero = jnp.zeros_like(qc)
                qs = jnp.concatenate([jnp.where(low_q, qc, zero), jnp.where(low_q, zero, qc)], axis=0)
                s_loc = _dot_t(qs, k_ref[pl.ds(start, n_loc), cols]) + bias_ref[r - r0, j]
                work.append((q0, start, j, cols, s_loc, _dot_t(qs, ckb_scr[:, cols])))
        for q0, start, j, cols, s_loc, s_ctx in work:
            o2 = _softmax_pv([(s_loc, vx_scr[j, pl.ds(start, n_loc), :]), (s_ctx, cvx_scr[j])])
            o = jnp.where(low_q, o2[:GRID_W], o2[GRID_W:])
            o_ref[pl.ds(q0, GRID_W), cols] = o.astype(o_ref.dtype)
        return carry

    lax.fori_loop(0, rows // rows_per_iter, body, 0)


def _attn_b_lat(main, kv, cache_k, cache_v, rpb_pad, l, seq, ob_all):
    m = main.shape[0]
    rows = seq // GRID_W
    kr = min(B_WIN_ROWS, rows)
    assert kr % 2 == 0 and 2 * GRID_W == LANES
    n_seq = m // seq
    past = cache_k.shape[2]
    first = (ob_all.shape[0] - m) // seq
    cache_spec = pl.BlockSpec((None, None, past, BRANCH_DIM), lambda b: (b, l, 0, 0))
    return pl.pallas_call(
        functools.partial(_attn_b_lat_kernel, rows=rows, kr=kr),
        grid=(n_seq,),
        in_specs=[
            pl.BlockSpec((seq, BRANCH_DIM), lambda b: (b, QX_QB // BRANCH_DIM)),
            pl.BlockSpec((seq, BRANCH_DIM), lambda b: (b, KV_KB // BRANCH_DIM)),
            pl.BlockSpec((seq, BRANCH_DIM), lambda b: (b, KV_VB // BRANCH_DIM)),
            cache_spec, cache_spec,
            _resident((None,) + rpb_pad.shape[1:], lambda b: (l, 0, 0, 0)),
            pl.BlockSpec(memory_space=pl.ANY),
        ],
        out_specs=pl.BlockSpec((seq, BRANCH_DIM), lambda b: (b + first, 0)),
        out_shape=jax.ShapeDtypeStruct(ob_all.shape, BF16),
        input_output_aliases={6: 0},
        scratch_shapes=[
            pltpu.VMEM((past, BRANCH_DIM), BF16),
            pltpu.VMEM((B_HEADS // 2, seq, 2 * LANES), BF16),
            pltpu.VMEM((B_HEADS // 2, past, 2 * LANES), BF16),
            pltpu.VMEM((kr, B_HEADS // 2, 2 * GRID_W, kr * GRID_W), F32),
        ],
        compiler_params=_cparams(("arbitrary",)),
        name="attn_b_lat",
    )(main, kv, kv, cache_k, cache_v, rpb_pad, ob_all)


def _fill_neighbourhood_bias(rpb_ref, bias_scr, *, kr):
    shape = (GRID_W, LANES)
    c = lax.broadcasted_iota(jnp.int32, shape, 0)
    lane = lax.broadcasted_iota(jnp.int32, shape, 1)
    c2 = lane % GRID_W
    ws = jnp.clip(c - B_WIN_COLS // 2, 0, GRID_W - B_WIN_COLS)
    ok = (c2 >= ws) & (c2 < ws + B_WIN_COLS)
    low = lane < GRID_W

    def toeplitz(h, dr, lane0):
        row = jnp.broadcast_to(rpb_ref[h, dr:dr + 1, :], shape)
        return pltpu.roll(row, (lane0 - (B_WIN_COLS - 1)) % LANES, 1, stride=1, stride_axis=0)

    for h in range(B_HEADS):
        rows = slice(h % 2 * GRID_W, (h % 2 + 1) * GRID_W)
        pairs = {}
        for v in range(kr):
            for m in range(0, kr, 2):
                dr = m - v + B_WIN_ROWS - 1
                if dr not in pairs:
                    pair = jnp.where(low, toeplitz(h, dr, 0), toeplitz(h, dr + 1, GRID_W))
                    pairs[dr] = jnp.where(ok, pair, NEG_INF)
                bias_scr[v, h // 2, rows, m * GRID_W:(m + 2) * GRID_W] = pairs[dr]


def _dft_tables(seq):
    cd = C_GROUP_DIM
    kc = (np.arange(cd)[:, None] * np.arange(cd)[None, :]) % cd
    ang_c = 2.0 * np.pi * kc / cd
    eye2 = np.eye(2)
    bd_cos = np.kron(eye2, np.cos(ang_c))
    bd_sin = np.kron(eye2, np.sin(ang_c))
    kt = (np.arange(seq)[:, None] * np.arange(seq)[None, :]) % seq
    ang_t = 2.0 * np.pi * kt / seq
    norm = 1.0 / np.sqrt(float(seq * cd))
    pos = np.concatenate([np.cos(ang_t), -np.sin(ang_t)], axis=1) * norm
    return (jnp.asarray(bd_cos, F32).astype(BF16), jnp.asarray(bd_sin, F32).astype(BF16),
            jnp.asarray(pos, F32).astype(BF16))


def _fourier_mix(u, bc_ref, bs_ref, pos_ref, o_ref):
    pair = 2 * C_GROUP_DIM
    seq = pos_ref.shape[0]
    uc, us = [], []
    for p in range(BRANCH_DIM // pair):
        up = u[:, p * pair:(p + 1) * pair]
        uc.append(_dot(up, bc_ref[...]))
        us.append(_dot(up, bs_ref[...]))
    zc = jnp.concatenate(uc, axis=1).astype(BF16)
    zs = jnp.concatenate(us, axis=1).astype(BF16)
    for s in range(u.shape[0] // seq):
        rows = slice(s * seq, (s + 1) * seq)
        z = jnp.concatenate([zc[rows], zs[rows]], axis=0)
        o_ref[rows, :] = _dot(pos_ref[...], z).astype(o_ref.dtype)


def _ffn_weight_copies(wfi_hbm, wfo_hbm, wfg_ref, wfu_ref, wfo_ref, sems):
    srcs = (wfi_hbm.at[:, 0:D_FF], wfi_hbm.at[:, D_FF:2 * D_FF], wfo_hbm)
    dsts = (wfg_ref, wfu_ref, wfo_ref)
    return [pltpu.make_async_copy(src, dst, sems.at[k]) for k, (src, dst) in enumerate(zip(srcs, dsts))]


def _tail_kernel(*refs, n_x, n_out, n_ctx_tiles):
    oa_ref, ob_ref, oc_ref = refs[:3]
    x_refs, refs = refs[3:3 + n_x], refs[3 + n_x:]
    mod_ref, gpre_ref, gpost_ref, wg_ref, wb_ref, wo_ref, wfi_hbm, wfo_hbm = refs[:8]
    o_refs, refs = refs[8:8 + n_out], refs[8 + n_out:]
    gpre0, gpre1 = gpre_ref[0:1, :], gpre_ref[1:2, :]
    gpost0, gpost1 = gpost_ref[0:1, :], gpost_ref[1:2, :]
    wfg_ref, wfu_ref, wfo_ref, sems = refs[:4]
    i = pl.program_id(0)
    first = i == 0
    is_ctx = i < n_ctx_tiles
    copies = _ffn_weight_copies(wfi_hbm, wfo_hbm, wfg_ref, wfu_ref, wfo_ref, sems)

    @pl.when(first)
    def _():
        for copy in copies:
            copy.start()

    if n_x == 1:
        x = x_refs[0][...]
    else:
        pick_ctx = jnp.full(x_refs[0].shape, jnp.where(is_ctx, 1, 0), jnp.int32) > 0
        x = jnp.where(pick_ctx, x_refs[0][...], x_refs[1][...])
    h = (_rms(x, gpre0 * (1.0 + mod_ref[1:2, :])) + mod_ref[0:1, :]).astype(BF16)
    gates = _dot(h, wg_ref[...])
    branches = [_dot(b_ref[...], wb_ref[k]) for k, b_ref in enumerate((oa_ref, ob_ref, oc_ref))]
    mix = None
    for k, branch in enumerate(branches):
        term = _sigmoid(gates[:, k * D_MODEL:(k + 1) * D_MODEL]) * branch
        mix = term if mix is None else mix + term
    y = _dot(mix.astype(BF16), wo_ref[...])
    x1 = x + _rms(y, mod_ref[2:3, :] * gpost0)
    h2 = (_rms(x1, gpre1 * (1.0 + mod_ref[4:5, :])) + mod_ref[3:4, :]).astype(BF16)

    @pl.when(first)
    def _():
        for copy in copies:
            copy.wait()

    act = _silu(_dot(h2, wfg_ref[...])) * _dot(h2, wfu_ref[...])
    f = _dot(act.astype(BF16), wfo_ref[...])
    out = x1 + _rms(f, mod_ref[5:6, :] * gpost1)
    if n_out == 1:
        o_refs[0][...] = out
    else:
        @pl.when(is_ctx)
        def _():
            o_refs[0][...] = out

        @pl.when(jnp.logical_not(is_ctx))
        def _():
            o_refs[1][...] = out


def _tail(oa, ob, oc, xs, mods, l, g_pre, g_post, w_gates, w_branch, w_out, w_ffn_in, w_ffn_out,
          *, tm, m_ctx, mod_row, split_out):
    m = oa.shape[0]
    n_ctx = m_ctx // tm
    o_spec = pl.BlockSpec((tm, BRANCH_DIM), lambda i: (i, 0))
    whole = pl.BlockSpec((tm, D_MODEL), lambda i: (i, 0))
    halves = [pl.BlockSpec((tm, D_MODEL), lambda i: (jnp.minimum(i, n_ctx - 1), 0)),
              pl.BlockSpec((tm, D_MODEL), lambda i: (jnp.maximum(i - n_ctx, 0), 0))]
    half_shapes = [jax.ShapeDtypeStruct((m_ctx, D_MODEL), F32),
                   jax.ShapeDtypeStruct((m - m_ctx, D_MODEL), F32)]
    gain_spec = pl.BlockSpec((None, 2, D_MODEL), lambda i: (l, 0, 0))
    scratch = [
        pltpu.VMEM((D_MODEL, D_FF), BF16),
        pltpu.VMEM((D_MODEL, D_FF), BF16),
        pltpu.VMEM((D_FF, D_MODEL), BF16),
        pltpu.SemaphoreType.DMA((3,)),
    ]
    return pl.pallas_call(
        functools.partial(_tail_kernel, n_x=len(xs), n_out=2 if split_out else 1, n_ctx_tiles=n_ctx),
        grid=(m // tm,),
        in_specs=[o_spec, o_spec, o_spec] + (halves if len(xs) == 2 else [whole]) + [
            pl.BlockSpec((None, None, 6, D_MODEL), lambda i: (l, mod_row(i), 0, 0)),
            gain_spec, gain_spec,
            _resident((D_MODEL, N_GATES), lambda i: (0, 0)),
            _resident((N_BRANCH, BRANCH_DIM, D_MODEL), lambda i: (0, 0, 0)),
            _resident((D_MODEL, D_MODEL), lambda i: (0, 0)),
            pl.BlockSpec(memory_space=pl.ANY),
            pl.BlockSpec(memory_space=pl.ANY),
        ],
        out_specs=halves if split_out else whole,
        out_shape=half_shapes if split_out else jax.ShapeDtypeStruct((m, D_MODEL), F32),
        scratch_shapes=scratch,
        compiler_params=_cparams(("arbitrary",)),
        name="tail",
    )(oa, ob, oc, *xs, mods, g_pre, g_post, w_gates, w_branch, w_out, w_ffn_in, w_ffn_out)


def _rope_tables(seq):
    t = np.arange(seq)
    row = (t // GRID_W).astype(np.float32)
    col = (t % GRID_W).astype(np.float32)
    n_pairs_axis = HEAD_DIM // 4
    inv = (np.float32(ROPE_BASE) ** (-np.arange(n_pairs_axis, dtype=np.float32) / n_pairs_axis)
           ).astype(np.float32)
    ang = np.concatenate([row[:, None] * inv, col[:, None] * inv], axis=-1).astype(np.float64)
    cos, sin = np.cos(ang), np.sin(ang)
    cos_l = np.tile(cos, (1, LANES // cos.shape[1]))
    sin_l = np.tile(np.concatenate([-sin, sin], axis=-1), (1, LANES // HEAD_DIM))
    return jnp.asarray(cos_l, F32), jnp.asarray(sin_l, F32)


def kernel(x_prompt, x_sample, cache_a_k, cache_a_v, cache_b_k, cache_b_v, c, c_ctx, w_ada, b_ada,
           norm_pre, norm_post, w_in, a_sink, b_rpb, w_branch, w_out, w_ffn_in, w_ffn_out):
    batch, seq, _ = x_prompt.shape
    dec_batch, dec_seq, _ = x_sample.shape
    past = cache_a_k.shape[2]
    assert dec_batch <= CTX_MOD_ROW and seq % A_BLOCK == 0 and dec_seq % A_BLOCK == 0

    cvec = jnp.concatenate(
        [c, c_ctx[None, :], jnp.zeros((MOD_ROWS - dec_batch - 1, D_MODEL), F32)], axis=0)
    assert w_in.shape[-1] == D_IN
    mods, w_proj_b, w_gates_b = _modulation(cvec, w_ada, b_ada, w_in)
    mods = mods.reshape(DEPTH, MOD_ROWS, 6, D_MODEL)
    w_branch_2d = w_branch.reshape(DEPTH, N_BRANCH * BRANCH_DIM, D_MODEL)
    g_pre, g_post = norm_pre, norm_post
    rope_tabs = _rope_tables(dec_seq)
    cak = cache_a_k.reshape(dec_batch, DEPTH, past, KV_A)
    cav = cache_a_v.reshape(dec_batch, DEPTH, past, KV_A)
    cbk = cache_b_k.reshape(dec_batch, DEPTH, past, BRANCH_DIM)
    cbv = cache_b_v.reshape(dec_batch, DEPTH, past, BRANCH_DIM)

    n_dr, n_dc = b_rpb.shape[2:]
    rpb_pad = jnp.pad(b_rpb.astype(F32), ((0, 0), (0, 0), (0, (-n_dr) % 8), (0, LANES - n_dc)))

    tm_proj, tm_tail = 1024, 512
    m_ctx, m_lat = batch * seq, dec_batch * dec_seq
    m_all = m_ctx + m_lat
    assert m_ctx % tm_proj == 0 and m_lat % tm_proj == 0
    ctx_row = lambda i: CTX_MOD_ROW
    lat_row = lambda i: (i * tm_proj) // dec_seq
    tail_row = lambda i: jnp.where(i < m_ctx // tm_tail, CTX_MOD_ROW, (i * tm_tail - m_ctx) // dec_seq)

    xs = (x_prompt.reshape(m_ctx, D_MODEL), x_sample.reshape(m_lat, D_MODEL))
    caches = None
    for l in range(DEPTH):
        x_ctx, x_lat = (xs[0], xs[1]) if len(xs) == 2 else (xs[0], xs[0])
        lat_tile0 = 0 if len(xs) == 2 else m_ctx // tm_proj
        qx, oc, *rest = _proj(x_ctx, mods, l, g_pre, w_proj_b, m=m_ctx, x_tile0=0, tm=tm_proj,
                              mod_row=ctx_row, seq=seq, oc_rows=m_all, caches=caches,
                              cast_weights=(w_ffn_in,))
        caches, (w_ffn_in_b,) = rest[:4], rest[4:]
        oa, ob, w_branch_b, w_out_b, w_ffn_out_b = _attn_ctx(
            qx, caches, a_sink, l, seq, (w_branch_2d, w_out, w_ffn_out), m_all)
        qx, kv, oc = _proj(x_lat, mods, l, g_pre, w_proj_b, m=m_lat, x_tile0=lat_tile0, tm=tm_proj,
                           mod_row=lat_row, seq=dec_seq, oc_all=oc, rope_tabs=rope_tabs)
        last = l == DEPTH - 1
        w_gates_l = w_gates_b
        if last:
            oa, = _attn_a_lat(qx, qx, cak, cav, a_sink, l, dec_seq, oa)
        else:
            oa, w_proj_b, w_gates_b = _attn_a_lat(qx, qx, cak, cav, a_sink, l, dec_seq, oa,
                                                  w_in_next=(w_in, l + 1))
        ob = _attn_b_lat(qx, kv, cbk, cbv, rpb_pad, l, dec_seq, ob)
        out = _tail(oa, ob, oc, xs, mods, l, g_pre, g_post, w_gates_l,
                    w_branch_b.reshape(N_BRANCH, BRANCH_DIM, D_MODEL), w_out_b, w_ffn_in_b, w_ffn_out_b,
                    tm=tm_tail, m_ctx=m_ctx, mod_row=tail_row, split_out=last)
        xs = tuple(out) if last else (out,)
    y_prompt = xs[0].reshape(batch, seq, D_MODEL)
    y_sample = xs[1].reshape(dec_batch, dec_seq, D_MODEL)
    nak, nav, nbk, nbv = caches
    new_a_k = nak.reshape(batch, DEPTH, seq, A_KV_HEADS, HEAD_DIM)
    new_a_v = nav.reshape(batch, DEPTH, seq, A_KV_HEADS, HEAD_DIM)
    new_b_k = nbk.reshape(batch, DEPTH, seq, B_HEADS, HEAD_DIM)
    new_b_v = nbv.reshape(batch, DEPTH, seq, B_HEADS, HEAD_DIM)
    return (y_prompt, y_sample, new_a_k, new_a_v, new_b_k, new_b_v)
```

```python
import functools

import numpy as np
import jax
import jax.numpy as jnp
from jax import lax
from jax.experimental import pallas as pl
from jax.experimental.pallas import tpu as pltpu

F32 = jnp.float32
BF16 = jnp.bfloat16

D_MODEL = 1024
DEPTH = 2
GRID_W = 64
HEAD_DIM = 64
BRANCH_DIM = D_MODEL // 2
A_Q_HEADS = BRANCH_DIM // HEAD_DIM
A_KV_HEADS = A_Q_HEADS // 4
A_GROUP = A_Q_HEADS // A_KV_HEADS
A_BLOCK = 128
B_HEADS = BRANCH_DIM // HEAD_DIM
B_WIN_ROWS = 8
B_WIN_COLS = 16
C_GROUPS = 4
C_GROUP_DIM = BRANCH_DIM // C_GROUPS
N_BRANCH = 3
D_FF = -(-8 * D_MODEL // (3 * 256)) * 256
ROPE_BASE = 10000.0
NORM_EPS = 1e-6
NEG_INF = -1e30
SCALE = HEAD_DIM ** -0.5

LANES = 128
KV_A = A_KV_HEADS * HEAD_DIM
N_GATES = N_BRANCH * D_MODEL
W_QA = 0
W_KA = W_QA + BRANCH_DIM
W_VA = W_KA + KV_A
W_QB = W_VA + KV_A
W_KB = W_QB + BRANCH_DIM
W_VB = W_KB + BRANCH_DIM
W_UC = W_VB + BRANCH_DIM
W_GATES = W_UC + BRANCH_DIM
D_IN = W_GATES + N_GATES
QX_QA, QX_QB, QX_KVA = 0, BRANCH_DIM, 2 * BRANCH_DIM
KV_KB, KV_VB = 0, BRANCH_DIM
MOD_ROWS = 8
CTX_MOD_ROW = 4
VMEM_LIMIT = 56 * 1024 * 1024
BF16_SUBLANES = 16
CTX_SEQS_PER_STEP = 4


def _cparams(sem):
    return pltpu.CompilerParams(dimension_semantics=sem, vmem_limit_bytes=VMEM_LIMIT)


def _resident(shape, index_map):
    return pl.BlockSpec(shape, index_map, pipeline_mode=pl.Buffered(1))


def _rms(x, w):
    return x * lax.rsqrt(jnp.mean(x * x, axis=-1, keepdims=True) + NORM_EPS) * w


def _sigmoid(x):
    return 0.5 * jnp.tanh(0.5 * x) + 0.5


def _silu(x):
    half = 0.5 * x
    return half + half * jnp.tanh(half)


def _dot_t(a, b):
    return lax.dot_general(a, b, (((1,), (1,)), ((), ())), preferred_element_type=F32)


def _dot(a, b):
    return jnp.dot(a, b, preferred_element_type=F32)


def _scaled(q):
    assert np.log2(SCALE) == round(np.log2(SCALE))
    return q * jnp.asarray(SCALE, q.dtype)


def _low_lanes(shape):
    return lax.broadcasted_iota(jnp.int32, shape, len(shape) - 1) < HEAD_DIM


def _mod_kernel(cv_ref, w_ref, b_ref, win_ref, o_ref, wproj_ref, wgates_ref):
    a, b = _silu(cv_ref[...]), w_ref[...]
    a_hi, b_hi = a.astype(BF16), b.astype(BF16)
    a_lo = (a - a_hi.astype(F32)).astype(BF16)
    b_lo = (b - b_hi.astype(F32)).astype(BF16)
    bias = b_ref[pl.ds(pl.program_id(0), 1), :]
    o_ref[...] = _dot(a_hi, b_hi) + (_dot(a_lo, b_hi) + _dot(a_hi, b_lo)) + bias
    @pl.when(pl.program_id(0) == 0)
    def _():
        _split_cast_w_in(win_ref, wproj_ref, wgates_ref)


def _split_cast_w_in(win_ref, wproj_ref, wgates_ref):
    wproj_ref[...] = win_ref[:, :W_GATES].astype(wproj_ref.dtype)
    wgates_ref[...] = win_ref[:, W_GATES:].astype(wgates_ref.dtype)


def _modulation(cvec, w_ada, b_ada, w_in):
    tn = 1536
    n = 6 * D_MODEL
    steps = n // tn
    rows = w_in.shape[1] // steps
    slab = lambda l, j: jnp.where(l == 0, j, steps - 1)
    return pl.pallas_call(
        _mod_kernel,
        grid=(DEPTH, steps),
        in_specs=[
            pl.BlockSpec((MOD_ROWS, D_MODEL), lambda l, j: (0, 0)),
            pl.BlockSpec((None, D_MODEL, tn), lambda l, j: (l, 0, j)),
            pl.BlockSpec((DEPTH, tn), lambda l, j: (0, j)),
            pl.BlockSpec((None, rows, D_IN), lambda l, j: (0, slab(l, j), 0)),
        ],
        out_specs=[pl.BlockSpec((None, MOD_ROWS, tn), lambda l, j: (l, 0, j)),
                   pl.BlockSpec((rows, W_GATES), lambda l, j: (slab(l, j), 0)),
                   pl.BlockSpec((rows, N_GATES), lambda l, j: (slab(l, j), 0))],
        out_shape=[jax.ShapeDtypeStruct((DEPTH, MOD_ROWS, n), F32),
                   jax.ShapeDtypeStruct((D_MODEL, W_GATES), BF16),
                   jax.ShapeDtypeStruct((D_MODEL, N_GATES), BF16)],
        compiler_params=_cparams(("arbitrary", "arbitrary")),
        name="modulation",
    )(cvec, w_ada, b_ada, w_in)


def _rope_cols(v, cos, sin):
    first = (lax.broadcasted_iota(jnp.int32, (v.shape[0], LANES), 1) % HEAD_DIM) < HEAD_DIM // 2
    outs = []
    for c in range(v.shape[1] // LANES):
        vc = v[:, c * LANES:(c + 1) * LANES]
        partner = jnp.where(first, pltpu.roll(vc, LANES - HEAD_DIM // 2, 1),
                            pltpu.roll(vc, HEAD_DIM // 2, 1))
        outs.append(vc * cos + partner * sin)
    return outs[0] if len(outs) == 1 else jnp.concatenate(outs, axis=1)


def _with_rope(acc, lo, hi, cos_ref, sin_ref):
    parts = []
    if lo > 0:
        parts.append(acc[:, :lo])
    parts.append(_rope_cols(acc[:, lo:hi], cos_ref[...], sin_ref[...]))
    if hi < acc.shape[1]:
        parts.append(acc[:, hi:])
    return parts[0] if len(parts) == 1 else jnp.concatenate(parts, axis=1)


def _proj_kernel(*refs, latent, n_alias, n_cast):
    x_ref, mod_ref, g_ref, w_ref, bc_ref, bs_ref, pos_ref = refs[:7]
    if latent:
        cos_ref, sin_ref, _, qx_ref, kv_ref, oc_ref = refs[7:]
    else:
        rest = refs[7 + n_alias:]
        cast_src, rest = rest[:n_cast], rest[n_cast:]
        qx_ref, oc_ref, ka_ref, va_ref, kb_ref, vb_ref = rest[:6]
        for src, dst in zip(cast_src, rest[6:6 + n_cast]):
            dst[...] = src[...].astype(dst.dtype)

    h = (_rms(x_ref[...], g_ref[0:1, :] * (1.0 + mod_ref[1:2, :])) + mod_ref[0:1, :]).astype(BF16)
    acc = _dot(h, w_ref[:, W_QA:W_KB])
    qa = acc[:, W_QA:W_QA + BRANCH_DIM]
    qb = acc[:, W_QB:W_QB + BRANCH_DIM]
    kva = acc[:, W_KA:W_KA + 2 * KV_A]
    if latent:
        qa = _rope_cols(qa, cos_ref[...], sin_ref[...])
        kva = _with_rope(kva, 0, KV_A, cos_ref, sin_ref)
        qx_ref[:, QX_KVA:QX_KVA + 2 * KV_A] = kva.astype(qx_ref.dtype)
    else:
        ka_ref[...] = kva[:, :KV_A].reshape(ka_ref.shape)
        va_ref[...] = kva[:, KV_A:].reshape(va_ref.shape)
    qx_ref[:, QX_QA:QX_QA + BRANCH_DIM] = qa.astype(qx_ref.dtype)
    qx_ref[:, QX_QB:QX_QB + BRANCH_DIM] = qb.astype(qx_ref.dtype)

    acc = _dot(h, w_ref[:, W_KB:W_GATES])
    kb = acc[:, 0:BRANCH_DIM]
    vb = acc[:, W_VB - W_KB:W_VB - W_KB + BRANCH_DIM]
    uc = acc[:, W_UC - W_KB:W_UC - W_KB + BRANCH_DIM]
    if latent:
        kv_ref[:, KV_KB:KV_KB + BRANCH_DIM] = kb.astype(kv_ref.dtype)
        kv_ref[:, KV_VB:KV_VB + BRANCH_DIM] = vb.astype(kv_ref.dtype)
    else:
        kb_ref[...] = kb.reshape(kb_ref.shape)
        vb_ref[...] = vb.reshape(vb_ref.shape)
    _fourier_mix(uc.astype(BF16), bc_ref, bs_ref, pos_ref, oc_ref)


def _proj(x, mods, l, g_pre, w_in, *, m, x_tile0, tm, mod_row, seq, oc_rows=None, oc_all=None,
          rope_tabs=None, caches=None, cast_weights=()):
    assert tm % seq == 0
    latent = rope_tabs is not None
    bc, bs, pos = _dft_tables(seq)
    pair = 2 * C_GROUP_DIM
    in_specs = [
        pl.BlockSpec((tm, D_MODEL), lambda i: (i + x_tile0, 0)),
        pl.BlockSpec((None, None, 6, D_MODEL), lambda i: (l, mod_row(i), 0, 0)),
        pl.BlockSpec((None, 2, D_MODEL), lambda i: (l, 0, 0)),
        _resident((D_MODEL, W_GATES), lambda i: (0, 0)),
        _resident((pair, pair), lambda i: (0, 0)),
        _resident((pair, pair), lambda i: (0, 0)),
        _resident((seq, 2 * seq), lambda i: (0, 0)),
    ]
    args = [x, mods, g_pre, w_in, bc, bs, pos]
    aliases = {}
    if latent:
        seq_tiles = seq // tm
        tab_spec = pl.BlockSpec((tm, LANES), lambda i: (i % seq_tiles, 0))
        in_specs += [tab_spec, tab_spec, pl.BlockSpec(memory_space=pl.ANY)]
        args += list(rope_tabs)
        widths = (2 * BRANCH_DIM + 2 * KV_A, 2 * BRANCH_DIM)
        aliases = {len(args): len(widths)}
        args.append(oc_all)
        oc_rows, oc_tile0 = oc_all.shape[0], (oc_all.shape[0] - m) // tm
        cache_specs, cache_shapes = [], []
    else:
        widths = (2 * BRANCH_DIM,)
        oc_tile0 = 0
        cache_widths = (KV_A, KV_A, BRANCH_DIM, BRANCH_DIM)
        if caches is not None:
            in_specs += [pl.BlockSpec(memory_space=pl.ANY)] * len(caches)
            aliases = {len(args) + k: len(widths) + 1 + k for k in range(len(caches))}
            args += list(caches)
        cache_specs = [pl.BlockSpec((tm // seq, None, seq, w), lambda i: (i, l, 0, 0))
                       for w in cache_widths]
        cache_shapes = [jax.ShapeDtypeStruct((m // seq, DEPTH, seq, w), F32) for w in cache_widths]
        steps = m // tm
        for w in cast_weights:
            _, rows, cols = w.shape
            assert rows % (steps * BF16_SUBLANES) == 0
            in_specs.append(pl.BlockSpec((None, rows // steps, cols), lambda i: (l, i, 0)))
            cache_specs.append(pl.BlockSpec((rows // steps, cols), lambda i: (i, 0)))
            cache_shapes.append(jax.ShapeDtypeStruct((rows, cols), BF16))
            args.append(w)
    return pl.pallas_call(
        functools.partial(_proj_kernel, latent=latent, n_alias=len(aliases), n_cast=len(cast_weights)),
        grid=(m // tm,),
        in_specs=in_specs,
        out_specs=([pl.BlockSpec((tm, w), lambda i: (i, 0)) for w in widths]
                   + [pl.BlockSpec((tm, BRANCH_DIM), lambda i: (i + oc_tile0, 0))] + cache_specs),
        out_shape=([jax.ShapeDtypeStruct((m, w), BF16) for w in widths]
                   + [jax.ShapeDtypeStruct((oc_rows, BRANCH_DIM), BF16)] + cache_shapes),
        input_output_aliases=aliases,
        compiler_params=_cparams(("parallel",)),
        name="proj_lat" if latent else "proj_ctx",
    )(*args)


def _with_ones(v):
    return jnp.concatenate([v, jnp.ones_like(v)], axis=1)


def _softmax_pv(parts, sink=None):
    m = parts[0][0].max(axis=-1, keepdims=True)
    for s, _ in parts[1:]:
        m = jnp.maximum(m, s.max(axis=-1, keepdims=True))
    if sink is not None:
        m = jnp.maximum(m, sink)
    ones_half = parts[0][1].shape[1] == 2 * LANES
    acc = None
    den = None
    for s, v in parts:
        e = jnp.exp(s - m)
        if not ones_half:
            d = e.sum(axis=-1, keepdims=True)
            den = d if den is None else den + d
        o = _dot(e.astype(BF16), v)
        acc = o if acc is None else acc + o
    if ones_half:
        acc, den = acc[:, :LANES], acc[:, LANES:]
    if sink is not None:
        den = den + jnp.exp(sink - m)
    return acc * (1.0 / den)


def _kv_head_variants(x2):
    low = _low_lanes(x2.shape)
    xr = pltpu.roll(x2, HEAD_DIM, 1)
    zero = jnp.zeros_like(x2)
    return [
        [jnp.where(low, x2, zero).astype(BF16), jnp.where(low, zero, xr).astype(BF16)],
        [jnp.where(low, xr, zero).astype(BF16), jnp.where(low, zero, x2).astype(BF16)],
    ]


def _attn_ctx_kernel(sink_ref, qa_ref, qb_ref, ka_ref, va_ref, kb_ref, vb_ref, *rest, n_weights, layer):
    w_src, (oa_ref, ob_ref), w_dst = rest[:n_weights], rest[n_weights:n_weights + 2], rest[n_weights + 2:]
    for src, dst in zip(w_src, w_dst):
        dst[...] = src[...].astype(dst.dtype)
    seq = ka_ref.shape[1]
    chunks = [slice(j * LANES, (j + 1) * LANES) for j in range(A_Q_HEADS // 2)]
    low = _low_lanes((seq, LANES))
    for s in range(ka_ref.shape[0]):
        rows = slice(s * seq, (s + 1) * seq)
        ka = [jnp.concatenate(v, axis=0) for v in _kv_head_variants(ka_ref[s])]
        va = [jnp.concatenate(v, axis=0) for v in _kv_head_variants(va_ref[s])]
        work = []
        for j, cols in enumerate(chunks):
            g = (2 * j) // A_GROUP
            s2 = _dot_t(_scaled(qa_ref[rows, cols]), ka[g])
            work.append((oa_ref, cols, s2, va[g], (sink_ref[layer, 2 * j], sink_ref[layer, 2 * j + 1])))
        for cols in chunks:
            kc, vc = kb_ref[s, :, cols], vb_ref[s, :, cols]
            zero = jnp.zeros_like(kc)
            k2 = jnp.concatenate([jnp.where(low, kc, zero), jnp.where(low, zero, kc)], axis=0).astype(BF16)
            v2 = jnp.concatenate([jnp.where(low, vc, zero), jnp.where(low, zero, vc)], axis=0).astype(BF16)
            work.append((ob_ref, cols, _dot_t(_scaled(qb_ref[rows, cols]), k2), v2, None))
        for o_ref, cols, s2, v2, sinks in work:
            o_ref[rows, cols] = _pair_softmax_pv(s2, v2, sinks).astype(o_ref.dtype)


def _pair_softmax_pv(s2, v2, sinks=None):
    tk = s2.shape[1] // 2
    es, extra = [], []
    for p in range(2):
        s = s2[:, p * tk:(p + 1) * tk]
        m = s.max(axis=-1, keepdims=True)
        if sinks is not None:
            m = jnp.maximum(m, sinks[p])
            extra.append(jnp.exp(sinks[p] - m))
        es.append(jnp.exp(s - m).astype(BF16))
    lane_half = lax.broadcasted_iota(jnp.int32, v2.shape, 1) // HEAD_DIM
    row_half = lax.broadcasted_iota(jnp.int32, v2.shape, 0) // tk
    ones2 = jnp.where(lane_half == row_half, 1.0, 0.0).astype(v2.dtype)
    acc = _dot(jnp.concatenate(es, axis=1), jnp.concatenate([v2, ones2], axis=1))
    den = acc[:, LANES:]
    if sinks is not None:
        den = den + jnp.where(_low_lanes(den.shape), extra[0], extra[1])
    return acc[:, :LANES] * (1.0 / den)


def _attn_ctx(qx, caches, sink, l, seq, weights, out_rows):
    m = qx.shape[0]
    per_step = CTX_SEQS_PER_STEP
    rows_step = per_step * seq
    steps = m // rows_step
    out = jax.ShapeDtypeStruct((out_rows, BRANCH_DIM), BF16)
    cache_specs = [pl.BlockSpec((per_step, None, seq, c.shape[-1]), lambda b: (b, l, 0, 0)) for c in caches]
    w_in_specs, w_out_specs, w_shapes = [], [], []
    for w in weights:
        _, rows, cols = w.shape
        assert rows % (steps * BF16_SUBLANES) == 0
        w_in_specs.append(pl.BlockSpec((None, rows // steps, cols), lambda b: (l, b, 0)))
        w_out_specs.append(pl.BlockSpec((rows // steps, cols), lambda b: (b, 0)))
        w_shapes.append(jax.ShapeDtypeStruct((rows, cols), BF16))
    return pl.pallas_call(
        functools.partial(_attn_ctx_kernel, n_weights=len(weights), layer=l),
        grid=(steps,),
        in_specs=[
            pl.BlockSpec(memory_space=pltpu.SMEM),
            pl.BlockSpec((rows_step, BRANCH_DIM), lambda b: (b, QX_QA // BRANCH_DIM)),
            pl.BlockSpec((rows_step, BRANCH_DIM), lambda b: (b, QX_QB // BRANCH_DIM)),
        ] + cache_specs + w_in_specs,
        out_specs=[pl.BlockSpec((rows_step, BRANCH_DIM), lambda b: (b, 0))] * 2 + w_out_specs,
        out_shape=[out, out] + w_shapes,
        compiler_params=_cparams(("parallel",)),
        name="attn_ctx",
    )(sink, qx, qx, *caches, *weights)


def _attn_a_lat_kernel(sink_ref, q_ref, kv_ref, ck_ref, cv_ref, _, *rest, n_blocks, cast_w_in, layer):
    if cast_w_in:
        win_ref, o_ref, wproj_ref, wgates_ref = rest[:4]
        _split_cast_w_in(win_ref, wproj_ref, wgates_ref)
        rest = rest[4:]
    else:
        o_ref, rest = rest[0], rest[1:]
    qh_scr, kp_scr, vp_scr, ckb_scr, cvb_scr, mask_scr = rest
    seq = n_blocks * A_BLOCK
    band = 3 * A_BLOCK
    low_seq = _low_lanes((seq, LANES))
    for h in range(A_Q_HEADS):
        j, p, g = h // 2, h % 2, h // A_GROUP
        x = q_ref[:, j * LANES:(j + 1) * LANES].astype(F32) * SCALE
        if p != g:
            x = pltpu.roll(x, HEAD_DIM, 1)
        qh_scr[h] = jnp.where(low_seq if g == 0 else jnp.logical_not(low_seq), x, 0.0).astype(BF16)
    pad = jnp.zeros((A_BLOCK, LANES), BF16)
    for scr, col in ((kp_scr, 0), (vp_scr, KV_A)):
        scr[0:A_BLOCK, :] = pad
        scr[A_BLOCK:A_BLOCK + seq, :] = kv_ref[:, col:col + KV_A]
        scr[A_BLOCK + seq:2 * A_BLOCK + seq, :] = pad
    ckb_scr[...] = ck_ref[...].astype(BF16)
    cvb_scr[...] = cv_ref[...].astype(BF16)

    rows = A_GROUP * A_BLOCK
    row = lax.broadcasted_iota(jnp.int32, (rows, band), 0)
    qi = row % A_BLOCK
    kj = lax.broadcasted_iota(jnp.int32, (rows, band), 1)
    head_row = lax.broadcasted_iota(jnp.int32, (rows, 1), 0) // A_BLOCK
    low_blk = _low_lanes((A_BLOCK, LANES))
    @pl.when(pl.program_id(0) == 0)
    def _():
        for case, (no_prev, no_next) in enumerate(((1, 0), (0, 0), (0, 1))):
            valid = (kj >= qi + no_prev * (A_BLOCK - qi)) & (kj <= 2 * A_BLOCK + qi - no_next * (qi + 1))
            mask_scr[case] = jnp.where(valid, 0.0, NEG_INF)

    def body(n, carry):
        start = pl.multiple_of(n * A_BLOCK, A_BLOCK)
        case = jnp.where(n > 0, 1, 0) + jnp.where(n < n_blocks - 1, 0, 1)
        scores = []
        for g in range(A_KV_HEADS):
            q = jnp.concatenate(
                [qh_scr[A_GROUP * g + i, pl.ds(start, A_BLOCK), :] for i in range(A_GROUP)], axis=0)
            s_band = _dot_t(q, kp_scr[pl.ds(start, band), :]) + mask_scr[case]
            scores.append((s_band, _dot_t(q, ckb_scr[...])))
        outs = []
        for g, (s_band, s_ctx) in enumerate(scores):
            sink = jnp.full((rows, 1), sink_ref[layer, A_GROUP * g], F32)
            for i in range(1, A_GROUP):
                sink = jnp.where(head_row == i, sink_ref[layer, A_GROUP * g + i], sink)
            outs.append(_softmax_pv([(s_band, vp_scr[pl.ds(start, band), :]), (s_ctx, cvb_scr[...])],
                                    sink=sink))
        for j in range(A_Q_HEADS // 2):
            halves = []
            for p in range(2):
                h = 2 * j + p
                g, i = h // A_GROUP, h % A_GROUP
                t = outs[g][i * A_BLOCK:(i + 1) * A_BLOCK]
                halves.append(t if p == g else pltpu.roll(t, HEAD_DIM, 1))
            o_ref[pl.ds(start, A_BLOCK), j * LANES:(j + 1) * LANES] = (
                jnp.where(low_blk, halves[0], halves[1]).astype(o_ref.dtype))
        return carry

    lax.fori_loop(0, n_blocks, body, 0, unroll=2)


def _attn_lat_kernel(*refs, n_blocks, cast_w_in, layer, rows, kr):
    n_a_in = 6 + (1 if cast_w_in else 0)
    n_a_out = 1 + (2 if cast_w_in else 0)
    a_in, refs = refs[:n_a_in], refs[n_a_in:]
    b_in, refs = refs[:7], refs[7:]
    a_out, refs = refs[:n_a_out], refs[n_a_out:]
    b_out, refs = refs[:1], refs[1:]
    a_scr, b_scr = refs[:6], refs[6:]
    _attn_a_lat_kernel(*a_in, *a_out, *a_scr, n_blocks=n_blocks, cast_w_in=cast_w_in, layer=layer)
    _attn_b_lat_kernel(*b_in, *b_out, *b_scr, rows=rows, kr=kr)


def _attn_lat(qx, kv, caches_a, caches_b, sink, rpb_pad, l, seq, oa_all, ob_all, w_in_next=None):
    m = qx.shape[0]
    nb = seq // A_BLOCK
    assert nb >= 2
    rows = seq // GRID_W
    kr = min(B_WIN_ROWS, rows)
    assert kr % 2 == 0 and 2 * GRID_W == LANES
    n_seq = m // seq
    past = caches_a[0].shape[2]
    first = (oa_all.shape[0] - m) // seq
    cache_a_spec = pl.BlockSpec((None, None, past, KV_A), lambda b: (b, l, 0, 0))
    cache_b_spec = pl.BlockSpec((None, None, past, BRANCH_DIM), lambda b: (b, l, 0, 0))
    out_spec = pl.BlockSpec((seq, BRANCH_DIM), lambda b: (b + first, 0))
    a_in_specs = [
        pl.BlockSpec(memory_space=pltpu.SMEM),
        pl.BlockSpec((seq, BRANCH_DIM), lambda b: (b, QX_QA // BRANCH_DIM)),
        pl.BlockSpec((seq, 2 * KV_A), lambda b: (b, QX_KVA // (2 * KV_A))),
        cache_a_spec, cache_a_spec,
        pl.BlockSpec(memory_space=pl.ANY),
    ]
    a_args = [sink, qx, qx, *caches_a, oa_all]
    a_out_specs = [out_spec]
    a_out_shape = [jax.ShapeDtypeStruct(oa_all.shape, BF16)]
    if w_in_next is not None:
        w_in, layer = w_in_next
        slab = w_in.shape[1] // n_seq
        assert w_in.shape[1] % (n_seq * BF16_SUBLANES) == 0
        a_in_specs.append(pl.BlockSpec((None, slab, D_IN), lambda b: (layer, b, 0)))
        a_args.append(w_in)
        a_out_specs += [pl.BlockSpec((slab, W_GATES), lambda b: (b, 0)),
                        pl.BlockSpec((slab, N_GATES), lambda b: (b, 0))]
        a_out_shape += [jax.ShapeDtypeStruct((D_MODEL, W_GATES), BF16),
                        jax.ShapeDtypeStruct((D_MODEL, N_GATES), BF16)]
    b_in_specs = [
        pl.BlockSpec((seq, BRANCH_DIM), lambda b: (b, QX_QB // BRANCH_DIM)),
        pl.BlockSpec((seq, BRANCH_DIM), lambda b: (b, KV_KB // BRANCH_DIM)),
        pl.BlockSpec((seq, BRANCH_DIM), lambda b: (b, KV_VB // BRANCH_DIM)),
        cache_b_spec, cache_b_spec,
        _resident((None,) + rpb_pad.shape[1:], lambda b: (l, 0, 0, 0)),
        pl.BlockSpec(memory_space=pl.ANY),
    ]
    b_args = [qx, kv, kv, *caches_b, rpb_pad, ob_all]
    return pl.pallas_call(
        functools.partial(_attn_lat_kernel, n_blocks=nb, cast_w_in=w_in_next is not None, layer=l,
                          rows=rows, kr=kr),
        grid=(n_seq,),
        in_specs=a_in_specs + b_in_specs,
        out_specs=a_out_specs + [out_spec],
        out_shape=a_out_shape + [jax.ShapeDtypeStruct(ob_all.shape, BF16)],
        input_output_aliases={5: 0, len(a_args) + 6: len(a_out_shape)},
        scratch_shapes=[
            pltpu.VMEM((A_Q_HEADS, seq, LANES), BF16),
            pltpu.VMEM((seq + 2 * A_BLOCK, LANES), BF16),
            pltpu.VMEM((seq + 2 * A_BLOCK, LANES), BF16),
            pltpu.VMEM((past, LANES), BF16),
            pltpu.VMEM((past, LANES), BF16),
            pltpu.VMEM((3, A_GROUP * A_BLOCK, 3 * A_BLOCK), F32),
            pltpu.VMEM((past, BRANCH_DIM), BF16),
            pltpu.VMEM((B_HEADS // 2, seq, 2 * LANES), BF16),
            pltpu.VMEM((B_HEADS // 2, past, 2 * LANES), BF16),
            pltpu.VMEM((kr, B_HEADS // 2, 2 * GRID_W, kr * GRID_W), F32),
        ],
        compiler_params=_cparams(("arbitrary",)),
        name="attn_lat",
    )(*a_args, *b_args)


def _attn_b_lat_kernel(q_ref, k_ref, v_ref, ck_ref, cv_ref, rpb_ref, _, o_ref, ckb_scr, vx_scr, cvx_scr,
                       bias_ref, *, rows, kr):
    @pl.when(pl.program_id(0) == 0)
    def _():
        _fill_neighbourhood_bias(rpb_ref, bias_ref, kr=kr)

    half = kr // 2
    n_loc = kr * GRID_W
    low_q = _low_lanes((GRID_W, LANES))
    chunks = [slice(j * LANES, (j + 1) * LANES) for j in range(B_HEADS // 2)]
    ckb_scr[...] = ck_ref[...].astype(BF16)
    for j, cols in enumerate(chunks):
        vx_scr[j] = _with_ones(v_ref[:, cols])
        cvx_scr[j] = _with_ones(cv_ref[:, cols].astype(BF16))
    rows_per_iter = 4
    assert rows % rows_per_iter == 0

    def body(it, carry):
        work = []
        for u in range(rows_per_iter):
            r = it * rows_per_iter + u
            r0 = jnp.clip(r - half, 0, rows - kr)
            q0 = pl.multiple_of(r * GRID_W, GRID_W)
            start = pl.multiple_of(r0 * GRID_W, GRID_W)
            for j, cols in enumerate(chunks):
                qc = _scaled(q_ref[pl.ds(q0, GRID_W), cols])
                zero = jnp.zeros_like(qc)
                qs = jnp.concatenate([jnp.where(low_q, qc, zero), jnp.where(low_q, zero, qc)], axis=0)
                s_loc = _dot_t(qs, k_ref[pl.ds(start, n_loc), cols]) + bias_ref[r - r0, j]
                work.append((q0, start, j, cols, s_loc, _dot_t(qs, ckb_scr[:, cols])))
        for q0, start, j, cols, s_loc, s_ctx in work:
            o2 = _softmax_pv([(s_loc, vx_scr[j, pl.ds(start, n_loc), :]), (s_ctx, cvx_scr[j])])
            o = jnp.where(low_q, o2[:GRID_W], o2[GRID_W:])
            o_ref[pl.ds(q0, GRID_W), cols] = o.astype(o_ref.dtype)
        return carry

    lax.fori_loop(0, rows // rows_per_iter, body, 0)


def _fill_neighbourhood_bias(rpb_ref, bias_scr, *, kr):
    shape = (GRID_W, LANES)
    c = lax.broadcasted_iota(jnp.int32, shape, 0)
    lane = lax.broadcasted_iota(jnp.int32, shape, 1)
    c2 = lane % GRID_W
    ws = jnp.clip(c - B_WIN_COLS // 2, 0, GRID_W - B_WIN_COLS)
    ok = (c2 >= ws) & (c2 < ws + B_WIN_COLS)
    low = lane < GRID_W

    def toeplitz(h, dr, lane0):
        row = jnp.broadcast_to(rpb_ref[h, dr:dr + 1, :], shape)
        return pltpu.roll(row, (lane0 - (B_WIN_COLS - 1)) % LANES, 1, stride=1, stride_axis=0)

    for h in range(B_HEADS):
        rows = slice(h % 2 * GRID_W, (h % 2 + 1) * GRID_W)
        pairs = {}
        for v in range(kr):
            for m in range(0, kr, 2):
                dr = m - v + B_WIN_ROWS - 1
                if dr not in pairs:
                    pair = jnp.where(low, toeplitz(h, dr, 0), toeplitz(h, dr + 1, GRID_W))
                    pairs[dr] = jnp.where(ok, pair, NEG_INF)
                bias_scr[v, h // 2, rows, m * GRID_W:(m + 2) * GRID_W] = pairs[dr]


def _dft_tables(seq):
    cd = C_GROUP_DIM
    kc = (np.arange(cd)[:, None] * np.arange(cd)[None, :]) % cd
    ang_c = 2.0 * np.pi * kc / cd
    eye2 = np.eye(2)
    bd_cos = np.kron(eye2, np.cos(ang_c))
    bd_sin = np.kron(eye2, np.sin(ang_c))
    kt = (np.arange(seq)[:, None] * np.arange(seq)[None, :]) % seq
    ang_t = 2.0 * np.pi * kt / seq
    norm = 1.0 / np.sqrt(float(seq * cd))
    pos = np.concatenate([np.cos(ang_t), -np.sin(ang_t)], axis=1) * norm
    return (jnp.asarray(bd_cos, F32).astype(BF16), jnp.asarray(bd_sin, F32).astype(BF16),
            jnp.asarray(pos, F32).astype(BF16))


def _fourier_mix(u, bc_ref, bs_ref, pos_ref, o_ref):
    pair = 2 * C_GROUP_DIM
    seq = pos_ref.shape[0]
    uc, us = [], []
    for p in range(BRANCH_DIM // pair):
        up = u[:, p * pair:(p + 1) * pair]
        uc.append(_dot(up, bc_ref[...]))
        us.append(_dot(up, bs_ref[...]))
    zc = jnp.concatenate(uc, axis=1).astype(BF16)
    zs = jnp.concatenate(us, axis=1).astype(BF16)
    for s in range(u.shape[0] // seq):
        rows = slice(s * seq, (s + 1) * seq)
        z = jnp.concatenate([zc[rows], zs[rows]], axis=0)
        o_ref[rows, :] = _dot(pos_ref[...], z).astype(o_ref.dtype)


def _ffn_weight_copies(wfi_hbm, wfo_hbm, wfg_ref, wfu_ref, wfo_ref, sems):
    srcs = (wfi_hbm.at[:, 0:D_FF], wfi_hbm.at[:, D_FF:2 * D_FF], wfo_hbm)
    dsts = (wfg_ref, wfu_ref, wfo_ref)
    return [pltpu.make_async_copy(src, dst, sems.at[k]) for k, (src, dst) in enumerate(zip(srcs, dsts))]


def _tail_kernel(*refs, n_x, n_out, n_ctx_tiles):
    oa_ref, ob_ref, oc_ref = refs[:3]
    x_refs, refs = refs[3:3 + n_x], refs[3 + n_x:]
    mod_ref, gpre_ref, gpost_ref, wg_ref, wb_ref, wo_ref, wfi_hbm, wfo_hbm = refs[:8]
    o_refs, refs = refs[8:8 + n_out], refs[8 + n_out:]
    gpre0, gpre1 = gpre_ref[0:1, :], gpre_ref[1:2, :]
    gpost0, gpost1 = gpost_ref[0:1, :], gpost_ref[1:2, :]
    wfg_ref, wfu_ref, wfo_ref, sems = refs[:4]
    i = pl.program_id(0)
    first = i == 0
    is_ctx = i < n_ctx_tiles
    copies = _ffn_weight_copies(wfi_hbm, wfo_hbm, wfg_ref, wfu_ref, wfo_ref, sems)

    @pl.when(first)
    def _():
        for copy in copies:
            copy.start()

    if n_x == 1:
        x = x_refs[0][...]
    else:
        pick_ctx = jnp.full(x_refs[0].shape, jnp.where(is_ctx, 1, 0), jnp.int32) > 0
        x = jnp.where(pick_ctx, x_refs[0][...], x_refs[1][...])
    h = (_rms(x, gpre0 * (1.0 + mod_ref[1:2, :])) + mod_ref[0:1, :]).astype(BF16)
    gates = _dot(h, wg_ref[...])
    branches = [_dot(b_ref[...], wb_ref[k]) for k, b_ref in enumerate((oa_ref, ob_ref, oc_ref))]
    mix = None
    for k, branch in enumerate(branches):
        term = _sigmoid(gates[:, k * D_MODEL:(k + 1) * D_MODEL]) * branch
        mix = term if mix is None else mix + term
    y = _dot(mix.astype(BF16), wo_ref[...])
    x1 = x + _rms(y, mod_ref[2:3, :] * gpost0)
    h2 = (_rms(x1, gpre1 * (1.0 + mod_ref[4:5, :])) + mod_ref[3:4, :]).astype(BF16)

    @pl.when(first)
    def _():
        for copy in copies:
            copy.wait()

    act = _silu(_dot(h2, wfg_ref[...])) * _dot(h2, wfu_ref[...])
    f = _dot(act.astype(BF16), wfo_ref[...])
    out = x1 + _rms(f, mod_ref[5:6, :] * gpost1)
    if n_out == 1:
        o_refs[0][...] = out
    else:
        @pl.when(is_ctx)
        def _():
            o_refs[0][...] = out

        @pl.when(jnp.logical_not(is_ctx))
        def _():
            o_refs[1][...] = out


def _tail(oa, ob, oc, xs, mods, l, g_pre, g_post, w_gates, w_branch, w_out, w_ffn_in, w_ffn_out,
          *, tm, m_ctx, mod_row, split_out):
    m = oa.shape[0]
    n_ctx = m_ctx // tm
    o_spec = pl.BlockSpec((tm, BRANCH_DIM), lambda i: (i, 0))
    whole = pl.BlockSpec((tm, D_MODEL), lambda i: (i, 0))
    halves = [pl.BlockSpec((tm, D_MODEL), lambda i: (jnp.minimum(i, n_ctx - 1), 0)),
              pl.BlockSpec((tm, D_MODEL), lambda i: (jnp.maximum(i - n_ctx, 0), 0))]
    half_shapes = [jax.ShapeDtypeStruct((m_ctx, D_MODEL), F32),
                   jax.ShapeDtypeStruct((m - m_ctx, D_MODEL), F32)]
    gain_spec = pl.BlockSpec((None, 2, D_MODEL), lambda i: (l, 0, 0))
    scratch = [
        pltpu.VMEM((D_MODEL, D_FF), BF16),
        pltpu.VMEM((D_MODEL, D_FF), BF16),
        pltpu.VMEM((D_FF, D_MODEL), BF16),
        pltpu.SemaphoreType.DMA((3,)),
    ]
    return pl.pallas_call(
        functools.partial(_tail_kernel, n_x=len(xs), n_out=2 if split_out else 1, n_ctx_tiles=n_ctx),
        grid=(m // tm,),
        in_specs=[o_spec, o_spec, o_spec] + (halves if len(xs) == 2 else [whole]) + [
            pl.BlockSpec((None, None, 6, D_MODEL), lambda i: (l, mod_row(i), 0, 0)),
            gain_spec, gain_spec,
            _resident((D_MODEL, N_GATES), lambda i: (0, 0)),
            _resident((N_BRANCH, BRANCH_DIM, D_MODEL), lambda i: (0, 0, 0)),
            _resident((D_MODEL, D_MODEL), lambda i: (0, 0)),
            pl.BlockSpec(memory_space=pl.ANY),
            pl.BlockSpec(memory_space=pl.ANY),
        ],
        out_specs=halves if split_out else whole,
        out_shape=half_shapes if split_out else jax.ShapeDtypeStruct((m, D_MODEL), F32),
        scratch_shapes=scratch,
        compiler_params=_cparams(("arbitrary",)),
        name="tail",
    )(oa, ob, oc, *xs, mods, g_pre, g_post, w_gates, w_branch, w_out, w_ffn_in, w_ffn_out)


def _rope_tables(seq):
    t = np.arange(seq)
    row = (t // GRID_W).astype(np.float32)
    col = (t % GRID_W).astype(np.float32)
    n_pairs_axis = HEAD_DIM // 4
    inv = (np.float32(ROPE_BASE) ** (-np.arange(n_pairs_axis, dtype=np.float32) / n_pairs_axis)
           ).astype(np.float32)
    ang = np.concatenate([row[:, None] * inv, col[:, None] * inv], axis=-1).astype(np.float64)
    cos, sin = np.cos(ang), np.sin(ang)
    cos_l = np.tile(cos, (1, LANES // cos.shape[1]))
    sin_l = np.tile(np.concatenate([-sin, sin], axis=-1), (1, LANES // HEAD_DIM))
    return jnp.asarray(cos_l, F32), jnp.asarray(sin_l, F32)


def kernel(x_prompt, x_sample, cache_a_k, cache_a_v, cache_b_k, cache_b_v, c, c_ctx, w_ada, b_ada,
           norm_pre, norm_post, w_in, a_sink, b_rpb, w_branch, w_out, w_ffn_in, w_ffn_out):
    batch, seq, _ = x_prompt.shape
    dec_batch, dec_seq, _ = x_sample.shape
    past = cache_a_k.shape[2]
    assert dec_batch <= CTX_MOD_ROW and seq % A_BLOCK == 0 and dec_seq % A_BLOCK == 0

    cvec = jnp.concatenate(
        [c, c_ctx[None, :], jnp.zeros((MOD_ROWS - dec_batch - 1, D_MODEL), F32)], axis=0)
    assert w_in.shape[-1] == D_IN
    mods, w_proj_b, w_gates_b = _modulation(cvec, w_ada, b_ada, w_in)
    mods = mods.reshape(DEPTH, MOD_ROWS, 6, D_MODEL)
    w_branch_2d = w_branch.reshape(DEPTH, N_BRANCH * BRANCH_DIM, D_MODEL)
    g_pre, g_post = norm_pre, norm_post
    rope_tabs = _rope_tables(dec_seq)
    cak = cache_a_k.reshape(dec_batch, DEPTH, past, KV_A)
    cav = cache_a_v.reshape(dec_batch, DEPTH, past, KV_A)
    cbk = cache_b_k.reshape(dec_batch, DEPTH, past, BRANCH_DIM)
    cbv = cache_b_v.reshape(dec_batch, DEPTH, past, BRANCH_DIM)

    n_dr, n_dc = b_rpb.shape[2:]
    rpb_pad = jnp.pad(b_rpb.astype(F32), ((0, 0), (0, 0), (0, (-n_dr) % 8), (0, LANES - n_dc)))

    tm_proj, tm_tail = 1024, 512
    m_ctx, m_lat = batch * seq, dec_batch * dec_seq
    m_all = m_ctx + m_lat
    assert m_ctx % tm_proj == 0 and m_lat % tm_proj == 0
    ctx_row = lambda i: CTX_MOD_ROW
    lat_row = lambda i: (i * tm_proj) // dec_seq
    tail_row = lambda i: jnp.where(i < m_ctx // tm_tail, CTX_MOD_ROW, (i * tm_tail - m_ctx) // dec_seq)

    xs = (x_prompt.reshape(m_ctx, D_MODEL), x_sample.reshape(m_lat, D_MODEL))
    caches = None
    for l in range(DEPTH):
        x_ctx, x_lat = (xs[0], xs[1]) if len(xs) == 2 else (xs[0], xs[0])
        lat_tile0 = 0 if len(xs) == 2 else m_ctx // tm_proj
        qx, oc, *rest = _proj(x_ctx, mods, l, g_pre, w_proj_b, m=m_ctx, x_tile0=0, tm=tm_proj,
                              mod_row=ctx_row, seq=seq, oc_rows=m_all, caches=caches,
                              cast_weights=(w_ffn_in,))
        caches, (w_ffn_in_b,) = rest[:4], rest[4:]
        oa, ob, w_branch_b, w_out_b, w_ffn_out_b = _attn_ctx(
            qx, caches, a_sink, l, seq, (w_branch_2d, w_out, w_ffn_out), m_all)
        qx, kv, oc = _proj(x_lat, mods, l, g_pre, w_proj_b, m=m_lat, x_tile0=lat_tile0, tm=tm_proj,
                           mod_row=lat_row, seq=dec_seq, oc_all=oc, rope_tabs=rope_tabs)
        last = l == DEPTH - 1
        w_gates_l = w_gates_b
        if last:
            oa, ob = _attn_lat(qx, kv, (cak, cav), (cbk, cbv), a_sink, rpb_pad, l, dec_seq, oa, ob)
        else:
            oa, w_proj_b, w_gates_b, ob = _attn_lat(qx, kv, (cak, cav), (cbk, cbv), a_sink, rpb_pad, l,
                                                    dec_seq, oa, ob, w_in_next=(w_in, l + 1))
        out = _tail(oa, ob, oc, xs, mods, l, g_pre, g_post, w_gates_l,
                    w_branch_b.reshape(N_BRANCH, BRANCH_DIM, D_MODEL), w_out_b, w_ffn_in_b, w_ffn_out_b,
                    tm=tm_tail, m_ctx=m_ctx, mod_row=tail_row, split_out=last)
        xs = tuple(out) if last else (out,)
    y_prompt = xs[0].reshape(batch, seq, D_MODEL)
    y_sample = xs[1].reshape(dec_batch, dec_seq, D_MODEL)
    nak, nav, nbk, nbv = caches
    new_a_k = nak.reshape(batch, DEPTH, seq, A_KV_HEADS, HEAD_DIM)
    new_a_v = nav.reshape(batch, DEPTH, seq, A_KV_HEADS, HEAD_DIM)
    new_b_k = nbk.reshape(batch, DEPTH, seq, B_HEADS, HEAD_DIM)
    new_b_v = nbv.reshape(batch, DEPTH, seq, B_HEADS, HEAD_DIM)
    return (y_prompt, y_sample, new_a_k, new_a_v, new_b_k, new_b_v)
```
